```python
import math
import jax, jax.numpy as jnp
from jax import lax
import numpy as np

D_MODEL = 1024
BATCH = 16
SEQ = 2048
DEPTH = 2

GRID_W = 64
CTX_LEN = 256
N_MIXERS = 4
GROUP_W = D_MODEL // N_MIXERS
D_MIX = N_MIXERS * GROUP_W
LN_EPS = 1e-5
CONF_K = 31
CONF_GROUPS = 4
NA_HEADS = 4
HEAD_DIM = GROUP_W // NA_HEADS
NA_KH = 8
NA_KW = 16
ROPE_BASE = 10000.0
HY_SHORT = 3
HY_BANDS = 16
HY_EMB = 1 + 2 * HY_BANDS
HY_HIDDEN = 64
HY_SIN_FREQ = 1.0
HY_MIN_DECAY = 3.0
HY_MAX_DECAY = 15.0
SSD_HEADS = 4
SSD_HEAD_DIM = GROUP_W // SSD_HEADS
SSD_GROUPS = 2
SSD_STATE = 64
SSD_CONV = 3
SSD_CHUNK = 128
XBC_W = GROUP_W + 2 * SSD_GROUPS * SSD_STATE
IN_COLS = 2 * GROUP_W + 3 * GROUP_W + 3 * GROUP_W + GROUP_W + XBC_W + 2 * SSD_HEADS
PEER_HEADS = 8
PEER_KEYS = 128
PEER_TOPK = 16
PEER_QDIM = 256
N_EXPERTS = PEER_KEYS * PEER_KEYS
PEER_BLOCK = 128

kernel_name = 'hybrid_diffusion_block'


def _standardize(x):
    xf = x.astype(jnp.float32)
    mu = jnp.mean(xf, -1, keepdims=True)
    var = jnp.mean(jnp.square(xf - mu), -1, keepdims=True)
    return (xf - mu) * lax.rsqrt(var + LN_EPS)


def layer_norm(x, g, b):
    return (_standardize(x) * g + b).astype(x.dtype)


def dwconv(x, w):
    K, C = w.shape
    pad = (K - 1) // 2
    return lax.conv_general_dilated(x, w[:, None, :].astype(x.dtype), window_strides=(1,),
                                    padding=[(pad, pad)], dimension_numbers=('NWC', 'WIO', 'NWC'),
                                    feature_group_count=C)


def _split_cols(p):
    sizes = (2 * GROUP_W, 3 * GROUP_W, 3 * GROUP_W, GROUP_W, XBC_W, 2 * SSD_HEADS)
    points = [int(v) for v in np.cumsum(sizes)[:-1]]
    return jnp.split(p, points, axis=-1)


def conformer_conv(p, dw_w, dw_b, n_g, n_b):
    a, gate = jnp.split(p, 2, -1)
    u = a * jax.nn.sigmoid(gate)
    u = dwconv(u, dw_w) + dw_b
    Bsz, L, C = u.shape
    un = _standardize(u.reshape(Bsz, L, CONF_GROUPS, C // CONF_GROUPS)).reshape(Bsz, L, C)
    un = (un * n_g + n_b).astype(u.dtype)
    return jax.nn.silu(un)


def axial_rope(rows, head_dim):
    n_f = head_dim // 4
    inv = ROPE_BASE ** (-jnp.arange(n_f, dtype=jnp.float32) / n_f)
    t = jnp.arange(rows * GRID_W)
    r = (t // GRID_W).astype(jnp.float32)
    col = (t % GRID_W).astype(jnp.float32)
    ang = jnp.concatenate([r[:, None] * inv, col[:, None] * inv], -1)
    return jnp.cos(ang), jnp.sin(ang)


def apply_rope(x, cos, sin):
    x1, x2 = jnp.split(x.astype(jnp.float32), 2, -1)
    c = cos[None, :, None, :]
    s = sin[None, :, None, :]
    return jnp.concatenate([x1 * c - x2 * s, x1 * s + x2 * c], -1).astype(x.dtype)


def neighborhood_attention(q_rot, k_rot, v, q_plain, k_ctx, v_ctx, rpb):
    Bsz, S, H, d = q_rot.shape
    rows = S // GRID_W
    kh = min(NA_KH, rows)
    qg = q_rot.reshape(Bsz, rows, GRID_W, H, d)
    kg = k_rot.reshape(Bsz, rows, GRID_W, H, d)
    vg = v.reshape(Bsz, rows, GRID_W, H, d)
    qpg = q_plain.reshape(Bsz, rows, GRID_W, H, d)
    cq = jnp.arange(GRID_W)
    col_idx = jnp.clip(cq - NA_KW // 2, 0, GRID_W - NA_KW)[:, None] + jnp.arange(NA_KW)[None, :]
    col_bias_idx = col_idx - cq[:, None] + (NA_KW - 1)
    scale = d ** -0.5

    def row_block(r):
        rs = jnp.clip(r - kh // 2, 0, rows - kh)
        kr = lax.dynamic_slice_in_dim(kg, rs, kh, axis=1)[:, :, col_idx]
        vr = lax.dynamic_slice_in_dim(vg, rs, kh, axis=1)[:, :, col_idx]
        qr = lax.dynamic_index_in_dim(qg, r, axis=1, keepdims=False)
        qpr = lax.dynamic_index_in_dim(qpg, r, axis=1, keepdims=False)
        row_bias_idx = rs + jnp.arange(kh) - r + (NA_KH - 1)
        bias = rpb[:, row_bias_idx][:, :, col_bias_idx].transpose(0, 2, 1, 3)
        s_loc = jnp.einsum('bchd,bicjhd->bhcij', qr, kr).astype(jnp.float32) * scale + bias
        s_ctx = jnp.einsum('bchd,bnhd->bhcn', qpr, k_ctx).astype(jnp.float32) * scale
        logits = jnp.concatenate([s_loc.reshape(Bsz, H, GRID_W, kh * NA_KW), s_ctx], -1)
        p = jax.nn.softmax(logits, -1).astype(v.dtype)
        p_loc = p[..., :kh * NA_KW].reshape(Bsz, H, GRID_W, kh, NA_KW)
        p_ctx = p[..., kh * NA_KW:]
        return (jnp.einsum('bhcij,bicjhd->bchd', p_loc, vr)
                + jnp.einsum('bhcn,bnhd->bchd', p_ctx, v_ctx))

    out = lax.map(row_block, jnp.arange(rows))
    return out.transpose(1, 0, 2, 3, 4).reshape(Bsz, S, H * d)


def context_attention(q, k, v):
    s = jnp.einsum('bqhd,bkhd->bhqk', q, k).astype(jnp.float32) * (q.shape[-1] ** -0.5)
    p = jax.nn.softmax(s, -1).astype(v.dtype)
    return jnp.einsum('bhqk,bkhd->bqhd', p, v)


def hyena_filters(L, w1, b1, w2, b2, w3, decay):
    tn = jnp.arange(L, dtype=jnp.float32)[:, None] / L
    bands = jnp.arange(1, HY_BANDS + 1, dtype=jnp.float32)[None, :]
    ang = 2.0 * math.pi * bands * tn
    z = jnp.concatenate([tn, jnp.sin(ang), jnp.cos(ang)], -1)
    hmid = jnp.sin(HY_SIN_FREQ * (z @ w1.astype(jnp.float32) + b1.astype(jnp.float32)))
    hmid = jnp.sin(HY_SIN_FREQ * (hmid @ w2.astype(jnp.float32) + b2.astype(jnp.float32)))
    k = (hmid @ w3.astype(jnp.float32)) * jnp.exp(-tn * decay.astype(jnp.float32))
    k = k / (jnp.sum(jnp.abs(k), axis=0, keepdims=True) + 1e-6)
    return k[:, :GROUP_W], k[:, GROUP_W:]


def bidir_fftconv(u, k_fwd, k_bwd):
    L = u.shape[1]
    k2 = jnp.concatenate([k_fwd, jnp.zeros_like(k_fwd[:1]), k_bwd[1:][::-1]], 0)
    kf = jnp.fft.rfft(k2, n=2 * L, axis=0)
    uf = jnp.fft.rfft(u.astype(jnp.float32), n=2 * L, axis=1)
    return jnp.fft.irfft(uf * kf[None], n=2 * L, axis=1)[:, :L].astype(u.dtype)


def hyena(p, short_w, short_b, w1, b1, w2, b2, w3, decay, skip):
    L = p.shape[1]
    p = dwconv(p, short_w) + short_b
    x0, x1, v = jnp.split(p, 3, -1)
    k_fwd, k_bwd = hyena_filters(L, w1, b1, w2, b2, w3, decay)
    u = v * x1
    y = bidir_fftconv(u, k_fwd, k_bwd) + u * skip
    return y * x0


def segsum(a):
    T = a.shape[-1]
    a_rep = jnp.broadcast_to(a[..., None], a.shape + (T,))
    a_rep = jnp.where(jnp.tril(jnp.ones((T, T), bool), -1), a_rep, 0.0)
    s = jnp.cumsum(a_rep, axis=-2)
    return jnp.where(jnp.tril(jnp.ones((T, T), bool)), s, -jnp.inf)


def ssd_scan(x, dt, A, Bh, Ch, init_state, want_y):
    Bsz, L, H, P = x.shape
    N = Bh.shape[-1]
    nc = L // SSD_CHUNK
    xd = (x.astype(jnp.float32) * dt[..., None]).reshape(Bsz, nc, SSD_CHUNK, H, P)
    a = (dt * A).reshape(Bsz, nc, SSD_CHUNK, H).transpose(0, 3, 1, 2)
    Bc = Bh.astype(jnp.float32).reshape(Bsz, nc, SSD_CHUNK, H, N)
    Cc = Ch.astype(jnp.float32).reshape(Bsz, nc, SSD_CHUNK, H, N)
    a_cum = jnp.cumsum(a, -1)
    decay_states = jnp.exp(a_cum[..., -1:] - a_cum)
    states = jnp.einsum('bclhn,bhcl,bclhp->bchpn', Bc, decay_states, xd)
    states = jnp.concatenate([init_state[:, None], states], 1)
    decay_chunk = jnp.exp(segsum(jnp.pad(a_cum[..., -1], ((0, 0), (0, 0), (1, 0)))))
    new_states = jnp.einsum('bhzc,bchpn->bzhpn', decay_chunk, states)
    prev_states, final = new_states[:, :-1], new_states[:, -1]
    if not want_y:
        return None, final
    Lmat = jnp.exp(segsum(a))
    y_diag = jnp.einsum('bclhn,bcshn,bhcls,bcshp->bclhp', Cc, Bc, Lmat, xd)
    y_off = jnp.einsum('bclhn,bchpn,bhcl->bclhp', Cc, prev_states, jnp.exp(a_cum))
    return (y_diag + y_off).reshape(Bsz, L, H, P), final


def _flip(t):
    return jnp.flip(t, axis=1)


def ssd_bidir(z, xbc, dt_raw, init_f, init_b, want_y, conv_w, conv_b, a_log, dt_bias, d_skip, norm_g):
    Bsz, L, _ = xbc.shape
    xbc = jax.nn.silu(dwconv(xbc, conv_w) + conv_b)
    xs = xbc[..., :GROUP_W].reshape(Bsz, L, SSD_HEADS, SSD_HEAD_DIM)
    bc = xbc[..., GROUP_W:].reshape(Bsz, L, 2, SSD_GROUPS, SSD_STATE)
    rep = SSD_HEADS // SSD_GROUPS
    Bh = jnp.repeat(bc[:, :, 0], rep, axis=2)
    Ch = jnp.repeat(bc[:, :, 1], rep, axis=2)
    dt = jax.nn.softplus(dt_raw.astype(jnp.float32).reshape(Bsz, L, 2, SSD_HEADS) + dt_bias.astype(jnp.float32))
    A = -jnp.exp(a_log.astype(jnp.float32))
    y_f, s_f = ssd_scan(xs, dt[:, :, 0], A[0], Bh, Ch, init_f, want_y)
    y_b, s_b = ssd_scan(_flip(xs), _flip(dt[:, :, 1]), A[1], _flip(Bh), _flip(Ch), init_b, want_y)
    if not want_y:
        return None, s_f, s_b
    y = y_f + _flip(y_b) + xs.astype(jnp.float32) * d_skip.astype(jnp.float32)[:, None]
    yg = (y.reshape(Bsz, L, GROUP_W) * jax.nn.silu(z.astype(jnp.float32)))
    yg = yg.reshape(Bsz, L, SSD_GROUPS, GROUP_W // SSD_GROUPS)
    yg = yg * lax.rsqrt(jnp.mean(jnp.square(yg), -1, keepdims=True) + LN_EPS)
    return (yg.reshape(Bsz, L, GROUP_W) * norm_g).astype(z.dtype), s_f, s_b


def mixer_sublayer(h, hc, ctx_out, w_in, w_out, conf_dw_w, conf_dw_b, conf_norm_g, conf_norm_b, na_rpb,
                   hy_short_w, hy_short_b, hy_w1, hy_b1, hy_w2, hy_b2, hy_w3, hy_decay, hy_bias,
                   ssd_conv_w, ssd_conv_b, ssd_a_log, ssd_dt_bias, ssd_d, ssd_norm_g):
    Bsz, S, _ = h.shape
    Lc = hc.shape[1]
    pa, pb, py, pz, pxbc, pdt = _split_cols(h @ w_in)
    ca, cb, cy, cz, cxbc, cdt = _split_cols(hc @ w_in)
    qc, kc, vc = [t.reshape(Bsz, Lc, NA_HEADS, HEAD_DIM) for t in jnp.split(cb, 3, -1)]
    q, k, v = [t.reshape(Bsz, S, NA_HEADS, HEAD_DIM) for t in jnp.split(pb, 3, -1)]
    cos, sin = axial_rope(S // GRID_W, HEAD_DIM)
    y_b = neighborhood_attention(apply_rope(q, cos, sin), apply_rope(k, cos, sin), v, q, kc, vc, na_rpb)
    zero = jnp.zeros((Bsz, SSD_HEADS, SSD_HEAD_DIM, SSD_STATE), jnp.float32)
    y_dc, s_f, s_b = ssd_bidir(cz, cxbc, cdt, zero, zero, ctx_out, ssd_conv_w, ssd_conv_b,
                               ssd_a_log, ssd_dt_bias, ssd_d, ssd_norm_g)
    y_d, _, _ = ssd_bidir(pz, pxbc, pdt, s_f, s_b, True, ssd_conv_w, ssd_conv_b,
                          ssd_a_log, ssd_dt_bias, ssd_d, ssd_norm_g)
    y_a = conformer_conv(pa, conf_dw_w, conf_dw_b, conf_norm_g, conf_norm_b)
    y_c = hyena(py, hy_short_w, hy_short_b, hy_w1, hy_b1, hy_w2, hy_b2, hy_w3, hy_decay, hy_bias)
    y = jnp.concatenate([y_a, y_b, y_c, y_d], -1) @ w_out
    if not ctx_out:
        return y, None
    y_ac = conformer_conv(ca, conf_dw_w, conf_dw_b, conf_norm_g, conf_norm_b)
    y_bc = context_attention(qc, kc, vc).reshape(Bsz, Lc, GROUP_W)
    y_cc = hyena(cy, hy_short_w, hy_short_b, hy_w1, hy_b1, hy_w2, hy_b2, hy_w3, hy_decay, hy_bias)
    yc = jnp.concatenate([y_ac, y_bc, y_cc, y_dc], -1) @ w_out
    return y, yc


def peer(h, wq, sub_keys, u_tab, v_tab):
    Bsz, L, D = h.shape
    ht = h.reshape((Bsz * L) // PEER_BLOCK, PEER_BLOCK, D)
    K = PEER_TOPK

    def block(hb):
        q = (hb @ wq).reshape(PEER_BLOCK, PEER_HEADS, 2, PEER_QDIM // 2)
        s = jnp.einsum('thpk,hpnk->thpn', q, sub_keys).astype(jnp.float32)
        v1, i1 = lax.top_k(s[:, :, 0], K)
        v2, i2 = lax.top_k(s[:, :, 1], K)
        cand = (v1[..., :, None] + v2[..., None, :]).reshape(PEER_BLOCK, PEER_HEADS, K * K)
        cv, ci = lax.top_k(cand, K)
        e = (jnp.take_along_axis(i1, ci // K, axis=-1) * PEER_KEYS
             + jnp.take_along_axis(i2, ci % K, axis=-1)).reshape(PEER_BLOCK, PEER_HEADS * K)
        g = jax.nn.softmax(cv, -1).reshape(PEER_BLOCK, PEER_HEADS * K)
        act = jax.nn.gelu(jnp.einsum('td,ted->te', hb, u_tab[e]).astype(jnp.float32))
        w = (g * act).astype(hb.dtype)
        return jnp.einsum('te,ted->td', w, v_tab[e])

    return lax.map(block, ht).reshape(Bsz, L, D)


def setup_inputs(seed: int = 0) -> dict:
    key = jax.random.key(seed)
    ks = iter(jax.random.split(key, 48))
    f32 = jnp.float32
    Dp = DEPTH
    G = GROUP_W
    beta = (8.0 * DEPTH) ** -0.25

    def nrm(shape, scale):
        return jax.random.normal(next(ks), shape, f32) * scale

    x = nrm((BATCH, SEQ, D_MODEL), 1.0)
    c = nrm((BATCH, D_MODEL), 1.0)
    ctx = nrm((BATCH, CTX_LEN, D_MODEL), 1.0)
    c_ctx = nrm((D_MODEL,), 1.0)
    w_ada = nrm((Dp, D_MODEL, 6 * D_MODEL), D_MODEL ** -0.5)
    b_ada = nrm((Dp, 6 * D_MODEL), 0.02)
    w_in = nrm((Dp, D_MODEL, IN_COLS), D_MODEL ** -0.5)
    w_out = nrm((Dp, D_MIX, D_MODEL), D_MIX ** -0.5 * beta)
    ln1_g = 1.0 + nrm((Dp, D_MODEL), 0.02)
    ln1_b = nrm((Dp, D_MODEL), 0.02)
    ln2_g = 1.0 + nrm((Dp, D_MODEL), 0.02)
    ln2_b = nrm((Dp, D_MODEL), 0.02)
    conf_dw_w = nrm((Dp, CONF_K, G), CONF_K ** -0.5)
    conf_dw_b = nrm((Dp, G), 0.02)
    conf_norm_g = 1.0 + nrm((Dp, G), 0.02)
    conf_norm_b = nrm((Dp, G), 0.02)
    na_rpb = nrm((Dp, NA_HEADS, 2 * NA_KH - 1, 2 * NA_KW - 1), 0.1)
    hy_short_w = nrm((Dp, HY_SHORT, 3 * G), HY_SHORT ** -0.5)
    hy_short_b = nrm((Dp, 3 * G), 0.02)
    hy_w1 = nrm((Dp, HY_EMB, HY_HIDDEN), HY_EMB ** -0.5)
    hy_b1 = nrm((Dp, HY_HIDDEN), 0.1)
    hy_w2 = nrm((Dp, HY_HIDDEN, HY_HIDDEN), HY_HIDDEN ** -0.5)
    hy_b2 = nrm((Dp, HY_HIDDEN), 0.1)
    hy_w3 = nrm((Dp, HY_HIDDEN, 2 * G), HY_HIDDEN ** -0.5)
    hy_decay = jnp.broadcast_to(jnp.linspace(HY_MIN_DECAY, HY_MAX_DECAY, 2 * G, dtype=f32), (Dp, 2 * G)) * (1.0 + nrm((Dp, 2 * G), 0.05))
    hy_bias = nrm((Dp, G), 0.5)
    ssd_conv_w = nrm((Dp, SSD_CONV, XBC_W), SSD_CONV ** -0.5)
    ssd_conv_b = nrm((Dp, XBC_W), 0.02)
    ssd_a_log = jnp.log(jax.random.uniform(next(ks), (Dp, 2, SSD_HEADS), f32, minval=1.0, maxval=16.0))
    dt0 = jnp.exp(jax.random.uniform(next(ks), (Dp, 2, SSD_HEADS), f32, minval=math.log(1e-3), maxval=math.log(1e-1)))
    ssd_dt_bias = dt0 + jnp.log(-jnp.expm1(-dt0))
    ssd_d = 1.0 + nrm((Dp, SSD_HEADS), 0.02)
    ssd_norm_g = 1.0 + nrm((Dp, G), 0.02)
    peer_wq = nrm((Dp, D_MODEL, PEER_HEADS * PEER_QDIM), D_MODEL ** -0.5)
    peer_keys = nrm((Dp, PEER_HEADS, 2, PEER_KEYS, PEER_QDIM // 2), (PEER_QDIM // 2) ** -0.5)
    peer_u = nrm((Dp, N_EXPERTS, D_MODEL), D_MODEL ** -0.5)
    peer_v = nrm((Dp, N_EXPERTS, D_MODEL), (PEER_HEADS * PEER_TOPK) ** -0.5 * beta)
    return {'x': x, 'c': c, 'ctx': ctx, 'c_ctx': c_ctx, 'w_ada': w_ada, 'b_ada': b_ada,
            'w_in': w_in, 'w_out': w_out, 'ln1_g': ln1_g, 'ln1_b': ln1_b, 'ln2_g': ln2_g, 'ln2_b': ln2_b,
            'conf_dw_w': conf_dw_w, 'conf_dw_b': conf_dw_b, 'conf_norm_g': conf_norm_g, 'conf_norm_b': conf_norm_b,
            'na_rpb': na_rpb, 'hy_short_w': hy_short_w, 'hy_short_b': hy_short_b, 'hy_w1': hy_w1, 'hy_b1': hy_b1,
            'hy_w2': hy_w2, 'hy_b2': hy_b2, 'hy_w3': hy_w3, 'hy_decay': hy_decay, 'hy_bias': hy_bias,
            'ssd_conv_w': ssd_conv_w, 'ssd_conv_b': ssd_conv_b, 'ssd_a_log': ssd_a_log, 'ssd_dt_bias': ssd_dt_bias,
            'ssd_d': ssd_d, 'ssd_norm_g': ssd_norm_g, 'peer_wq': peer_wq, 'peer_keys': peer_keys,
            'peer_u': peer_u, 'peer_v': peer_v}


def reference(x, c, ctx, c_ctx, w_ada, b_ada, w_in, w_out, ln1_g, ln1_b, ln2_g, ln2_b,
              conf_dw_w, conf_dw_b, conf_norm_g, conf_norm_b, na_rpb, hy_short_w, hy_short_b,
              hy_w1, hy_b1, hy_w2, hy_b2, hy_w3, hy_decay, hy_bias, ssd_conv_w, ssd_conv_b,
              ssd_a_log, ssd_dt_bias, ssd_d, ssd_norm_g, peer_wq, peer_keys, peer_u, peer_v):
    alpha = (2.0 * DEPTH) ** 0.25
    s_c = jax.nn.silu(c)
    s_cc = jax.nn.silu(c_ctx)
    xc = ctx
    for l in range(DEPTH):
        ctx_out = l < DEPTH - 1
        mod = (s_c @ w_ada[l] + b_ada[l])[:, None, :]
        mod_c = (s_cc @ w_ada[l] + b_ada[l])[None, None, :]
        sh1, sc1, g1, sh2, sc2, g2 = jnp.split(mod, 6, -1)
        sh1c, sc1c, g1c, sh2c, sc2c, g2c = jnp.split(mod_c, 6, -1)
        y, yc = mixer_sublayer(x * (1.0 + sc1) + sh1, xc * (1.0 + sc1c) + sh1c, ctx_out,
                               w_in[l], w_out[l], conf_dw_w[l], conf_dw_b[l], conf_norm_g[l], conf_norm_b[l],
                               na_rpb[l], hy_short_w[l], hy_short_b[l], hy_w1[l], hy_b1[l], hy_w2[l], hy_b2[l],
                               hy_w3[l], hy_decay[l], hy_bias[l], ssd_conv_w[l], ssd_conv_b[l],
                               ssd_a_log[l], ssd_dt_bias[l], ssd_d[l], ssd_norm_g[l])
        x = layer_norm(alpha * x + g1 * y, ln1_g[l], ln1_b[l])
        x = layer_norm(alpha * x + g2 * peer(x * (1.0 + sc2) + sh2, peer_wq[l], peer_keys[l], peer_u[l], peer_v[l]),
                       ln2_g[l], ln2_b[l])
        if ctx_out:
            xc = layer_norm(alpha * xc + g1c * yc, ln1_g[l], ln1_b[l])
            xc = layer_norm(alpha * xc + g2c * peer(xc * (1.0 + sc2c) + sh2c, peer_wq[l], peer_keys[l], peer_u[l], peer_v[l]),
                            ln2_g[l], ln2_b[l])
    return x
```

```python
import math
import jax, jax.numpy as jnp
from jax import lax
import numpy as np
from jax.experimental import pallas as pl
from jax.experimental.pallas import tpu as pltpu

D_MODEL = 1024
BATCH = 16
SEQ = 2048
DEPTH = 2

GRID_W = 64
CTX_LEN = 256
N_MIXERS = 4
GROUP_W = D_MODEL // N_MIXERS
D_MIX = N_MIXERS * GROUP_W
LN_EPS = 1e-5
CONF_K = 31
CONF_GROUPS = 4
NA_HEADS = 4
HEAD_DIM = GROUP_W // NA_HEADS
NA_KH = 8
NA_KW = 16
ROPE_BASE = 10000.0
HY_SHORT = 3
HY_BANDS = 16
HY_EMB = 1 + 2 * HY_BANDS
HY_HIDDEN = 64
HY_SIN_FREQ = 1.0
SSD_HEADS = 4
SSD_HEAD_DIM = GROUP_W // SSD_HEADS
SSD_GROUPS = 2
SSD_STATE = 64
SSD_CONV = 3
SSD_CHUNK = 128
XBC_W = GROUP_W + 2 * SSD_GROUPS * SSD_STATE
IN_COLS = 2 * GROUP_W + 3 * GROUP_W + 3 * GROUP_W + GROUP_W + XBC_W + 2 * SSD_HEADS
PEER_HEADS = 8
PEER_KEYS = 128
PEER_TOPK = 16
PEER_QDIM = 256
N_EXPERTS = PEER_KEYS * PEER_KEYS
PEER_BLOCK = 128

_VMEM_LIMIT = 56 * 1024 * 1024
_IN_PAD = 2944


def _standardize(x):
    xf = x.astype(jnp.float32)
    mu = jnp.mean(xf, -1, keepdims=True)
    var = jnp.mean(jnp.square(xf - mu), -1, keepdims=True)
    return (xf - mu) * lax.rsqrt(var + LN_EPS)


def layer_norm(x, g, b):
    return (_standardize(x) * g + b).astype(x.dtype)


def dwconv(x, w):
    K, C = w.shape
    pad = (K - 1) // 2
    return lax.conv_general_dilated(x, w[:, None, :].astype(x.dtype), window_strides=(1,),
                                    padding=[(pad, pad)], dimension_numbers=('NWC', 'WIO', 'NWC'),
                                    feature_group_count=C)


def _split_cols(p):
    sizes = (2 * GROUP_W, 3 * GROUP_W, 3 * GROUP_W, GROUP_W, XBC_W, 2 * SSD_HEADS)
    points = [int(v) for v in np.cumsum(sizes)[:-1]]
    return jnp.split(p, points, axis=-1)


def _matmul_body(x_ref, w_ref, o_ref):
    o_ref[...] = jnp.dot(x_ref[...].astype(jnp.bfloat16), w_ref[...], preferred_element_type=jnp.float32)


def _pl_matmul(x, w, tm=512):
    M, K = x.shape
    N = w.shape[1]
    return pl.pallas_call(
        _matmul_body,
        grid=(M // tm,),
        in_specs=[pl.BlockSpec((tm, K), lambda i: (i, 0)),
                  pl.BlockSpec((K, N), lambda i: (0, 0))],
        out_specs=pl.BlockSpec((tm, N), lambda i: (i, 0)),
        out_shape=jax.ShapeDtypeStruct((M, N), jnp.float32),
        compiler_params=pltpu.CompilerParams(dimension_semantics=("arbitrary",),
                                             vmem_limit_bytes=_VMEM_LIMIT),
    )(x, w)


def _proj_in(h, w_in):
    Bsz, L, D = h.shape
    w = jnp.pad(w_in, ((0, 0), (0, _IN_PAD - IN_COLS))).astype(jnp.bfloat16)
    return _pl_matmul(h.reshape(Bsz * L, D), w)[:, :IN_COLS].reshape(Bsz, L, IN_COLS)


def _proj_out(y, w_out):
    Bsz, L, D = y.shape
    return _pl_matmul(y.reshape(Bsz * L, D), w_out.astype(jnp.bfloat16)).reshape(Bsz, L, w_out.shape[1])


def conformer_conv(p, dw_w, dw_b, n_g, n_b):
    a, gate = jnp.split(p, 2, -1)
    u = a * jax.nn.sigmoid(gate)
    u = dwconv(u, dw_w) + dw_b
    Bsz, L, C = u.shape
    un = _standardize(u.reshape(Bsz, L, CONF_GROUPS, C // CONF_GROUPS)).reshape(Bsz, L, C)
    un = (un * n_g + n_b).astype(u.dtype)
    return jax.nn.silu(un)


def axial_rope(rows, head_dim):
    n_f = head_dim // 4
    inv = ROPE_BASE ** (-jnp.arange(n_f, dtype=jnp.float32) / n_f)
    t = jnp.arange(rows * GRID_W)
    r = (t // GRID_W).astype(jnp.float32)
    col = (t % GRID_W).astype(jnp.float32)
    ang = jnp.concatenate([r[:, None] * inv, col[:, None] * inv], -1)
    return jnp.cos(ang), jnp.sin(ang)


def apply_rope(x, cos, sin):
    x1, x2 = jnp.split(x.astype(jnp.float32), 2, -1)
    c = cos[None, :, None, :]
    s = sin[None, :, None, :]
    return jnp.concatenate([x1 * c - x2 * s, x1 * s + x2 * c], -1).astype(x.dtype)


def neighborhood_attention(q_rot, k_rot, v, q_plain, k_ctx, v_ctx, rpb):
    Bsz, S, H, d = q_rot.shape
    rows = S // GRID_W
    kh = min(NA_KH, rows)
    qg = q_rot.reshape(Bsz, rows, GRID_W, H, d)
    kg = k_rot.reshape(Bsz, rows, GRID_W, H, d)
    vg = v.reshape(Bsz, rows, GRID_W, H, d)
    qpg = q_plain.reshape(Bsz, rows, GRID_W, H, d)
    cq = jnp.arange(GRID_W)
    col_idx = jnp.clip(cq - NA_KW // 2, 0, GRID_W - NA_KW)[:, None] + jnp.arange(NA_KW)[None, :]
    col_bias_idx = col_idx - cq[:, None] + (NA_KW - 1)
    scale = d ** -0.5

    def row_block(r):
        rs = jnp.clip(r - kh // 2, 0, rows - kh)
        kr = lax.dynamic_slice_in_dim(kg, rs, kh, axis=1)[:, :, col_idx]
        vr = lax.dynamic_slice_in_dim(vg, rs, kh, axis=1)[:, :, col_idx]
        qr = lax.dynamic_index_in_dim(qg, r, axis=1, keepdims=False)
        qpr = lax.dynamic_index_in_dim(qpg, r, axis=1, keepdims=False)
        row_bias_idx = rs + jnp.arange(kh) - r + (NA_KH - 1)
        bias = rpb[:, row_bias_idx][:, :, col_bias_idx].transpose(0, 2, 1, 3)
        s_loc = jnp.einsum('bchd,bicjhd->bhcij', qr, kr).astype(jnp.float32) * scale + bias
        s_ctx = jnp.einsum('bchd,bnhd->bhcn', qpr, k_ctx).astype(jnp.float32) * scale
        logits = jnp.concatenate([s_loc.reshape(Bsz, H, GRID_W, kh * NA_KW), s_ctx], -1)
        p = jax.nn.softmax(logits, -1).astype(v.dtype)
        p_loc = p[..., :kh * NA_KW].reshape(Bsz, H, GRID_W, kh, NA_KW)
        p_ctx = p[..., kh * NA_KW:]
        return (jnp.einsum('bhcij,bicjhd->bchd', p_loc, vr)
                + jnp.einsum('bhcn,bnhd->bchd', p_ctx, v_ctx))

    out = lax.map(row_block, jnp.arange(rows))
    return out.transpose(1, 0, 2, 3, 4).reshape(Bsz, S, H * d)


def context_attention(q, k, v):
    s = jnp.einsum('bqhd,bkhd->bhqk', q, k).astype(jnp.float32) * (q.shape[-1] ** -0.5)
    p = jax.nn.softmax(s, -1).astype(v.dtype)
    return jnp.einsum('bhqk,bkhd->bqhd', p, v)


def hyena_filters(L, w1, b1, w2, b2, w3, decay):
    tn = jnp.arange(L, dtype=jnp.float32)[:, None] / L
    bands = jnp.arange(1, HY_BANDS + 1, dtype=jnp.float32)[None, :]
    ang = 2.0 * math.pi * bands * tn
    z = jnp.concatenate([tn, jnp.sin(ang), jnp.cos(ang)], -1)
    hmid = jnp.sin(HY_SIN_FREQ * (z @ w1.astype(jnp.float32) + b1.astype(jnp.float32)))
    hmid = jnp.sin(HY_SIN_FREQ * (hmid @ w2.astype(jnp.float32) + b2.astype(jnp.float32)))
    k = (hmid @ w3.astype(jnp.float32)) * jnp.exp(-tn * decay.astype(jnp.float32))
    k = k / (jnp.sum(jnp.abs(k), axis=0, keepdims=True) + 1e-6)
    return k[:, :GROUP_W], k[:, GROUP_W:]


def bidir_fftconv(u, k_fwd, k_bwd):
    L = u.shape[1]
    k2 = jnp.concatenate([k_fwd, jnp.zeros_like(k_fwd[:1]), k_bwd[1:][::-1]], 0)
    kf = jnp.fft.rfft(k2, n=2 * L, axis=0)
    uf = jnp.fft.rfft(u.astype(jnp.float32), n=2 * L, axis=1)
    return jnp.fft.irfft(uf * kf[None], n=2 * L, axis=1)[:, :L].astype(u.dtype)


def hyena(p, short_w, short_b, w1, b1, w2, b2, w3, decay, skip):
    L = p.shape[1]
    p = dwconv(p, short_w) + short_b
    x0, x1, v = jnp.split(p, 3, -1)
    k_fwd, k_bwd = hyena_filters(L, w1, b1, w2, b2, w3, decay)
    u = v * x1
    y = bidir_fftconv(u, k_fwd, k_bwd) + u * skip
    return y * x0


def segsum(a):
    T = a.shape[-1]
    a_rep = jnp.broadcast_to(a[..., None], a.shape + (T,))
    a_rep = jnp.where(jnp.tril(jnp.ones((T, T), bool), -1), a_rep, 0.0)
    s = jnp.cumsum(a_rep, axis=-2)
    return jnp.where(jnp.tril(jnp.ones((T, T), bool)), s, -jnp.inf)


def ssd_scan(x, dt, A, Bh, Ch, init_state, want_y):
    Bsz, L, H, P = x.shape
    N = Bh.shape[-1]
    nc = L // SSD_CHUNK
    xd = (x.astype(jnp.float32) * dt[..., None]).reshape(Bsz, nc, SSD_CHUNK, H, P)
    a = (dt * A).reshape(Bsz, nc, SSD_CHUNK, H).transpose(0, 3, 1, 2)
    Bc = Bh.astype(jnp.float32).reshape(Bsz, nc, SSD_CHUNK, H, N)
    Cc = Ch.astype(jnp.float32).reshape(Bsz, nc, SSD_CHUNK, H, N)
    a_cum = jnp.cumsum(a, -1)
    decay_states = jnp.exp(a_cum[..., -1:] - a_cum)
    states = jnp.einsum('bclhn,bhcl,bclhp->bchpn', Bc, decay_states, xd)
    states = jnp.concatenate([init_state[:, None], states], 1)
    decay_chunk = jnp.exp(segsum(jnp.pad(a_cum[..., -1], ((0, 0), (0, 0), (1, 0)))))
    new_states = jnp.einsum('bhzc,bchpn->bzhpn', decay_chunk, states)
    prev_states, final = new_states[:, :-1], new_states[:, -1]
    if not want_y:
        return None, final
    Lmat = jnp.exp(segsum(a))
    y_diag = jnp.einsum('bclhn,bcshn,bhcls,bcshp->bclhp', Cc, Bc, Lmat, xd)
    y_off = jnp.einsum('bclhn,bchpn,bhcl->bclhp', Cc, prev_states, jnp.exp(a_cum))
    return (y_diag + y_off).reshape(Bsz, L, H, P), final


def _flip(t):
    return jnp.flip(t, axis=1)


def ssd_bidir(z, xbc, dt_raw, init_f, init_b, want_y, conv_w, conv_b, a_log, dt_bias, d_skip, norm_g):
    Bsz, L, _ = xbc.shape
    xbc = jax.nn.silu(dwconv(xbc, conv_w) + conv_b)
    xs = xbc[..., :GROUP_W].reshape(Bsz, L, SSD_HEADS, SSD_HEAD_DIM)
    bc = xbc[..., GROUP_W:].reshape(Bsz, L, 2, SSD_GROUPS, SSD_STATE)
    rep = SSD_HEADS // SSD_GROUPS
    Bh = jnp.repeat(bc[:, :, 0], rep, axis=2)
    Ch = jnp.repeat(bc[:, :, 1], rep, axis=2)
    dt = jax.nn.softplus(dt_raw.astype(jnp.float32).reshape(Bsz, L, 2, SSD_HEADS) + dt_bias.astype(jnp.float32))
    A = -jnp.exp(a_log.astype(jnp.float32))
    y_f, s_f = ssd_scan(xs, dt[:, :, 0], A[0], Bh, Ch, init_f, want_y)
    y_b, s_b = ssd_scan(_flip(xs), _flip(dt[:, :, 1]), A[1], _flip(Bh), _flip(Ch), init_b, want_y)
    if not want_y:
        return None, s_f, s_b
    y = y_f + _flip(y_b) + xs.astype(jnp.float32) * d_skip.astype(jnp.float32)[:, None]
    yg = (y.reshape(Bsz, L, GROUP_W) * jax.nn.silu(z.astype(jnp.float32)))
    yg = yg.reshape(Bsz, L, SSD_GROUPS, GROUP_W // SSD_GROUPS)
    yg = yg * lax.rsqrt(jnp.mean(jnp.square(yg), -1, keepdims=True) + LN_EPS)
    return (yg.reshape(Bsz, L, GROUP_W) * norm_g).astype(z.dtype), s_f, s_b


def mixer_sublayer(h, hc, ctx_out, w_in, w_out, conf_dw_w, conf_dw_b, conf_norm_g, conf_norm_b, na_rpb,
                   hy_short_w, hy_short_b, hy_w1, hy_b1, hy_w2, hy_b2, hy_w3, hy_decay, hy_bias,
                   ssd_conv_w, ssd_conv_b, ssd_a_log, ssd_dt_bias, ssd_d, ssd_norm_g):
    Bsz, S, _ = h.shape
    Lc = hc.shape[1]
    pa, pb, py, pz, pxbc, pdt = _split_cols(_proj_in(h, w_in))
    ca, cb, cy, cz, cxbc, cdt = _split_cols(_proj_in(hc, w_in))
    qc, kc, vc = [t.reshape(Bsz, Lc, NA_HEADS, HEAD_DIM) for t in jnp.split(cb, 3, -1)]
    q, k, v = [t.reshape(Bsz, S, NA_HEADS, HEAD_DIM) for t in jnp.split(pb, 3, -1)]
    cos, sin = axial_rope(S // GRID_W, HEAD_DIM)
    y_b = neighborhood_attention(apply_rope(q, cos, sin), apply_rope(k, cos, sin), v, q, kc, vc, na_rpb)
    zero = jnp.zeros((Bsz, SSD_HEADS, SSD_HEAD_DIM, SSD_STATE), jnp.float32)
    y_dc, s_f, s_b = ssd_bidir(cz, cxbc, cdt, zero, zero, ctx_out, ssd_conv_w, ssd_conv_b,
                               ssd_a_log, ssd_dt_bias, ssd_d, ssd_norm_g)
    y_d, _, _ = ssd_bidir(pz, pxbc, pdt, s_f, s_b, True, ssd_conv_w, ssd_conv_b,
                          ssd_a_log, ssd_dt_bias, ssd_d, ssd_norm_g)
    y_a = conformer_conv(pa, conf_dw_w, conf_dw_b, conf_norm_g, conf_norm_b)
    y_c = hyena(py, hy_short_w, hy_short_b, hy_w1, hy_b1, hy_w2, hy_b2, hy_w3, hy_decay, hy_bias)
    y = _proj_out(jnp.concatenate([y_a, y_b, y_c, y_d], -1), w_out)
    if not ctx_out:
        return y, None
    y_ac = conformer_conv(ca, conf_dw_w, conf_dw_b, conf_norm_g, conf_norm_b)
    y_bc = context_attention(qc, kc, vc).reshape(Bsz, Lc, GROUP_W)
    y_cc = hyena(cy, hy_short_w, hy_short_b, hy_w1, hy_b1, hy_w2, hy_b2, hy_w3, hy_decay, hy_bias)
    yc = _proj_out(jnp.concatenate([y_ac, y_bc, y_cc, y_dc], -1), w_out)
    return y, yc


def peer(h, wq, sub_keys, u_tab, v_tab):
    Bsz, L, D = h.shape
    ht = h.reshape((Bsz * L) // PEER_BLOCK, PEER_BLOCK, D)
    K = PEER_TOPK

    def block(hb):
        q = (hb @ wq).reshape(PEER_BLOCK, PEER_HEADS, 2, PEER_QDIM // 2)
        s = jnp.einsum('thpk,hpnk->thpn', q, sub_keys).astype(jnp.float32)
        v1, i1 = lax.top_k(s[:, :, 0], K)
        v2, i2 = lax.top_k(s[:, :, 1], K)
        cand = (v1[..., :, None] + v2[..., None, :]).reshape(PEER_BLOCK, PEER_HEADS, K * K)
        cv, ci = lax.top_k(cand, K)
        e = (jnp.take_along_axis(i1, ci // K, axis=-1) * PEER_KEYS
             + jnp.take_along_axis(i2, ci % K, axis=-1)).reshape(PEER_BLOCK, PEER_HEADS * K)
        g = jax.nn.softmax(cv, -1).reshape(PEER_BLOCK, PEER_HEADS * K)
        act = jax.nn.gelu(jnp.einsum('td,ted->te', hb, u_tab[e]).astype(jnp.float32))
        w = (g * act).astype(hb.dtype)
        return jnp.einsum('te,ted->td', w, v_tab[e])

    return lax.map(block, ht).reshape(Bsz, L, D)


def kernel(x, c, ctx, c_ctx, w_ada, b_ada, w_in, w_out, ln1_g, ln1_b, ln2_g, ln2_b,
           conf_dw_w, conf_dw_b, conf_norm_g, conf_norm_b, na_rpb, hy_short_w, hy_short_b,
           hy_w1, hy_b1, hy_w2, hy_b2, hy_w3, hy_decay, hy_bias, ssd_conv_w, ssd_conv_b,
           ssd_a_log, ssd_dt_bias, ssd_d, ssd_norm_g, peer_wq, peer_keys, peer_u, peer_v):
    alpha = (2.0 * DEPTH) ** 0.25
    s_c = jax.nn.silu(c)
    s_cc = jax.nn.silu(c_ctx)
    xc = ctx
    for l in range(DEPTH):
        ctx_out = l < DEPTH - 1
        mod = (s_c @ w_ada[l] + b_ada[l])[:, None, :]
        mod_c = (s_cc @ w_ada[l] + b_ada[l])[None, None, :]
        sh1, sc1, g1, sh2, sc2, g2 = jnp.split(mod, 6, -1)
        sh1c, sc1c, g1c, sh2c, sc2c, g2c = jnp.split(mod_c, 6, -1)
        y, yc = mixer_sublayer(x * (1.0 + sc1) + sh1, xc * (1.0 + sc1c) + sh1c, ctx_out,
                               w_in[l], w_out[l], conf_dw_w[l], conf_dw_b[l], conf_norm_g[l], conf_norm_b[l],
                               na_rpb[l], hy_short_w[l], hy_short_b[l], hy_w1[l], hy_b1[l], hy_w2[l], hy_b2[l],
                               hy_w3[l], hy_decay[l], hy_bias[l], ssd_conv_w[l], ssd_conv_b[l],
                               ssd_a_log[l], ssd_dt_bias[l], ssd_d[l], ssd_norm_g[l])
        x = layer_norm(alpha * x + g1 * y, ln1_g[l], ln1_b[l])
        x = layer_norm(alpha * x + g2 * peer(x * (1.0 + sc2) + sh2, peer_wq[l], peer_keys[l], peer_u[l], peer_v[l]),
                       ln2_g[l], ln2_b[l])
        if ctx_out:
            xc = layer_norm(alpha * xc + g1c * yc, ln1_g[l], ln1_b[l])
            xc = layer_norm(alpha * xc + g2c * peer(xc * (1.0 + sc2c) + sh2c, peer_wq[l], peer_keys[l], peer_u[l], peer_v[l]),
                            ln2_g[l], ln2_b[l])
    return x
```

```python
import functools
import math
import jax, jax.numpy as jnp
from jax import lax
import numpy as np
from jax.experimental import pallas as pl
from jax.experimental.pallas import tpu as pltpu

D_MODEL = 1024
BATCH = 16
SEQ = 2048
DEPTH = 2

GRID_W = 64
CTX_LEN = 256
N_MIXERS = 4
GROUP_W = D_MODEL // N_MIXERS
D_MIX = N_MIXERS * GROUP_W
LN_EPS = 1e-5
CONF_K = 31
CONF_GROUPS = 4
NA_HEADS = 4
HEAD_DIM = GROUP_W // NA_HEADS
NA_KH = 8
NA_KW = 16
ROPE_BASE = 10000.0
HY_SHORT = 3
HY_BANDS = 16
HY_EMB = 1 + 2 * HY_BANDS
HY_HIDDEN = 64
HY_SIN_FREQ = 1.0
SSD_HEADS = 4
SSD_HEAD_DIM = GROUP_W // SSD_HEADS
SSD_GROUPS = 2
SSD_STATE = 64
SSD_CONV = 3
SSD_CHUNK = 128
XBC_W = GROUP_W + 2 * SSD_GROUPS * SSD_STATE
IN_COLS = 2 * GROUP_W + 3 * GROUP_W + 3 * GROUP_W + GROUP_W + XBC_W + 2 * SSD_HEADS
PEER_HEADS = 8
PEER_KEYS = 128
PEER_TOPK = 16
PEER_QDIM = 256
N_EXPERTS = PEER_KEYS * PEER_KEYS
PEER_BLOCK = 128

_VMEM_LIMIT = 56 * 1024 * 1024
_IN_PAD = 2944


def _standardize(x):
    xf = x.astype(jnp.float32)
    mu = jnp.mean(xf, -1, keepdims=True)
    var = jnp.mean(jnp.square(xf - mu), -1, keepdims=True)
    return (xf - mu) * lax.rsqrt(var + LN_EPS)


def layer_norm(x, g, b):
    return (_standardize(x) * g + b).astype(x.dtype)


def dwconv(x, w):
    K, C = w.shape
    pad = (K - 1) // 2
    return lax.conv_general_dilated(x, w[:, None, :].astype(x.dtype), window_strides=(1,),
                                    padding=[(pad, pad)], dimension_numbers=('NWC', 'WIO', 'NWC'),
                                    feature_group_count=C)


def _split_cols(p):
    sizes = (2 * GROUP_W, 3 * GROUP_W, 3 * GROUP_W, GROUP_W, XBC_W, 2 * SSD_HEADS)
    points = [int(v) for v in np.cumsum(sizes)[:-1]]
    return jnp.split(p, points, axis=-1)


def _matmul_body(x_ref, w_ref, o_ref):
    o_ref[...] = jnp.dot(x_ref[...].astype(jnp.bfloat16), w_ref[...], preferred_element_type=jnp.float32)


def _pl_matmul(x, w, tm=512):
    M, K = x.shape
    N = w.shape[1]
    return pl.pallas_call(
        _matmul_body,
        grid=(M // tm,),
        in_specs=[pl.BlockSpec((tm, K), lambda i: (i, 0)),
                  pl.BlockSpec((K, N), lambda i: (0, 0))],
        out_specs=pl.BlockSpec((tm, N), lambda i: (i, 0)),
        out_shape=jax.ShapeDtypeStruct((M, N), jnp.float32),
        compiler_params=pltpu.CompilerParams(dimension_semantics=("arbitrary",),
                                             vmem_limit_bytes=_VMEM_LIMIT),
    )(x, w)


def _proj_in(h, w_in):
    Bsz, L, D = h.shape
    w = jnp.pad(w_in, ((0, 0), (0, _IN_PAD - IN_COLS))).astype(jnp.bfloat16)
    return _pl_matmul(h.reshape(Bsz * L, D), w)[:, :IN_COLS].reshape(Bsz, L, IN_COLS)


def _proj_out(y, w_out):
    Bsz, L, D = y.shape
    return _pl_matmul(y.reshape(Bsz * L, D), w_out.astype(jnp.bfloat16)).reshape(Bsz, L, w_out.shape[1])


def conformer_conv(p, dw_w, dw_b, n_g, n_b):
    a, gate = jnp.split(p, 2, -1)
    u = a * jax.nn.sigmoid(gate)
    u = dwconv(u, dw_w) + dw_b
    Bsz, L, C = u.shape
    un = _standardize(u.reshape(Bsz, L, CONF_GROUPS, C // CONF_GROUPS)).reshape(Bsz, L, C)
    un = (un * n_g + n_b).astype(u.dtype)
    return jax.nn.silu(un)


def axial_rope(rows, head_dim):
    n_f = head_dim // 4
    inv = ROPE_BASE ** (-jnp.arange(n_f, dtype=jnp.float32) / n_f)
    t = jnp.arange(rows * GRID_W)
    r = (t // GRID_W).astype(jnp.float32)
    col = (t % GRID_W).astype(jnp.float32)
    ang = jnp.concatenate([r[:, None] * inv, col[:, None] * inv], -1)
    return jnp.cos(ang), jnp.sin(ang)


def apply_rope(x, cos, sin):
    x1, x2 = jnp.split(x.astype(jnp.float32), 2, -1)
    c = cos[None, :, None, :]
    s = sin[None, :, None, :]
    return jnp.concatenate([x1 * c - x2 * s, x1 * s + x2 * c], -1).astype(x.dtype)


def neighborhood_attention(q_rot, k_rot, v, q_plain, k_ctx, v_ctx, rpb):
    Bsz, S, H, d = q_rot.shape
    rows = S // GRID_W
    kh = min(NA_KH, rows)
    qg = q_rot.reshape(Bsz, rows, GRID_W, H, d)
    kg = k_rot.reshape(Bsz, rows, GRID_W, H, d)
    vg = v.reshape(Bsz, rows, GRID_W, H, d)
    qpg = q_plain.reshape(Bsz, rows, GRID_W, H, d)
    cq = jnp.arange(GRID_W)
    col_idx = jnp.clip(cq - NA_KW // 2, 0, GRID_W - NA_KW)[:, None] + jnp.arange(NA_KW)[None, :]
    col_bias_idx = col_idx - cq[:, None] + (NA_KW - 1)
    scale = d ** -0.5

    def row_block(r):
        rs = jnp.clip(r - kh // 2, 0, rows - kh)
        kr = lax.dynamic_slice_in_dim(kg, rs, kh, axis=1)[:, :, col_idx]
        vr = lax.dynamic_slice_in_dim(vg, rs, kh, axis=1)[:, :, col_idx]
        qr = lax.dynamic_index_in_dim(qg, r, axis=1, keepdims=False)
        qpr = lax.dynamic_index_in_dim(qpg, r, axis=1, keepdims=False)
        row_bias_idx = rs + jnp.arange(kh) - r + (NA_KH - 1)
        bias = rpb[:, row_bias_idx][:, :, col_bias_idx].transpose(0, 2, 1, 3)
        s_loc = jnp.einsum('bchd,bicjhd->bhcij', qr, kr).astype(jnp.float32) * scale + bias
        s_ctx = jnp.einsum('bchd,bnhd->bhcn', qpr, k_ctx).astype(jnp.float32) * scale
        logits = jnp.concatenate([s_loc.reshape(Bsz, H, GRID_W, kh * NA_KW), s_ctx], -1)
        p = jax.nn.softmax(logits, -1).astype(v.dtype)
        p_loc = p[..., :kh * NA_KW].reshape(Bsz, H, GRID_W, kh, NA_KW)
        p_ctx = p[..., kh * NA_KW:]
        return (jnp.einsum('bhcij,bicjhd->bchd', p_loc, vr)
                + jnp.einsum('bhcn,bnhd->bchd', p_ctx, v_ctx))

    out = lax.map(row_block, jnp.arange(rows))
    return out.transpose(1, 0, 2, 3, 4).reshape(Bsz, S, H * d)


_NA_MASK = -1e30


def _na_tables(rows, rpb):
    cos, sin = axial_rope(rows, HEAD_DIM)
    cos_f = jnp.tile(cos, (1, 2 * NA_HEADS))
    sin_s = jnp.tile(jnp.concatenate([-sin, sin], -1), (1, NA_HEADS))
    cq = jnp.arange(GRID_W)
    cs = jnp.clip(cq - NA_KW // 2, 0, GRID_W - NA_KW)
    col = jnp.arange(GRID_W)
    in_band = (col[None, :] >= cs[:, None]) & (col[None, :] < cs[:, None] + NA_KW)
    cb_idx = jnp.clip(col[None, :] - cq[:, None] + (NA_KW - 1), 0, 2 * NA_KW - 2)
    po = jnp.arange(NA_KH)
    rb_idx = jnp.arange(NA_KH)[None, :] - po[:, None] + (NA_KH - 1)
    bias = rpb[:, rb_idx][:, :, :, cb_idx]
    bias = jnp.where(in_band[None, None, None], bias, _NA_MASK)
    bias = bias.transpose(1, 0, 3, 2, 4).reshape(NA_KH, NA_HEADS, GRID_W, NA_KH * GRID_W)
    return cos_f, sin_s, bias


def _rope_lanes(x, cos_f, sin_s):
    n = x.shape[-1]
    hd = HEAD_DIM // 2
    first = (lax.broadcasted_iota(jnp.int32, x.shape, 1) % HEAD_DIM) < hd
    partner = jnp.where(first, pltpu.roll(x, n - hd, 1), pltpu.roll(x, hd, 1))
    return x * cos_f + partner * sin_s


def _softmax_pv(s_parts, v_parts):
    m = s_parts[0].max(-1, keepdims=True)
    for s in s_parts[1:]:
        m = jnp.maximum(m, s.max(-1, keepdims=True))
    l = 0.0
    o = 0.0
    for s, v in zip(s_parts, v_parts):
        p = jnp.exp(s - m)
        l = l + p.sum(-1, keepdims=True)
        o = o + jnp.dot(p.astype(jnp.bfloat16), v, preferred_element_type=jnp.float32)
    return o / l


def _na_body(rows, q_ref, k_ref, v_ref, kc_ref, vc_ref, cosq_ref, sinq_ref, cosk_ref, sink_ref, bias_ref,
             o_ref, krot_ref, vbf_ref):
    r = pl.program_id(1)
    scale = HEAD_DIM ** -0.5

    @pl.when(r == 0)
    def _():
        krot_ref[...] = _rope_lanes(k_ref[0], cosk_ref[...], sink_ref[...]).astype(jnp.bfloat16)
        vbf_ref[...] = v_ref[0].astype(jnp.bfloat16)

    rs = jnp.clip(r - NA_KH // 2, 0, rows - NA_KH)
    start = pl.multiple_of(rs * GRID_W, GRID_W)
    win = NA_KH * GRID_W
    q = q_ref[0] * scale
    q_rot = _rope_lanes(q, cosq_ref[...], sinq_ref[...]).astype(jnp.bfloat16)
    q_plain = q.astype(jnp.bfloat16)
    kw = krot_ref[pl.ds(start, win), :]
    vw = vbf_ref[pl.ds(start, win), :]
    kc = kc_ref[0].astype(jnp.bfloat16)
    vc = vc_ref[0].astype(jnp.bfloat16)
    outs = []
    for h in range(NA_HEADS):
        hs = slice(h * HEAD_DIM, (h + 1) * HEAD_DIM)
        s_loc = lax.dot_general(q_rot[:, hs], kw[:, hs], _NT, preferred_element_type=jnp.float32) + bias_ref[0, h]
        s_ctx = lax.dot_general(q_plain[:, hs], kc[:, hs], _NT, preferred_element_type=jnp.float32)
        outs.append(_softmax_pv([s_loc, s_ctx], [vw[:, hs], vc[:, hs]]))
    o_ref[0] = jnp.concatenate(outs, axis=-1)


def _na_attention(pb, cb, rpb):
    Bsz, S, _ = pb.shape
    Lc = cb.shape[1]
    rows = S // GRID_W
    G = GROUP_W
    cos_f, sin_s, bias = _na_tables(rows, rpb)

    def bias_idx(b, r):
        rs = jnp.clip(r - NA_KH // 2, 0, rows - NA_KH)
        return (r - rs, 0, 0, 0)

    return pl.pallas_call(
        functools.partial(_na_body, rows),
        grid=(Bsz, rows),
        in_specs=[pl.BlockSpec((1, GRID_W, G), lambda b, r: (b, r, 0)),
                  pl.BlockSpec((1, S, G), lambda b, r: (b, 0, 1)),
                  pl.BlockSpec((1, S, G), lambda b, r: (b, 0, 2)),
                  pl.BlockSpec((1, Lc, G), lambda b, r: (b, 0, 1)),
                  pl.BlockSpec((1, Lc, G), lambda b, r: (b, 0, 2)),
                  pl.BlockSpec((GRID_W, G), lambda b, r: (r, 0)),
                  pl.BlockSpec((GRID_W, G), lambda b, r: (r, 0)),
                  pl.BlockSpec((S, G), lambda b, r: (0, 0)),
                  pl.BlockSpec((S, G), lambda b, r: (0, 0)),
                  pl.BlockSpec((1, NA_HEADS, GRID_W, NA_KH * GRID_W), bias_idx)],
        out_specs=pl.BlockSpec((1, GRID_W, G), lambda b, r: (b, r, 0)),
        out_shape=jax.ShapeDtypeStruct((Bsz, S, G), jnp.float32),
        scratch_shapes=[pltpu.VMEM((S, G), jnp.bfloat16), pltpu.VMEM((S, G), jnp.bfloat16)],
        compiler_params=pltpu.CompilerParams(dimension_semantics=("arbitrary", "arbitrary"),
                                             vmem_limit_bytes=_VMEM_LIMIT),
        name="na_attention",
    )(pb, pb, pb, cb, cb, cos_f, sin_s, cos_f, sin_s, bias)


def _ctx_attn_body(q_ref, k_ref, v_ref, o_ref):
    q = (q_ref[0] * HEAD_DIM ** -0.5).astype(jnp.bfloat16)
    k = k_ref[0].astype(jnp.bfloat16)
    v = v_ref[0].astype(jnp.bfloat16)
    outs = []
    for h in range(NA_HEADS):
        hs = slice(h * HEAD_DIM, (h + 1) * HEAD_DIM)
        s = lax.dot_general(q[:, hs], k[:, hs], _NT, preferred_element_type=jnp.float32)
        outs.append(_softmax_pv([s], [v[:, hs]]))
    o_ref[0] = jnp.concatenate(outs, axis=-1)


def _ctx_attention(cb):
    Bsz, Lc, _ = cb.shape
    G = GROUP_W
    return pl.pallas_call(
        _ctx_attn_body,
        grid=(Bsz,),
        in_specs=[pl.BlockSpec((1, Lc, G), lambda b: (b, 0, 0)),
                  pl.BlockSpec((1, Lc, G), lambda b: (b, 0, 1)),
                  pl.BlockSpec((1, Lc, G), lambda b: (b, 0, 2))],
        out_specs=pl.BlockSpec((1, Lc, G), lambda b: (b, 0, 0)),
        out_shape=jax.ShapeDtypeStruct((Bsz, Lc, G), jnp.float32),
        compiler_params=pltpu.CompilerParams(dimension_semantics=("arbitrary",)),
        name="ctx_attention",
    )(cb, cb, cb)


def context_attention(q, k, v):
    s = jnp.einsum('bqhd,bkhd->bhqk', q, k).astype(jnp.float32) * (q.shape[-1] ** -0.5)
    p = jax.nn.softmax(s, -1).astype(v.dtype)
    return jnp.einsum('bhqk,bkhd->bqhd', p, v)


def hyena_filters(L, w1, b1, w2, b2, w3, decay):
    tn = jnp.arange(L, dtype=jnp.float32)[:, None] / L
    bands = jnp.arange(1, HY_BANDS + 1, dtype=jnp.float32)[None, :]
    ang = 2.0 * math.pi * bands * tn
    z = jnp.concatenate([tn, jnp.sin(ang), jnp.cos(ang)], -1)
    hmid = jnp.sin(HY_SIN_FREQ * (z @ w1.astype(jnp.float32) + b1.astype(jnp.float32)))
    hmid = jnp.sin(HY_SIN_FREQ * (hmid @ w2.astype(jnp.float32) + b2.astype(jnp.float32)))
    k = (hmid @ w3.astype(jnp.float32)) * jnp.exp(-tn * decay.astype(jnp.float32))
    k = k / (jnp.sum(jnp.abs(k), axis=0, keepdims=True) + 1e-6)
    return k[:, :GROUP_W], k[:, GROUP_W:]


def bidir_fftconv(u, k_fwd, k_bwd):
    L = u.shape[1]
    k2 = jnp.concatenate([k_fwd, jnp.zeros_like(k_fwd[:1]), k_bwd[1:][::-1]], 0)
    kf = jnp.fft.rfft(k2, n=2 * L, axis=0)
    uf = jnp.fft.rfft(u.astype(jnp.float32), n=2 * L, axis=1)
    return jnp.fft.irfft(uf * kf[None], n=2 * L, axis=1)[:, :L].astype(u.dtype)


def hyena(p, short_w, short_b, w1, b1, w2, b2, w3, decay, skip):
    L = p.shape[1]
    p = dwconv(p, short_w) + short_b
    x0, x1, v = jnp.split(p, 3, -1)
    k_fwd, k_bwd = hyena_filters(L, w1, b1, w2, b2, w3, decay)
    u = v * x1
    y = bidir_fftconv(u, k_fwd, k_bwd) + u * skip
    return y * x0


def segsum(a):
    T = a.shape[-1]
    a_rep = jnp.broadcast_to(a[..., None], a.shape + (T,))
    a_rep = jnp.where(jnp.tril(jnp.ones((T, T), bool), -1), a_rep, 0.0)
    s = jnp.cumsum(a_rep, axis=-2)
    return jnp.where(jnp.tril(jnp.ones((T, T), bool)), s, -jnp.inf)


def ssd_scan(x, dt, A, Bh, Ch, init_state, want_y):
    Bsz, L, H, P = x.shape
    N = Bh.shape[-1]
    nc = L // SSD_CHUNK
    xd = (x.astype(jnp.float32) * dt[..., None]).reshape(Bsz, nc, SSD_CHUNK, H, P)
    a = (dt * A).reshape(Bsz, nc, SSD_CHUNK, H).transpose(0, 3, 1, 2)
    Bc = Bh.astype(jnp.float32).reshape(Bsz, nc, SSD_CHUNK, H, N)
    Cc = Ch.astype(jnp.float32).reshape(Bsz, nc, SSD_CHUNK, H, N)
    a_cum = jnp.cumsum(a, -1)
    decay_states = jnp.exp(a_cum[..., -1:] - a_cum)
    states = jnp.einsum('bclhn,bhcl,bclhp->bchpn', Bc, decay_states, xd)
    states = jnp.concatenate([init_state[:, None], states], 1)
    decay_chunk = jnp.exp(segsum(jnp.pad(a_cum[..., -1], ((0, 0), (0, 0), (1, 0)))))
    new_states = jnp.einsum('bhzc,bchpn->bzhpn', decay_chunk, states)
    prev_states, final = new_states[:, :-1], new_states[:, -1]
    if not want_y:
        return None, final
    Lmat = jnp.exp(segsum(a))
    y_diag = jnp.einsum('bclhn,bcshn,bhcls,bcshp->bclhp', Cc, Bc, Lmat, xd)
    y_off = jnp.einsum('bclhn,bchpn,bhcl->bclhp', Cc, prev_states, jnp.exp(a_cum))
    return (y_diag + y_off).reshape(Bsz, L, H, P), final


def _flip(t):
    return jnp.flip(t, axis=1)


def ssd_bidir(z, xbc, dt_raw, init_f, init_b, want_y, conv_w, conv_b, a_log, dt_bias, d_skip, norm_g):
    Bsz, L, _ = xbc.shape
    xbc = jax.nn.silu(dwconv(xbc, conv_w) + conv_b)
    xs = xbc[..., :GROUP_W].reshape(Bsz, L, SSD_HEADS, SSD_HEAD_DIM)
    bc = xbc[..., GROUP_W:].reshape(Bsz, L, 2, SSD_GROUPS, SSD_STATE)
    rep = SSD_HEADS // SSD_GROUPS
    Bh = jnp.repeat(bc[:, :, 0], rep, axis=2)
    Ch = jnp.repeat(bc[:, :, 1], rep, axis=2)
    dt = jax.nn.softplus(dt_raw.astype(jnp.float32).reshape(Bsz, L, 2, SSD_HEADS) + dt_bias.astype(jnp.float32))
    A = -jnp.exp(a_log.astype(jnp.float32))
    y_f, s_f = ssd_scan(xs, dt[:, :, 0], A[0], Bh, Ch, init_f, want_y)
    y_b, s_b = ssd_scan(_flip(xs), _flip(dt[:, :, 1]), A[1], _flip(Bh), _flip(Ch), init_b, want_y)
    if not want_y:
        return None, s_f, s_b
    y = y_f + _flip(y_b) + xs.astype(jnp.float32) * d_skip.astype(jnp.float32)[:, None]
    yg = (y.reshape(Bsz, L, GROUP_W) * jax.nn.silu(z.astype(jnp.float32)))
    yg = yg.reshape(Bsz, L, SSD_GROUPS, GROUP_W // SSD_GROUPS)
    yg = yg * lax.rsqrt(jnp.mean(jnp.square(yg), -1, keepdims=True) + LN_EPS)
    return (yg.reshape(Bsz, L, GROUP_W) * norm_g).astype(z.dtype), s_f, s_b


def mixer_sublayer(h, hc, ctx_out, w_in, w_out, conf_dw_w, conf_dw_b, conf_norm_g, conf_norm_b, na_rpb,
                   hy_short_w, hy_short_b, hy_w1, hy_b1, hy_w2, hy_b2, hy_w3, hy_decay, hy_bias,
                   ssd_conv_w, ssd_conv_b, ssd_a_log, ssd_dt_bias, ssd_d, ssd_norm_g):
    Bsz, S, _ = h.shape
    Lc = hc.shape[1]
    pa, pb, py, pz, pxbc, pdt = _split_cols(_proj_in(h, w_in))
    ca, cb, cy, cz, cxbc, cdt = _split_cols(_proj_in(hc, w_in))
    y_b = _na_attention(pb, cb, na_rpb)
    zero = jnp.zeros((Bsz, SSD_HEADS, SSD_HEAD_DIM, SSD_STATE), jnp.float32)
    y_dc, s_f, s_b = ssd_bidir(cz, cxbc, cdt, zero, zero, ctx_out, ssd_conv_w, ssd_conv_b,
                               ssd_a_log, ssd_dt_bias, ssd_d, ssd_norm_g)
    y_d, _, _ = ssd_bidir(pz, pxbc, pdt, s_f, s_b, True, ssd_conv_w, ssd_conv_b,
                          ssd_a_log, ssd_dt_bias, ssd_d, ssd_norm_g)
    y_a = conformer_conv(pa, conf_dw_w, conf_dw_b, conf_norm_g, conf_norm_b)
    y_c = hyena(py, hy_short_w, hy_short_b, hy_w1, hy_b1, hy_w2, hy_b2, hy_w3, hy_decay, hy_bias)
    y = _proj_out(jnp.concatenate([y_a, y_b, y_c, y_d], -1), w_out)
    if not ctx_out:
        return y, None
    y_ac = conformer_conv(ca, conf_dw_w, conf_dw_b, conf_norm_g, conf_norm_b)
    y_bc = _ctx_attention(cb)
    y_cc = hyena(cy, hy_short_w, hy_short_b, hy_w1, hy_b1, hy_w2, hy_b2, hy_w3, hy_decay, hy_bias)
    yc = _proj_out(jnp.concatenate([y_ac, y_bc, y_cc, y_dc], -1), w_out)
    return y, yc


_NEG = -1e30
_ROUTE_TT = 256
_DENSE_TT = 512
_DENSE_EC = 1024
_NT = (((1,), (1,)), ((), ()))


def _top16_rows(s):
    rows = []
    stacked = jnp.zeros((PEER_TOPK, s.shape[1]), jnp.float32)
    ridx = lax.broadcasted_iota(jnp.int32, stacked.shape, 0)
    cur = s
    for k in range(PEER_TOPK):
        m = jnp.max(cur, axis=0, keepdims=True)
        rows.append(m)
        stacked = jnp.where(ridx == k, m, stacked)
        if k + 1 < PEER_TOPK:
            cur = jnp.where(cur >= m, _NEG, cur)
    return rows, stacked


def _pair_candidates(r1, V1, r2, V2, op, fill):
    half = V1[0:8]
    keep = lax.broadcasted_iota(jnp.int32, half.shape, 0) >= 4
    pieces = [op(r1[0], V2), op(r1[1], V2[0:8]), op(r1[2], V2[0:8]), op(r1[3], V2[0:8]), op(r2[0], V1[8:16])]
    for b in range(3):
        pieces.append(jnp.where(keep, op(r2[b], half), fill))
    return jnp.concatenate(pieces, axis=0)


def _peer_route_body(x_ref, sc_ref, sh_ref, wqT_ref, keys_ref, hm_ref, e1_ref, e2_ref, pthr_ref, qT_ref):
    hm = (x_ref[0] * (1.0 + sc_ref[0]) + sh_ref[0]).astype(jnp.bfloat16)
    hm_ref[0] = hm
    qT_ref[...] = lax.dot_general(wqT_ref[...], hm, _NT, preferred_element_type=jnp.float32)

    def head(h, carry):
        base = pl.multiple_of(h * PEER_QDIM, PEER_QDIM)
        half_q = PEER_QDIM // 2
        s_both = []
        for p in range(2):
            qb = qT_ref[pl.ds(base + p * half_q, half_q), :].astype(jnp.bfloat16)
            s_both.append(jnp.dot(keys_ref[h, p], qb, preferred_element_type=jnp.float32))
        for half in range(_ROUTE_TT // 128):
            lanes = slice(half * 128, (half + 1) * 128)
            s1 = s_both[0][:, lanes]
            s2 = s_both[1][:, lanes]
            r1, V1 = _top16_rows(s1)
            r2, V2 = _top16_rows(s2)
            cand = _pair_candidates(r1, V1, r2, V2, lambda a, b: a + b, _NEG)
            cur = cand
            top = r1[0] + r2[0]
            z = jnp.ones_like(top)
            m = top
            for k in range(1, PEER_TOPK):
                cur = jnp.where(cur >= m, _NEG, cur)
                m = jnp.max(cur, axis=0, keepdims=True)
                z = z + jnp.exp(m - top)
            rz = 1.0 / z
            e1r = [jnp.exp(r - r1[0]) * rz for r in r1]
            e2r = [jnp.exp(r - r2[0]) for r in r2]
            E1 = jnp.exp(V1 - r1[0]) * rz
            E2 = jnp.exp(V2 - r2[0])
            prod = _pair_candidates(e1r, E1, e2r, E2, lambda a, b: a * b, 0.0)
            pthr = jnp.min(jnp.where(cand >= m, prod, 1e30), axis=0, keepdims=True)
            e1_ref[h, :, lanes] = jnp.exp(s1 - r1[0]) * rz
            e2_ref[h, :, lanes] = jnp.exp(s2 - r2[0])
            pthr_ref[h, :, lanes] = pthr
        return carry

    lax.fori_loop(0, PEER_HEADS, head, 0)


def _peer_route(x, sc, sh, wqT, keys):
    Bsz, S, D = x.shape
    nt = S // _ROUTE_TT
    T = Bsz * S
    tab = jax.ShapeDtypeStruct((PEER_HEADS, PEER_KEYS, T), jnp.float32)
    return pl.pallas_call(
        _peer_route_body,
        grid=(Bsz, nt),
        in_specs=[pl.BlockSpec((1, _ROUTE_TT, D), lambda b, i: (b, i, 0)),
                  pl.BlockSpec((1, 1, D), lambda b, i: (b, 0, 0)),
                  pl.BlockSpec((1, 1, D), lambda b, i: (b, 0, 0)),
                  pl.BlockSpec(wqT.shape, lambda b, i: (0, 0)),
                  pl.BlockSpec(keys.shape, lambda b, i: (0, 0, 0, 0))],
        out_specs=[pl.BlockSpec((1, _ROUTE_TT, D), lambda b, i: (b, i, 0)),
                   pl.BlockSpec((PEER_HEADS, PEER_KEYS, _ROUTE_TT), lambda b, i: (0, 0, b * nt + i)),
                   pl.BlockSpec((PEER_HEADS, PEER_KEYS, _ROUTE_TT), lambda b, i: (0, 0, b * nt + i)),
                   pl.BlockSpec((PEER_HEADS, 1, _ROUTE_TT), lambda b, i: (0, 0, b * nt + i))],
        out_shape=[jax.ShapeDtypeStruct((Bsz, S, D), jnp.bfloat16), tab, tab,
                   jax.ShapeDtypeStruct((PEER_HEADS, 1, T), jnp.float32)],
        scratch_shapes=[pltpu.VMEM((PEER_HEADS * PEER_QDIM, _ROUTE_TT), jnp.float32)],
        compiler_params=pltpu.CompilerParams(dimension_semantics=("arbitrary", "arbitrary"),
                                             vmem_limit_bytes=_VMEM_LIMIT),
        name="peer_route",
    )(x, sc, sh, wqT, keys)


_GELU_K = math.sqrt(2.0 / math.pi)


def _gated_gelu(gate, x):
    half = (0.5 * x) * gate
    return half + half * jnp.tanh(x * (_GELU_K + (_GELU_K * 0.044715) * (x * x)))


def _peer_dense_body(alpha, hm_ref, e1_ref, e2_ref, pthr_ref, u_ref, vt_ref, x_ref, g_ref, lng_ref, lnb_ref,
                     o_ref, acc_ref, wt_ref, e1b_ref, pthrb_ref):
    c = pl.program_id(2)
    sub = 8
    jrows = 32

    @pl.when(c == 0)
    def _():
        acc_ref[...] = jnp.zeros_like(acc_ref)
        for h in range(PEER_HEADS):
            pthrb_ref[h] = jnp.broadcast_to(pthr_ref[h], (sub, _DENSE_TT))

    for ii in range(_DENSE_EC // PEER_KEYS):
        actT = lax.dot_general(u_ref[ii * PEER_KEYS:(ii + 1) * PEER_KEYS, :], hm_ref[0], _NT,
                               preferred_element_type=jnp.float32)
        for h in range(PEER_HEADS):
            e1b_ref[ii % 2, h] = jnp.broadcast_to(e1_ref[h, ii:ii + 1, :], (sub, _DENSE_TT))
        for jb in range(PEER_KEYS // jrows):
            gate = jnp.zeros((jrows // sub, sub, _DENSE_TT), jnp.float32)
            for h in range(PEER_HEADS):
                e2 = e2_ref[h, jb * jrows:(jb + 1) * jrows, :].reshape(jrows // sub, sub, _DENSE_TT)
                val = e2 * e1b_ref[ii % 2, h][None]
                gate = gate + jnp.where(val >= pthrb_ref[h][None], val, 0.0)
            r0 = ii * PEER_KEYS + jb * jrows
            w = _gated_gelu(gate.reshape(jrows, _DENSE_TT), actT[jb * jrows:(jb + 1) * jrows, :])
            wt_ref[r0:r0 + jrows, :] = w.astype(jnp.bfloat16)
    acc_ref[...] += jnp.dot(vt_ref[...], wt_ref[...], preferred_element_type=jnp.float32)

    @pl.when(c == pl.num_programs(2) - 1)
    def _():
        y = alpha * x_ref[0] + g_ref[0] * acc_ref[...].T
        mu = jnp.mean(y, -1, keepdims=True)
        yc = y - mu
        var = jnp.mean(yc * yc, -1, keepdims=True)
        o_ref[0] = yc * lax.rsqrt(var + LN_EPS) * lng_ref[...] + lnb_ref[...]


def _peer_dense(hm, e1, e2, pthr, u_bf, vt_bf, x, g, ln_g, ln_b, alpha):
    Bsz, S, D = x.shape
    nt = S // _DENSE_TT
    nchunk = N_EXPERTS // _DENSE_EC
    rows_i = _DENSE_EC // PEER_KEYS
    return pl.pallas_call(
        functools.partial(_peer_dense_body, alpha),
        grid=(Bsz, nt, nchunk),
        in_specs=[pl.BlockSpec((1, _DENSE_TT, D), lambda b, i, c: (b, i, 0)),
                  pl.BlockSpec((PEER_HEADS, rows_i, _DENSE_TT), lambda b, i, c: (0, c, b * nt + i)),
                  pl.BlockSpec((PEER_HEADS, PEER_KEYS, _DENSE_TT), lambda b, i, c: (0, 0, b * nt + i)),
                  pl.BlockSpec((PEER_HEADS, 1, _DENSE_TT), lambda b, i, c: (0, 0, b * nt + i)),
                  pl.BlockSpec((_DENSE_EC, D), lambda b, i, c: (c, 0)),
                  pl.BlockSpec((D, _DENSE_EC), lambda b, i, c: (0, c)),
                  pl.BlockSpec((1, _DENSE_TT, D), lambda b, i, c: (b, i, 0)),
                  pl.BlockSpec((1, 1, D), lambda b, i, c: (b, 0, 0)),
                  pl.BlockSpec((1, D), lambda b, i, c: (0, 0)),
                  pl.BlockSpec((1, D), lambda b, i, c: (0, 0))],
        out_specs=pl.BlockSpec((1, _DENSE_TT, D), lambda b, i, c: (b, i, 0)),
        out_shape=jax.ShapeDtypeStruct((Bsz, S, D), jnp.float32),
        scratch_shapes=[pltpu.VMEM((D, _DENSE_TT), jnp.float32),
                        pltpu.VMEM((_DENSE_EC, _DENSE_TT), jnp.bfloat16),
                        pltpu.VMEM((2, PEER_HEADS, 8, _DENSE_TT), jnp.float32),
                        pltpu.VMEM((PEER_HEADS, 8, _DENSE_TT), jnp.float32)],
        compiler_params=pltpu.CompilerParams(dimension_semantics=("arbitrary", "arbitrary", "arbitrary"),
                                             vmem_limit_bytes=_VMEM_LIMIT),
        name="peer_dense",
    )(hm, e1, e2, pthr, u_bf, vt_bf, x, g, ln_g.reshape(1, D), ln_b.reshape(1, D))


def _peer_weights(wq, sub_keys, u_tab, v_tab):
    return (wq.T.astype(jnp.bfloat16), sub_keys.astype(jnp.bfloat16),
            u_tab.astype(jnp.bfloat16), v_tab.T.astype(jnp.bfloat16))


def _peer_ln(x, sc, sh, g, pw, ln_g, ln_b, alpha):
    wqT, keys, u_bf, vt_bf = pw
    hm, e1, e2, pthr = _peer_route(x, sc, sh, wqT, keys)
    return _peer_dense(hm, e1, e2, pthr, u_bf, vt_bf, x, g, ln_g, ln_b, alpha)


def kernel(x, c, ctx, c_ctx, w_ada, b_ada, w_in, w_out, ln1_g, ln1_b, ln2_g, ln2_b,
           conf_dw_w, conf_dw_b, conf_norm_g, conf_norm_b, na_rpb, hy_short_w, hy_short_b,
           hy_w1, hy_b1, hy_w2, hy_b2, hy_w3, hy_decay, hy_bias, ssd_conv_w, ssd_conv_b,
           ssd_a_log, ssd_dt_bias, ssd_d, ssd_norm_g, peer_wq, peer_keys, peer_u, peer_v):
    alpha = (2.0 * DEPTH) ** 0.25
    s_c = jax.nn.silu(c)
    s_cc = jax.nn.silu(c_ctx)
    xc = ctx
    for l in range(DEPTH):
        ctx_out = l < DEPTH - 1
        mod = (s_c @ w_ada[l] + b_ada[l])[:, None, :]
        mod_c = (s_cc @ w_ada[l] + b_ada[l])[None, None, :]
        sh1, sc1, g1, sh2, sc2, g2 = jnp.split(mod, 6, -1)
        sh1c, sc1c, g1c, sh2c, sc2c, g2c = jnp.split(mod_c, 6, -1)
        y, yc = mixer_sublayer(x * (1.0 + sc1) + sh1, xc * (1.0 + sc1c) + sh1c, ctx_out,
                               w_in[l], w_out[l], conf_dw_w[l], conf_dw_b[l], conf_norm_g[l], conf_norm_b[l],
                               na_rpb[l], hy_short_w[l], hy_short_b[l], hy_w1[l], hy_b1[l], hy_w2[l], hy_b2[l],
                               hy_w3[l], hy_decay[l], hy_bias[l], ssd_conv_w[l], ssd_conv_b[l],
                               ssd_a_log[l], ssd_dt_bias[l], ssd_d[l], ssd_norm_g[l])
        pw = _peer_weights(peer_wq[l], peer_keys[l], peer_u[l], peer_v[l])
        x = layer_norm(alpha * x + g1 * y, ln1_g[l], ln1_b[l])
        x = _peer_ln(x, sc2, sh2, g2, pw, ln2_g[l], ln2_b[l], alpha)
        if ctx_out:
            xc = layer_norm(alpha * xc + g1c * yc, ln1_g[l], ln1_b[l])
            Bc, Lc, D = xc.shape
            xc = _peer_ln(xc.reshape(1, Bc * Lc, D), sc2c, sh2c, g2c, pw, ln2_g[l], ln2_b[l], alpha).reshape(Bc, Lc, D)
    return x
```

```python
import functools
import math
import jax, jax.numpy as jnp
from jax import lax
import numpy as np
from jax.experimental import pallas as pl
from jax.experimental.pallas import tpu as pltpu

D_MODEL = 1024
BATCH = 16
SEQ = 2048
DEPTH = 2

GRID_W = 64
CTX_LEN = 256
N_MIXERS = 4
GROUP_W = D_MODEL // N_MIXERS
D_MIX = N_MIXERS * GROUP_W
LN_EPS = 1e-5
CONF_K = 31
CONF_GROUPS = 4
NA_HEADS = 4
HEAD_DIM = GROUP_W // NA_HEADS
NA_KH = 8
NA_KW = 16
ROPE_BASE = 10000.0
HY_SHORT = 3
HY_BANDS = 16
HY_EMB = 1 + 2 * HY_BANDS
HY_HIDDEN = 64
HY_SIN_FREQ = 1.0
SSD_HEADS = 4
SSD_HEAD_DIM = GROUP_W // SSD_HEADS
SSD_GROUPS = 2
SSD_STATE = 64
SSD_CONV = 3
SSD_CHUNK = 128
XBC_W = GROUP_W + 2 * SSD_GROUPS * SSD_STATE
IN_COLS = 2 * GROUP_W + 3 * GROUP_W + 3 * GROUP_W + GROUP_W + XBC_W + 2 * SSD_HEADS
PEER_HEADS = 8
PEER_KEYS = 128
PEER_TOPK = 16
PEER_QDIM = 256
N_EXPERTS = PEER_KEYS * PEER_KEYS
PEER_BLOCK = 128

_VMEM_LIMIT = 56 * 1024 * 1024
_IN_PAD = 2944


def _standardize(x):
    xf = x.astype(jnp.float32)
    mu = jnp.mean(xf, -1, keepdims=True)
    var = jnp.mean(jnp.square(xf - mu), -1, keepdims=True)
    return (xf - mu) * lax.rsqrt(var + LN_EPS)


def layer_norm(x, g, b):
    return (_standardize(x) * g + b).astype(x.dtype)


def dwconv(x, w):
    K, C = w.shape
    pad = (K - 1) // 2
    return lax.conv_general_dilated(x, w[:, None, :].astype(x.dtype), window_strides=(1,),
                                    padding=[(pad, pad)], dimension_numbers=('NWC', 'WIO', 'NWC'),
                                    feature_group_count=C)


def _split_cols(p):
    sizes = (2 * GROUP_W, 3 * GROUP_W, 3 * GROUP_W, GROUP_W, XBC_W, 2 * SSD_HEADS)
    points = [int(v) for v in np.cumsum(sizes)[:-1]]
    return jnp.split(p, points, axis=-1)


_IN_SPLITS = ((0, 2 * GROUP_W), (2 * GROUP_W, 5 * GROUP_W), (5 * GROUP_W, 8 * GROUP_W),
              (8 * GROUP_W, 9 * GROUP_W + XBC_W), (9 * GROUP_W + XBC_W, _IN_PAD))
_PROJ_TM = 512


def _in_proj_body(x_ref, sc_ref, sh_ref, w_ref, *o_refs):
    h = (x_ref[0] * (1.0 + sc_ref[0]) + sh_ref[0]).astype(jnp.bfloat16)
    for (lo, hi), o_ref in zip(_IN_SPLITS, o_refs):
        o_ref[0] = jnp.dot(h, w_ref[:, lo:hi], preferred_element_type=jnp.float32)


def _in_proj(x, sc, sh, w_in):
    Bsz, L, D = x.shape
    tm = min(_PROJ_TM, L)
    w = jnp.pad(w_in, ((0, 0), (0, _IN_PAD - IN_COLS))).astype(jnp.bfloat16)
    return pl.pallas_call(
        _in_proj_body,
        grid=(Bsz, L // tm),
        in_specs=[pl.BlockSpec((1, tm, D), lambda b, i: (b, i, 0)),
                  pl.BlockSpec((1, 1, D), lambda b, i: (b, 0, 0)),
                  pl.BlockSpec((1, 1, D), lambda b, i: (b, 0, 0)),
                  pl.BlockSpec((D, _IN_PAD), lambda b, i: (0, 0))],
        out_specs=[pl.BlockSpec((1, tm, hi - lo), lambda b, i: (b, i, 0)) for lo, hi in _IN_SPLITS],
        out_shape=[jax.ShapeDtypeStruct((Bsz, L, hi - lo), jnp.float32) for lo, hi in _IN_SPLITS],
        compiler_params=pltpu.CompilerParams(dimension_semantics=("arbitrary", "arbitrary"),
                                             vmem_limit_bytes=_VMEM_LIMIT),
        name="in_proj",
    )(x, sc, sh, w)


def _out_proj_ln_body(alpha, ya_ref, yb_ref, yc_ref, yd_ref, w_ref, x_ref, g_ref, lng_ref, lnb_ref, o_ref):
    y = 0.0
    for m, y_ref in enumerate((ya_ref, yb_ref, yc_ref, yd_ref)):
        y = y + jnp.dot(y_ref[0].astype(jnp.bfloat16), w_ref[m * GROUP_W:(m + 1) * GROUP_W, :],
                        preferred_element_type=jnp.float32)
    r = alpha * x_ref[0] + g_ref[0] * y
    mu = jnp.mean(r, -1, keepdims=True)
    rc = r - mu
    var = jnp.mean(rc * rc, -1, keepdims=True)
    o_ref[0] = rc * lax.rsqrt(var + LN_EPS) * lng_ref[...] + lnb_ref[...]


def _out_proj_ln(ys, w_out, x, g, ln_g, ln_b, alpha):
    Bsz, L, D = x.shape
    tm = min(_PROJ_TM, L)
    mix = pl.BlockSpec((1, tm, GROUP_W), lambda b, i: (b, i, 0))
    return pl.pallas_call(
        functools.partial(_out_proj_ln_body, alpha),
        grid=(Bsz, L // tm),
        in_specs=[mix, mix, mix, mix,
                  pl.BlockSpec((D_MIX, D), lambda b, i: (0, 0)),
                  pl.BlockSpec((1, tm, D), lambda b, i: (b, i, 0)),
                  pl.BlockSpec((1, 1, D), lambda b, i: (b, 0, 0)),
                  pl.BlockSpec((1, D), lambda b, i: (0, 0)),
                  pl.BlockSpec((1, D), lambda b, i: (0, 0))],
        out_specs=pl.BlockSpec((1, tm, D), lambda b, i: (b, i, 0)),
        out_shape=jax.ShapeDtypeStruct((Bsz, L, D), jnp.float32),
        compiler_params=pltpu.CompilerParams(dimension_semantics=("arbitrary", "arbitrary"),
                                             vmem_limit_bytes=_VMEM_LIMIT),
        name="out_proj_ln",
    )(*ys, w_out.astype(jnp.bfloat16), x, g, ln_g.reshape(1, D), ln_b.reshape(1, D))


def _shifted_taps(win, n_taps, rows):
    return [win[k:k + rows] for k in range(n_taps)]


_CONV_HALO = 16
_CONV_TILE = 256


def _group_mean_matrix(width, group):
    r = lax.broadcasted_iota(jnp.int32, (width, width), 0) // group
    c = lax.broadcasted_iota(jnp.int32, (width, width), 1) // group
    return jnp.where(r == c, 1.0 / group, 0.0).astype(jnp.float32)


def _conformer_body(L, p_ref, w_ref, b_ref, ng_ref, nb_ref, o_ref, u_ref):
    G = GROUP_W
    pad = (CONF_K - 1) // 2
    halo = jnp.zeros((_CONV_HALO, G), jnp.float32)
    u_ref[0:_CONV_HALO, :] = halo
    u_ref[_CONV_HALO + L:_CONV_HALO + L + _CONV_HALO, :] = halo
    u_ref[_CONV_HALO:_CONV_HALO + L, :] = p_ref[0, :, 0:G] * jax.nn.sigmoid(p_ref[0, :, G:2 * G])
    avg = _group_mean_matrix(G, G // CONF_GROUPS)
    tile = min(_CONV_TILE, L)
    for t in range(L // tile):
        win = u_ref[t * tile:t * tile + tile + 2 * _CONV_HALO, :]
        acc = jnp.zeros((tile, G), jnp.float32) + b_ref[...]
        for k, tap in enumerate(_shifted_taps(win[_CONV_HALO - pad:], CONF_K, tile)):
            acc = acc + tap * w_ref[k:k + 1, :]
        mu = jnp.dot(acc, avg, preferred_element_type=jnp.float32, precision=lax.Precision.HIGHEST)
        cen = acc - mu
        var = jnp.dot(cen * cen, avg, preferred_element_type=jnp.float32, precision=lax.Precision.HIGHEST)
        un = cen * lax.rsqrt(var + LN_EPS) * ng_ref[...] + nb_ref[...]
        o_ref[0, t * tile:(t + 1) * tile, :] = un * jax.nn.sigmoid(un)


def _conformer(pa, dw_w, dw_b, n_g, n_b):
    Bsz, L, _ = pa.shape
    G = GROUP_W
    vec = pl.BlockSpec((1, G), lambda b: (0, 0))
    return pl.pallas_call(
        functools.partial(_conformer_body, L),
        grid=(Bsz,),
        in_specs=[pl.BlockSpec((1, L, 2 * G), lambda b: (b, 0, 0)),
                  pl.BlockSpec((CONF_K, G), lambda b: (0, 0)), vec, vec, vec],
        out_specs=pl.BlockSpec((1, L, G), lambda b: (b, 0, 0)),
        out_shape=jax.ShapeDtypeStruct((Bsz, L, G), jnp.float32),
        scratch_shapes=[pltpu.VMEM((L + 2 * _CONV_HALO, G), jnp.float32)],
        compiler_params=pltpu.CompilerParams(dimension_semantics=("arbitrary",),
                                             vmem_limit_bytes=_VMEM_LIMIT),
        name="conformer",
    )(pa, dw_w, dw_b.reshape(1, G), n_g.reshape(1, G), n_b.reshape(1, G))


def conformer_conv(p, dw_w, dw_b, n_g, n_b):
    a, gate = jnp.split(p, 2, -1)
    u = a * jax.nn.sigmoid(gate)
    u = dwconv(u, dw_w) + dw_b
    Bsz, L, C = u.shape
    un = _standardize(u.reshape(Bsz, L, CONF_GROUPS, C // CONF_GROUPS)).reshape(Bsz, L, C)
    un = (un * n_g + n_b).astype(u.dtype)
    return jax.nn.silu(un)


def axial_rope(rows, head_dim):
    n_f = head_dim // 4
    inv = ROPE_BASE ** (-jnp.arange(n_f, dtype=jnp.float32) / n_f)
    t = jnp.arange(rows * GRID_W)
    r = (t // GRID_W).astype(jnp.float32)
    col = (t % GRID_W).astype(jnp.float32)
    ang = jnp.concatenate([r[:, None] * inv, col[:, None] * inv], -1)
    return jnp.cos(ang), jnp.sin(ang)


def apply_rope(x, cos, sin):
    x1, x2 = jnp.split(x.astype(jnp.float32), 2, -1)
    c = cos[None, :, None, :]
    s = sin[None, :, None, :]
    return jnp.concatenate([x1 * c - x2 * s, x1 * s + x2 * c], -1).astype(x.dtype)


def neighborhood_attention(q_rot, k_rot, v, q_plain, k_ctx, v_ctx, rpb):
    Bsz, S, H, d = q_rot.shape
    rows = S // GRID_W
    kh = min(NA_KH, rows)
    qg = q_rot.reshape(Bsz, rows, GRID_W, H, d)
    kg = k_rot.reshape(Bsz, rows, GRID_W, H, d)
    vg = v.reshape(Bsz, rows, GRID_W, H, d)
    qpg = q_plain.reshape(Bsz, rows, GRID_W, H, d)
    cq = jnp.arange(GRID_W)
    col_idx = jnp.clip(cq - NA_KW // 2, 0, GRID_W - NA_KW)[:, None] + jnp.arange(NA_KW)[None, :]
    col_bias_idx = col_idx - cq[:, None] + (NA_KW - 1)
    scale = d ** -0.5

    def row_block(r):
        rs = jnp.clip(r - kh // 2, 0, rows - kh)
        kr = lax.dynamic_slice_in_dim(kg, rs, kh, axis=1)[:, :, col_idx]
        vr = lax.dynamic_slice_in_dim(vg, rs, kh, axis=1)[:, :, col_idx]
        qr = lax.dynamic_index_in_dim(qg, r, axis=1, keepdims=False)
        qpr = lax.dynamic_index_in_dim(qpg, r, axis=1, keepdims=False)
        row_bias_idx = rs + jnp.arange(kh) - r + (NA_KH - 1)
        bias = rpb[:, row_bias_idx][:, :, col_bias_idx].transpose(0, 2, 1, 3)
        s_loc = jnp.einsum('bchd,bicjhd->bhcij', qr, kr).astype(jnp.float32) * scale + bias
        s_ctx = jnp.einsum('bchd,bnhd->bhcn', qpr, k_ctx).astype(jnp.float32) * scale
        logits = jnp.concatenate([s_loc.reshape(Bsz, H, GRID_W, kh * NA_KW), s_ctx], -1)
        p = jax.nn.softmax(logits, -1).astype(v.dtype)
        p_loc = p[..., :kh * NA_KW].reshape(Bsz, H, GRID_W, kh, NA_KW)
        p_ctx = p[..., kh * NA_KW:]
        return (jnp.einsum('bhcij,bicjhd->bchd', p_loc, vr)
                + jnp.einsum('bhcn,bnhd->bchd', p_ctx, v_ctx))

    out = lax.map(row_block, jnp.arange(rows))
    return out.transpose(1, 0, 2, 3, 4).reshape(Bsz, S, H * d)


_NA_MASK = -1e30


def _na_tables(rows, rpb):
    cos, sin = axial_rope(rows, HEAD_DIM)
    cos_f = jnp.tile(cos, (1, 2 * NA_HEADS))
    sin_s = jnp.tile(jnp.concatenate([-sin, sin], -1), (1, NA_HEADS))
    cq = jnp.arange(GRID_W)
    cs = jnp.clip(cq - NA_KW // 2, 0, GRID_W - NA_KW)
    col = jnp.arange(GRID_W)
    in_band = (col[None, :] >= cs[:, None]) & (col[None, :] < cs[:, None] + NA_KW)
    cb_idx = jnp.clip(col[None, :] - cq[:, None] + (NA_KW - 1), 0, 2 * NA_KW - 2)
    po = jnp.arange(NA_KH)
    rb_idx = jnp.arange(NA_KH)[None, :] - po[:, None] + (NA_KH - 1)
    bias = rpb[:, rb_idx][:, :, :, cb_idx]
    bias = jnp.where(in_band[None, None, None], bias, _NA_MASK)
    bias = bias.transpose(1, 0, 3, 2, 4).reshape(NA_KH, NA_HEADS, GRID_W, NA_KH * GRID_W)
    return cos_f, sin_s, bias


def _rope_lanes(x, cos_f, sin_s):
    n = x.shape[-1]
    hd = HEAD_DIM // 2
    first = (lax.broadcasted_iota(jnp.int32, x.shape, 1) % HEAD_DIM) < hd
    partner = jnp.where(first, pltpu.roll(x, n - hd, 1), pltpu.roll(x, hd, 1))
    return x * cos_f + partner * sin_s


def _softmax_pv(s_parts, v_parts):
    m = s_parts[0].max(-1, keepdims=True)
    for s in s_parts[1:]:
        m = jnp.maximum(m, s.max(-1, keepdims=True))
    l = 0.0
    o = 0.0
    for s, v in zip(s_parts, v_parts):
        p = jnp.exp(s - m)
        l = l + p.sum(-1, keepdims=True)
        o = o + jnp.dot(p.astype(jnp.bfloat16), v, preferred_element_type=jnp.float32)
    return o / l


def _na_body(rows, q_ref, k_ref, v_ref, kc_ref, vc_ref, cosq_ref, sinq_ref, cosk_ref, sink_ref, bias_ref,
             o_ref, krot_ref, vbf_ref):
    r = pl.program_id(1)
    scale = HEAD_DIM ** -0.5

    @pl.when(r == 0)
    def _():
        krot_ref[...] = _rope_lanes(k_ref[0], cosk_ref[...], sink_ref[...]).astype(jnp.bfloat16)
        vbf_ref[...] = v_ref[0].astype(jnp.bfloat16)

    rs = jnp.clip(r - NA_KH // 2, 0, rows - NA_KH)
    start = pl.multiple_of(rs * GRID_W, GRID_W)
    win = NA_KH * GRID_W
    q = q_ref[0] * scale
    q_rot = _rope_lanes(q, cosq_ref[...], sinq_ref[...]).astype(jnp.bfloat16)
    q_plain = q.astype(jnp.bfloat16)
    kw = krot_ref[pl.ds(start, win), :]
    vw = vbf_ref[pl.ds(start, win), :]
    kc = kc_ref[0].astype(jnp.bfloat16)
    vc = vc_ref[0].astype(jnp.bfloat16)
    outs = []
    for h in range(NA_HEADS):
        hs = slice(h * HEAD_DIM, (h + 1) * HEAD_DIM)
        s_loc = lax.dot_general(q_rot[:, hs], kw[:, hs], _NT, preferred_element_type=jnp.float32) + bias_ref[0, h]
        s_ctx = lax.dot_general(q_plain[:, hs], kc[:, hs], _NT, preferred_element_type=jnp.float32)
        outs.append(_softmax_pv([s_loc, s_ctx], [vw[:, hs], vc[:, hs]]))
    o_ref[0] = jnp.concatenate(outs, axis=-1)


def _na_attention(pb, cb, rpb):
    Bsz, S, _ = pb.shape
    Lc = cb.shape[1]
    rows = S // GRID_W
    G = GROUP_W
    cos_f, sin_s, bias = _na_tables(rows, rpb)

    def bias_idx(b, r):
        rs = jnp.clip(r - NA_KH // 2, 0, rows - NA_KH)
        return (r - rs, 0, 0, 0)

    return pl.pallas_call(
        functools.partial(_na_body, rows),
        grid=(Bsz, rows),
        in_specs=[pl.BlockSpec((1, GRID_W, G), lambda b, r: (b, r, 0)),
                  pl.BlockSpec((1, S, G), lambda b, r: (b, 0, 1)),
                  pl.BlockSpec((1, S, G), lambda b, r: (b, 0, 2)),
                  pl.BlockSpec((1, Lc, G), lambda b, r: (b, 0, 1)),
                  pl.BlockSpec((1, Lc, G), lambda b, r: (b, 0, 2)),
                  pl.BlockSpec((GRID_W, G), lambda b, r: (r, 0)),
                  pl.BlockSpec((GRID_W, G), lambda b, r: (r, 0)),
                  pl.BlockSpec((S, G), lambda b, r: (0, 0)),
                  pl.BlockSpec((S, G), lambda b, r: (0, 0)),
                  pl.BlockSpec((1, NA_HEADS, GRID_W, NA_KH * GRID_W), bias_idx)],
        out_specs=pl.BlockSpec((1, GRID_W, G), lambda b, r: (b, r, 0)),
        out_shape=jax.ShapeDtypeStruct((Bsz, S, G), jnp.float32),
        scratch_shapes=[pltpu.VMEM((S, G), jnp.bfloat16), pltpu.VMEM((S, G), jnp.bfloat16)],
        compiler_params=pltpu.CompilerParams(dimension_semantics=("arbitrary", "arbitrary"),
                                             vmem_limit_bytes=_VMEM_LIMIT),
        name="na_attention",
    )(pb, pb, pb, cb, cb, cos_f, sin_s, cos_f, sin_s, bias)


def _ctx_attn_body(q_ref, k_ref, v_ref, o_ref):
    q = (q_ref[0] * HEAD_DIM ** -0.5).astype(jnp.bfloat16)
    k = k_ref[0].astype(jnp.bfloat16)
    v = v_ref[0].astype(jnp.bfloat16)
    outs = []
    for h in range(NA_HEADS):
        hs = slice(h * HEAD_DIM, (h + 1) * HEAD_DIM)
        s = lax.dot_general(q[:, hs], k[:, hs], _NT, preferred_element_type=jnp.float32)
        outs.append(_softmax_pv([s], [v[:, hs]]))
    o_ref[0] = jnp.concatenate(outs, axis=-1)


def _ctx_attention(cb):
    Bsz, Lc, _ = cb.shape
    G = GROUP_W
    return pl.pallas_call(
        _ctx_attn_body,
        grid=(Bsz,),
        in_specs=[pl.BlockSpec((1, Lc, G), lambda b: (b, 0, 0)),
                  pl.BlockSpec((1, Lc, G), lambda b: (b, 0, 1)),
                  pl.BlockSpec((1, Lc, G), lambda b: (b, 0, 2))],
        out_specs=pl.BlockSpec((1, Lc, G), lambda b: (b, 0, 0)),
        out_shape=jax.ShapeDtypeStruct((Bsz, Lc, G), jnp.float32),
        compiler_params=pltpu.CompilerParams(dimension_semantics=("arbitrary",)),
        name="ctx_attention",
    )(cb, cb, cb)


def context_attention(q, k, v):
    s = jnp.einsum('bqhd,bkhd->bhqk', q, k).astype(jnp.float32) * (q.shape[-1] ** -0.5)
    p = jax.nn.softmax(s, -1).astype(v.dtype)
    return jnp.einsum('bhqk,bkhd->bqhd', p, v)


def hyena_filters(L, w1, b1, w2, b2, w3, decay):
    tn = jnp.arange(L, dtype=jnp.float32)[:, None] / L
    bands = jnp.arange(1, HY_BANDS + 1, dtype=jnp.float32)[None, :]
    ang = 2.0 * math.pi * bands * tn
    z = jnp.concatenate([tn, jnp.sin(ang), jnp.cos(ang)], -1)
    hmid = jnp.sin(HY_SIN_FREQ * (z @ w1.astype(jnp.float32) + b1.astype(jnp.float32)))
    hmid = jnp.sin(HY_SIN_FREQ * (hmid @ w2.astype(jnp.float32) + b2.astype(jnp.float32)))
    k = (hmid @ w3.astype(jnp.float32)) * jnp.exp(-tn * decay.astype(jnp.float32))
    k = k / (jnp.sum(jnp.abs(k), axis=0, keepdims=True) + 1e-6)
    return k[:, :GROUP_W], k[:, GROUP_W:]


def bidir_fftconv(u, k_fwd, k_bwd):
    L = u.shape[1]
    k2 = jnp.concatenate([k_fwd, jnp.zeros_like(k_fwd[:1]), k_bwd[1:][::-1]], 0)
    kf = jnp.fft.rfft(k2, n=2 * L, axis=0)
    uf = jnp.fft.rfft(u.astype(jnp.float32), n=2 * L, axis=1)
    return jnp.fft.irfft(uf * kf[None], n=2 * L, axis=1)[:, :L].astype(u.dtype)


def hyena(p, short_w, short_b, w1, b1, w2, b2, w3, decay, skip):
    L = p.shape[1]
    p = dwconv(p, short_w) + short_b
    x0, x1, v = jnp.split(p, 3, -1)
    k_fwd, k_bwd = hyena_filters(L, w1, b1, w2, b2, w3, decay)
    u = v * x1
    y = bidir_fftconv(u, k_fwd, k_bwd) + u * skip
    return y * x0


def segsum(a):
    T = a.shape[-1]
    a_rep = jnp.broadcast_to(a[..., None], a.shape + (T,))
    a_rep = jnp.where(jnp.tril(jnp.ones((T, T), bool), -1), a_rep, 0.0)
    s = jnp.cumsum(a_rep, axis=-2)
    return jnp.where(jnp.tril(jnp.ones((T, T), bool)), s, -jnp.inf)


def ssd_scan(x, dt, A, Bh, Ch, init_state, want_y):
    Bsz, L, H, P = x.shape
    N = Bh.shape[-1]
    nc = L // SSD_CHUNK
    xd = (x.astype(jnp.float32) * dt[..., None]).reshape(Bsz, nc, SSD_CHUNK, H, P)
    a = (dt * A).reshape(Bsz, nc, SSD_CHUNK, H).transpose(0, 3, 1, 2)
    Bc = Bh.astype(jnp.float32).reshape(Bsz, nc, SSD_CHUNK, H, N)
    Cc = Ch.astype(jnp.float32).reshape(Bsz, nc, SSD_CHUNK, H, N)
    a_cum = jnp.cumsum(a, -1)
    decay_states = jnp.exp(a_cum[..., -1:] - a_cum)
    states = jnp.einsum('bclhn,bhcl,bclhp->bchpn', Bc, decay_states, xd)
    states = jnp.concatenate([init_state[:, None], states], 1)
    decay_chunk = jnp.exp(segsum(jnp.pad(a_cum[..., -1], ((0, 0), (0, 0), (1, 0)))))
    new_states = jnp.einsum('bhzc,bchpn->bzhpn', decay_chunk, states)
    prev_states, final = new_states[:, :-1], new_states[:, -1]
    if not want_y:
        return None, final
    Lmat = jnp.exp(segsum(a))
    y_diag = jnp.einsum('bclhn,bcshn,bhcls,bcshp->bclhp', Cc, Bc, Lmat, xd)
    y_off = jnp.einsum('bclhn,bchpn,bhcl->bclhp', Cc, prev_states, jnp.exp(a_cum))
    return (y_diag + y_off).reshape(Bsz, L, H, P), final


def _flip(t):
    return jnp.flip(t, axis=1)


def ssd_bidir(z, xbc, dt_raw, init_f, init_b, want_y, conv_w, conv_b, a_log, dt_bias, d_skip, norm_g):
    Bsz, L, _ = xbc.shape
    xbc = jax.nn.silu(dwconv(xbc, conv_w) + conv_b)
    xs = xbc[..., :GROUP_W].reshape(Bsz, L, SSD_HEADS, SSD_HEAD_DIM)
    bc = xbc[..., GROUP_W:].reshape(Bsz, L, 2, SSD_GROUPS, SSD_STATE)
    rep = SSD_HEADS // SSD_GROUPS
    Bh = jnp.repeat(bc[:, :, 0], rep, axis=2)
    Ch = jnp.repeat(bc[:, :, 1], rep, axis=2)
    dt = jax.nn.softplus(dt_raw.astype(jnp.float32).reshape(Bsz, L, 2, SSD_HEADS) + dt_bias.astype(jnp.float32))
    A = -jnp.exp(a_log.astype(jnp.float32))
    y_f, s_f = ssd_scan(xs, dt[:, :, 0], A[0], Bh, Ch, init_f, want_y)
    y_b, s_b = ssd_scan(_flip(xs), _flip(dt[:, :, 1]), A[1], _flip(Bh), _flip(Ch), init_b, want_y)
    if not want_y:
        return None, s_f, s_b
    y = y_f + _flip(y_b) + xs.astype(jnp.float32) * d_skip.astype(jnp.float32)[:, None]
    yg = (y.reshape(Bsz, L, GROUP_W) * jax.nn.silu(z.astype(jnp.float32)))
    yg = yg.reshape(Bsz, L, SSD_GROUPS, GROUP_W // SSD_GROUPS)
    yg = yg * lax.rsqrt(jnp.mean(jnp.square(yg), -1, keepdims=True) + LN_EPS)
    return (yg.reshape(Bsz, L, GROUP_W) * norm_g).astype(z.dtype), s_f, s_b


_HY_TB = 256
_HY_CB = 8
_LANES = 128
_SUBLANES = 8


def _hyena_filter_body(L, w1_ref, b1_ref, w2_ref, b2_ref, w3_ref, dec_ref, o_ref):
    G = GROUP_W
    hp = lax.Precision.HIGHEST
    p = lax.broadcasted_iota(jnp.int32, (2 * L, _LANES), 0)
    lane = lax.broadcasted_iota(jnp.int32, (2 * L, _LANES), 1)
    tn = jnp.abs(p - L).astype(jnp.float32) / L
    band = ((lane - 1) % HY_BANDS + 1).astype(jnp.float32)
    ang = 2.0 * math.pi * band * tn
    z = jnp.where(lane == 0, tn, jnp.where(lane <= HY_BANDS, jnp.sin(ang), jnp.cos(ang)))
    z = jnp.where(lane < HY_EMB, z, 0.0)
    h = jnp.sin(HY_SIN_FREQ * (jnp.dot(z, w1_ref[...], precision=hp, preferred_element_type=jnp.float32) + b1_ref[...]))
    h = jnp.sin(HY_SIN_FREQ * (jnp.dot(h, w2_ref[...], precision=hp, preferred_element_type=jnp.float32) + b2_ref[...]))
    k = jnp.dot(h, w3_ref[...], precision=hp, preferred_element_type=jnp.float32) * jnp.exp(-tn[:, 0:1] * dec_ref[...])
    kf, kb = k[:, 0:G], k[:, G:2 * G]
    n = lax.broadcasted_iota(jnp.int32, (2 * L, G), 0) - L
    nf = jnp.sum(jnp.where(n >= 0, jnp.abs(kf), 0.0), axis=0, keepdims=True) + 1e-6
    nb = jnp.sum(jnp.where(n <= 0, jnp.where(n > -L, jnp.abs(kb), 0.0), 0.0), axis=0, keepdims=True) + 1e-6
    rev = jnp.where(n > 0, kb / nb, jnp.where(n > -L, kf / nf, 0.0))
    o_ref[...] = rev.T


def _hyena_filters_rev(L, w1, b1, w2, b2, w3, decay):
    G = GROUP_W
    hp = _LANES - HY_HIDDEN
    w1p = jnp.pad(w1.astype(jnp.float32), ((0, _LANES - HY_EMB), (0, hp)))
    args = (w1p, jnp.pad(b1.reshape(1, -1), ((0, 0), (0, hp))), jnp.pad(w2, ((0, hp), (0, hp))),
            jnp.pad(b2.reshape(1, -1), ((0, 0), (0, hp))), jnp.pad(w3, ((0, hp), (0, 0))), decay.reshape(1, -1))
    return pl.pallas_call(
        functools.partial(_hyena_filter_body, L),
        out_shape=jax.ShapeDtypeStruct((G, 2 * L), jnp.float32),
        compiler_params=pltpu.CompilerParams(vmem_limit_bytes=_VMEM_LIMIT),
        name="hyena_filters",
    )(*[a.astype(jnp.float32) for a in args])


def _hyena_pre_body(L, p_ref, w_ref, b_ref, u_ref, ut_ref, x0_ref, xp_ref):
    G = GROUP_W
    W = 3 * G
    halo = jnp.zeros((_CONV_HALO, W), jnp.float32)
    xp_ref[0:_CONV_HALO, :] = halo
    xp_ref[_CONV_HALO + L:_CONV_HALO + L + _CONV_HALO, :] = halo
    xp_ref[_CONV_HALO:_CONV_HALO + L, :] = p_ref[0]
    pad = (HY_SHORT - 1) // 2
    tile = min(_CONV_TILE, L)
    for t in range(L // tile):
        win = xp_ref[t * tile:t * tile + tile + 2 * _CONV_HALO, :]
        acc = jnp.zeros((tile, W), jnp.float32) + b_ref[...]
        for k, tap in enumerate(_shifted_taps(win[_CONV_HALO - pad:], HY_SHORT, tile)):
            acc = acc + tap * w_ref[k:k + 1, :]
        rows = slice(t * tile, (t + 1) * tile)
        u = acc[:, 2 * G:3 * G] * acc[:, G:2 * G]
        x0_ref[0, rows, :] = acc[:, 0:G]
        u_ref[0, rows, :] = u
        ut_ref[0, :, rows] = u.T


def _hyena_pre(py, short_w, short_b):
    Bsz, L, W = py.shape
    G = GROUP_W
    f32 = jnp.float32
    return pl.pallas_call(
        functools.partial(_hyena_pre_body, L),
        grid=(Bsz,),
        in_specs=[pl.BlockSpec((1, L, W), lambda b: (b, 0, 0)),
                  pl.BlockSpec((HY_SHORT, W), lambda b: (0, 0)),
                  pl.BlockSpec((1, W), lambda b: (0, 0))],
        out_specs=[pl.BlockSpec((1, L, G), lambda b: (b, 0, 0)),
                   pl.BlockSpec((1, G, L), lambda b: (b, 0, 0)),
                   pl.BlockSpec((1, L, G), lambda b: (b, 0, 0))],
        out_shape=[jax.ShapeDtypeStruct((Bsz, L, G), f32), jax.ShapeDtypeStruct((Bsz, G, L), f32),
                   jax.ShapeDtypeStruct((Bsz, L, G), f32)],
        scratch_shapes=[pltpu.VMEM((L + 2 * _CONV_HALO, W), f32)],
        compiler_params=pltpu.CompilerParams(dimension_semantics=("arbitrary",), vmem_limit_bytes=_VMEM_LIMIT),
        name="hyena_pre",
    )(py, short_w, short_b.reshape(1, W))


def _toeplitz_tile(w8):
    TB = _HY_TB
    nq = 2 * TB // _LANES
    per = _LANES // _SUBLANES
    i = lax.broadcasted_iota(jnp.int32, (_SUBLANES, _LANES), 0)
    l = lax.broadcasted_iota(jnp.int32, (_SUBLANES, _LANES), 1)
    rolled = [[pltpu.roll(w8[:, q * _LANES:(q + 1) * _LANES], (_SUBLANES * k) % _LANES, 1, stride=1, stride_axis=0)
               for k in range(per)] for q in range(nq)]
    row_blocks = []
    for rg in range(TB // _SUBLANES):
        k = rg % per
        pieces = []
        for lg in range(TB // _LANES):
            o = TB + _LANES * lg - _SUBLANES * rg
            q, rho = divmod(o, _LANES)
            if rho == 0:
                pieces.append(jnp.where(l - i < 0, rolled[q - 1][k], rolled[q][k]))
            else:
                pieces.append(jnp.where(l + rho - i >= _LANES, rolled[q + 1][k], rolled[q][k]))
        row_blocks.append(jnp.concatenate(pieces, axis=1))
    return jnp.concatenate(row_blocks, axis=0).astype(jnp.bfloat16)


def _hyena_conv_body(L, Bsz, g_ref, u_ref, o_ref):
    TB = min(_HY_TB, L)
    nb = L // TB
    cols = Bsz * nb
    lane = lax.broadcasted_iota(jnp.int32, (TB, cols), 1) % nb

    def channel(ci, carry):
        u = u_ref[ci] if nb == 1 else u_ref[:, ci].reshape(cols, TB)
        u = u.astype(jnp.bfloat16)
        acc = jnp.zeros((TB, cols), jnp.float32)
        for d in range(-(nb - 1), nb):
            start = L - TB * d - TB
            w8 = jnp.broadcast_to(g_ref[pl.ds(ci, 1), start:start + 2 * TB], (_SUBLANES, 2 * TB))
            z = lax.dot_general(_toeplitz_tile(w8), u, _NT, preferred_element_type=jnp.float32)
            if d != 0:
                z = jnp.where((lane - d >= 0) & (lane - d < nb), pltpu.roll(z, d % cols, 1), 0.0)
            acc = acc + z
        if nb == 1:
            o_ref[ci] = acc.T
        else:
            o_ref[:, ci] = acc.T.reshape(Bsz, nb, TB)
        return carry

    lax.fori_loop(0, _HY_CB, channel, 0)


def _hyena_conv(g_rev, u_t):
    Bsz, G, L = u_t.shape
    TB = min(_HY_TB, L)
    nb = L // TB
    if nb == 1:
        blk = pl.BlockSpec((_HY_CB, Bsz, TB), lambda c: (c, 0, 0))
        operand, out_shape = jnp.swapaxes(u_t, 0, 1), (G, Bsz, TB)
    else:
        blk = pl.BlockSpec((Bsz, _HY_CB, nb, TB), lambda c: (0, c, 0, 0))
        operand, out_shape = u_t.reshape(Bsz, G, nb, TB), (Bsz, G, nb, TB)
    out = pl.pallas_call(
        functools.partial(_hyena_conv_body, L, Bsz),
        grid=(G // _HY_CB,),
        in_specs=[pl.BlockSpec((_HY_CB, 2 * L), lambda c: (c, 0)), blk],
        out_specs=blk,
        out_shape=jax.ShapeDtypeStruct(out_shape, jnp.float32),
        compiler_params=pltpu.CompilerParams(dimension_semantics=("arbitrary",), vmem_limit_bytes=_VMEM_LIMIT),
        name="hyena_conv",
    )(g_rev, operand)
    return jnp.swapaxes(out, 0, 1) if nb == 1 else out.reshape(Bsz, G, L)


def _hyena_post_body(yt_ref, u_ref, x0_ref, skip_ref, o_ref):
    o_ref[0] = (yt_ref[0].T + u_ref[0] * skip_ref[...]) * x0_ref[0]


def _hyena_post(y_t, u, x0, skip):
    Bsz, L, G = u.shape
    tok = pl.BlockSpec((1, L, G), lambda b: (b, 0, 0))
    return pl.pallas_call(
        _hyena_post_body,
        grid=(Bsz,),
        in_specs=[pl.BlockSpec((1, G, L), lambda b: (b, 0, 0)), tok, tok, pl.BlockSpec((1, G), lambda b: (0, 0))],
        out_specs=tok,
        out_shape=jax.ShapeDtypeStruct((Bsz, L, G), jnp.float32),
        compiler_params=pltpu.CompilerParams(dimension_semantics=("arbitrary",), vmem_limit_bytes=_VMEM_LIMIT),
        name="hyena_post",
    )(y_t, u, x0, skip.reshape(1, G))


def _hyena(py, short_w, short_b, w1, b1, w2, b2, w3, decay, skip):
    L = py.shape[1]
    g_rev = _hyena_filters_rev(L, w1, b1, w2, b2, w3, decay)
    u, u_t, x0 = _hyena_pre(py, short_w, short_b)
    return _hyena_post(_hyena_conv(g_rev, u_t), u, x0, skip)


def _split3_dot(a, b_bf16, dims=None):
    hi = a.astype(jnp.bfloat16)
    r1 = a - hi.astype(jnp.float32)
    mid = r1.astype(jnp.bfloat16)
    lo = (r1 - mid.astype(jnp.float32)).astype(jnp.bfloat16)
    out = 0.0
    for part in (hi, mid, lo):
        if dims is None:
            out = out + jnp.dot(part, b_bf16, preferred_element_type=jnp.float32)
        else:
            out = out + jnp.dot(b_bf16, part, preferred_element_type=jnp.float32)
    return out


def _softplus(x):
    return jnp.maximum(x, 0.0) + jnp.log(1.0 + jnp.exp(-jnp.abs(x)))


def _ssd_body(L, zx_ref, dtc_ref, dtr_ref, cw_ref, cb_ref, arow_ref, acol_ref, brow_ref, bcol_ref, dsk_ref,
              ng_ref, init_ref, o_ref, fin_ref, xp_ref, xc_ref, bt_ref, y_ref, ccol_ref, crow_ref, edec_ref,
              tot_ref):
    G = GROUP_W
    Q = SSD_CHUNK
    nc = L // Q
    P = SSD_HEAD_DIM
    N = SSD_STATE
    H = SSD_HEADS
    f32 = jnp.float32
    bf16 = jnp.bfloat16
    halo = jnp.zeros((_CONV_HALO, XBC_W), f32)
    xp_ref[0:_CONV_HALO, :] = halo
    xp_ref[_CONV_HALO + L:_CONV_HALO + L + _CONV_HALO, :] = halo
    xp_ref[_CONV_HALO:_CONV_HALO + L, :] = zx_ref[0, :, G:G + XBC_W]
    pad = (SSD_CONV - 1) // 2
    tile = min(_CONV_TILE, L)
    for t in range(L // tile):
        win = xp_ref[t * tile:t * tile + tile + 2 * _CONV_HALO, :]
        acc = jnp.zeros((tile, XBC_W), f32) + cb_ref[...]
        for k, tap in enumerate(_shifted_taps(win[_CONV_HALO - pad:], SSD_CONV, tile)):
            acc = acc + tap * cw_ref[k:k + 1, :]
        xc_ref[t * tile:(t + 1) * tile, :] = acc * jax.nn.sigmoid(acc)
    for c in range(nc):
        bt_ref[c] = xc_ref[c * Q:(c + 1) * Q, G:G + SSD_GROUPS * N].T
    dt_col = _softplus(dtc_ref[0] + brow_ref[...])
    a_col = dt_col * arow_ref[...]
    a_row = _softplus(dtr_ref[0] + bcol_ref[...]) * acol_ref[...]
    a_stack = jnp.concatenate([a_row[:, c * Q:(c + 1) * Q] for c in range(nc)], axis=0)
    ri = lax.broadcasted_iota(jnp.int32, (Q, Q), 0)
    ci = lax.broadcasted_iota(jnp.int32, (Q, Q), 1)
    one = lambda m: jnp.where(m, 1.0, 0.0).astype(bf16)
    tot_ref[...] = _split3_dot(a_stack, jnp.ones((Q, Q), bf16))

    def direction(d, y_store):
        fwd = d == 0
        m_col = one(ci <= ri) if fwd else one(ci >= ri)
        for c in range(nc):
            ccol_ref[c * Q:(c + 1) * Q, :] = _split3_dot(a_col[c * Q:(c + 1) * Q, :], m_col, dims="left")
        crow_ref[...] = _split3_dot(a_stack, one(ri <= ci) if fwd else one(ri >= ci))
        edec_ref[...] = _split3_dot(a_stack, one(ri > ci) if fwd else one(ri < ci))
        keep = (ri >= ci) if fwd else (ri <= ci)

        def chunk(step, states):
            c = step if fwd else nc - 1 - step
            r0 = pl.multiple_of(c * Q, Q)
            j0 = pl.multiple_of(c * 2 * H, 2 * H)
            xc = xc_ref[pl.ds(r0, Q), :]
            ccol = ccol_ref[pl.ds(r0, Q), :]
            crow = crow_ref[pl.ds(j0, 2 * H), :]
            edec = edec_ref[pl.ds(j0, 2 * H), :]
            tot = tot_ref[pl.ds(j0, 2 * H), :]
            dtc = _softplus(dtc_ref[0, pl.ds(r0, Q), :] + brow_ref[...])
            bt = bt_ref[c]
            new_states = []
            outs = []
            for g in range(SSD_GROUPS):
                cm = xc[:, G + SSD_GROUPS * N + g * N:G + SSD_GROUPS * N + (g + 1) * N].astype(bf16)
                bm = xc[:, G + g * N:G + (g + 1) * N].astype(bf16)
                cb = lax.dot_general(cm, bm, _NT, preferred_element_type=f32)
                for hh in range(H // SSD_GROUPS):
                    h = g * (H // SSD_GROUPS) + hh
                    j = d * H + h
                    col = jnp.broadcast_to(ccol[:, j:j + 1], (Q, Q))
                    lmat = jnp.exp(jnp.where(keep, col - crow[j:j + 1, :], _NEG))
                    xd = (xc[:, h * P:(h + 1) * P] * jnp.broadcast_to(dtc[:, j:j + 1], (Q, P))).astype(bf16)
                    st = states[h]
                    y = jnp.dot((cb * lmat).astype(bf16), xd, preferred_element_type=f32)
                    y = y + jnp.dot(cm, st.astype(bf16), preferred_element_type=f32) * jnp.exp(col[:, 0:P])
                    outs.append(y)
                    btd = (bt[g * N:(g + 1) * N, :] * jnp.exp(edec[j:j + 1, :])).astype(bf16)
                    new_states.append(jnp.exp(tot[j:j + 1, 0:P]) * st
                                      + jnp.dot(btd, xd, preferred_element_type=f32))
            y_store(r0, jnp.concatenate(outs, axis=-1))
            return tuple(new_states)

        init = tuple(init_ref[0, d, h] for h in range(H))
        final = lax.fori_loop(0, nc, chunk, init)
        for h in range(H):
            fin_ref[0, d, h] = final[h]

    def store_fwd(r0, y):
        y_ref[pl.ds(r0, Q), :] = y

    def store_bwd(r0, y):
        y_ref[pl.ds(r0, Q), :] += y

    direction(0, store_fwd)
    direction(1, store_bwd)
    gw = G // SSD_GROUPS
    for t in range(L // tile):
        rows = slice(t * tile, (t + 1) * tile)
        z = zx_ref[0, rows, 0:G]
        yg = (y_ref[rows, :] + xc_ref[rows, 0:G] * dsk_ref[...]) * (z * jax.nn.sigmoid(z))
        parts = []
        for g in range(SSD_GROUPS):
            v = yg[:, g * gw:(g + 1) * gw]
            parts.append(v * lax.rsqrt(jnp.mean(v * v, -1, keepdims=True) + LN_EPS))
        o_ref[0, rows, :] = jnp.concatenate(parts, axis=-1) * ng_ref[...]


def _ssd(pzx, pdt, init, conv_w, conv_b, a_log, dt_bias, d_skip, norm_g):
    Bsz, L, _ = pzx.shape
    G, H, Q = GROUP_W, SSD_HEADS, SSD_CHUNK
    lanes = pdt.shape[-1]
    nc = L // Q
    neg_a = -jnp.exp(a_log.astype(jnp.float32)).reshape(1, 2 * H)
    a_rowv = jnp.pad(neg_a, ((0, 0), (0, lanes - 2 * H)))
    b_rowv = jnp.pad(dt_bias.astype(jnp.float32).reshape(1, 2 * H), ((0, 0), (0, lanes - 2 * H)))
    dt_rows = jnp.swapaxes(pdt[:, :, :2 * H], 1, 2)
    d_lane = jnp.repeat(d_skip.astype(jnp.float32), SSD_HEAD_DIM).reshape(1, G)
    const = lambda shape: pl.BlockSpec(shape, lambda b: (0,) * len(shape))
    st_spec = pl.BlockSpec((1, 2, H, SSD_STATE, SSD_HEAD_DIM), lambda b: (b, 0, 0, 0, 0))
    f32 = jnp.float32
    return pl.pallas_call(
        functools.partial(_ssd_body, L),
        grid=(Bsz,),
        in_specs=[pl.BlockSpec((1, L, G + XBC_W), lambda b: (b, 0, 0)),
                  pl.BlockSpec((1, L, lanes), lambda b: (b, 0, 0)),
                  pl.BlockSpec((1, 2 * H, L), lambda b: (b, 0, 0)),
                  const((SSD_CONV, XBC_W)), const((1, XBC_W)),
                  const((1, lanes)), const((2 * H, 1)), const((1, lanes)), const((2 * H, 1)),
                  const((1, G)), const((1, G)), st_spec],
        out_specs=[pl.BlockSpec((1, L, G), lambda b: (b, 0, 0)), st_spec],
        out_shape=[jax.ShapeDtypeStruct((Bsz, L, G), f32),
                   jax.ShapeDtypeStruct((Bsz, 2, H, SSD_STATE, SSD_HEAD_DIM), f32)],
        scratch_shapes=[pltpu.VMEM((L + 2 * _CONV_HALO, XBC_W), f32),
                        pltpu.VMEM((L, XBC_W), f32),
                        pltpu.VMEM((nc, SSD_GROUPS * SSD_STATE, Q), f32),
                        pltpu.VMEM((L, G), f32),
                        pltpu.VMEM((L, lanes), f32),
                        pltpu.VMEM((nc * 2 * H, Q), f32),
                        pltpu.VMEM((nc * 2 * H, Q), f32),
                        pltpu.VMEM((nc * 2 * H, Q), f32)],
        compiler_params=pltpu.CompilerParams(dimension_semantics=("arbitrary",),
                                             vmem_limit_bytes=_VMEM_LIMIT),
        name="ssd",
    )(pzx, pdt, dt_rows, conv_w, conv_b.reshape(1, XBC_W), a_rowv, neg_a.reshape(2 * H, 1),
      b_rowv, dt_bias.astype(f32).reshape(2 * H, 1), d_lane, norm_g.reshape(1, G), init)


def _mixer_ln(x, xc, mod, mod_c, ctx_out, alpha, w_in, w_out, ln_g, ln_b, conf, rpb, hy, ssd):
    Bsz = x.shape[0]
    sh, sc, g = mod
    shc, scc, gc = mod_c
    pa, pb, py, pzx, pdt = _in_proj(x, sc, sh, w_in)
    ca, cb, cy, czx, cdt = _in_proj(xc, scc, shc, w_in)
    zero = jnp.zeros((Bsz, 2, SSD_HEADS, SSD_STATE, SSD_HEAD_DIM), jnp.float32)
    y_dc, ctx_states = _ssd(czx, cdt, zero, *ssd)
    y_d, _ = _ssd(pzx, pdt, ctx_states, *ssd)
    ys = [_conformer(pa, *conf), _na_attention(pb, cb, rpb), _hyena(py, *hy), y_d]
    x_new = _out_proj_ln(ys, w_out, x, g, ln_g, ln_b, alpha)
    if not ctx_out:
        return x_new, None
    ycs = [_conformer(ca, *conf), _ctx_attention(cb), _hyena(cy, *hy), y_dc]
    return x_new, _out_proj_ln(ycs, w_out, xc, gc, ln_g, ln_b, alpha)


_NEG = -1e30
_ROUTE_TT = 256
_DENSE_TT = 512
_DENSE_EC = 1024
_NT = (((1,), (1,)), ((), ()))


def _top16_rows(s):
    rows = []
    stacked = jnp.zeros((PEER_TOPK, s.shape[1]), jnp.float32)
    ridx = lax.broadcasted_iota(jnp.int32, stacked.shape, 0)
    cur = s
    for k in range(PEER_TOPK):
        m = jnp.max(cur, axis=0, keepdims=True)
        rows.append(m)
        stacked = jnp.where(ridx == k, m, stacked)
        if k + 1 < PEER_TOPK:
            cur = jnp.where(cur >= m, _NEG, cur)
    return rows, stacked


def _pair_candidates(r1, V1, r2, V2, op, fill):
    half = V1[0:8]
    keep = lax.broadcasted_iota(jnp.int32, half.shape, 0) >= 4
    pieces = [op(r1[0], V2), op(r1[1], V2[0:8]), op(r1[2], V2[0:8]), op(r1[3], V2[0:8]), op(r2[0], V1[8:16])]
    for b in range(3):
        pieces.append(jnp.where(keep, op(r2[b], half), fill))
    return jnp.concatenate(pieces, axis=0)


def _peer_route_body(x_ref, sc_ref, sh_ref, wqT_ref, keys_ref, hm_ref, e1_ref, e2_ref, pthr_ref, qT_ref):
    hm = (x_ref[0] * (1.0 + sc_ref[0]) + sh_ref[0]).astype(jnp.bfloat16)
    hm_ref[0] = hm
    qT_ref[...] = lax.dot_general(wqT_ref[...], hm, _NT, preferred_element_type=jnp.float32)

    def head(h, carry):
        base = pl.multiple_of(h * PEER_QDIM, PEER_QDIM)
        half_q = PEER_QDIM // 2
        s_both = []
        for p in range(2):
            qb = qT_ref[pl.ds(base + p * half_q, half_q), :].astype(jnp.bfloat16)
            s_both.append(jnp.dot(keys_ref[h, p], qb, preferred_element_type=jnp.float32))
        for half in range(_ROUTE_TT // 128):
            lanes = slice(half * 128, (half + 1) * 128)
            s1 = s_both[0][:, lanes]
            s2 = s_both[1][:, lanes]
            r1, V1 = _top16_rows(s1)
            r2, V2 = _top16_rows(s2)
            cand = _pair_candidates(r1, V1, r2, V2, lambda a, b: a + b, _NEG)
            cur = cand
            top = r1[0] + r2[0]
            z = jnp.ones_like(top)
            m = top
            for k in range(1, PEER_TOPK):
                cur = jnp.where(cur >= m, _NEG, cur)
                m = jnp.max(cur, axis=0, keepdims=True)
                z = z + jnp.exp(m - top)
            rz = 1.0 / z
            e1r = [jnp.exp(r - r1[0]) * rz for r in r1]
            e2r = [jnp.exp(r - r2[0]) for r in r2]
            E1 = jnp.exp(V1 - r1[0]) * rz
            E2 = jnp.exp(V2 - r2[0])
            prod = _pair_candidates(e1r, E1, e2r, E2, lambda a, b: a * b, 0.0)
            pthr = jnp.min(jnp.where(cand >= m, prod, 1e30), axis=0, keepdims=True)
            e1_ref[h, :, lanes] = jnp.exp(s1 - r1[0]) * rz
            e2_ref[h, :, lanes] = jnp.exp(s2 - r2[0])
            pthr_ref[h, :, lanes] = pthr
        return carry

    lax.fori_loop(0, PEER_HEADS, head, 0)


def _peer_route(x, sc, sh, wqT, keys):
    Bsz, S, D = x.shape
    assert S % _ROUTE_TT == 0, S
    nt = S // _ROUTE_TT
    T = Bsz * S
    tab = jax.ShapeDtypeStruct((PEER_HEADS, PEER_KEYS, T), jnp.float32)
    return pl.pallas_call(
        _peer_route_body,
        grid=(Bsz, nt),
        in_specs=[pl.BlockSpec((1, _ROUTE_TT, D), lambda b, i: (b, i, 0)),
                  pl.BlockSpec((1, 1, D), lambda b, i: (b, 0, 0)),
                  pl.BlockSpec((1, 1, D), lambda b, i: (b, 0, 0)),
                  pl.BlockSpec(wqT.shape, lambda b, i: (0, 0)),
                  pl.BlockSpec(keys.shape, lambda b, i: (0, 0, 0, 0))],
        out_specs=[pl.BlockSpec((1, _ROUTE_TT, D), lambda b, i: (b, i, 0)),
                   pl.BlockSpec((PEER_HEADS, PEER_KEYS, _ROUTE_TT), lambda b, i: (0, 0, b * nt + i)),
                   pl.BlockSpec((PEER_HEADS, PEER_KEYS, _ROUTE_TT), lambda b, i: (0, 0, b * nt + i)),
                   pl.BlockSpec((PEER_HEADS, 1, _ROUTE_TT), lambda b, i: (0, 0, b * nt + i))],
        out_shape=[jax.ShapeDtypeStruct((Bsz, S, D), jnp.bfloat16), tab, tab,
                   jax.ShapeDtypeStruct((PEER_HEADS, 1, T), jnp.float32)],
        scratch_shapes=[pltpu.VMEM((PEER_HEADS * PEER_QDIM, _ROUTE_TT), jnp.float32)],
        compiler_params=pltpu.CompilerParams(dimension_semantics=("arbitrary", "arbitrary"),
                                             vmem_limit_bytes=_VMEM_LIMIT),
        name="peer_route",
    )(x, sc, sh, wqT, keys)


_GELU_K = math.sqrt(2.0 / math.pi)


def _gated_gelu(gate, x):
    half = (0.5 * x) * gate
    return half + half * jnp.tanh(x * (_GELU_K + (_GELU_K * 0.044715) * (x * x)))


def _peer_dense_body(alpha, hm_ref, e1_ref, e2_ref, pthr_ref, u_ref, vt_ref, x_ref, g_ref, lng_ref, lnb_ref,
                     o_ref, acc_ref, wt_ref, e1b_ref, pthrb_ref):
    c = pl.program_id(2)
    sub = 8
    jrows = 32

    @pl.when(c == 0)
    def _():
        acc_ref[...] = jnp.zeros_like(acc_ref)
        for h in range(PEER_HEADS):
            pthrb_ref[h] = jnp.broadcast_to(pthr_ref[h], (sub, _DENSE_TT))

    for ii in range(_DENSE_EC // PEER_KEYS):
        actT = lax.dot_general(u_ref[ii * PEER_KEYS:(ii + 1) * PEER_KEYS, :], hm_ref[0], _NT,
                               preferred_element_type=jnp.float32)
        for h in range(PEER_HEADS):
            e1b_ref[ii % 2, h] = jnp.broadcast_to(e1_ref[h, ii:ii + 1, :], (sub, _DENSE_TT))
        for jb in range(PEER_KEYS // jrows):
            gate = jnp.zeros((jrows // sub, sub, _DENSE_TT), jnp.float32)
            for h in range(PEER_HEADS):
                e2 = e2_ref[h, jb * jrows:(jb + 1) * jrows, :].reshape(jrows // sub, sub, _DENSE_TT)
                val = e2 * e1b_ref[ii % 2, h][None]
                gate = gate + jnp.where(val >= pthrb_ref[h][None], val, 0.0)
            r0 = ii * PEER_KEYS + jb * jrows
            w = _gated_gelu(gate.reshape(jrows, _DENSE_TT), actT[jb * jrows:(jb + 1) * jrows, :])
            wt_ref[r0:r0 + jrows, :] = w.astype(jnp.bfloat16)
    acc_ref[...] += jnp.dot(vt_ref[...], wt_ref[...], preferred_element_type=jnp.float32)

    @pl.when(c == pl.num_programs(2) - 1)
    def _():
        y = alpha * x_ref[0] + g_ref[0] * acc_ref[...].T
        mu = jnp.mean(y, -1, keepdims=True)
        yc = y - mu
        var = jnp.mean(yc * yc, -1, keepdims=True)
        o_ref[0] = yc * lax.rsqrt(var + LN_EPS) * lng_ref[...] + lnb_ref[...]


def _peer_dense(hm, e1, e2, pthr, u_bf, vt_bf, x, g, ln_g, ln_b, alpha):
    Bsz, S, D = x.shape
    assert S % _DENSE_TT == 0, S
    nt = S // _DENSE_TT
    nchunk = N_EXPERTS // _DENSE_EC
    rows_i = _DENSE_EC // PEER_KEYS
    return pl.pallas_call(
        functools.partial(_peer_dense_body, alpha),
        grid=(Bsz, nt, nchunk),
        in_specs=[pl.BlockSpec((1, _DENSE_TT, D), lambda b, i, c: (b, i, 0)),
                  pl.BlockSpec((PEER_HEADS, rows_i, _DENSE_TT), lambda b, i, c: (0, c, b * nt + i)),
                  pl.BlockSpec((PEER_HEADS, PEER_KEYS, _DENSE_TT), lambda b, i, c: (0, 0, b * nt + i)),
                  pl.BlockSpec((PEER_HEADS, 1, _DENSE_TT), lambda b, i, c: (0, 0, b * nt + i)),
                  pl.BlockSpec((_DENSE_EC, D), lambda b, i, c: (c, 0)),
                  pl.BlockSpec((D, _DENSE_EC), lambda b, i, c: (0, c)),
                  pl.BlockSpec((1, _DENSE_TT, D), lambda b, i, c: (b, i, 0)),
                  pl.BlockSpec((1, 1, D), lambda b, i, c: (b, 0, 0)),
                  pl.BlockSpec((1, D), lambda b, i, c: (0, 0)),
                  pl.BlockSpec((1, D), lambda b, i, c: (0, 0))],
        out_specs=pl.BlockSpec((1, _DENSE_TT, D), lambda b, i, c: (b, i, 0)),
        out_shape=jax.ShapeDtypeStruct((Bsz, S, D), jnp.float32),
        scratch_shapes=[pltpu.VMEM((D, _DENSE_TT), jnp.float32),
                        pltpu.VMEM((_DENSE_EC, _DENSE_TT), jnp.bfloat16),
                        pltpu.VMEM((2, PEER_HEADS, 8, _DENSE_TT), jnp.float32),
                        pltpu.VMEM((PEER_HEADS, 8, _DENSE_TT), jnp.float32)],
        compiler_params=pltpu.CompilerParams(dimension_semantics=("arbitrary", "arbitrary", "arbitrary"),
                                             vmem_limit_bytes=_VMEM_LIMIT),
        name="peer_dense",
    )(hm, e1, e2, pthr, u_bf, vt_bf, x, g, ln_g.reshape(1, D), ln_b.reshape(1, D))


def _peer_weights(wq, sub_keys, u_tab, v_tab):
    return (wq.T.astype(jnp.bfloat16), sub_keys.astype(jnp.bfloat16),
            u_tab.astype(jnp.bfloat16), v_tab.T.astype(jnp.bfloat16))


def _peer_ln(x, sc, sh, g, pw, ln_g, ln_b, alpha):
    wqT, keys, u_bf, vt_bf = pw
    hm, e1, e2, pthr = _peer_route(x, sc, sh, wqT, keys)
    return _peer_dense(hm, e1, e2, pthr, u_bf, vt_bf, x, g, ln_g, ln_b, alpha)


def kernel(x, c, ctx, c_ctx, w_ada, b_ada, w_in, w_out, ln1_g, ln1_b, ln2_g, ln2_b,
           conf_dw_w, conf_dw_b, conf_norm_g, conf_norm_b, na_rpb, hy_short_w, hy_short_b,
           hy_w1, hy_b1, hy_w2, hy_b2, hy_w3, hy_decay, hy_bias, ssd_conv_w, ssd_conv_b,
           ssd_a_log, ssd_dt_bias, ssd_d, ssd_norm_g, peer_wq, peer_keys, peer_u, peer_v):
    alpha = (2.0 * DEPTH) ** 0.25
    s_c = jax.nn.silu(c)
    s_cc = jax.nn.silu(c_ctx)
    xc = ctx
    Bsz, Lc, D = ctx.shape
    for l in range(DEPTH):
        ctx_out = l < DEPTH - 1
        mod = (s_c @ w_ada[l] + b_ada[l])[:, None, :]
        mod_c = jnp.broadcast_to((s_cc @ w_ada[l] + b_ada[l])[None, None, :], mod.shape)
        sh1, sc1, g1, sh2, sc2, g2 = jnp.split(mod, 6, -1)
        sh1c, sc1c, g1c, sh2c, sc2c, g2c = jnp.split(mod_c, 6, -1)
        x, xc = _mixer_ln(
            x, xc, (sh1, sc1, g1), (sh1c, sc1c, g1c), ctx_out, alpha, w_in[l], w_out[l], ln1_g[l], ln1_b[l],
            (conf_dw_w[l], conf_dw_b[l], conf_norm_g[l], conf_norm_b[l]), na_rpb[l],
            (hy_short_w[l], hy_short_b[l], hy_w1[l], hy_b1[l], hy_w2[l], hy_b2[l], hy_w3[l], hy_decay[l], hy_bias[l]),
            (ssd_conv_w[l], ssd_conv_b[l], ssd_a_log[l], ssd_dt_bias[l], ssd_d[l], ssd_norm_g[l]))
        pw = _peer_weights(peer_wq[l], peer_keys[l], peer_u[l], peer_v[l])
        x = _peer_ln(x, sc2, sh2, g2, pw, ln2_g[l], ln2_b[l], alpha)
        if ctx_out:
            xc = _peer_ln(xc.reshape(1, Bsz * Lc, D), sc2c[:1], sh2c[:1], g2c[:1], pw, ln2_g[l], ln2_b[l],
                          alpha).reshape(Bsz, Lc, D)
    return x
```

```python
import functools
import math
import jax, jax.numpy as jnp
from jax import lax
import numpy as np
from jax.experimental import pallas as pl
from jax.experimental.pallas import tpu as pltpu

D_MODEL = 1024
BATCH = 16
SEQ = 2048
DEPTH = 2

GRID_W = 64
CTX_LEN = 256
N_MIXERS = 4
GROUP_W = D_MODEL // N_MIXERS
D_MIX = N_MIXERS * GROUP_W
LN_EPS = 1e-5
CONF_K = 31
CONF_GROUPS = 4
NA_HEADS = 4
HEAD_DIM = GROUP_W // NA_HEADS
NA_KH = 8
NA_KW = 16
ROPE_BASE = 10000.0
HY_SHORT = 3
HY_BANDS = 16
HY_EMB = 1 + 2 * HY_BANDS
HY_HIDDEN = 64
HY_SIN_FREQ = 1.0
SSD_HEADS = 4
SSD_HEAD_DIM = GROUP_W // SSD_HEADS
SSD_GROUPS = 2
SSD_STATE = 64
SSD_CONV = 3
SSD_CHUNK = 128
XBC_W = GROUP_W + 2 * SSD_GROUPS * SSD_STATE
IN_COLS = 2 * GROUP_W + 3 * GROUP_W + 3 * GROUP_W + GROUP_W + XBC_W + 2 * SSD_HEADS
PEER_HEADS = 8
PEER_KEYS = 128
PEER_TOPK = 16
PEER_QDIM = 256
N_EXPERTS = PEER_KEYS * PEER_KEYS
PEER_BLOCK = 128

_VMEM_LIMIT = 56 * 1024 * 1024
_IN_PAD = 2944


def _standardize(x):
    xf = x.astype(jnp.float32)
    mu = jnp.mean(xf, -1, keepdims=True)
    var = jnp.mean(jnp.square(xf - mu), -1, keepdims=True)
    return (xf - mu) * lax.rsqrt(var + LN_EPS)


def layer_norm(x, g, b):
    return (_standardize(x) * g + b).astype(x.dtype)


def dwconv(x, w):
    K, C = w.shape
    pad = (K - 1) // 2
    return lax.conv_general_dilated(x, w[:, None, :].astype(x.dtype), window_strides=(1,),
                                    padding=[(pad, pad)], dimension_numbers=('NWC', 'WIO', 'NWC'),
                                    feature_group_count=C)


def _split_cols(p):
    sizes = (2 * GROUP_W, 3 * GROUP_W, 3 * GROUP_W, GROUP_W, XBC_W, 2 * SSD_HEADS)
    points = [int(v) for v in np.cumsum(sizes)[:-1]]
    return jnp.split(p, points, axis=-1)


_IN_SPLITS = ((0, 2 * GROUP_W), (2 * GROUP_W, 5 * GROUP_W), (5 * GROUP_W, 8 * GROUP_W),
              (8 * GROUP_W, 9 * GROUP_W + XBC_W), (9 * GROUP_W + XBC_W, _IN_PAD))
_PROJ_TM = 512


def _in_proj_body(x_ref, sc_ref, sh_ref, w_ref, *o_refs):
    h = (x_ref[0] * (1.0 + sc_ref[0]) + sh_ref[0]).astype(jnp.bfloat16)
    for (lo, hi), o_ref in zip(_IN_SPLITS, o_refs):
        o_ref[0] = jnp.dot(h, w_ref[:, lo:hi], preferred_element_type=jnp.float32)


def _in_proj(x, sc, sh, w_in):
    Bsz, L, D = x.shape
    tm = min(_PROJ_TM, L)
    w = jnp.pad(w_in, ((0, 0), (0, _IN_PAD - IN_COLS))).astype(jnp.bfloat16)
    return pl.pallas_call(
        _in_proj_body,
        grid=(Bsz, L // tm),
        in_specs=[pl.BlockSpec((1, tm, D), lambda b, i: (b, i, 0)),
                  pl.BlockSpec((1, 1, D), lambda b, i: (b, 0, 0)),
                  pl.BlockSpec((1, 1, D), lambda b, i: (b, 0, 0)),
                  pl.BlockSpec((D, _IN_PAD), lambda b, i: (0, 0))],
        out_specs=[pl.BlockSpec((1, tm, hi - lo), lambda b, i: (b, i, 0)) for lo, hi in _IN_SPLITS],
        out_shape=[jax.ShapeDtypeStruct((Bsz, L, hi - lo), jnp.float32) for lo, hi in _IN_SPLITS],
        compiler_params=pltpu.CompilerParams(dimension_semantics=("arbitrary", "arbitrary"),
                                             vmem_limit_bytes=_VMEM_LIMIT),
        name="in_proj",
    )(x, sc, sh, w)


def _out_proj_ln_body(alpha, ya_ref, yb_ref, yc_ref, yd_ref, w_ref, x_ref, g_ref, lng_ref, lnb_ref, o_ref):
    y = 0.0
    for m, y_ref in enumerate((ya_ref, yb_ref, yc_ref, yd_ref)):
        y = y + jnp.dot(y_ref[0].astype(jnp.bfloat16), w_ref[m * GROUP_W:(m + 1) * GROUP_W, :],
                        preferred_element_type=jnp.float32)
    r = alpha * x_ref[0] + g_ref[0] * y
    mu = jnp.mean(r, -1, keepdims=True)
    rc = r - mu
    var = jnp.mean(rc * rc, -1, keepdims=True)
    o_ref[0] = rc * lax.rsqrt(var + LN_EPS) * lng_ref[...] + lnb_ref[...]


def _out_proj_ln(ys, w_out, x, g, ln_g, ln_b, alpha):
    Bsz, L, D = x.shape
    tm = min(_PROJ_TM, L)
    mix = pl.BlockSpec((1, tm, GROUP_W), lambda b, i: (b, i, 0))
    return pl.pallas_call(
        functools.partial(_out_proj_ln_body, alpha),
        grid=(Bsz, L // tm),
        in_specs=[mix, mix, mix, mix,
                  pl.BlockSpec((D_MIX, D), lambda b, i: (0, 0)),
                  pl.BlockSpec((1, tm, D), lambda b, i: (b, i, 0)),
                  pl.BlockSpec((1, 1, D), lambda b, i: (b, 0, 0)),
                  pl.BlockSpec((1, D), lambda b, i: (0, 0)),
                  pl.BlockSpec((1, D), lambda b, i: (0, 0))],
        out_specs=pl.BlockSpec((1, tm, D), lambda b, i: (b, i, 0)),
        out_shape=jax.ShapeDtypeStruct((Bsz, L, D), jnp.float32),
        compiler_params=pltpu.CompilerParams(dimension_semantics=("arbitrary", "arbitrary"),
                                             vmem_limit_bytes=_VMEM_LIMIT),
        name="out_proj_ln",
    )(*ys, w_out.astype(jnp.bfloat16), x, g, ln_g.reshape(1, D), ln_b.reshape(1, D))


def _shifted_taps(win, n_taps, rows):
    return [win[k:k + rows] for k in range(n_taps)]


_CONV_HALO = 16
_CONV_TILE = 256


def _group_mean_matrix(width, group):
    r = lax.broadcasted_iota(jnp.int32, (width, width), 0) // group
    c = lax.broadcasted_iota(jnp.int32, (width, width), 1) // group
    return jnp.where(r == c, 1.0 / group, 0.0).astype(jnp.float32)


def _conformer_body(L, p_ref, w_ref, b_ref, ng_ref, nb_ref, o_ref, u_ref):
    G = GROUP_W
    pad = (CONF_K - 1) // 2
    halo = jnp.zeros((_CONV_HALO, G), jnp.float32)
    u_ref[0:_CONV_HALO, :] = halo
    u_ref[_CONV_HALO + L:_CONV_HALO + L + _CONV_HALO, :] = halo
    u_ref[_CONV_HALO:_CONV_HALO + L, :] = p_ref[0, :, 0:G] * jax.nn.sigmoid(p_ref[0, :, G:2 * G])
    avg = _group_mean_matrix(G, G // CONF_GROUPS)
    tile = min(_CONV_TILE, L)
    for t in range(L // tile):
        win = u_ref[t * tile:t * tile + tile + 2 * _CONV_HALO, :]
        acc = jnp.zeros((tile, G), jnp.float32) + b_ref[...]
        for k, tap in enumerate(_shifted_taps(win[_CONV_HALO - pad:], CONF_K, tile)):
            acc = acc + tap * w_ref[k:k + 1, :]
        mu = jnp.dot(acc, avg, preferred_element_type=jnp.float32, precision=lax.Precision.HIGHEST)
        cen = acc - mu
        var = jnp.dot(cen * cen, avg, preferred_element_type=jnp.float32, precision=lax.Precision.HIGHEST)
        un = cen * lax.rsqrt(var + LN_EPS) * ng_ref[...] + nb_ref[...]
        o_ref[0, t * tile:(t + 1) * tile, :] = un * jax.nn.sigmoid(un)


def _conformer(pa, dw_w, dw_b, n_g, n_b):
    Bsz, L, _ = pa.shape
    G = GROUP_W
    vec = pl.BlockSpec((1, G), lambda b: (0, 0))
    return pl.pallas_call(
        functools.partial(_conformer_body, L),
        grid=(Bsz,),
        in_specs=[pl.BlockSpec((1, L, 2 * G), lambda b: (b, 0, 0)),
                  pl.BlockSpec((CONF_K, G), lambda b: (0, 0)), vec, vec, vec],
        out_specs=pl.BlockSpec((1, L, G), lambda b: (b, 0, 0)),
        out_shape=jax.ShapeDtypeStruct((Bsz, L, G), jnp.float32),
        scratch_shapes=[pltpu.VMEM((L + 2 * _CONV_HALO, G), jnp.float32)],
        compiler_params=pltpu.CompilerParams(dimension_semantics=("arbitrary",),
                                             vmem_limit_bytes=_VMEM_LIMIT),
        name="conformer",
    )(pa, dw_w, dw_b.reshape(1, G), n_g.reshape(1, G), n_b.reshape(1, G))


def conformer_conv(p, dw_w, dw_b, n_g, n_b):
    a, gate = jnp.split(p, 2, -1)
    u = a * jax.nn.sigmoid(gate)
    u = dwconv(u, dw_w) + dw_b
    Bsz, L, C = u.shape
    un = _standardize(u.reshape(Bsz, L, CONF_GROUPS, C // CONF_GROUPS)).reshape(Bsz, L, C)
    un = (un * n_g + n_b).astype(u.dtype)
    return jax.nn.silu(un)


def axial_rope(rows, head_dim):
    n_f = head_dim // 4
    inv = ROPE_BASE ** (-jnp.arange(n_f, dtype=jnp.float32) / n_f)
    t = jnp.arange(rows * GRID_W)
    r = (t // GRID_W).astype(jnp.float32)
    col = (t % GRID_W).astype(jnp.float32)
    ang = jnp.concatenate([r[:, None] * inv, col[:, None] * inv], -1)
    return jnp.cos(ang), jnp.sin(ang)


def apply_rope(x, cos, sin):
    x1, x2 = jnp.split(x.astype(jnp.float32), 2, -1)
    c = cos[None, :, None, :]
    s = sin[None, :, None, :]
    return jnp.concatenate([x1 * c - x2 * s, x1 * s + x2 * c], -1).astype(x.dtype)


def neighborhood_attention(q_rot, k_rot, v, q_plain, k_ctx, v_ctx, rpb):
    Bsz, S, H, d = q_rot.shape
    rows = S // GRID_W
    kh = min(NA_KH, rows)
    qg = q_rot.reshape(Bsz, rows, GRID_W, H, d)
    kg = k_rot.reshape(Bsz, rows, GRID_W, H, d)
    vg = v.reshape(Bsz, rows, GRID_W, H, d)
    qpg = q_plain.reshape(Bsz, rows, GRID_W, H, d)
    cq = jnp.arange(GRID_W)
    col_idx = jnp.clip(cq - NA_KW // 2, 0, GRID_W - NA_KW)[:, None] + jnp.arange(NA_KW)[None, :]
    col_bias_idx = col_idx - cq[:, None] + (NA_KW - 1)
    scale = d ** -0.5

    def row_block(r):
        rs = jnp.clip(r - kh // 2, 0, rows - kh)
        kr = lax.dynamic_slice_in_dim(kg, rs, kh, axis=1)[:, :, col_idx]
        vr = lax.dynamic_slice_in_dim(vg, rs, kh, axis=1)[:, :, col_idx]
        qr = lax.dynamic_index_in_dim(qg, r, axis=1, keepdims=False)
        qpr = lax.dynamic_index_in_dim(qpg, r, axis=1, keepdims=False)
        row_bias_idx = rs + jnp.arange(kh) - r + (NA_KH - 1)
        bias = rpb[:, row_bias_idx][:, :, col_bias_idx].transpose(0, 2, 1, 3)
        s_loc = jnp.einsum('bchd,bicjhd->bhcij', qr, kr).astype(jnp.float32) * scale + bias
        s_ctx = jnp.einsum('bchd,bnhd->bhcn', qpr, k_ctx).astype(jnp.float32) * scale
        logits = jnp.concatenate([s_loc.reshape(Bsz, H, GRID_W, kh * NA_KW), s_ctx], -1)
        p = jax.nn.softmax(logits, -1).astype(v.dtype)
        p_loc = p[..., :kh * NA_KW].reshape(Bsz, H, GRID_W, kh, NA_KW)
        p_ctx = p[..., kh * NA_KW:]
        return (jnp.einsum('bhcij,bicjhd->bchd', p_loc, vr)
                + jnp.einsum('bhcn,bnhd->bchd', p_ctx, v_ctx))

    out = lax.map(row_block, jnp.arange(rows))
    return out.transpose(1, 0, 2, 3, 4).reshape(Bsz, S, H * d)


_NA_MASK = -1e30


def _na_tables(rows, rpb):
    cos, sin = axial_rope(rows, HEAD_DIM)
    cos_f = jnp.tile(cos, (1, 2 * NA_HEADS))
    sin_s = jnp.tile(jnp.concatenate([-sin, sin], -1), (1, NA_HEADS))
    cq = jnp.arange(GRID_W)
    cs = jnp.clip(cq - NA_KW // 2, 0, GRID_W - NA_KW)
    col = jnp.arange(GRID_W)
    in_band = (col[None, :] >= cs[:, None]) & (col[None, :] < cs[:, None] + NA_KW)
    cb_idx = jnp.clip(col[None, :] - cq[:, None] + (NA_KW - 1), 0, 2 * NA_KW - 2)
    po = jnp.arange(NA_KH)
    rb_idx = jnp.arange(NA_KH)[None, :] - po[:, None] + (NA_KH - 1)
    bias = rpb[:, rb_idx][:, :, :, cb_idx]
    bias = jnp.where(in_band[None, None, None], bias, _NA_MASK)
    bias = bias.transpose(1, 0, 3, 2, 4).reshape(NA_KH, NA_HEADS, GRID_W, NA_KH * GRID_W)
    return cos_f, sin_s, bias


def _rope_lanes(x, cos_f, sin_s):
    n = x.shape[-1]
    hd = HEAD_DIM // 2
    first = (lax.broadcasted_iota(jnp.int32, x.shape, 1) % HEAD_DIM) < hd
    partner = jnp.where(first, pltpu.roll(x, n - hd, 1), pltpu.roll(x, hd, 1))
    return x * cos_f + partner * sin_s


def _softmax_pv(s_parts, v_parts):
    m = s_parts[0].max(-1, keepdims=True)
    for s in s_parts[1:]:
        m = jnp.maximum(m, s.max(-1, keepdims=True))
    l = 0.0
    o = 0.0
    for s, v in zip(s_parts, v_parts):
        p = jnp.exp(s - m)
        l = l + p.sum(-1, keepdims=True)
        o = o + jnp.dot(p.astype(jnp.bfloat16), v, preferred_element_type=jnp.float32)
    return o / l


def _na_body(rows, q_ref, k_ref, v_ref, kc_ref, vc_ref, cosq_ref, sinq_ref, cosk_ref, sink_ref, bias_ref,
             o_ref, krot_ref, vbf_ref):
    r = pl.program_id(1)
    scale = HEAD_DIM ** -0.5

    @pl.when(r == 0)
    def _():
        krot_ref[...] = _rope_lanes(k_ref[0], cosk_ref[...], sink_ref[...]).astype(jnp.bfloat16)
        vbf_ref[...] = v_ref[0].astype(jnp.bfloat16)

    rs = jnp.clip(r - NA_KH // 2, 0, rows - NA_KH)
    start = pl.multiple_of(rs * GRID_W, GRID_W)
    win = NA_KH * GRID_W
    q = q_ref[0] * scale
    q_rot = _rope_lanes(q, cosq_ref[...], sinq_ref[...]).astype(jnp.bfloat16)
    q_plain = q.astype(jnp.bfloat16)
    kw = krot_ref[pl.ds(start, win), :]
    vw = vbf_ref[pl.ds(start, win), :]
    kc = kc_ref[0].astype(jnp.bfloat16)
    vc = vc_ref[0].astype(jnp.bfloat16)
    outs = []
    for h in range(NA_HEADS):
        hs = slice(h * HEAD_DIM, (h + 1) * HEAD_DIM)
        s_loc = lax.dot_general(q_rot[:, hs], kw[:, hs], _NT, preferred_element_type=jnp.float32) + bias_ref[0, h]
        s_ctx = lax.dot_general(q_plain[:, hs], kc[:, hs], _NT, preferred_element_type=jnp.float32)
        outs.append(_softmax_pv([s_loc, s_ctx], [vw[:, hs], vc[:, hs]]))
    o_ref[0] = jnp.concatenate(outs, axis=-1)


def _na_attention(pb, cb, rpb):
    Bsz, S, _ = pb.shape
    Lc = cb.shape[1]
    rows = S // GRID_W
    G = GROUP_W
    cos_f, sin_s, bias = _na_tables(rows, rpb)

    def bias_idx(b, r):
        rs = jnp.clip(r - NA_KH // 2, 0, rows - NA_KH)
        return (r - rs, 0, 0, 0)

    return pl.pallas_call(
        functools.partial(_na_body, rows),
        grid=(Bsz, rows),
        in_specs=[pl.BlockSpec((1, GRID_W, G), lambda b, r: (b, r, 0)),
                  pl.BlockSpec((1, S, G), lambda b, r: (b, 0, 1)),
                  pl.BlockSpec((1, S, G), lambda b, r: (b, 0, 2)),
                  pl.BlockSpec((1, Lc, G), lambda b, r: (b, 0, 1)),
                  pl.BlockSpec((1, Lc, G), lambda b, r: (b, 0, 2)),
                  pl.BlockSpec((GRID_W, G), lambda b, r: (r, 0)),
                  pl.BlockSpec((GRID_W, G), lambda b, r: (r, 0)),
                  pl.BlockSpec((S, G), lambda b, r: (0, 0)),
                  pl.BlockSpec((S, G), lambda b, r: (0, 0)),
                  pl.BlockSpec((1, NA_HEADS, GRID_W, NA_KH * GRID_W), bias_idx)],
        out_specs=pl.BlockSpec((1, GRID_W, G), lambda b, r: (b, r, 0)),
        out_shape=jax.ShapeDtypeStruct((Bsz, S, G), jnp.float32),
        scratch_shapes=[pltpu.VMEM((S, G), jnp.bfloat16), pltpu.VMEM((S, G), jnp.bfloat16)],
        compiler_params=pltpu.CompilerParams(dimension_semantics=("arbitrary", "arbitrary"),
                                             vmem_limit_bytes=_VMEM_LIMIT),
        name="na_attention",
    )(pb, pb, pb, cb, cb, cos_f, sin_s, cos_f, sin_s, bias)


def _ctx_attn_body(q_ref, k_ref, v_ref, o_ref):
    q = (q_ref[0] * HEAD_DIM ** -0.5).astype(jnp.bfloat16)
    k = k_ref[0].astype(jnp.bfloat16)
    v = v_ref[0].astype(jnp.bfloat16)
    outs = []
    for h in range(NA_HEADS):
        hs = slice(h * HEAD_DIM, (h + 1) * HEAD_DIM)
        s = lax.dot_general(q[:, hs], k[:, hs], _NT, preferred_element_type=jnp.float32)
        outs.append(_softmax_pv([s], [v[:, hs]]))
    o_ref[0] = jnp.concatenate(outs, axis=-1)


def _ctx_attention(cb):
    Bsz, Lc, _ = cb.shape
    G = GROUP_W
    return pl.pallas_call(
        _ctx_attn_body,
        grid=(Bsz,),
        in_specs=[pl.BlockSpec((1, Lc, G), lambda b: (b, 0, 0)),
                  pl.BlockSpec((1, Lc, G), lambda b: (b, 0, 1)),
                  pl.BlockSpec((1, Lc, G), lambda b: (b, 0, 2))],
        out_specs=pl.BlockSpec((1, Lc, G), lambda b: (b, 0, 0)),
        out_shape=jax.ShapeDtypeStruct((Bsz, Lc, G), jnp.float32),
        compiler_params=pltpu.CompilerParams(dimension_semantics=("arbitrary",)),
        name="ctx_attention",
    )(cb, cb, cb)


def context_attention(q, k, v):
    s = jnp.einsum('bqhd,bkhd->bhqk', q, k).astype(jnp.float32) * (q.shape[-1] ** -0.5)
    p = jax.nn.softmax(s, -1).astype(v.dtype)
    return jnp.einsum('bhqk,bkhd->bqhd', p, v)


def hyena_filters(L, w1, b1, w2, b2, w3, decay):
    tn = jnp.arange(L, dtype=jnp.float32)[:, None] / L
    bands = jnp.arange(1, HY_BANDS + 1, dtype=jnp.float32)[None, :]
    ang = 2.0 * math.pi * bands * tn
    z = jnp.concatenate([tn, jnp.sin(ang), jnp.cos(ang)], -1)
    hmid = jnp.sin(HY_SIN_FREQ * (z @ w1.astype(jnp.float32) + b1.astype(jnp.float32)))
    hmid = jnp.sin(HY_SIN_FREQ * (hmid @ w2.astype(jnp.float32) + b2.astype(jnp.float32)))
    k = (hmid @ w3.astype(jnp.float32)) * jnp.exp(-tn * decay.astype(jnp.float32))
    k = k / (jnp.sum(jnp.abs(k), axis=0, keepdims=True) + 1e-6)
    return k[:, :GROUP_W], k[:, GROUP_W:]


def bidir_fftconv(u, k_fwd, k_bwd):
    L = u.shape[1]
    k2 = jnp.concatenate([k_fwd, jnp.zeros_like(k_fwd[:1]), k_bwd[1:][::-1]], 0)
    kf = jnp.fft.rfft(k2, n=2 * L, axis=0)
    uf = jnp.fft.rfft(u.astype(jnp.float32), n=2 * L, axis=1)
    return jnp.fft.irfft(uf * kf[None], n=2 * L, axis=1)[:, :L].astype(u.dtype)


def hyena(p, short_w, short_b, w1, b1, w2, b2, w3, decay, skip):
    L = p.shape[1]
    p = dwconv(p, short_w) + short_b
    x0, x1, v = jnp.split(p, 3, -1)
    k_fwd, k_bwd = hyena_filters(L, w1, b1, w2, b2, w3, decay)
    u = v * x1
    y = bidir_fftconv(u, k_fwd, k_bwd) + u * skip
    return y * x0


def segsum(a):
    T = a.shape[-1]
    a_rep = jnp.broadcast_to(a[..., None], a.shape + (T,))
    a_rep = jnp.where(jnp.tril(jnp.ones((T, T), bool), -1), a_rep, 0.0)
    s = jnp.cumsum(a_rep, axis=-2)
    return jnp.where(jnp.tril(jnp.ones((T, T), bool)), s, -jnp.inf)


def ssd_scan(x, dt, A, Bh, Ch, init_state, want_y):
    Bsz, L, H, P = x.shape
    N = Bh.shape[-1]
    nc = L // SSD_CHUNK
    xd = (x.astype(jnp.float32) * dt[..., None]).reshape(Bsz, nc, SSD_CHUNK, H, P)
    a = (dt * A).reshape(Bsz, nc, SSD_CHUNK, H).transpose(0, 3, 1, 2)
    Bc = Bh.astype(jnp.float32).reshape(Bsz, nc, SSD_CHUNK, H, N)
    Cc = Ch.astype(jnp.float32).reshape(Bsz, nc, SSD_CHUNK, H, N)
    a_cum = jnp.cumsum(a, -1)
    decay_states = jnp.exp(a_cum[..., -1:] - a_cum)
    states = jnp.einsum('bclhn,bhcl,bclhp->bchpn', Bc, decay_states, xd)
    states = jnp.concatenate([init_state[:, None], states], 1)
    decay_chunk = jnp.exp(segsum(jnp.pad(a_cum[..., -1], ((0, 0), (0, 0), (1, 0)))))
    new_states = jnp.einsum('bhzc,bchpn->bzhpn', decay_chunk, states)
    prev_states, final = new_states[:, :-1], new_states[:, -1]
    if not want_y:
        return None, final
    Lmat = jnp.exp(segsum(a))
    y_diag = jnp.einsum('bclhn,bcshn,bhcls,bcshp->bclhp', Cc, Bc, Lmat, xd)
    y_off = jnp.einsum('bclhn,bchpn,bhcl->bclhp', Cc, prev_states, jnp.exp(a_cum))
    return (y_diag + y_off).reshape(Bsz, L, H, P), final


def _flip(t):
    return jnp.flip(t, axis=1)


def ssd_bidir(z, xbc, dt_raw, init_f, init_b, want_y, conv_w, conv_b, a_log, dt_bias, d_skip, norm_g):
    Bsz, L, _ = xbc.shape
    xbc = jax.nn.silu(dwconv(xbc, conv_w) + conv_b)
    xs = xbc[..., :GROUP_W].reshape(Bsz, L, SSD_HEADS, SSD_HEAD_DIM)
    bc = xbc[..., GROUP_W:].reshape(Bsz, L, 2, SSD_GROUPS, SSD_STATE)
    rep = SSD_HEADS // SSD_GROUPS
    Bh = jnp.repeat(bc[:, :, 0], rep, axis=2)
    Ch = jnp.repeat(bc[:, :, 1], rep, axis=2)
    dt = jax.nn.softplus(dt_raw.astype(jnp.float32).reshape(Bsz, L, 2, SSD_HEADS) + dt_bias.astype(jnp.float32))
    A = -jnp.exp(a_log.astype(jnp.float32))
    y_f, s_f = ssd_scan(xs, dt[:, :, 0], A[0], Bh, Ch, init_f, want_y)
    y_b, s_b = ssd_scan(_flip(xs), _flip(dt[:, :, 1]), A[1], _flip(Bh), _flip(Ch), init_b, want_y)
    if not want_y:
        return None, s_f, s_b
    y = y_f + _flip(y_b) + xs.astype(jnp.float32) * d_skip.astype(jnp.float32)[:, None]
    yg = (y.reshape(Bsz, L, GROUP_W) * jax.nn.silu(z.astype(jnp.float32)))
    yg = yg.reshape(Bsz, L, SSD_GROUPS, GROUP_W // SSD_GROUPS)
    yg = yg * lax.rsqrt(jnp.mean(jnp.square(yg), -1, keepdims=True) + LN_EPS)
    return (yg.reshape(Bsz, L, GROUP_W) * norm_g).astype(z.dtype), s_f, s_b


_HY_TB = 256
_HY_CB = 8
_LANES = 128
_SUBLANES = 8


def _hyena_filter_body(L, w1_ref, b1_ref, w2_ref, b2_ref, w3_ref, dec_ref, o_ref):
    G = GROUP_W
    hp = lax.Precision.HIGHEST
    p = lax.broadcasted_iota(jnp.int32, (2 * L, _LANES), 0)
    lane = lax.broadcasted_iota(jnp.int32, (2 * L, _LANES), 1)
    tn = jnp.abs(p - L).astype(jnp.float32) / L
    band = ((lane - 1) % HY_BANDS + 1).astype(jnp.float32)
    ang = 2.0 * math.pi * band * tn
    z = jnp.where(lane == 0, tn, jnp.where(lane <= HY_BANDS, jnp.sin(ang), jnp.cos(ang)))
    z = jnp.where(lane < HY_EMB, z, 0.0)
    h = jnp.sin(HY_SIN_FREQ * (jnp.dot(z, w1_ref[...], precision=hp, preferred_element_type=jnp.float32) + b1_ref[...]))
    h = jnp.sin(HY_SIN_FREQ * (jnp.dot(h, w2_ref[...], precision=hp, preferred_element_type=jnp.float32) + b2_ref[...]))
    k = jnp.dot(h, w3_ref[...], precision=hp, preferred_element_type=jnp.float32) * jnp.exp(-tn[:, 0:1] * dec_ref[...])
    kf, kb = k[:, 0:G], k[:, G:2 * G]
    n = lax.broadcasted_iota(jnp.int32, (2 * L, G), 0) - L
    nf = jnp.sum(jnp.where(n >= 0, jnp.abs(kf), 0.0), axis=0, keepdims=True) + 1e-6
    nb = jnp.sum(jnp.where(n <= 0, jnp.where(n > -L, jnp.abs(kb), 0.0), 0.0), axis=0, keepdims=True) + 1e-6
    rev = jnp.where(n > 0, kb / nb, jnp.where(n > -L, kf / nf, 0.0))
    o_ref[...] = rev.T


def _hyena_filters_rev(L, w1, b1, w2, b2, w3, decay):
    G = GROUP_W
    hp = _LANES - HY_HIDDEN
    w1p = jnp.pad(w1.astype(jnp.float32), ((0, _LANES - HY_EMB), (0, hp)))
    args = (w1p, jnp.pad(b1.reshape(1, -1), ((0, 0), (0, hp))), jnp.pad(w2, ((0, hp), (0, hp))),
            jnp.pad(b2.reshape(1, -1), ((0, 0), (0, hp))), jnp.pad(w3, ((0, hp), (0, 0))), decay.reshape(1, -1))
    return pl.pallas_call(
        functools.partial(_hyena_filter_body, L),
        out_shape=jax.ShapeDtypeStruct((G, 2 * L), jnp.float32),
        compiler_params=pltpu.CompilerParams(vmem_limit_bytes=_VMEM_LIMIT),
        name="hyena_filters",
    )(*[a.astype(jnp.float32) for a in args])


def _hyena_pre_body(L, p_ref, w_ref, b_ref, u_ref, ut_ref, x0_ref, xp_ref):
    G = GROUP_W
    W = 3 * G
    halo = jnp.zeros((_CONV_HALO, W), jnp.float32)
    xp_ref[0:_CONV_HALO, :] = halo
    xp_ref[_CONV_HALO + L:_CONV_HALO + L + _CONV_HALO, :] = halo
    xp_ref[_CONV_HALO:_CONV_HALO + L, :] = p_ref[0]
    pad = (HY_SHORT - 1) // 2
    tile = min(_CONV_TILE, L)
    for t in range(L // tile):
        win = xp_ref[t * tile:t * tile + tile + 2 * _CONV_HALO, :]
        acc = jnp.zeros((tile, W), jnp.float32) + b_ref[...]
        for k, tap in enumerate(_shifted_taps(win[_CONV_HALO - pad:], HY_SHORT, tile)):
            acc = acc + tap * w_ref[k:k + 1, :]
        rows = slice(t * tile, (t + 1) * tile)
        u = acc[:, 2 * G:3 * G] * acc[:, G:2 * G]
        x0_ref[0, rows, :] = acc[:, 0:G]
        u_ref[0, rows, :] = u
        ut_ref[0, :, rows] = u.T


def _hyena_pre(py, short_w, short_b):
    Bsz, L, W = py.shape
    G = GROUP_W
    f32 = jnp.float32
    return pl.pallas_call(
        functools.partial(_hyena_pre_body, L),
        grid=(Bsz,),
        in_specs=[pl.BlockSpec((1, L, W), lambda b: (b, 0, 0)),
                  pl.BlockSpec((HY_SHORT, W), lambda b: (0, 0)),
                  pl.BlockSpec((1, W), lambda b: (0, 0))],
        out_specs=[pl.BlockSpec((1, L, G), lambda b: (b, 0, 0)),
                   pl.BlockSpec((1, G, L), lambda b: (b, 0, 0)),
                   pl.BlockSpec((1, L, G), lambda b: (b, 0, 0))],
        out_shape=[jax.ShapeDtypeStruct((Bsz, L, G), f32), jax.ShapeDtypeStruct((Bsz, G, L), f32),
                   jax.ShapeDtypeStruct((Bsz, L, G), f32)],
        scratch_shapes=[pltpu.VMEM((L + 2 * _CONV_HALO, W), f32)],
        compiler_params=pltpu.CompilerParams(dimension_semantics=("arbitrary",), vmem_limit_bytes=_VMEM_LIMIT),
        name="hyena_pre",
    )(py, short_w, short_b.reshape(1, W))


def _toeplitz_tile(w8):
    TB = _HY_TB
    nq = 2 * TB // _LANES
    per = _LANES // _SUBLANES
    i = lax.broadcasted_iota(jnp.int32, (_SUBLANES, _LANES), 0)
    l = lax.broadcasted_iota(jnp.int32, (_SUBLANES, _LANES), 1)
    rolled = [[pltpu.roll(w8[:, q * _LANES:(q + 1) * _LANES], (_SUBLANES * k) % _LANES, 1, stride=1, stride_axis=0)
               for k in range(per)] for q in range(nq)]
    row_blocks = []
    for rg in range(TB // _SUBLANES):
        k = rg % per
        pieces = []
        for lg in range(TB // _LANES):
            o = TB + _LANES * lg - _SUBLANES * rg
            q, rho = divmod(o, _LANES)
            if rho == 0:
                pieces.append(jnp.where(l - i < 0, rolled[q - 1][k], rolled[q][k]))
            else:
                pieces.append(jnp.where(l + rho - i >= _LANES, rolled[q + 1][k], rolled[q][k]))
        row_blocks.append(jnp.concatenate(pieces, axis=1))
    return jnp.concatenate(row_blocks, axis=0).astype(jnp.bfloat16)


def _hyena_conv_body(L, Bsz, g_ref, u_ref, o_ref):
    TB = min(_HY_TB, L)
    nb = L // TB
    cols = Bsz * nb
    lane = lax.broadcasted_iota(jnp.int32, (TB, cols), 1) % nb

    def channel(ci, carry):
        u = u_ref[ci] if nb == 1 else u_ref[:, ci].reshape(cols, TB)
        u = u.astype(jnp.bfloat16)
        acc = jnp.zeros((TB, cols), jnp.float32)
        for d in range(-(nb - 1), nb):
            start = L - TB * d - TB
            w8 = jnp.broadcast_to(g_ref[pl.ds(ci, 1), start:start + 2 * TB], (_SUBLANES, 2 * TB))
            z = lax.dot_general(_toeplitz_tile(w8), u, _NT, preferred_element_type=jnp.float32)
            if d != 0:
                z = jnp.where((lane - d >= 0) & (lane - d < nb), pltpu.roll(z, d % cols, 1), 0.0)
            acc = acc + z
        if nb == 1:
            o_ref[ci] = acc.T
        else:
            o_ref[:, ci] = acc.T.reshape(Bsz, nb, TB)
        return carry

    lax.fori_loop(0, _HY_CB, channel, 0)


def _hyena_conv(g_rev, u_t):
    Bsz, G, L = u_t.shape
    TB = min(_HY_TB, L)
    nb = L // TB
    if nb == 1:
        blk = pl.BlockSpec((_HY_CB, Bsz, TB), lambda c: (c, 0, 0))
        operand, out_shape = jnp.swapaxes(u_t, 0, 1), (G, Bsz, TB)
    else:
        blk = pl.BlockSpec((Bsz, _HY_CB, nb, TB), lambda c: (0, c, 0, 0))
        operand, out_shape = u_t.reshape(Bsz, G, nb, TB), (Bsz, G, nb, TB)
    out = pl.pallas_call(
        functools.partial(_hyena_conv_body, L, Bsz),
        grid=(G // _HY_CB,),
        in_specs=[pl.BlockSpec((_HY_CB, 2 * L), lambda c: (c, 0)), blk],
        out_specs=blk,
        out_shape=jax.ShapeDtypeStruct(out_shape, jnp.float32),
        compiler_params=pltpu.CompilerParams(dimension_semantics=("arbitrary",), vmem_limit_bytes=_VMEM_LIMIT),
        name="hyena_conv",
    )(g_rev, operand)
    return jnp.swapaxes(out, 0, 1) if nb == 1 else out.reshape(Bsz, G, L)


def _hyena_post_body(yt_ref, u_ref, x0_ref, skip_ref, o_ref):
    o_ref[0] = (yt_ref[0].T + u_ref[0] * skip_ref[...]) * x0_ref[0]


def _hyena_post(y_t, u, x0, skip):
    Bsz, L, G = u.shape
    tok = pl.BlockSpec((1, L, G), lambda b: (b, 0, 0))
    return pl.pallas_call(
        _hyena_post_body,
        grid=(Bsz,),
        in_specs=[pl.BlockSpec((1, G, L), lambda b: (b, 0, 0)), tok, tok, pl.BlockSpec((1, G), lambda b: (0, 0))],
        out_specs=tok,
        out_shape=jax.ShapeDtypeStruct((Bsz, L, G), jnp.float32),
        compiler_params=pltpu.CompilerParams(dimension_semantics=("arbitrary",), vmem_limit_bytes=_VMEM_LIMIT),
        name="hyena_post",
    )(y_t, u, x0, skip.reshape(1, G))


def _hyena(py, short_w, short_b, w1, b1, w2, b2, w3, decay, skip):
    L = py.shape[1]
    g_rev = _hyena_filters_rev(L, w1, b1, w2, b2, w3, decay)
    u, u_t, x0 = _hyena_pre(py, short_w, short_b)
    return _hyena_post(_hyena_conv(g_rev, u_t), u, x0, skip)


def _split3_dot(a, b_bf16, dims=None):
    hi = a.astype(jnp.bfloat16)
    r1 = a - hi.astype(jnp.float32)
    mid = r1.astype(jnp.bfloat16)
    lo = (r1 - mid.astype(jnp.float32)).astype(jnp.bfloat16)
    out = 0.0
    for part in (hi, mid, lo):
        if dims is None:
            out = out + jnp.dot(part, b_bf16, preferred_element_type=jnp.float32)
        else:
            out = out + jnp.dot(b_bf16, part, preferred_element_type=jnp.float32)
    return out


def _softplus(x):
    return jnp.maximum(x, 0.0) + jnp.log(1.0 + jnp.exp(-jnp.abs(x)))


def _ssd_body(L, zx_ref, dtc_ref, dtr_ref, cw_ref, cb_ref, arow_ref, acol_ref, brow_ref, bcol_ref, dsk_ref,
              ng_ref, init_ref, o_ref, fin_ref, xp_ref, xc_ref, bt_ref, y_ref, ccol_ref, crow_ref, edec_ref,
              tot_ref):
    G = GROUP_W
    Q = SSD_CHUNK
    nc = L // Q
    P = SSD_HEAD_DIM
    N = SSD_STATE
    H = SSD_HEADS
    f32 = jnp.float32
    bf16 = jnp.bfloat16
    halo = jnp.zeros((_CONV_HALO, XBC_W), f32)
    xp_ref[0:_CONV_HALO, :] = halo
    xp_ref[_CONV_HALO + L:_CONV_HALO + L + _CONV_HALO, :] = halo
    xp_ref[_CONV_HALO:_CONV_HALO + L, :] = zx_ref[0, :, G:G + XBC_W]
    pad = (SSD_CONV - 1) // 2
    tile = min(_CONV_TILE, L)
    for t in range(L // tile):
        win = xp_ref[t * tile:t * tile + tile + 2 * _CONV_HALO, :]
        acc = jnp.zeros((tile, XBC_W), f32) + cb_ref[...]
        for k, tap in enumerate(_shifted_taps(win[_CONV_HALO - pad:], SSD_CONV, tile)):
            acc = acc + tap * cw_ref[k:k + 1, :]
        xc_ref[t * tile:(t + 1) * tile, :] = acc * jax.nn.sigmoid(acc)
    for c in range(nc):
        bt_ref[c] = xc_ref[c * Q:(c + 1) * Q, G:G + SSD_GROUPS * N].T
    dt_col = _softplus(dtc_ref[0] + brow_ref[...])
    a_col = dt_col * arow_ref[...]
    a_row = _softplus(dtr_ref[0] + bcol_ref[...]) * acol_ref[...]
    a_stack = jnp.concatenate([a_row[:, c * Q:(c + 1) * Q] for c in range(nc)], axis=0)
    ri = lax.broadcasted_iota(jnp.int32, (Q, Q), 0)
    ci = lax.broadcasted_iota(jnp.int32, (Q, Q), 1)
    one = lambda m: jnp.where(m, 1.0, 0.0).astype(bf16)
    tot_ref[...] = _split3_dot(a_stack, jnp.ones((Q, Q), bf16))

    def direction(d, y_store):
        fwd = d == 0
        m_col = one(ci <= ri) if fwd else one(ci >= ri)
        for c in range(nc):
            ccol_ref[c * Q:(c + 1) * Q, :] = _split3_dot(a_col[c * Q:(c + 1) * Q, :], m_col, dims="left")
        crow_ref[...] = _split3_dot(a_stack, one(ri <= ci) if fwd else one(ri >= ci))
        edec_ref[...] = _split3_dot(a_stack, one(ri > ci) if fwd else one(ri < ci))
        keep = (ri >= ci) if fwd else (ri <= ci)

        def chunk(step, states):
            c = step if fwd else nc - 1 - step
            r0 = pl.multiple_of(c * Q, Q)
            j0 = pl.multiple_of(c * 2 * H, 2 * H)
            xc = xc_ref[pl.ds(r0, Q), :]
            ccol = ccol_ref[pl.ds(r0, Q), :]
            crow = crow_ref[pl.ds(j0, 2 * H), :]
            edec = edec_ref[pl.ds(j0, 2 * H), :]
            tot = tot_ref[pl.ds(j0, 2 * H), :]
            dtc = _softplus(dtc_ref[0, pl.ds(r0, Q), :] + brow_ref[...])
            bt = bt_ref[c]
            new_states = []
            outs = []
            for g in range(SSD_GROUPS):
                cm = xc[:, G + SSD_GROUPS * N + g * N:G + SSD_GROUPS * N + (g + 1) * N].astype(bf16)
                bm = xc[:, G + g * N:G + (g + 1) * N].astype(bf16)
                cb = lax.dot_general(cm, bm, _NT, preferred_element_type=f32)
                for hh in range(H // SSD_GROUPS):
                    h = g * (H // SSD_GROUPS) + hh
                    j = d * H + h
                    col = jnp.broadcast_to(ccol[:, j:j + 1], (Q, Q))
                    lmat = jnp.exp(jnp.where(keep, col - crow[j:j + 1, :], _NEG))
                    xd = (xc[:, h * P:(h + 1) * P] * jnp.broadcast_to(dtc[:, j:j + 1], (Q, P))).astype(bf16)
                    st = states[h]
                    y = jnp.dot((cb * lmat).astype(bf16), xd, preferred_element_type=f32)
                    y = y + jnp.dot(cm, st.astype(bf16), preferred_element_type=f32) * jnp.exp(col[:, 0:P])
                    outs.append(y)
                    btd = (bt[g * N:(g + 1) * N, :] * jnp.exp(edec[j:j + 1, :])).astype(bf16)
                    new_states.append(jnp.exp(tot[j:j + 1, 0:P]) * st
                                      + jnp.dot(btd, xd, preferred_element_type=f32))
            y_store(r0, jnp.concatenate(outs, axis=-1))
            return tuple(new_states)

        init = tuple(init_ref[0, d, h] for h in range(H))
        final = lax.fori_loop(0, nc, chunk, init)
        for h in range(H):
            fin_ref[0, d, h] = final[h]

    def store_fwd(r0, y):
        y_ref[pl.ds(r0, Q), :] = y

    def store_bwd(r0, y):
        y_ref[pl.ds(r0, Q), :] += y

    direction(0, store_fwd)
    direction(1, store_bwd)
    gw = G // SSD_GROUPS
    for t in range(L // tile):
        rows = slice(t * tile, (t + 1) * tile)
        z = zx_ref[0, rows, 0:G]
        yg = (y_ref[rows, :] + xc_ref[rows, 0:G] * dsk_ref[...]) * (z * jax.nn.sigmoid(z))
        parts = []
        for g in range(SSD_GROUPS):
            v = yg[:, g * gw:(g + 1) * gw]
            parts.append(v * lax.rsqrt(jnp.mean(v * v, -1, keepdims=True) + LN_EPS))
        o_ref[0, rows, :] = jnp.concatenate(parts, axis=-1) * ng_ref[...]


def _ssd(pzx, pdt, init, conv_w, conv_b, a_log, dt_bias, d_skip, norm_g):
    Bsz, L, _ = pzx.shape
    G, H, Q = GROUP_W, SSD_HEADS, SSD_CHUNK
    lanes = pdt.shape[-1]
    nc = L // Q
    neg_a = -jnp.exp(a_log.astype(jnp.float32)).reshape(1, 2 * H)
    a_rowv = jnp.pad(neg_a, ((0, 0), (0, lanes - 2 * H)))
    b_rowv = jnp.pad(dt_bias.astype(jnp.float32).reshape(1, 2 * H), ((0, 0), (0, lanes - 2 * H)))
    dt_rows = jnp.swapaxes(pdt[:, :, :2 * H], 1, 2)
    d_lane = jnp.repeat(d_skip.astype(jnp.float32), SSD_HEAD_DIM).reshape(1, G)
    const = lambda shape: pl.BlockSpec(shape, lambda b: (0,) * len(shape))
    st_spec = pl.BlockSpec((1, 2, H, SSD_STATE, SSD_HEAD_DIM), lambda b: (b, 0, 0, 0, 0))
    f32 = jnp.float32
    return pl.pallas_call(
        functools.partial(_ssd_body, L),
        grid=(Bsz,),
        in_specs=[pl.BlockSpec((1, L, G + XBC_W), lambda b: (b, 0, 0)),
                  pl.BlockSpec((1, L, lanes), lambda b: (b, 0, 0)),
                  pl.BlockSpec((1, 2 * H, L), lambda b: (b, 0, 0)),
                  const((SSD_CONV, XBC_W)), const((1, XBC_W)),
                  const((1, lanes)), const((2 * H, 1)), const((1, lanes)), const((2 * H, 1)),
                  const((1, G)), const((1, G)), st_spec],
        out_specs=[pl.BlockSpec((1, L, G), lambda b: (b, 0, 0)), st_spec],
        out_shape=[jax.ShapeDtypeStruct((Bsz, L, G), f32),
                   jax.ShapeDtypeStruct((Bsz, 2, H, SSD_STATE, SSD_HEAD_DIM), f32)],
        scratch_shapes=[pltpu.VMEM((L + 2 * _CONV_HALO, XBC_W), f32),
                        pltpu.VMEM((L, XBC_W), f32),
                        pltpu.VMEM((nc, SSD_GROUPS * SSD_STATE, Q), f32),
                        pltpu.VMEM((L, G), f32),
                        pltpu.VMEM((L, lanes), f32),
                        pltpu.VMEM((nc * 2 * H, Q), f32),
                        pltpu.VMEM((nc * 2 * H, Q), f32),
                        pltpu.VMEM((nc * 2 * H, Q), f32)],
        compiler_params=pltpu.CompilerParams(dimension_semantics=("arbitrary",),
                                             vmem_limit_bytes=_VMEM_LIMIT),
        name="ssd",
    )(pzx, pdt, dt_rows, conv_w, conv_b.reshape(1, XBC_W), a_rowv, neg_a.reshape(2 * H, 1),
      b_rowv, dt_bias.astype(f32).reshape(2 * H, 1), d_lane, norm_g.reshape(1, G), init)


def _mixer_ln(x, xc, mod, mod_c, ctx_out, alpha, w_in, w_out, ln_g, ln_b, conf, rpb, hy, ssd):
    Bsz = x.shape[0]
    sh, sc, g = mod
    shc, scc, gc = mod_c
    pa, pb, py, pzx, pdt = _in_proj(x, sc, sh, w_in)
    ca, cb, cy, czx, cdt = _in_proj(xc, scc, shc, w_in)
    zero = jnp.zeros((Bsz, 2, SSD_HEADS, SSD_STATE, SSD_HEAD_DIM), jnp.float32)
    y_dc, ctx_states = _ssd(czx, cdt, zero, *ssd)
    y_d, _ = _ssd(pzx, pdt, ctx_states, *ssd)
    ys = [_conformer(pa, *conf), _na_attention(pb, cb, rpb), _hyena(py, *hy), y_d]
    x_new = _out_proj_ln(ys, w_out, x, g, ln_g, ln_b, alpha)
    if not ctx_out:
        return x_new, None
    ycs = [_conformer(ca, *conf), _ctx_attention(cb), _hyena(cy, *hy), y_dc]
    return x_new, _out_proj_ln(ycs, w_out, xc, gc, ln_g, ln_b, alpha)


_NEG = -1e30
_ROUTE_TT = 256
_DENSE_TT = 512
_DENSE_EC = 1024
_NT = (((1,), (1,)), ((), ()))


def _bf16_round(x):
    return x.astype(jnp.bfloat16).astype(jnp.float32)


def _top16_rows(s):
    rows = []
    stacked = jnp.zeros((PEER_TOPK, s.shape[1]), jnp.float32)
    ridx = lax.broadcasted_iota(jnp.int32, stacked.shape, 0)
    cur = s
    for k in range(PEER_TOPK):
        m = jnp.max(cur, axis=0, keepdims=True)
        rows.append(m)
        stacked = jnp.where(ridx == k, m, stacked)
        if k + 1 < PEER_TOPK:
            cur = jnp.where(cur >= m, _NEG, cur)
    return rows, stacked


def _pair_candidates(r1, V1, r2, V2, op, fill):
    half = V1[0:8]
    keep = lax.broadcasted_iota(jnp.int32, half.shape, 0) >= 4
    pieces = [op(r1[0], V2), op(r1[1], V2[0:8]), op(r1[2], V2[0:8]), op(r1[3], V2[0:8]), op(r2[0], V1[8:16])]
    for b in range(3):
        pieces.append(jnp.where(keep, op(r2[b], half), fill))
    return jnp.concatenate(pieces, axis=0)


def _peer_route_body(x_ref, sc_ref, sh_ref, wqT_ref, keys_ref, hm_ref, e1_ref, e2_ref, pthr_ref, qT_ref):
    hm = (x_ref[0] * (1.0 + sc_ref[0]) + sh_ref[0]).astype(jnp.bfloat16)
    hm_ref[0] = hm
    qT_ref[...] = lax.dot_general(wqT_ref[...], hm, _NT, preferred_element_type=jnp.float32)

    def head(h, carry):
        base = pl.multiple_of(h * PEER_QDIM, PEER_QDIM)
        half_q = PEER_QDIM // 2
        s_both = []
        for p in range(2):
            qb = qT_ref[pl.ds(base + p * half_q, half_q), :].astype(jnp.bfloat16)
            s_both.append(jnp.dot(keys_ref[h, p], qb, preferred_element_type=jnp.float32))
        for half in range(_ROUTE_TT // 128):
            lanes = slice(half * 128, (half + 1) * 128)
            s1 = s_both[0][:, lanes]
            s2 = s_both[1][:, lanes]
            r1, V1 = _top16_rows(s1)
            r2, V2 = _top16_rows(s2)
            cand = _pair_candidates(r1, V1, r2, V2, lambda a, b: a + b, _NEG)
            cur = cand
            top = r1[0] + r2[0]
            z = jnp.ones_like(top)
            m = top
            for k in range(1, PEER_TOPK):
                cur = jnp.where(cur >= m, _NEG, cur)
                m = jnp.max(cur, axis=0, keepdims=True)
                z = z + jnp.exp(m - top)
            rz = 1.0 / z
            e1r = [_bf16_round(jnp.exp(r - r1[0]) * rz) for r in r1]
            e2r = [_bf16_round(jnp.exp(r - r2[0])) for r in r2]
            E1 = _bf16_round(jnp.exp(V1 - r1[0]) * rz)
            E2 = _bf16_round(jnp.exp(V2 - r2[0]))
            prod = _pair_candidates(e1r, E1, e2r, E2, lambda a, b: _bf16_round(a * b), 0.0)
            pthr = jnp.min(jnp.where(cand >= m, prod, 1e30), axis=0, keepdims=True)
            e1_ref[h, :, lanes] = jnp.exp(s1 - r1[0]) * rz
            e2_ref[h, :, lanes] = (jnp.exp(s2 - r2[0])).astype(jnp.bfloat16)
            pthr_ref[h, :, lanes] = pthr
        return carry

    lax.fori_loop(0, PEER_HEADS, head, 0)


def _peer_route(x, sc, sh, wqT, keys):
    Bsz, S, D = x.shape
    assert S % _ROUTE_TT == 0, S
    nt = S // _ROUTE_TT
    T = Bsz * S
    tab = jax.ShapeDtypeStruct((PEER_HEADS, PEER_KEYS, T), jnp.float32)
    return pl.pallas_call(
        _peer_route_body,
        grid=(Bsz, nt),
        in_specs=[pl.BlockSpec((1, _ROUTE_TT, D), lambda b, i: (b, i, 0)),
                  pl.BlockSpec((1, 1, D), lambda b, i: (b, 0, 0)),
                  pl.BlockSpec((1, 1, D), lambda b, i: (b, 0, 0)),
                  pl.BlockSpec(wqT.shape, lambda b, i: (0, 0)),
                  pl.BlockSpec(keys.shape, lambda b, i: (0, 0, 0, 0))],
        out_specs=[pl.BlockSpec((1, _ROUTE_TT, D), lambda b, i: (b, i, 0)),
                   pl.BlockSpec((PEER_HEADS, PEER_KEYS, _ROUTE_TT), lambda b, i: (0, 0, b * nt + i)),
                   pl.BlockSpec((PEER_HEADS, PEER_KEYS, _ROUTE_TT), lambda b, i: (0, 0, b * nt + i)),
                   pl.BlockSpec((PEER_HEADS, 1, _ROUTE_TT), lambda b, i: (0, 0, b * nt + i))],
        out_shape=[jax.ShapeDtypeStruct((Bsz, S, D), jnp.bfloat16), tab,
                   jax.ShapeDtypeStruct(tab.shape, jnp.bfloat16),
                   jax.ShapeDtypeStruct((PEER_HEADS, 1, T), jnp.float32)],
        scratch_shapes=[pltpu.VMEM((PEER_HEADS * PEER_QDIM, _ROUTE_TT), jnp.float32)],
        compiler_params=pltpu.CompilerParams(dimension_semantics=("arbitrary", "arbitrary"),
                                             vmem_limit_bytes=_VMEM_LIMIT),
        name="peer_route",
    )(x, sc, sh, wqT, keys)


_GELU_K = math.sqrt(2.0 / math.pi)


def _gelu_tanh(x):
    half = 0.5 * x
    return half + half * jnp.tanh(x * (_GELU_K + (_GELU_K * 0.044715) * (x * x)))


_PACK = 16


def _peer_dense_body(alpha, hm_ref, e1_ref, e2_ref, pthr_ref, u_ref, vt_ref, x_ref, g_ref, lng_ref, lnb_ref,
                     o_ref, acc_ref, wt_ref, e1b_ref, pthrb_ref):
    c = pl.program_id(2)
    jrows = 32
    bf16 = jnp.bfloat16
    th = _DENSE_TT // 2

    @pl.when(c == 0)
    def _():
        acc_ref[...] = jnp.zeros_like(acc_ref)
        for h in range(PEER_HEADS):
            pthrb_ref[h] = jnp.broadcast_to(pthr_ref[h], (_PACK, _DENSE_TT)).astype(bf16)

    halves = [slice(t * th, (t + 1) * th) for t in range(2)]
    acts = [lax.dot_general(u_ref[...], hm_ref[0, toks, :], _NT, preferred_element_type=jnp.float32)
            for toks in halves]
    for t, toks in enumerate(halves):
        for ii in range(_DENSE_EC // PEER_KEYS):
            for h in range(PEER_HEADS):
                e1b_ref[ii % 2, h] = jnp.broadcast_to(e1_ref[h, ii:ii + 1, toks], (_PACK, th)).astype(bf16)
            for jb in range(PEER_KEYS // jrows):
                r0 = ii * PEER_KEYS + jb * jrows
                gate = jnp.zeros((jrows // _PACK, _PACK, th), bf16)
                for h in range(PEER_HEADS):
                    e2 = e2_ref[h, jb * jrows:(jb + 1) * jrows, toks].reshape(jrows // _PACK, _PACK, th)
                    val = e2 * e1b_ref[ii % 2, h][None]
                    gate = gate + jnp.where(val >= pthrb_ref[h, :, toks][None], val, jnp.zeros_like(val))
                gel = _gelu_tanh(acts[t][r0:r0 + jrows, :]).astype(bf16)
                wt_ref[r0:r0 + jrows, toks] = gate.reshape(jrows, th) * gel
        acc_ref[:, toks] += jnp.dot(vt_ref[...], wt_ref[:, toks], preferred_element_type=jnp.float32)

    @pl.when(c == pl.num_programs(2) - 1)
    def _():
        y = alpha * x_ref[0] + g_ref[0] * acc_ref[...].T
        mu = jnp.mean(y, -1, keepdims=True)
        yc = y - mu
        var = jnp.mean(yc * yc, -1, keepdims=True)
        o_ref[0] = yc * lax.rsqrt(var + LN_EPS) * lng_ref[...] + lnb_ref[...]


def _peer_dense(hm, e1, e2, pthr, u_bf, vt_bf, x, g, ln_g, ln_b, alpha):
    Bsz, S, D = x.shape
    assert S % _DENSE_TT == 0, S
    nt = S // _DENSE_TT
    nchunk = N_EXPERTS // _DENSE_EC
    rows_i = _DENSE_EC // PEER_KEYS
    return pl.pallas_call(
        functools.partial(_peer_dense_body, alpha),
        grid=(Bsz, nt, nchunk),
        in_specs=[pl.BlockSpec((1, _DENSE_TT, D), lambda b, i, c: (b, i, 0)),
                  pl.BlockSpec((PEER_HEADS, rows_i, _DENSE_TT), lambda b, i, c: (0, c, b * nt + i)),
                  pl.BlockSpec((PEER_HEADS, PEER_KEYS, _DENSE_TT), lambda b, i, c: (0, 0, b * nt + i)),
                  pl.BlockSpec((PEER_HEADS, 1, _DENSE_TT), lambda b, i, c: (0, 0, b * nt + i)),
                  pl.BlockSpec((_DENSE_EC, D), lambda b, i, c: (c, 0)),
                  pl.BlockSpec((D, _DENSE_EC), lambda b, i, c: (0, c)),
                  pl.BlockSpec((1, _DENSE_TT, D), lambda b, i, c: (b, i, 0)),
                  pl.BlockSpec((1, 1, D), lambda b, i, c: (b, 0, 0)),
                  pl.BlockSpec((1, D), lambda b, i, c: (0, 0)),
                  pl.BlockSpec((1, D), lambda b, i, c: (0, 0))],
        out_specs=pl.BlockSpec((1, _DENSE_TT, D), lambda b, i, c: (b, i, 0)),
        out_shape=jax.ShapeDtypeStruct((Bsz, S, D), jnp.float32),
        scratch_shapes=[pltpu.VMEM((D, _DENSE_TT), jnp.float32),
                        pltpu.VMEM((_DENSE_EC, _DENSE_TT), jnp.bfloat16),
                        pltpu.VMEM((2, PEER_HEADS, _PACK, _DENSE_TT // 2), jnp.bfloat16),
                        pltpu.VMEM((PEER_HEADS, _PACK, _DENSE_TT), jnp.bfloat16)],
        compiler_params=pltpu.CompilerParams(dimension_semantics=("arbitrary", "arbitrary", "arbitrary"),
                                             vmem_limit_bytes=_VMEM_LIMIT),
        name="peer_dense",
    )(hm, e1, e2, pthr, u_bf, vt_bf, x, g, ln_g.reshape(1, D), ln_b.reshape(1, D))


def _peer_weights(wq, sub_keys, u_tab, v_tab):
    return (wq.T.astype(jnp.bfloat16), sub_keys.astype(jnp.bfloat16),
            u_tab.astype(jnp.bfloat16), v_tab.T.astype(jnp.bfloat16))


def _peer_ln(x, sc, sh, g, pw, ln_g, ln_b, alpha):
    wqT, keys, u_bf, vt_bf = pw
    hm, e1, e2, pthr = _peer_route(x, sc, sh, wqT, keys)
    return _peer_dense(hm, e1, e2, pthr, u_bf, vt_bf, x, g, ln_g, ln_b, alpha)


def kernel(x, c, ctx, c_ctx, w_ada, b_ada, w_in, w_out, ln1_g, ln1_b, ln2_g, ln2_b,
           conf_dw_w, conf_dw_b, conf_norm_g, conf_norm_b, na_rpb, hy_short_w, hy_short_b,
           hy_w1, hy_b1, hy_w2, hy_b2, hy_w3, hy_decay, hy_bias, ssd_conv_w, ssd_conv_b,
           ssd_a_log, ssd_dt_bias, ssd_d, ssd_norm_g, peer_wq, peer_keys, peer_u, peer_v):
    alpha = (2.0 * DEPTH) ** 0.25
    s_c = jax.nn.silu(c)
    s_cc = jax.nn.silu(c_ctx)
    xc = ctx
    Bsz, Lc, D = ctx.shape
    for l in range(DEPTH):
        ctx_out = l < DEPTH - 1
        mod = (s_c @ w_ada[l] + b_ada[l])[:, None, :]
        mod_c = jnp.broadcast_to((s_cc @ w_ada[l] + b_ada[l])[None, None, :], mod.shape)
        sh1, sc1, g1, sh2, sc2, g2 = jnp.split(mod, 6, -1)
        sh1c, sc1c, g1c, sh2c, sc2c, g2c = jnp.split(mod_c, 6, -1)
        x, xc = _mixer_ln(
            x, xc, (sh1, sc1, g1), (sh1c, sc1c, g1c), ctx_out, alpha, w_in[l], w_out[l], ln1_g[l], ln1_b[l],
            (conf_dw_w[l], conf_dw_b[l], conf_norm_g[l], conf_norm_b[l]), na_rpb[l],
            (hy_short_w[l], hy_short_b[l], hy_w1[l], hy_b1[l], hy_w2[l], hy_b2[l], hy_w3[l], hy_decay[l], hy_bias[l]),
            (ssd_conv_w[l], ssd_conv_b[l], ssd_a_log[l], ssd_dt_bias[l], ssd_d[l], ssd_norm_g[l]))
        pw = _peer_weights(peer_wq[l], peer_keys[l], peer_u[l], peer_v[l])
        x = _peer_ln(x, sc2, sh2, g2, pw, ln2_g[l], ln2_b[l], alpha)
        if ctx_out:
            xc = _peer_ln(xc.reshape(1, Bsz * Lc, D), sc2c[:1], sh2c[:1], g2c[:1], pw, ln2_g[l], ln2_b[l],
                          alpha).reshape(Bsz, Lc, D)
    return x
```

```python
import functools
import math
import jax, jax.numpy as jnp
from jax import lax
import numpy as np
from jax.experimental import pallas as pl
from jax.experimental.pallas import tpu as pltpu

D_MODEL = 1024
BATCH = 16
SEQ = 2048
DEPTH = 2

GRID_W = 64
CTX_LEN = 256
N_MIXERS = 4
GROUP_W = D_MODEL // N_MIXERS
D_MIX = N_MIXERS * GROUP_W
LN_EPS = 1e-5
CONF_K = 31
CONF_GROUPS = 4
NA_HEADS = 4
HEAD_DIM = GROUP_W // NA_HEADS
NA_KH = 8
NA_KW = 16
ROPE_BASE = 10000.0
HY_SHORT = 3
HY_BANDS = 16
HY_EMB = 1 + 2 * HY_BANDS
HY_HIDDEN = 64
HY_SIN_FREQ = 1.0
SSD_HEADS = 4
SSD_HEAD_DIM = GROUP_W // SSD_HEADS
SSD_GROUPS = 2
SSD_STATE = 64
SSD_CONV = 3
SSD_CHUNK = 128
XBC_W = GROUP_W + 2 * SSD_GROUPS * SSD_STATE
IN_COLS = 2 * GROUP_W + 3 * GROUP_W + 3 * GROUP_W + GROUP_W + XBC_W + 2 * SSD_HEADS
PEER_HEADS = 8
PEER_KEYS = 128
PEER_TOPK = 16
PEER_QDIM = 256
N_EXPERTS = PEER_KEYS * PEER_KEYS
PEER_BLOCK = 128

_VMEM_LIMIT = 56 * 1024 * 1024
_IN_PAD = 2944


def _standardize(x):
    xf = x.astype(jnp.float32)
    mu = jnp.mean(xf, -1, keepdims=True)
    var = jnp.mean(jnp.square(xf - mu), -1, keepdims=True)
    return (xf - mu) * lax.rsqrt(var + LN_EPS)


def layer_norm(x, g, b):
    return (_standardize(x) * g + b).astype(x.dtype)


def dwconv(x, w):
    K, C = w.shape
    pad = (K - 1) // 2
    return lax.conv_general_dilated(x, w[:, None, :].astype(x.dtype), window_strides=(1,),
                                    padding=[(pad, pad)], dimension_numbers=('NWC', 'WIO', 'NWC'),
                                    feature_group_count=C)


def _split_cols(p):
    sizes = (2 * GROUP_W, 3 * GROUP_W, 3 * GROUP_W, GROUP_W, XBC_W, 2 * SSD_HEADS)
    points = [int(v) for v in np.cumsum(sizes)[:-1]]
    return jnp.split(p, points, axis=-1)


_IN_SPLITS = ((0, 2 * GROUP_W), (2 * GROUP_W, 5 * GROUP_W), (5 * GROUP_W, 8 * GROUP_W),
              (8 * GROUP_W, 9 * GROUP_W + XBC_W), (9 * GROUP_W + XBC_W, _IN_PAD))
_PROJ_TM = 512


def _in_proj_body(x_ref, sc_ref, sh_ref, w_ref, *o_refs):
    h = (x_ref[0] * (1.0 + sc_ref[0]) + sh_ref[0]).astype(jnp.bfloat16)
    for (lo, hi), o_ref in zip(_IN_SPLITS, o_refs):
        o_ref[0] = jnp.dot(h, w_ref[:, lo:hi], preferred_element_type=jnp.float32)


def _in_proj(x, sc, sh, w_in):
    Bsz, L, D = x.shape
    tm = min(_PROJ_TM, L)
    w = jnp.pad(w_in, ((0, 0), (0, _IN_PAD - IN_COLS))).astype(jnp.bfloat16)
    return pl.pallas_call(
        _in_proj_body,
        grid=(Bsz, L // tm),
        in_specs=[pl.BlockSpec((1, tm, D), lambda b, i: (b, i, 0)),
                  pl.BlockSpec((1, 1, D), lambda b, i: (b, 0, 0)),
                  pl.BlockSpec((1, 1, D), lambda b, i: (b, 0, 0)),
                  pl.BlockSpec((D, _IN_PAD), lambda b, i: (0, 0))],
        out_specs=[pl.BlockSpec((1, tm, hi - lo), lambda b, i: (b, i, 0)) for lo, hi in _IN_SPLITS],
        out_shape=[jax.ShapeDtypeStruct((Bsz, L, hi - lo), jnp.float32) for lo, hi in _IN_SPLITS],
        compiler_params=pltpu.CompilerParams(dimension_semantics=("arbitrary", "arbitrary"),
                                             vmem_limit_bytes=_VMEM_LIMIT),
        name="in_proj",
    )(x, sc, sh, w)


def _out_proj_ln_body(alpha, ya_ref, yb_ref, yc_ref, yd_ref, w_ref, x_ref, g_ref, lng_ref, lnb_ref, o_ref):
    y = 0.0
    for m, y_ref in enumerate((ya_ref, yb_ref, yc_ref, yd_ref)):
        y = y + jnp.dot(y_ref[0].astype(jnp.bfloat16), w_ref[m * GROUP_W:(m + 1) * GROUP_W, :],
                        preferred_element_type=jnp.float32)
    r = alpha * x_ref[0] + g_ref[0] * y
    mu = jnp.mean(r, -1, keepdims=True)
    rc = r - mu
    var = jnp.mean(rc * rc, -1, keepdims=True)
    o_ref[0] = rc * lax.rsqrt(var + LN_EPS) * lng_ref[...] + lnb_ref[...]


def _out_proj_ln(ys, w_out, x, g, ln_g, ln_b, alpha):
    Bsz, L, D = x.shape
    tm = min(_PROJ_TM, L)
    mix = pl.BlockSpec((1, tm, GROUP_W), lambda b, i: (b, i, 0))
    return pl.pallas_call(
        functools.partial(_out_proj_ln_body, alpha),
        grid=(Bsz, L // tm),
        in_specs=[mix, mix, mix, mix,
                  pl.BlockSpec((D_MIX, D), lambda b, i: (0, 0)),
                  pl.BlockSpec((1, tm, D), lambda b, i: (b, i, 0)),
                  pl.BlockSpec((1, 1, D), lambda b, i: (b, 0, 0)),
                  pl.BlockSpec((1, D), lambda b, i: (0, 0)),
                  pl.BlockSpec((1, D), lambda b, i: (0, 0))],
        out_specs=pl.BlockSpec((1, tm, D), lambda b, i: (b, i, 0)),
        out_shape=jax.ShapeDtypeStruct((Bsz, L, D), jnp.float32),
        compiler_params=pltpu.CompilerParams(dimension_semantics=("arbitrary", "arbitrary"),
                                             vmem_limit_bytes=_VMEM_LIMIT),
        name="out_proj_ln",
    )(*ys, w_out.astype(jnp.bfloat16), x, g, ln_g.reshape(1, D), ln_b.reshape(1, D))


def _shifted_taps(win, n_taps, rows):
    return [win[k:k + rows] for k in range(n_taps)]


_CONV_HALO = 16
_CONV_TILE = 256


def _group_mean_matrix(width, group):
    r = lax.broadcasted_iota(jnp.int32, (width, width), 0) // group
    c = lax.broadcasted_iota(jnp.int32, (width, width), 1) // group
    return jnp.where(r == c, 1.0 / group, 0.0).astype(jnp.float32)


def _conformer_body(L, p_ref, w_ref, b_ref, ng_ref, nb_ref, o_ref, u_ref):
    G = GROUP_W
    pad = (CONF_K - 1) // 2
    halo = jnp.zeros((_CONV_HALO, G), jnp.float32)
    u_ref[0:_CONV_HALO, :] = halo
    u_ref[_CONV_HALO + L:_CONV_HALO + L + _CONV_HALO, :] = halo
    u_ref[_CONV_HALO:_CONV_HALO + L, :] = p_ref[0, :, 0:G] * jax.nn.sigmoid(p_ref[0, :, G:2 * G])
    avg = _group_mean_matrix(G, G // CONF_GROUPS)
    tile = min(_CONV_TILE, L)
    for t in range(L // tile):
        win = u_ref[t * tile:t * tile + tile + 2 * _CONV_HALO, :]
        acc = jnp.zeros((tile, G), jnp.float32) + b_ref[...]
        for k, tap in enumerate(_shifted_taps(win[_CONV_HALO - pad:], CONF_K, tile)):
            acc = acc + tap * w_ref[k:k + 1, :]
        mu = jnp.dot(acc, avg, preferred_element_type=jnp.float32, precision=lax.Precision.HIGHEST)
        cen = acc - mu
        var = jnp.dot(cen * cen, avg, preferred_element_type=jnp.float32, precision=lax.Precision.HIGHEST)
        un = cen * lax.rsqrt(var + LN_EPS) * ng_ref[...] + nb_ref[...]
        o_ref[0, t * tile:(t + 1) * tile, :] = un * jax.nn.sigmoid(un)


def _conformer(pa, dw_w, dw_b, n_g, n_b):
    Bsz, L, _ = pa.shape
    G = GROUP_W
    vec = pl.BlockSpec((1, G), lambda b: (0, 0))
    return pl.pallas_call(
        functools.partial(_conformer_body, L),
        grid=(Bsz,),
        in_specs=[pl.BlockSpec((1, L, 2 * G), lambda b: (b, 0, 0)),
                  pl.BlockSpec((CONF_K, G), lambda b: (0, 0)), vec, vec, vec],
        out_specs=pl.BlockSpec((1, L, G), lambda b: (b, 0, 0)),
        out_shape=jax.ShapeDtypeStruct((Bsz, L, G), jnp.float32),
        scratch_shapes=[pltpu.VMEM((L + 2 * _CONV_HALO, G), jnp.float32)],
        compiler_params=pltpu.CompilerParams(dimension_semantics=("arbitrary",),
                                             vmem_limit_bytes=_VMEM_LIMIT),
        name="conformer",
    )(pa, dw_w, dw_b.reshape(1, G), n_g.reshape(1, G), n_b.reshape(1, G))


def conformer_conv(p, dw_w, dw_b, n_g, n_b):
    a, gate = jnp.split(p, 2, -1)
    u = a * jax.nn.sigmoid(gate)
    u = dwconv(u, dw_w) + dw_b
    Bsz, L, C = u.shape
    un = _standardize(u.reshape(Bsz, L, CONF_GROUPS, C // CONF_GROUPS)).reshape(Bsz, L, C)
    un = (un * n_g + n_b).astype(u.dtype)
    return jax.nn.silu(un)


def axial_rope(rows, head_dim):
    n_f = head_dim // 4
    inv = ROPE_BASE ** (-jnp.arange(n_f, dtype=jnp.float32) / n_f)
    t = jnp.arange(rows * GRID_W)
    r = (t // GRID_W).astype(jnp.float32)
    col = (t % GRID_W).astype(jnp.float32)
    ang = jnp.concatenate([r[:, None] * inv, col[:, None] * inv], -1)
    return jnp.cos(ang), jnp.sin(ang)


def apply_rope(x, cos, sin):
    x1, x2 = jnp.split(x.astype(jnp.float32), 2, -1)
    c = cos[None, :, None, :]
    s = sin[None, :, None, :]
    return jnp.concatenate([x1 * c - x2 * s, x1 * s + x2 * c], -1).astype(x.dtype)


def neighborhood_attention(q_rot, k_rot, v, q_plain, k_ctx, v_ctx, rpb):
    Bsz, S, H, d = q_rot.shape
    rows = S // GRID_W
    kh = min(NA_KH, rows)
    qg = q_rot.reshape(Bsz, rows, GRID_W, H, d)
    kg = k_rot.reshape(Bsz, rows, GRID_W, H, d)
    vg = v.reshape(Bsz, rows, GRID_W, H, d)
    qpg = q_plain.reshape(Bsz, rows, GRID_W, H, d)
    cq = jnp.arange(GRID_W)
    col_idx = jnp.clip(cq - NA_KW // 2, 0, GRID_W - NA_KW)[:, None] + jnp.arange(NA_KW)[None, :]
    col_bias_idx = col_idx - cq[:, None] + (NA_KW - 1)
    scale = d ** -0.5

    def row_block(r):
        rs = jnp.clip(r - kh // 2, 0, rows - kh)
        kr = lax.dynamic_slice_in_dim(kg, rs, kh, axis=1)[:, :, col_idx]
        vr = lax.dynamic_slice_in_dim(vg, rs, kh, axis=1)[:, :, col_idx]
        qr = lax.dynamic_index_in_dim(qg, r, axis=1, keepdims=False)
        qpr = lax.dynamic_index_in_dim(qpg, r, axis=1, keepdims=False)
        row_bias_idx = rs + jnp.arange(kh) - r + (NA_KH - 1)
        bias = rpb[:, row_bias_idx][:, :, col_bias_idx].transpose(0, 2, 1, 3)
        s_loc = jnp.einsum('bchd,bicjhd->bhcij', qr, kr).astype(jnp.float32) * scale + bias
        s_ctx = jnp.einsum('bchd,bnhd->bhcn', qpr, k_ctx).astype(jnp.float32) * scale
        logits = jnp.concatenate([s_loc.reshape(Bsz, H, GRID_W, kh * NA_KW), s_ctx], -1)
        p = jax.nn.softmax(logits, -1).astype(v.dtype)
        p_loc = p[..., :kh * NA_KW].reshape(Bsz, H, GRID_W, kh, NA_KW)
        p_ctx = p[..., kh * NA_KW:]
        return (jnp.einsum('bhcij,bicjhd->bchd', p_loc, vr)
                + jnp.einsum('bhcn,bnhd->bchd', p_ctx, v_ctx))

    out = lax.map(row_block, jnp.arange(rows))
    return out.transpose(1, 0, 2, 3, 4).reshape(Bsz, S, H * d)


_NA_MASK = -1e30


def _na_tables(rows, rpb):
    cos, sin = axial_rope(rows, HEAD_DIM)
    cos_f = jnp.tile(cos, (1, 2 * NA_HEADS))
    sin_s = jnp.tile(jnp.concatenate([-sin, sin], -1), (1, NA_HEADS))
    cq = jnp.arange(GRID_W)
    cs = jnp.clip(cq - NA_KW // 2, 0, GRID_W - NA_KW)
    col = jnp.arange(GRID_W)
    in_band = (col[None, :] >= cs[:, None]) & (col[None, :] < cs[:, None] + NA_KW)
    cb_idx = jnp.clip(col[None, :] - cq[:, None] + (NA_KW - 1), 0, 2 * NA_KW - 2)
    po = jnp.arange(NA_KH)
    rb_idx = jnp.arange(NA_KH)[None, :] - po[:, None] + (NA_KH - 1)
    bias = rpb[:, rb_idx][:, :, :, cb_idx]
    bias = jnp.where(in_band[None, None, None], bias, _NA_MASK)
    bias = bias.transpose(1, 0, 3, 2, 4).reshape(NA_KH, NA_HEADS, GRID_W, NA_KH * GRID_W)
    return cos_f, sin_s, bias


def _rope_lanes(x, cos_f, sin_s):
    n = x.shape[-1]
    hd = HEAD_DIM // 2
    first = (lax.broadcasted_iota(jnp.int32, x.shape, 1) % HEAD_DIM) < hd
    partner = jnp.where(first, pltpu.roll(x, n - hd, 1), pltpu.roll(x, hd, 1))
    return x * cos_f + partner * sin_s


def _softmax_pv(s_parts, v_parts):
    m = s_parts[0].max(-1, keepdims=True)
    for s in s_parts[1:]:
        m = jnp.maximum(m, s.max(-1, keepdims=True))
    l = 0.0
    o = 0.0
    for s, v in zip(s_parts, v_parts):
        p = jnp.exp(s - m)
        l = l + p.sum(-1, keepdims=True)
        o = o + jnp.dot(p.astype(jnp.bfloat16), v, preferred_element_type=jnp.float32)
    return o / l


def _na_body(rows, q_ref, k_ref, v_ref, kc_ref, vc_ref, cosq_ref, sinq_ref, cosk_ref, sink_ref, bias_ref,
             o_ref, krot_ref, vbf_ref):
    r = pl.program_id(1)
    scale = HEAD_DIM ** -0.5

    @pl.when(r == 0)
    def _():
        krot_ref[...] = _rope_lanes(k_ref[0], cosk_ref[...], sink_ref[...]).astype(jnp.bfloat16)
        vbf_ref[...] = v_ref[0].astype(jnp.bfloat16)

    rs = jnp.clip(r - NA_KH // 2, 0, rows - NA_KH)
    start = pl.multiple_of(rs * GRID_W, GRID_W)
    win = NA_KH * GRID_W
    q = q_ref[0] * scale
    q_rot = _rope_lanes(q, cosq_ref[...], sinq_ref[...]).astype(jnp.bfloat16)
    q_plain = q.astype(jnp.bfloat16)
    kw = krot_ref[pl.ds(start, win), :]
    vw = vbf_ref[pl.ds(start, win), :]
    kc = kc_ref[0].astype(jnp.bfloat16)
    vc = vc_ref[0].astype(jnp.bfloat16)
    outs = []
    for h in range(NA_HEADS):
        hs = slice(h * HEAD_DIM, (h + 1) * HEAD_DIM)
        s_loc = lax.dot_general(q_rot[:, hs], kw[:, hs], _NT, preferred_element_type=jnp.float32) + bias_ref[0, h]
        s_ctx = lax.dot_general(q_plain[:, hs], kc[:, hs], _NT, preferred_element_type=jnp.float32)
        outs.append(_softmax_pv([s_loc, s_ctx], [vw[:, hs], vc[:, hs]]))
    o_ref[0] = jnp.concatenate(outs, axis=-1)


def _na_attention(pb, cb, rpb):
    Bsz, S, _ = pb.shape
    Lc = cb.shape[1]
    rows = S // GRID_W
    G = GROUP_W
    cos_f, sin_s, bias = _na_tables(rows, rpb)

    def bias_idx(b, r):
        rs = jnp.clip(r - NA_KH // 2, 0, rows - NA_KH)
        return (r - rs, 0, 0, 0)

    return pl.pallas_call(
        functools.partial(_na_body, rows),
        grid=(Bsz, rows),
        in_specs=[pl.BlockSpec((1, GRID_W, G), lambda b, r: (b, r, 0)),
                  pl.BlockSpec((1, S, G), lambda b, r: (b, 0, 1)),
                  pl.BlockSpec((1, S, G), lambda b, r: (b, 0, 2)),
                  pl.BlockSpec((1, Lc, G), lambda b, r: (b, 0, 1)),
                  pl.BlockSpec((1, Lc, G), lambda b, r: (b, 0, 2)),
                  pl.BlockSpec((GRID_W, G), lambda b, r: (r, 0)),
                  pl.BlockSpec((GRID_W, G), lambda b, r: (r, 0)),
                  pl.BlockSpec((S, G), lambda b, r: (0, 0)),
                  pl.BlockSpec((S, G), lambda b, r: (0, 0)),
                  pl.BlockSpec((1, NA_HEADS, GRID_W, NA_KH * GRID_W), bias_idx)],
        out_specs=pl.BlockSpec((1, GRID_W, G), lambda b, r: (b, r, 0)),
        out_shape=jax.ShapeDtypeStruct((Bsz, S, G), jnp.float32),
        scratch_shapes=[pltpu.VMEM((S, G), jnp.bfloat16), pltpu.VMEM((S, G), jnp.bfloat16)],
        compiler_params=pltpu.CompilerParams(dimension_semantics=("arbitrary", "arbitrary"),
                                             vmem_limit_bytes=_VMEM_LIMIT),
        name="na_attention",
    )(pb, pb, pb, cb, cb, cos_f, sin_s, cos_f, sin_s, bias)


def _ctx_attn_body(q_ref, k_ref, v_ref, o_ref):
    q = (q_ref[0] * HEAD_DIM ** -0.5).astype(jnp.bfloat16)
    k = k_ref[0].astype(jnp.bfloat16)
    v = v_ref[0].astype(jnp.bfloat16)
    outs = []
    for h in range(NA_HEADS):
        hs = slice(h * HEAD_DIM, (h + 1) * HEAD_DIM)
        s = lax.dot_general(q[:, hs], k[:, hs], _NT, preferred_element_type=jnp.float32)
        outs.append(_softmax_pv([s], [v[:, hs]]))
    o_ref[0] = jnp.concatenate(outs, axis=-1)


def _ctx_attention(cb):
    Bsz, Lc, _ = cb.shape
    G = GROUP_W
    return pl.pallas_call(
        _ctx_attn_body,
        grid=(Bsz,),
        in_specs=[pl.BlockSpec((1, Lc, G), lambda b: (b, 0, 0)),
                  pl.BlockSpec((1, Lc, G), lambda b: (b, 0, 1)),
                  pl.BlockSpec((1, Lc, G), lambda b: (b, 0, 2))],
        out_specs=pl.BlockSpec((1, Lc, G), lambda b: (b, 0, 0)),
        out_shape=jax.ShapeDtypeStruct((Bsz, Lc, G), jnp.float32),
        compiler_params=pltpu.CompilerParams(dimension_semantics=("arbitrary",)),
        name="ctx_attention",
    )(cb, cb, cb)


def context_attention(q, k, v):
    s = jnp.einsum('bqhd,bkhd->bhqk', q, k).astype(jnp.float32) * (q.shape[-1] ** -0.5)
    p = jax.nn.softmax(s, -1).astype(v.dtype)
    return jnp.einsum('bhqk,bkhd->bqhd', p, v)


def hyena_filters(L, w1, b1, w2, b2, w3, decay):
    tn = jnp.arange(L, dtype=jnp.float32)[:, None] / L
    bands = jnp.arange(1, HY_BANDS + 1, dtype=jnp.float32)[None, :]
    ang = 2.0 * math.pi * bands * tn
    z = jnp.concatenate([tn, jnp.sin(ang), jnp.cos(ang)], -1)
    hmid = jnp.sin(HY_SIN_FREQ * (z @ w1.astype(jnp.float32) + b1.astype(jnp.float32)))
    hmid = jnp.sin(HY_SIN_FREQ * (hmid @ w2.astype(jnp.float32) + b2.astype(jnp.float32)))
    k = (hmid @ w3.astype(jnp.float32)) * jnp.exp(-tn * decay.astype(jnp.float32))
    k = k / (jnp.sum(jnp.abs(k), axis=0, keepdims=True) + 1e-6)
    return k[:, :GROUP_W], k[:, GROUP_W:]


def bidir_fftconv(u, k_fwd, k_bwd):
    L = u.shape[1]
    k2 = jnp.concatenate([k_fwd, jnp.zeros_like(k_fwd[:1]), k_bwd[1:][::-1]], 0)
    kf = jnp.fft.rfft(k2, n=2 * L, axis=0)
    uf = jnp.fft.rfft(u.astype(jnp.float32), n=2 * L, axis=1)
    return jnp.fft.irfft(uf * kf[None], n=2 * L, axis=1)[:, :L].astype(u.dtype)


def hyena(p, short_w, short_b, w1, b1, w2, b2, w3, decay, skip):
    L = p.shape[1]
    p = dwconv(p, short_w) + short_b
    x0, x1, v = jnp.split(p, 3, -1)
    k_fwd, k_bwd = hyena_filters(L, w1, b1, w2, b2, w3, decay)
    u = v * x1
    y = bidir_fftconv(u, k_fwd, k_bwd) + u * skip
    return y * x0


def segsum(a):
    T = a.shape[-1]
    a_rep = jnp.broadcast_to(a[..., None], a.shape + (T,))
    a_rep = jnp.where(jnp.tril(jnp.ones((T, T), bool), -1), a_rep, 0.0)
    s = jnp.cumsum(a_rep, axis=-2)
    return jnp.where(jnp.tril(jnp.ones((T, T), bool)), s, -jnp.inf)


def ssd_scan(x, dt, A, Bh, Ch, init_state, want_y):
    Bsz, L, H, P = x.shape
    N = Bh.shape[-1]
    nc = L // SSD_CHUNK
    xd = (x.astype(jnp.float32) * dt[..., None]).reshape(Bsz, nc, SSD_CHUNK, H, P)
    a = (dt * A).reshape(Bsz, nc, SSD_CHUNK, H).transpose(0, 3, 1, 2)
    Bc = Bh.astype(jnp.float32).reshape(Bsz, nc, SSD_CHUNK, H, N)
    Cc = Ch.astype(jnp.float32).reshape(Bsz, nc, SSD_CHUNK, H, N)
    a_cum = jnp.cumsum(a, -1)
    decay_states = jnp.exp(a_cum[..., -1:] - a_cum)
    states = jnp.einsum('bclhn,bhcl,bclhp->bchpn', Bc, decay_states, xd)
    states = jnp.concatenate([init_state[:, None], states], 1)
    decay_chunk = jnp.exp(segsum(jnp.pad(a_cum[..., -1], ((0, 0), (0, 0), (1, 0)))))
    new_states = jnp.einsum('bhzc,bchpn->bzhpn', decay_chunk, states)
    prev_states, final = new_states[:, :-1], new_states[:, -1]
    if not want_y:
        return None, final
    Lmat = jnp.exp(segsum(a))
    y_diag = jnp.einsum('bclhn,bcshn,bhcls,bcshp->bclhp', Cc, Bc, Lmat, xd)
    y_off = jnp.einsum('bclhn,bchpn,bhcl->bclhp', Cc, prev_states, jnp.exp(a_cum))
    return (y_diag + y_off).reshape(Bsz, L, H, P), final


def _flip(t):
    return jnp.flip(t, axis=1)


def ssd_bidir(z, xbc, dt_raw, init_f, init_b, want_y, conv_w, conv_b, a_log, dt_bias, d_skip, norm_g):
    Bsz, L, _ = xbc.shape
    xbc = jax.nn.silu(dwconv(xbc, conv_w) + conv_b)
    xs = xbc[..., :GROUP_W].reshape(Bsz, L, SSD_HEADS, SSD_HEAD_DIM)
    bc = xbc[..., GROUP_W:].reshape(Bsz, L, 2, SSD_GROUPS, SSD_STATE)
    rep = SSD_HEADS // SSD_GROUPS
    Bh = jnp.repeat(bc[:, :, 0], rep, axis=2)
    Ch = jnp.repeat(bc[:, :, 1], rep, axis=2)
    dt = jax.nn.softplus(dt_raw.astype(jnp.float32).reshape(Bsz, L, 2, SSD_HEADS) + dt_bias.astype(jnp.float32))
    A = -jnp.exp(a_log.astype(jnp.float32))
    y_f, s_f = ssd_scan(xs, dt[:, :, 0], A[0], Bh, Ch, init_f, want_y)
    y_b, s_b = ssd_scan(_flip(xs), _flip(dt[:, :, 1]), A[1], _flip(Bh), _flip(Ch), init_b, want_y)
    if not want_y:
        return None, s_f, s_b
    y = y_f + _flip(y_b) + xs.astype(jnp.float32) * d_skip.astype(jnp.float32)[:, None]
    yg = (y.reshape(Bsz, L, GROUP_W) * jax.nn.silu(z.astype(jnp.float32)))
    yg = yg.reshape(Bsz, L, SSD_GROUPS, GROUP_W // SSD_GROUPS)
    yg = yg * lax.rsqrt(jnp.mean(jnp.square(yg), -1, keepdims=True) + LN_EPS)
    return (yg.reshape(Bsz, L, GROUP_W) * norm_g).astype(z.dtype), s_f, s_b


_HY_TB = 256
_HY_CB = 8
_LANES = 128
_SUBLANES = 8


def _hyena_filter_body(L, w1_ref, b1_ref, w2_ref, b2_ref, w3_ref, dec_ref, o_ref):
    G = GROUP_W
    hp = lax.Precision.HIGHEST
    p = lax.broadcasted_iota(jnp.int32, (2 * L, _LANES), 0)
    lane = lax.broadcasted_iota(jnp.int32, (2 * L, _LANES), 1)
    tn = jnp.abs(p - L).astype(jnp.float32) / L
    band = ((lane - 1) % HY_BANDS + 1).astype(jnp.float32)
    ang = 2.0 * math.pi * band * tn
    z = jnp.where(lane == 0, tn, jnp.where(lane <= HY_BANDS, jnp.sin(ang), jnp.cos(ang)))
    z = jnp.where(lane < HY_EMB, z, 0.0)
    h = jnp.sin(HY_SIN_FREQ * (jnp.dot(z, w1_ref[...], precision=hp, preferred_element_type=jnp.float32) + b1_ref[...]))
    h = jnp.sin(HY_SIN_FREQ * (jnp.dot(h, w2_ref[...], precision=hp, preferred_element_type=jnp.float32) + b2_ref[...]))
    k = jnp.dot(h, w3_ref[...], precision=hp, preferred_element_type=jnp.float32) * jnp.exp(-tn[:, 0:1] * dec_ref[...])
    kf, kb = k[:, 0:G], k[:, G:2 * G]
    n = lax.broadcasted_iota(jnp.int32, (2 * L, G), 0) - L
    nf = jnp.sum(jnp.where(n >= 0, jnp.abs(kf), 0.0), axis=0, keepdims=True) + 1e-6
    nb = jnp.sum(jnp.where(n <= 0, jnp.where(n > -L, jnp.abs(kb), 0.0), 0.0), axis=0, keepdims=True) + 1e-6
    rev = jnp.where(n > 0, kb / nb, jnp.where(n > -L, kf / nf, 0.0))
    o_ref[...] = rev.T


def _hyena_filters_rev(L, w1, b1, w2, b2, w3, decay):
    G = GROUP_W
    hp = _LANES - HY_HIDDEN
    w1p = jnp.pad(w1.astype(jnp.float32), ((0, _LANES - HY_EMB), (0, hp)))
    args = (w1p, jnp.pad(b1.reshape(1, -1), ((0, 0), (0, hp))), jnp.pad(w2, ((0, hp), (0, hp))),
            jnp.pad(b2.reshape(1, -1), ((0, 0), (0, hp))), jnp.pad(w3, ((0, hp), (0, 0))), decay.reshape(1, -1))
    return pl.pallas_call(
        functools.partial(_hyena_filter_body, L),
        out_shape=jax.ShapeDtypeStruct((G, 2 * L), jnp.float32),
        compiler_params=pltpu.CompilerParams(vmem_limit_bytes=_VMEM_LIMIT),
        name="hyena_filters",
    )(*[a.astype(jnp.float32) for a in args])


def _hyena_pre_body(L, p_ref, w_ref, b_ref, u_ref, ut_ref, x0_ref, xp_ref):
    G = GROUP_W
    W = 3 * G
    halo = jnp.zeros((_CONV_HALO, W), jnp.float32)
    xp_ref[0:_CONV_HALO, :] = halo
    xp_ref[_CONV_HALO + L:_CONV_HALO + L + _CONV_HALO, :] = halo
    xp_ref[_CONV_HALO:_CONV_HALO + L, :] = p_ref[0]
    pad = (HY_SHORT - 1) // 2
    tile = min(_CONV_TILE, L)
    for t in range(L // tile):
        win = xp_ref[t * tile:t * tile + tile + 2 * _CONV_HALO, :]
        acc = jnp.zeros((tile, W), jnp.float32) + b_ref[...]
        for k, tap in enumerate(_shifted_taps(win[_CONV_HALO - pad:], HY_SHORT, tile)):
            acc = acc + tap * w_ref[k:k + 1, :]
        rows = slice(t * tile, (t + 1) * tile)
        u = acc[:, 2 * G:3 * G] * acc[:, G:2 * G]
        x0_ref[0, rows, :] = acc[:, 0:G]
        u_ref[0, rows, :] = u
        ut_ref[0, :, rows] = u.T


def _hyena_pre(py, short_w, short_b):
    Bsz, L, W = py.shape
    G = GROUP_W
    f32 = jnp.float32
    return pl.pallas_call(
        functools.partial(_hyena_pre_body, L),
        grid=(Bsz,),
        in_specs=[pl.BlockSpec((1, L, W), lambda b: (b, 0, 0)),
                  pl.BlockSpec((HY_SHORT, W), lambda b: (0, 0)),
                  pl.BlockSpec((1, W), lambda b: (0, 0))],
        out_specs=[pl.BlockSpec((1, L, G), lambda b: (b, 0, 0)),
                   pl.BlockSpec((1, G, L), lambda b: (b, 0, 0)),
                   pl.BlockSpec((1, L, G), lambda b: (b, 0, 0))],
        out_shape=[jax.ShapeDtypeStruct((Bsz, L, G), f32), jax.ShapeDtypeStruct((Bsz, G, L), f32),
                   jax.ShapeDtypeStruct((Bsz, L, G), f32)],
        scratch_shapes=[pltpu.VMEM((L + 2 * _CONV_HALO, W), f32)],
        compiler_params=pltpu.CompilerParams(dimension_semantics=("arbitrary",), vmem_limit_bytes=_VMEM_LIMIT),
        name="hyena_pre",
    )(py, short_w, short_b.reshape(1, W))


def _toeplitz_tile(w8):
    TB = _HY_TB
    nq = 2 * TB // _LANES
    per = _LANES // _SUBLANES
    i = lax.broadcasted_iota(jnp.int32, (_SUBLANES, _LANES), 0)
    l = lax.broadcasted_iota(jnp.int32, (_SUBLANES, _LANES), 1)
    rolled = [[pltpu.roll(w8[:, q * _LANES:(q + 1) * _LANES], (_SUBLANES * k) % _LANES, 1, stride=1, stride_axis=0)
               for k in range(per)] for q in range(nq)]
    row_blocks = []
    for rg in range(TB // _SUBLANES):
        k = rg % per
        pieces = []
        for lg in range(TB // _LANES):
            o = TB + _LANES * lg - _SUBLANES * rg
            q, rho = divmod(o, _LANES)
            if rho == 0:
                pieces.append(jnp.where(l - i < 0, rolled[q - 1][k], rolled[q][k]))
            else:
                pieces.append(jnp.where(l + rho - i >= _LANES, rolled[q + 1][k], rolled[q][k]))
        row_blocks.append(jnp.concatenate(pieces, axis=1))
    return jnp.concatenate(row_blocks, axis=0).astype(jnp.bfloat16)


def _hyena_conv_body(L, Bsz, g_ref, u_ref, o_ref):
    TB = min(_HY_TB, L)
    nb = L // TB
    cols = Bsz * nb
    lane = lax.broadcasted_iota(jnp.int32, (TB, cols), 1) % nb

    def channel(ci, carry):
        u = u_ref[ci] if nb == 1 else u_ref[:, ci].reshape(cols, TB)
        u = u.astype(jnp.bfloat16)
        acc = jnp.zeros((TB, cols), jnp.float32)
        for d in range(-(nb - 1), nb):
            start = L - TB * d - TB
            w8 = jnp.broadcast_to(g_ref[pl.ds(ci, 1), start:start + 2 * TB], (_SUBLANES, 2 * TB))
            z = lax.dot_general(_toeplitz_tile(w8), u, _NT, preferred_element_type=jnp.float32)
            if d != 0:
                z = jnp.where((lane - d >= 0) & (lane - d < nb), pltpu.roll(z, d % cols, 1), 0.0)
            acc = acc + z
        if nb == 1:
            o_ref[ci] = acc.T
        else:
            o_ref[:, ci] = acc.T.reshape(Bsz, nb, TB)
        return carry

    lax.fori_loop(0, _HY_CB, channel, 0)


def _hyena_conv(g_rev, u_t):
    Bsz, G, L = u_t.shape
    TB = min(_HY_TB, L)
    nb = L // TB
    if nb == 1:
        blk = pl.BlockSpec((_HY_CB, Bsz, TB), lambda c: (c, 0, 0))
        operand, out_shape = jnp.swapaxes(u_t, 0, 1), (G, Bsz, TB)
    else:
        blk = pl.BlockSpec((Bsz, _HY_CB, nb, TB), lambda c: (0, c, 0, 0))
        operand, out_shape = u_t.reshape(Bsz, G, nb, TB), (Bsz, G, nb, TB)
    out = pl.pallas_call(
        functools.partial(_hyena_conv_body, L, Bsz),
        grid=(G // _HY_CB,),
        in_specs=[pl.BlockSpec((_HY_CB, 2 * L), lambda c: (c, 0)), blk],
        out_specs=blk,
        out_shape=jax.ShapeDtypeStruct(out_shape, jnp.float32),
        compiler_params=pltpu.CompilerParams(dimension_semantics=("arbitrary",), vmem_limit_bytes=_VMEM_LIMIT),
        name="hyena_conv",
    )(g_rev, operand)
    return jnp.swapaxes(out, 0, 1) if nb == 1 else out.reshape(Bsz, G, L)


def _hyena_post_body(yt_ref, u_ref, x0_ref, skip_ref, o_ref):
    o_ref[0] = (yt_ref[0].T + u_ref[0] * skip_ref[...]) * x0_ref[0]


def _hyena_post(y_t, u, x0, skip):
    Bsz, L, G = u.shape
    tok = pl.BlockSpec((1, L, G), lambda b: (b, 0, 0))
    return pl.pallas_call(
        _hyena_post_body,
        grid=(Bsz,),
        in_specs=[pl.BlockSpec((1, G, L), lambda b: (b, 0, 0)), tok, tok, pl.BlockSpec((1, G), lambda b: (0, 0))],
        out_specs=tok,
        out_shape=jax.ShapeDtypeStruct((Bsz, L, G), jnp.float32),
        compiler_params=pltpu.CompilerParams(dimension_semantics=("arbitrary",), vmem_limit_bytes=_VMEM_LIMIT),
        name="hyena_post",
    )(y_t, u, x0, skip.reshape(1, G))


def _hyena(py, short_w, short_b, w1, b1, w2, b2, w3, decay, skip):
    L = py.shape[1]
    g_rev = _hyena_filters_rev(L, w1, b1, w2, b2, w3, decay)
    u, u_t, x0 = _hyena_pre(py, short_w, short_b)
    return _hyena_post(_hyena_conv(g_rev, u_t), u, x0, skip)


def _split3_dot(a, b_bf16, dims=None):
    hi = a.astype(jnp.bfloat16)
    r1 = a - hi.astype(jnp.float32)
    mid = r1.astype(jnp.bfloat16)
    lo = (r1 - mid.astype(jnp.float32)).astype(jnp.bfloat16)
    out = 0.0
    for part in (hi, mid, lo):
        if dims is None:
            out = out + jnp.dot(part, b_bf16, preferred_element_type=jnp.float32)
        else:
            out = out + jnp.dot(b_bf16, part, preferred_element_type=jnp.float32)
    return out


def _softplus(x):
    return jnp.maximum(x, 0.0) + jnp.log(1.0 + jnp.exp(-jnp.abs(x)))


def _ssd_body(L, zx_ref, dtc_ref, dtr_ref, cw_ref, cb_ref, arow_ref, acol_ref, brow_ref, bcol_ref, dsk_ref,
              ng_ref, init_ref, o_ref, fin_ref, xp_ref, xc_ref, bt_ref, y_ref, ccol_ref, crow_ref, edec_ref,
              tot_ref):
    G = GROUP_W
    Q = SSD_CHUNK
    nc = L // Q
    P = SSD_HEAD_DIM
    N = SSD_STATE
    H = SSD_HEADS
    f32 = jnp.float32
    bf16 = jnp.bfloat16
    halo = jnp.zeros((_CONV_HALO, XBC_W), f32)
    xp_ref[0:_CONV_HALO, :] = halo
    xp_ref[_CONV_HALO + L:_CONV_HALO + L + _CONV_HALO, :] = halo
    xp_ref[_CONV_HALO:_CONV_HALO + L, :] = zx_ref[0, :, G:G + XBC_W]
    pad = (SSD_CONV - 1) // 2
    tile = min(_CONV_TILE, L)
    for t in range(L // tile):
        win = xp_ref[t * tile:t * tile + tile + 2 * _CONV_HALO, :]
        acc = jnp.zeros((tile, XBC_W), f32) + cb_ref[...]
        for k, tap in enumerate(_shifted_taps(win[_CONV_HALO - pad:], SSD_CONV, tile)):
            acc = acc + tap * cw_ref[k:k + 1, :]
        xc_ref[t * tile:(t + 1) * tile, :] = acc * jax.nn.sigmoid(acc)
    for c in range(nc):
        bt_ref[c] = xc_ref[c * Q:(c + 1) * Q, G:G + SSD_GROUPS * N].T
    dt_col = _softplus(dtc_ref[0] + brow_ref[...])
    a_col = dt_col * arow_ref[...]
    a_row = _softplus(dtr_ref[0] + bcol_ref[...]) * acol_ref[...]
    a_stack = jnp.concatenate([a_row[:, c * Q:(c + 1) * Q] for c in range(nc)], axis=0)
    ri = lax.broadcasted_iota(jnp.int32, (Q, Q), 0)
    ci = lax.broadcasted_iota(jnp.int32, (Q, Q), 1)
    one = lambda m: jnp.where(m, 1.0, 0.0).astype(bf16)
    tot_ref[...] = _split3_dot(a_stack, jnp.ones((Q, Q), bf16))

    def direction(d, y_store):
        fwd = d == 0
        m_col = one(ci <= ri) if fwd else one(ci >= ri)
        for c in range(nc):
            ccol_ref[c * Q:(c + 1) * Q, :] = _split3_dot(a_col[c * Q:(c + 1) * Q, :], m_col, dims="left")
        crow_ref[...] = _split3_dot(a_stack, one(ri <= ci) if fwd else one(ri >= ci))
        edec_ref[...] = _split3_dot(a_stack, one(ri > ci) if fwd else one(ri < ci))
        keep = (ri >= ci) if fwd else (ri <= ci)

        def chunk(step, states):
            c = step if fwd else nc - 1 - step
            r0 = pl.multiple_of(c * Q, Q)
            j0 = pl.multiple_of(c * 2 * H, 2 * H)
            xc = xc_ref[pl.ds(r0, Q), :]
            ccol = ccol_ref[pl.ds(r0, Q), :]
            crow = crow_ref[pl.ds(j0, 2 * H), :]
            edec = edec_ref[pl.ds(j0, 2 * H), :]
            tot = tot_ref[pl.ds(j0, 2 * H), :]
            dtc = _softplus(dtc_ref[0, pl.ds(r0, Q), :] + brow_ref[...])
            bt = bt_ref[c]
            new_states = []
            outs = []
            for g in range(SSD_GROUPS):
                cm = xc[:, G + SSD_GROUPS * N + g * N:G + SSD_GROUPS * N + (g + 1) * N].astype(bf16)
                bm = xc[:, G + g * N:G + (g + 1) * N].astype(bf16)
                cb = lax.dot_general(cm, bm, _NT, preferred_element_type=f32)
                for hh in range(H // SSD_GROUPS):
                    h = g * (H // SSD_GROUPS) + hh
                    j = d * H + h
                    col = jnp.broadcast_to(ccol[:, j:j + 1], (Q, Q))
                    lmat = jnp.exp(jnp.where(keep, col - crow[j:j + 1, :], _NEG))
                    xd = (xc[:, h * P:(h + 1) * P] * jnp.broadcast_to(dtc[:, j:j + 1], (Q, P))).astype(bf16)
                    st = states[h]
                    y = jnp.dot((cb * lmat).astype(bf16), xd, preferred_element_type=f32)
                    y = y + jnp.dot(cm, st.astype(bf16), preferred_element_type=f32) * jnp.exp(col[:, 0:P])
                    outs.append(y)
                    btd = (bt[g * N:(g + 1) * N, :] * jnp.exp(edec[j:j + 1, :])).astype(bf16)
                    new_states.append(jnp.exp(tot[j:j + 1, 0:P]) * st
                                      + jnp.dot(btd, xd, preferred_element_type=f32))
            y_store(r0, jnp.concatenate(outs, axis=-1))
            return tuple(new_states)

        init = tuple(init_ref[0, d, h] for h in range(H))
        final = lax.fori_loop(0, nc, chunk, init)
        for h in range(H):
            fin_ref[0, d, h] = final[h]

    def store_fwd(r0, y):
        y_ref[pl.ds(r0, Q), :] = y

    def store_bwd(r0, y):
        y_ref[pl.ds(r0, Q), :] += y

    direction(0, store_fwd)
    direction(1, store_bwd)
    gw = G // SSD_GROUPS
    for t in range(L // tile):
        rows = slice(t * tile, (t + 1) * tile)
        z = zx_ref[0, rows, 0:G]
        yg = (y_ref[rows, :] + xc_ref[rows, 0:G] * dsk_ref[...]) * (z * jax.nn.sigmoid(z))
        parts = []
        for g in range(SSD_GROUPS):
            v = yg[:, g * gw:(g + 1) * gw]
            parts.append(v * lax.rsqrt(jnp.mean(v * v, -1, keepdims=True) + LN_EPS))
        o_ref[0, rows, :] = jnp.concatenate(parts, axis=-1) * ng_ref[...]


def _ssd(pzx, pdt, init, conv_w, conv_b, a_log, dt_bias, d_skip, norm_g):
    Bsz, L, _ = pzx.shape
    G, H, Q = GROUP_W, SSD_HEADS, SSD_CHUNK
    lanes = pdt.shape[-1]
    nc = L // Q
    neg_a = -jnp.exp(a_log.astype(jnp.float32)).reshape(1, 2 * H)
    a_rowv = jnp.pad(neg_a, ((0, 0), (0, lanes - 2 * H)))
    b_rowv = jnp.pad(dt_bias.astype(jnp.float32).reshape(1, 2 * H), ((0, 0), (0, lanes - 2 * H)))
    dt_rows = jnp.swapaxes(pdt[:, :, :2 * H], 1, 2)
    d_lane = jnp.repeat(d_skip.astype(jnp.float32), SSD_HEAD_DIM).reshape(1, G)
    const = lambda shape: pl.BlockSpec(shape, lambda b: (0,) * len(shape))
    st_spec = pl.BlockSpec((1, 2, H, SSD_STATE, SSD_HEAD_DIM), lambda b: (b, 0, 0, 0, 0))
    f32 = jnp.float32
    return pl.pallas_call(
        functools.partial(_ssd_body, L),
        grid=(Bsz,),
        in_specs=[pl.BlockSpec((1, L, G + XBC_W), lambda b: (b, 0, 0)),
                  pl.BlockSpec((1, L, lanes), lambda b: (b, 0, 0)),
                  pl.BlockSpec((1, 2 * H, L), lambda b: (b, 0, 0)),
                  const((SSD_CONV, XBC_W)), const((1, XBC_W)),
                  const((1, lanes)), const((2 * H, 1)), const((1, lanes)), const((2 * H, 1)),
                  const((1, G)), const((1, G)), st_spec],
        out_specs=[pl.BlockSpec((1, L, G), lambda b: (b, 0, 0)), st_spec],
        out_shape=[jax.ShapeDtypeStruct((Bsz, L, G), f32),
                   jax.ShapeDtypeStruct((Bsz, 2, H, SSD_STATE, SSD_HEAD_DIM), f32)],
        scratch_shapes=[pltpu.VMEM((L + 2 * _CONV_HALO, XBC_W), f32),
                        pltpu.VMEM((L, XBC_W), f32),
                        pltpu.VMEM((nc, SSD_GROUPS * SSD_STATE, Q), f32),
                        pltpu.VMEM((L, G), f32),
                        pltpu.VMEM((L, lanes), f32),
                        pltpu.VMEM((nc * 2 * H, Q), f32),
                        pltpu.VMEM((nc * 2 * H, Q), f32),
                        pltpu.VMEM((nc * 2 * H, Q), f32)],
        compiler_params=pltpu.CompilerParams(dimension_semantics=("arbitrary",),
                                             vmem_limit_bytes=_VMEM_LIMIT),
        name="ssd",
    )(pzx, pdt, dt_rows, conv_w, conv_b.reshape(1, XBC_W), a_rowv, neg_a.reshape(2 * H, 1),
      b_rowv, dt_bias.astype(f32).reshape(2 * H, 1), d_lane, norm_g.reshape(1, G), init)


def _mixer_ln(x, xc, mod, mod_c, ctx_out, alpha, w_in, w_out, ln_g, ln_b, conf, rpb, hy, ssd):
    Bsz = x.shape[0]
    sh, sc, g = mod
    shc, scc, gc = mod_c
    pa, pb, py, pzx, pdt = _in_proj(x, sc, sh, w_in)
    ca, cb, cy, czx, cdt = _in_proj(xc, scc, shc, w_in)
    zero = jnp.zeros((Bsz, 2, SSD_HEADS, SSD_STATE, SSD_HEAD_DIM), jnp.float32)
    y_dc, ctx_states = _ssd(czx, cdt, zero, *ssd)
    y_d, _ = _ssd(pzx, pdt, ctx_states, *ssd)
    ys = [_conformer(pa, *conf), _na_attention(pb, cb, rpb), _hyena(py, *hy), y_d]
    x_new = _out_proj_ln(ys, w_out, x, g, ln_g, ln_b, alpha)
    if not ctx_out:
        return x_new, None
    ycs = [_conformer(ca, *conf), _ctx_attention(cb), _hyena(cy, *hy), y_dc]
    return x_new, _out_proj_ln(ycs, w_out, xc, gc, ln_g, ln_b, alpha)


_NEG = -1e30
_ROUTE_TT = 256
_DENSE_TT = 512
_DENSE_EC = 1024
_NT = (((1,), (1,)), ((), ()))


def _bf16_round(x):
    return x.astype(jnp.bfloat16).astype(jnp.float32)


def _oddeven_sort_pairs(n):
    pairs = []
    p = 1
    while p < n:
        k = p
        while k >= 1:
            for j in range(k % p, n - k, 2 * k):
                for i in range(min(k, n - j - k)):
                    if (i + j) // (2 * p) == (i + j + k) // (2 * p):
                        pairs.append((i + j, i + j + k))
            k //= 2
        p *= 2
    return pairs


_SORT16 = _oddeven_sort_pairs(PEER_TOPK)


def _order_pair(vs, i, j):
    a, b = vs[i], vs[j]
    if b is None:
        return
    if a is None:
        vs[i], vs[j] = b, None
        return
    vs[i], vs[j] = jnp.maximum(a, b), jnp.minimum(a, b)


def _top16_replicated(vs):
    vs = list(vs) + [None] * (PEER_TOPK - len(vs))
    for i, j in _SORT16:
        _order_pair(vs, i, j)
    for shift in (4, 2, 1):
        other = [None if v is None else pltpu.roll(v, shift, 0) for v in vs]
        merged = []
        for k in range(PEER_TOPK):
            a, b = vs[k], other[PEER_TOPK - 1 - k]
            merged.append(b if a is None else a if b is None else jnp.maximum(a, b))
        vs = merged
        stride = PEER_TOPK // 2
        while stride >= 1:
            for i in range(PEER_TOPK):
                if (i // stride) % 2 == 0:
                    _order_pair(vs, i, i + stride)
            stride //= 2
    return vs


def _pack_sublanes(blocks):
    sub = lax.broadcasted_iota(jnp.int32, blocks[0].shape, 0)
    out = blocks[0]
    for r in range(1, len(blocks)):
        out = jnp.where(sub == r, blocks[r], out)
    return out


def _pair_candidates(r1, lo1, hi1, r2, lo2, hi2, op, fill):
    keep = lax.broadcasted_iota(jnp.int32, lo1.shape, 0) >= 4
    pieces = [op(r1[0], lo2), op(r1[0], hi2), op(r1[1], lo2), op(r1[2], lo2), op(r1[3], lo2), op(r2[0], hi1)]
    for b in range(3):
        pieces.append(jnp.where(keep, op(r2[b], lo1), fill))
    return pieces


def _peer_route_body(x_ref, sc_ref, sh_ref, wqT_ref, keys_ref, hm_ref, e1_ref, e2_ref, pthr_ref, qT_ref):
    hm = (x_ref[0] * (1.0 + sc_ref[0]) + sh_ref[0]).astype(jnp.bfloat16)
    hm_ref[0] = hm
    qT_ref[...] = lax.dot_general(wqT_ref[...], hm, _NT, preferred_element_type=jnp.float32)
    nblk = PEER_KEYS // 8

    def head(h, carry):
        base = pl.multiple_of(h * PEER_QDIM, PEER_QDIM)
        half_q = PEER_QDIM // 2
        s_both = []
        for p in range(2):
            qb = qT_ref[pl.ds(base + p * half_q, half_q), :].astype(jnp.bfloat16)
            s_both.append(jnp.dot(keys_ref[h, p], qb, preferred_element_type=jnp.float32))
        for half in range(_ROUTE_TT // 128):
            lanes = slice(half * 128, (half + 1) * 128)
            s1 = s_both[0][:, lanes]
            s2 = s_both[1][:, lanes]
            r1 = _top16_replicated([s1[8 * k:8 * k + 8] for k in range(nblk)])
            r2 = _top16_replicated([s2[8 * k:8 * k + 8] for k in range(nblk)])
            lo1, hi1 = _pack_sublanes(r1[:8]), _pack_sublanes(r1[8:])
            lo2, hi2 = _pack_sublanes(r2[:8]), _pack_sublanes(r2[8:])
            cand = _pair_candidates(r1, lo1, hi1, r2, lo2, hi2, lambda a, b: a + b, _NEG)
            cv = _top16_replicated(cand)
            top, thr = cv[0], cv[PEER_TOPK - 1]
            z = jnp.ones_like(top)
            for k in range(1, PEER_TOPK):
                z = z + jnp.exp(cv[k] - top)
            rz = 1.0 / z
            f1 = lambda v: _bf16_round(jnp.exp(v - r1[0]) * rz)
            f2 = lambda v: _bf16_round(jnp.exp(v - r2[0]))
            prod = _pair_candidates([f1(v) for v in r1[:4]], f1(lo1), f1(hi1), [f2(v) for v in r2[:3]], f2(lo2), f2(hi2),
                                    lambda a, b: _bf16_round(a * b), 0.0)
            low = None
            for cpiece, ppiece in zip(cand, prod):
                sel = jnp.where(cpiece >= thr, ppiece, 1e30)
                low = sel if low is None else jnp.minimum(low, sel)
            e1_ref[h, :, lanes] = jnp.exp(s1 - r1[0][0:1]) * rz[0:1]
            e2_ref[h, :, lanes] = (jnp.exp(s2 - r2[0][0:1])).astype(jnp.bfloat16)
            pthr_ref[h, :, lanes] = jnp.min(low, axis=0, keepdims=True)
        return carry

    lax.fori_loop(0, PEER_HEADS, head, 0)


def _peer_route(x, sc, sh, wqT, keys):
    Bsz, S, D = x.shape
    assert S % _ROUTE_TT == 0, S
    nt = S // _ROUTE_TT
    T = Bsz * S
    tab = jax.ShapeDtypeStruct((PEER_HEADS, PEER_KEYS, T), jnp.float32)
    return pl.pallas_call(
        _peer_route_body,
        grid=(Bsz, nt),
        in_specs=[pl.BlockSpec((1, _ROUTE_TT, D), lambda b, i: (b, i, 0)),
                  pl.BlockSpec((1, 1, D), lambda b, i: (b, 0, 0)),
                  pl.BlockSpec((1, 1, D), lambda b, i: (b, 0, 0)),
                  pl.BlockSpec(wqT.shape, lambda b, i: (0, 0)),
                  pl.BlockSpec(keys.shape, lambda b, i: (0, 0, 0, 0))],
        out_specs=[pl.BlockSpec((1, _ROUTE_TT, D), lambda b, i: (b, i, 0)),
                   pl.BlockSpec((PEER_HEADS, PEER_KEYS, _ROUTE_TT), lambda b, i: (0, 0, b * nt + i)),
                   pl.BlockSpec((PEER_HEADS, PEER_KEYS, _ROUTE_TT), lambda b, i: (0, 0, b * nt + i)),
                   pl.BlockSpec((PEER_HEADS, 1, _ROUTE_TT), lambda b, i: (0, 0, b * nt + i))],
        out_shape=[jax.ShapeDtypeStruct((Bsz, S, D), jnp.bfloat16), tab,
                   jax.ShapeDtypeStruct(tab.shape, jnp.bfloat16),
                   jax.ShapeDtypeStruct((PEER_HEADS, 1, T), jnp.float32)],
        scratch_shapes=[pltpu.VMEM((PEER_HEADS * PEER_QDIM, _ROUTE_TT), jnp.float32)],
        compiler_params=pltpu.CompilerParams(dimension_semantics=("arbitrary", "arbitrary"),
                                             vmem_limit_bytes=_VMEM_LIMIT),
        name="peer_route",
    )(x, sc, sh, wqT, keys)


_GELU_K = math.sqrt(2.0 / math.pi)


def _gelu_tanh(x):
    half = 0.5 * x
    return half + half * jnp.tanh(x * (_GELU_K + (_GELU_K * 0.044715) * (x * x)))


_PACK = 16


def _peer_dense_body(alpha, hm_ref, e1_ref, e2_ref, pthr_ref, u_ref, vt_ref, x_ref, g_ref, lng_ref, lnb_ref,
                     o_ref, acc_ref, wt_ref, e1b_ref, pthrb_ref):
    c = pl.program_id(2)
    jrows = 32
    bf16 = jnp.bfloat16
    th = _DENSE_TT // 2

    @pl.when(c == 0)
    def _():
        acc_ref[...] = jnp.zeros_like(acc_ref)
        for h in range(PEER_HEADS):
            pthrb_ref[h] = jnp.broadcast_to(pthr_ref[h], (_PACK, _DENSE_TT)).astype(bf16)

    halves = [slice(t * th, (t + 1) * th) for t in range(2)]
    acts = [lax.dot_general(u_ref[...], hm_ref[0, toks, :], _NT, preferred_element_type=jnp.float32)
            for toks in halves]
    for t, toks in enumerate(halves):
        for ii in range(_DENSE_EC // PEER_KEYS):
            for h in range(PEER_HEADS):
                e1b_ref[ii % 2, h] = jnp.broadcast_to(e1_ref[h, ii:ii + 1, toks], (_PACK, th)).astype(bf16)
            for jb in range(PEER_KEYS // jrows):
                r0 = ii * PEER_KEYS + jb * jrows
                gate = jnp.zeros((jrows // _PACK, _PACK, th), bf16)
                for h in range(PEER_HEADS):
                    e2 = e2_ref[h, jb * jrows:(jb + 1) * jrows, toks].reshape(jrows // _PACK, _PACK, th)
                    val = e2 * e1b_ref[ii % 2, h][None]
                    gate = gate + jnp.where(val >= pthrb_ref[h, :, toks][None], val, jnp.zeros_like(val))
                gel = _gelu_tanh(acts[t][r0:r0 + jrows, :]).astype(bf16)
                wt_ref[r0:r0 + jrows, toks] = gate.reshape(jrows, th) * gel
        acc_ref[:, toks] += jnp.dot(vt_ref[...], wt_ref[:, toks], preferred_element_type=jnp.float32)

    @pl.when(c == pl.num_programs(2) - 1)
    def _():
        y = alpha * x_ref[0] + g_ref[0] * acc_ref[...].T
        mu = jnp.mean(y, -1, keepdims=True)
        yc = y - mu
        var = jnp.mean(yc * yc, -1, keepdims=True)
        o_ref[0] = yc * lax.rsqrt(var + LN_EPS) * lng_ref[...] + lnb_ref[...]


def _peer_dense(hm, e1, e2, pthr, u_bf, vt_bf, x, g, ln_g, ln_b, alpha):
    Bsz, S, D = x.shape
    assert S % _DENSE_TT == 0, S
    nt = S // _DENSE_TT
    nchunk = N_EXPERTS // _DENSE_EC
    rows_i = _DENSE_EC // PEER_KEYS
    return pl.pallas_call(
        functools.partial(_peer_dense_body, alpha),
        grid=(Bsz, nt, nchunk),
        in_specs=[pl.BlockSpec((1, _DENSE_TT, D), lambda b, i, c: (b, i, 0)),
                  pl.BlockSpec((PEER_HEADS, rows_i, _DENSE_TT), lambda b, i, c: (0, c, b * nt + i)),
                  pl.BlockSpec((PEER_HEADS, PEER_KEYS, _DENSE_TT), lambda b, i, c: (0, 0, b * nt + i)),
                  pl.BlockSpec((PEER_HEADS, 1, _DENSE_TT), lambda b, i, c: (0, 0, b * nt + i)),
                  pl.BlockSpec((_DENSE_EC, D), lambda b, i, c: (c, 0)),
                  pl.BlockSpec((D, _DENSE_EC), lambda b, i, c: (0, c)),
                  pl.BlockSpec((1, _DENSE_TT, D), lambda b, i, c: (b, i, 0)),
                  pl.BlockSpec((1, 1, D), lambda b, i, c: (b, 0, 0)),
                  pl.BlockSpec((1, D), lambda b, i, c: (0, 0)),
                  pl.BlockSpec((1, D), lambda b, i, c: (0, 0))],
        out_specs=pl.BlockSpec((1, _DENSE_TT, D), lambda b, i, c: (b, i, 0)),
        out_shape=jax.ShapeDtypeStruct((Bsz, S, D), jnp.float32),
        scratch_shapes=[pltpu.VMEM((D, _DENSE_TT), jnp.float32),
                        pltpu.VMEM((_DENSE_EC, _DENSE_TT), jnp.bfloat16),
                        pltpu.VMEM((2, PEER_HEADS, _PACK, _DENSE_TT // 2), jnp.bfloat16),
                        pltpu.VMEM((PEER_HEADS, _PACK, _DENSE_TT), jnp.bfloat16)],
        compiler_params=pltpu.CompilerParams(dimension_semantics=("arbitrary", "arbitrary", "arbitrary"),
                                             vmem_limit_bytes=_VMEM_LIMIT),
        name="peer_dense",
    )(hm, e1, e2, pthr, u_bf, vt_bf, x, g, ln_g.reshape(1, D), ln_b.reshape(1, D))


def _peer_weights(wq, sub_keys, u_tab, v_tab):
    return (wq.T.astype(jnp.bfloat16), sub_keys.astype(jnp.bfloat16),
            u_tab.astype(jnp.bfloat16), v_tab.T.astype(jnp.bfloat16))


def _peer_ln(x, sc, sh, g, pw, ln_g, ln_b, alpha):
    wqT, keys, u_bf, vt_bf = pw
    hm, e1, e2, pthr = _peer_route(x, sc, sh, wqT, keys)
    return _peer_dense(hm, e1, e2, pthr, u_bf, vt_bf, x, g, ln_g, ln_b, alpha)


def kernel(x, c, ctx, c_ctx, w_ada, b_ada, w_in, w_out, ln1_g, ln1_b, ln2_g, ln2_b,
           conf_dw_w, conf_dw_b, conf_norm_g, conf_norm_b, na_rpb, hy_short_w, hy_short_b,
           hy_w1, hy_b1, hy_w2, hy_b2, hy_w3, hy_decay, hy_bias, ssd_conv_w, ssd_conv_b,
           ssd_a_log, ssd_dt_bias, ssd_d, ssd_norm_g, peer_wq, peer_keys, peer_u, peer_v):
    alpha = (2.0 * DEPTH) ** 0.25
    s_c = jax.nn.silu(c)
    s_cc = jax.nn.silu(c_ctx)
    xc = ctx
    Bsz, Lc, D = ctx.shape
    for l in range(DEPTH):
        ctx_out = l < DEPTH - 1
        mod = (s_c @ w_ada[l] + b_ada[l])[:, None, :]
        mod_c = jnp.broadcast_to((s_cc @ w_ada[l] + b_ada[l])[None, None, :], mod.shape)
        sh1, sc1, g1, sh2, sc2, g2 = jnp.split(mod, 6, -1)
        sh1c, sc1c, g1c, sh2c, sc2c, g2c = jnp.split(mod_c, 6, -1)
        x, xc = _mixer_ln(
            x, xc, (sh1, sc1, g1), (sh1c, sc1c, g1c), ctx_out, alpha, w_in[l], w_out[l], ln1_g[l], ln1_b[l],
            (conf_dw_w[l], conf_dw_b[l], conf_norm_g[l], conf_norm_b[l]), na_rpb[l],
            (hy_short_w[l], hy_short_b[l], hy_w1[l], hy_b1[l], hy_w2[l], hy_b2[l], hy_w3[l], hy_decay[l], hy_bias[l]),
            (ssd_conv_w[l], ssd_conv_b[l], ssd_a_log[l], ssd_dt_bias[l], ssd_d[l], ssd_norm_g[l]))
        pw = _peer_weights(peer_wq[l], peer_keys[l], peer_u[l], peer_v[l])
        x = _peer_ln(x, sc2, sh2, g2, pw, ln2_g[l], ln2_b[l], alpha)
        if ctx_out:
            xc = _peer_ln(xc.reshape(1, Bsz * Lc, D), sc2c[:1], sh2c[:1], g2c[:1], pw, ln2_g[l], ln2_b[l],
                          alpha).reshape(Bsz, Lc, D)
    return x
```

```python
import functools
import math
import jax, jax.numpy as jnp
from jax import lax
import numpy as np
from jax.experimental import pallas as pl
from jax.experimental.pallas import tpu as pltpu

D_MODEL = 1024
BATCH = 16
SEQ = 2048
DEPTH = 2

GRID_W = 64
CTX_LEN = 256
N_MIXERS = 4
GROUP_W = D_MODEL // N_MIXERS
D_MIX = N_MIXERS * GROUP_W
LN_EPS = 1e-5
CONF_K = 31
CONF_GROUPS = 4
NA_HEADS = 4
HEAD_DIM = GROUP_W // NA_HEADS
NA_KH = 8
NA_KW = 16
ROPE_BASE = 10000.0
HY_SHORT = 3
HY_BANDS = 16
HY_EMB = 1 + 2 * HY_BANDS
HY_HIDDEN = 64
HY_SIN_FREQ = 1.0
SSD_HEADS = 4
SSD_HEAD_DIM = GROUP_W // SSD_HEADS
SSD_GROUPS = 2
SSD_STATE = 64
SSD_CONV = 3
SSD_CHUNK = 128
XBC_W = GROUP_W + 2 * SSD_GROUPS * SSD_STATE
IN_COLS = 2 * GROUP_W + 3 * GROUP_W + 3 * GROUP_W + GROUP_W + XBC_W + 2 * SSD_HEADS
PEER_HEADS = 8
PEER_KEYS = 128
PEER_TOPK = 16
PEER_QDIM = 256
N_EXPERTS = PEER_KEYS * PEER_KEYS
PEER_BLOCK = 128

_VMEM_LIMIT = 56 * 1024 * 1024
_IN_PAD = 2944


def _standardize(x):
    xf = x.astype(jnp.float32)
    mu = jnp.mean(xf, -1, keepdims=True)
    var = jnp.mean(jnp.square(xf - mu), -1, keepdims=True)
    return (xf - mu) * lax.rsqrt(var + LN_EPS)


def layer_norm(x, g, b):
    return (_standardize(x) * g + b).astype(x.dtype)


def dwconv(x, w):
    K, C = w.shape
    pad = (K - 1) // 2
    return lax.conv_general_dilated(x, w[:, None, :].astype(x.dtype), window_strides=(1,),
                                    padding=[(pad, pad)], dimension_numbers=('NWC', 'WIO', 'NWC'),
                                    feature_group_count=C)


def _split_cols(p):
    sizes = (2 * GROUP_W, 3 * GROUP_W, 3 * GROUP_W, GROUP_W, XBC_W, 2 * SSD_HEADS)
    points = [int(v) for v in np.cumsum(sizes)[:-1]]
    return jnp.split(p, points, axis=-1)


_IN_SPLITS = ((0, 2 * GROUP_W), (2 * GROUP_W, 5 * GROUP_W), (5 * GROUP_W, 8 * GROUP_W),
              (8 * GROUP_W, 9 * GROUP_W + XBC_W), (9 * GROUP_W + XBC_W, _IN_PAD))
_PROJ_TM = 512


def _in_proj_body(x_ref, sc_ref, sh_ref, w_ref, *o_refs):
    h = (x_ref[0] * (1.0 + sc_ref[0]) + sh_ref[0]).astype(jnp.bfloat16)
    for (lo, hi), o_ref in zip(_IN_SPLITS, o_refs):
        o_ref[0] = jnp.dot(h, w_ref[:, lo:hi], preferred_element_type=jnp.float32)


def _in_proj(x, sc, sh, w_in):
    Bsz, L, D = x.shape
    tm = min(_PROJ_TM, L)
    w = jnp.pad(w_in, ((0, 0), (0, _IN_PAD - IN_COLS))).astype(jnp.bfloat16)
    return pl.pallas_call(
        _in_proj_body,
        grid=(Bsz, L // tm),
        in_specs=[pl.BlockSpec((1, tm, D), lambda b, i: (b, i, 0)),
                  pl.BlockSpec((1, 1, D), lambda b, i: (b, 0, 0)),
                  pl.BlockSpec((1, 1, D), lambda b, i: (b, 0, 0)),
                  pl.BlockSpec((D, _IN_PAD), lambda b, i: (0, 0))],
        out_specs=[pl.BlockSpec((1, tm, hi - lo), lambda b, i: (b, i, 0)) for lo, hi in _IN_SPLITS],
        out_shape=[jax.ShapeDtypeStruct((Bsz, L, hi - lo), jnp.float32) for lo, hi in _IN_SPLITS],
        compiler_params=pltpu.CompilerParams(dimension_semantics=("arbitrary", "arbitrary"),
                                             vmem_limit_bytes=_VMEM_LIMIT),
        name="in_proj",
    )(x, sc, sh, w)


def _out_proj_ln_body(alpha, ya_ref, yb_ref, yc_ref, yd_ref, w_ref, x_ref, g_ref, lng_ref, lnb_ref, o_ref):
    y = 0.0
    for m, y_ref in enumerate((ya_ref, yb_ref, yc_ref, yd_ref)):
        y = y + jnp.dot(y_ref[0].astype(jnp.bfloat16), w_ref[m * GROUP_W:(m + 1) * GROUP_W, :],
                        preferred_element_type=jnp.float32)
    r = alpha * x_ref[0] + g_ref[0] * y
    mu = jnp.mean(r, -1, keepdims=True)
    rc = r - mu
    var = jnp.mean(rc * rc, -1, keepdims=True)
    o_ref[0] = rc * lax.rsqrt(var + LN_EPS) * lng_ref[...] + lnb_ref[...]


def _out_proj_ln(ys, w_out, x, g, ln_g, ln_b, alpha):
    Bsz, L, D = x.shape
    tm = min(_PROJ_TM, L)
    mix = pl.BlockSpec((1, tm, GROUP_W), lambda b, i: (b, i, 0))
    return pl.pallas_call(
        functools.partial(_out_proj_ln_body, alpha),
        grid=(Bsz, L // tm),
        in_specs=[mix, mix, mix, mix,
                  pl.BlockSpec((D_MIX, D), lambda b, i: (0, 0)),
                  pl.BlockSpec((1, tm, D), lambda b, i: (b, i, 0)),
                  pl.BlockSpec((1, 1, D), lambda b, i: (b, 0, 0)),
                  pl.BlockSpec((1, D), lambda b, i: (0, 0)),
                  pl.BlockSpec((1, D), lambda b, i: (0, 0))],
        out_specs=pl.BlockSpec((1, tm, D), lambda b, i: (b, i, 0)),
        out_shape=jax.ShapeDtypeStruct((Bsz, L, D), jnp.float32),
        compiler_params=pltpu.CompilerParams(dimension_semantics=("arbitrary", "arbitrary"),
                                             vmem_limit_bytes=_VMEM_LIMIT),
        name="out_proj_ln",
    )(*ys, w_out.astype(jnp.bfloat16), x, g, ln_g.reshape(1, D), ln_b.reshape(1, D))


def _shifted_taps(win, n_taps, rows):
    return [win[k:k + rows] for k in range(n_taps)]


_CONV_HALO = 16
_CONV_TILE = 256


def _group_mean_matrix(width, group):
    r = lax.broadcasted_iota(jnp.int32, (width, width), 0) // group
    c = lax.broadcasted_iota(jnp.int32, (width, width), 1) // group
    return jnp.where(r == c, 1.0 / group, 0.0).astype(jnp.float32)


def _conformer_body(L, p_ref, w_ref, b_ref, ng_ref, nb_ref, o_ref, u_ref):
    G = GROUP_W
    pad = (CONF_K - 1) // 2
    halo = jnp.zeros((_CONV_HALO, G), jnp.float32)
    u_ref[0:_CONV_HALO, :] = halo
    u_ref[_CONV_HALO + L:_CONV_HALO + L + _CONV_HALO, :] = halo
    u_ref[_CONV_HALO:_CONV_HALO + L, :] = p_ref[0, :, 0:G] * jax.nn.sigmoid(p_ref[0, :, G:2 * G])
    avg = _group_mean_matrix(G, G // CONF_GROUPS)
    tile = min(_CONV_TILE, L)
    for t in range(L // tile):
        win = u_ref[t * tile:t * tile + tile + 2 * _CONV_HALO, :]
        acc = jnp.zeros((tile, G), jnp.float32) + b_ref[...]
        for k, tap in enumerate(_shifted_taps(win[_CONV_HALO - pad:], CONF_K, tile)):
            acc = acc + tap * w_ref[k:k + 1, :]
        mu = jnp.dot(acc, avg, preferred_element_type=jnp.float32, precision=lax.Precision.HIGHEST)
        cen = acc - mu
        var = jnp.dot(cen * cen, avg, preferred_element_type=jnp.float32, precision=lax.Precision.HIGHEST)
        un = cen * lax.rsqrt(var + LN_EPS) * ng_ref[...] + nb_ref[...]
        o_ref[0, t * tile:(t + 1) * tile, :] = un * jax.nn.sigmoid(un)


def _conformer(pa, dw_w, dw_b, n_g, n_b):
    Bsz, L, _ = pa.shape
    G = GROUP_W
    vec = pl.BlockSpec((1, G), lambda b: (0, 0))
    return pl.pallas_call(
        functools.partial(_conformer_body, L),
        grid=(Bsz,),
        in_specs=[pl.BlockSpec((1, L, 2 * G), lambda b: (b, 0, 0)),
                  pl.BlockSpec((CONF_K, G), lambda b: (0, 0)), vec, vec, vec],
        out_specs=pl.BlockSpec((1, L, G), lambda b: (b, 0, 0)),
        out_shape=jax.ShapeDtypeStruct((Bsz, L, G), jnp.float32),
        scratch_shapes=[pltpu.VMEM((L + 2 * _CONV_HALO, G), jnp.float32)],
        compiler_params=pltpu.CompilerParams(dimension_semantics=("arbitrary",),
                                             vmem_limit_bytes=_VMEM_LIMIT),
        name="conformer",
    )(pa, dw_w, dw_b.reshape(1, G), n_g.reshape(1, G), n_b.reshape(1, G))


def conformer_conv(p, dw_w, dw_b, n_g, n_b):
    a, gate = jnp.split(p, 2, -1)
    u = a * jax.nn.sigmoid(gate)
    u = dwconv(u, dw_w) + dw_b
    Bsz, L, C = u.shape
    un = _standardize(u.reshape(Bsz, L, CONF_GROUPS, C // CONF_GROUPS)).reshape(Bsz, L, C)
    un = (un * n_g + n_b).astype(u.dtype)
    return jax.nn.silu(un)


def axial_rope(rows, head_dim):
    n_f = head_dim // 4
    inv = ROPE_BASE ** (-jnp.arange(n_f, dtype=jnp.float32) / n_f)
    t = jnp.arange(rows * GRID_W)
    r = (t // GRID_W).astype(jnp.float32)
    col = (t % GRID_W).astype(jnp.float32)
    ang = jnp.concatenate([r[:, None] * inv, col[:, None] * inv], -1)
    return jnp.cos(ang), jnp.sin(ang)


def apply_rope(x, cos, sin):
    x1, x2 = jnp.split(x.astype(jnp.float32), 2, -1)
    c = cos[None, :, None, :]
    s = sin[None, :, None, :]
    return jnp.concatenate([x1 * c - x2 * s, x1 * s + x2 * c], -1).astype(x.dtype)


def neighborhood_attention(q_rot, k_rot, v, q_plain, k_ctx, v_ctx, rpb):
    Bsz, S, H, d = q_rot.shape
    rows = S // GRID_W
    kh = min(NA_KH, rows)
    qg = q_rot.reshape(Bsz, rows, GRID_W, H, d)
    kg = k_rot.reshape(Bsz, rows, GRID_W, H, d)
    vg = v.reshape(Bsz, rows, GRID_W, H, d)
    qpg = q_plain.reshape(Bsz, rows, GRID_W, H, d)
    cq = jnp.arange(GRID_W)
    col_idx = jnp.clip(cq - NA_KW // 2, 0, GRID_W - NA_KW)[:, None] + jnp.arange(NA_KW)[None, :]
    col_bias_idx = col_idx - cq[:, None] + (NA_KW - 1)
    scale = d ** -0.5

    def row_block(r):
        rs = jnp.clip(r - kh // 2, 0, rows - kh)
        kr = lax.dynamic_slice_in_dim(kg, rs, kh, axis=1)[:, :, col_idx]
        vr = lax.dynamic_slice_in_dim(vg, rs, kh, axis=1)[:, :, col_idx]
        qr = lax.dynamic_index_in_dim(qg, r, axis=1, keepdims=False)
        qpr = lax.dynamic_index_in_dim(qpg, r, axis=1, keepdims=False)
        row_bias_idx = rs + jnp.arange(kh) - r + (NA_KH - 1)
        bias = rpb[:, row_bias_idx][:, :, col_bias_idx].transpose(0, 2, 1, 3)
        s_loc = jnp.einsum('bchd,bicjhd->bhcij', qr, kr).astype(jnp.float32) * scale + bias
        s_ctx = jnp.einsum('bchd,bnhd->bhcn', qpr, k_ctx).astype(jnp.float32) * scale
        logits = jnp.concatenate([s_loc.reshape(Bsz, H, GRID_W, kh * NA_KW), s_ctx], -1)
        p = jax.nn.softmax(logits, -1).astype(v.dtype)
        p_loc = p[..., :kh * NA_KW].reshape(Bsz, H, GRID_W, kh, NA_KW)
        p_ctx = p[..., kh * NA_KW:]
        return (jnp.einsum('bhcij,bicjhd->bchd', p_loc, vr)
                + jnp.einsum('bhcn,bnhd->bchd', p_ctx, v_ctx))

    out = lax.map(row_block, jnp.arange(rows))
    return out.transpose(1, 0, 2, 3, 4).reshape(Bsz, S, H * d)


_NA_MASK = -1e30


def _na_tables(rows, rpb):
    cos, sin = axial_rope(rows, HEAD_DIM)
    cos_f = jnp.tile(cos, (1, 2 * NA_HEADS))
    sin_s = jnp.tile(jnp.concatenate([-sin, sin], -1), (1, NA_HEADS))
    cq = jnp.arange(GRID_W)
    cs = jnp.clip(cq - NA_KW // 2, 0, GRID_W - NA_KW)
    col = jnp.arange(GRID_W)
    in_band = (col[None, :] >= cs[:, None]) & (col[None, :] < cs[:, None] + NA_KW)
    cb_idx = jnp.clip(col[None, :] - cq[:, None] + (NA_KW - 1), 0, 2 * NA_KW - 2)
    po = jnp.arange(NA_KH)
    rb_idx = jnp.arange(NA_KH)[None, :] - po[:, None] + (NA_KH - 1)
    bias = rpb[:, rb_idx][:, :, :, cb_idx]
    bias = jnp.where(in_band[None, None, None], bias, _NA_MASK)
    bias = bias.transpose(1, 0, 3, 2, 4).reshape(NA_KH, NA_HEADS, GRID_W, NA_KH * GRID_W)
    return cos_f, sin_s, bias


def _rope_lanes(x, cos_f, sin_s):
    n = x.shape[-1]
    hd = HEAD_DIM // 2
    first = (lax.broadcasted_iota(jnp.int32, x.shape, 1) % HEAD_DIM) < hd
    partner = jnp.where(first, pltpu.roll(x, n - hd, 1), pltpu.roll(x, hd, 1))
    return x * cos_f + partner * sin_s


def _softmax_pv(s_parts, v_parts):
    m = s_parts[0].max(-1, keepdims=True)
    for s in s_parts[1:]:
        m = jnp.maximum(m, s.max(-1, keepdims=True))
    l = 0.0
    o = 0.0
    for s, v in zip(s_parts, v_parts):
        p = jnp.exp(s - m)
        l = l + p.sum(-1, keepdims=True)
        o = o + jnp.dot(p.astype(jnp.bfloat16), v, preferred_element_type=jnp.float32)
    return o / l


def _na_body(rows, q_ref, k_ref, v_ref, kc_ref, vc_ref, cosq_ref, sinq_ref, cosk_ref, sink_ref, bias_ref,
             o_ref, krot_ref, vbf_ref):
    r = pl.program_id(1)
    scale = HEAD_DIM ** -0.5

    @pl.when(r == 0)
    def _():
        krot_ref[...] = _rope_lanes(k_ref[0], cosk_ref[...], sink_ref[...]).astype(jnp.bfloat16)
        vbf_ref[...] = v_ref[0].astype(jnp.bfloat16)

    rs = jnp.clip(r - NA_KH // 2, 0, rows - NA_KH)
    start = pl.multiple_of(rs * GRID_W, GRID_W)
    win = NA_KH * GRID_W
    q = q_ref[0] * scale
    q_rot = _rope_lanes(q, cosq_ref[...], sinq_ref[...]).astype(jnp.bfloat16)
    q_plain = q.astype(jnp.bfloat16)
    kw = krot_ref[pl.ds(start, win), :]
    vw = vbf_ref[pl.ds(start, win), :]
    kc = kc_ref[0].astype(jnp.bfloat16)
    vc = vc_ref[0].astype(jnp.bfloat16)
    outs = []
    for h in range(NA_HEADS):
        hs = slice(h * HEAD_DIM, (h + 1) * HEAD_DIM)
        s_loc = lax.dot_general(q_rot[:, hs], kw[:, hs], _NT, preferred_element_type=jnp.float32) + bias_ref[0, h]
        s_ctx = lax.dot_general(q_plain[:, hs], kc[:, hs], _NT, preferred_element_type=jnp.float32)
        outs.append(_softmax_pv([s_loc, s_ctx], [vw[:, hs], vc[:, hs]]))
    o_ref[0] = jnp.concatenate(outs, axis=-1)


def _na_attention(pb, cb, rpb):
    Bsz, S, _ = pb.shape
    Lc = cb.shape[1]
    rows = S // GRID_W
    G = GROUP_W
    cos_f, sin_s, bias = _na_tables(rows, rpb)

    def bias_idx(b, r):
        rs = jnp.clip(r - NA_KH // 2, 0, rows - NA_KH)
        return (r - rs, 0, 0, 0)

    return pl.pallas_call(
        functools.partial(_na_body, rows),
        grid=(Bsz, rows),
        in_specs=[pl.BlockSpec((1, GRID_W, G), lambda b, r: (b, r, 0)),
                  pl.BlockSpec((1, S, G), lambda b, r: (b, 0, 1)),
                  pl.BlockSpec((1, S, G), lambda b, r: (b, 0, 2)),
                  pl.BlockSpec((1, Lc, G), lambda b, r: (b, 0, 1)),
                  pl.BlockSpec((1, Lc, G), lambda b, r: (b, 0, 2)),
                  pl.BlockSpec((GRID_W, G), lambda b, r: (r, 0)),
                  pl.BlockSpec((GRID_W, G), lambda b, r: (r, 0)),
                  pl.BlockSpec((S, G), lambda b, r: (0, 0)),
                  pl.BlockSpec((S, G), lambda b, r: (0, 0)),
                  pl.BlockSpec((1, NA_HEADS, GRID_W, NA_KH * GRID_W), bias_idx)],
        out_specs=pl.BlockSpec((1, GRID_W, G), lambda b, r: (b, r, 0)),
        out_shape=jax.ShapeDtypeStruct((Bsz, S, G), jnp.float32),
        scratch_shapes=[pltpu.VMEM((S, G), jnp.bfloat16), pltpu.VMEM((S, G), jnp.bfloat16)],
        compiler_params=pltpu.CompilerParams(dimension_semantics=("arbitrary", "arbitrary"),
                                             vmem_limit_bytes=_VMEM_LIMIT),
        name="na_attention",
    )(pb, pb, pb, cb, cb, cos_f, sin_s, cos_f, sin_s, bias)


def _ctx_attn_body(q_ref, k_ref, v_ref, o_ref):
    q = (q_ref[0] * HEAD_DIM ** -0.5).astype(jnp.bfloat16)
    k = k_ref[0].astype(jnp.bfloat16)
    v = v_ref[0].astype(jnp.bfloat16)
    outs = []
    for h in range(NA_HEADS):
        hs = slice(h * HEAD_DIM, (h + 1) * HEAD_DIM)
        s = lax.dot_general(q[:, hs], k[:, hs], _NT, preferred_element_type=jnp.float32)
        outs.append(_softmax_pv([s], [v[:, hs]]))
    o_ref[0] = jnp.concatenate(outs, axis=-1)


def _ctx_attention(cb):
    Bsz, Lc, _ = cb.shape
    G = GROUP_W
    return pl.pallas_call(
        _ctx_attn_body,
        grid=(Bsz,),
        in_specs=[pl.BlockSpec((1, Lc, G), lambda b: (b, 0, 0)),
                  pl.BlockSpec((1, Lc, G), lambda b: (b, 0, 1)),
                  pl.BlockSpec((1, Lc, G), lambda b: (b, 0, 2))],
        out_specs=pl.BlockSpec((1, Lc, G), lambda b: (b, 0, 0)),
        out_shape=jax.ShapeDtypeStruct((Bsz, Lc, G), jnp.float32),
        compiler_params=pltpu.CompilerParams(dimension_semantics=("arbitrary",)),
        name="ctx_attention",
    )(cb, cb, cb)


def context_attention(q, k, v):
    s = jnp.einsum('bqhd,bkhd->bhqk', q, k).astype(jnp.float32) * (q.shape[-1] ** -0.5)
    p = jax.nn.softmax(s, -1).astype(v.dtype)
    return jnp.einsum('bhqk,bkhd->bqhd', p, v)


def hyena_filters(L, w1, b1, w2, b2, w3, decay):
    tn = jnp.arange(L, dtype=jnp.float32)[:, None] / L
    bands = jnp.arange(1, HY_BANDS + 1, dtype=jnp.float32)[None, :]
    ang = 2.0 * math.pi * bands * tn
    z = jnp.concatenate([tn, jnp.sin(ang), jnp.cos(ang)], -1)
    hmid = jnp.sin(HY_SIN_FREQ * (z @ w1.astype(jnp.float32) + b1.astype(jnp.float32)))
    hmid = jnp.sin(HY_SIN_FREQ * (hmid @ w2.astype(jnp.float32) + b2.astype(jnp.float32)))
    k = (hmid @ w3.astype(jnp.float32)) * jnp.exp(-tn * decay.astype(jnp.float32))
    k = k / (jnp.sum(jnp.abs(k), axis=0, keepdims=True) + 1e-6)
    return k[:, :GROUP_W], k[:, GROUP_W:]


def bidir_fftconv(u, k_fwd, k_bwd):
    L = u.shape[1]
    k2 = jnp.concatenate([k_fwd, jnp.zeros_like(k_fwd[:1]), k_bwd[1:][::-1]], 0)
    kf = jnp.fft.rfft(k2, n=2 * L, axis=0)
    uf = jnp.fft.rfft(u.astype(jnp.float32), n=2 * L, axis=1)
    return jnp.fft.irfft(uf * kf[None], n=2 * L, axis=1)[:, :L].astype(u.dtype)


def hyena(p, short_w, short_b, w1, b1, w2, b2, w3, decay, skip):
    L = p.shape[1]
    p = dwconv(p, short_w) + short_b
    x0, x1, v = jnp.split(p, 3, -1)
    k_fwd, k_bwd = hyena_filters(L, w1, b1, w2, b2, w3, decay)
    u = v * x1
    y = bidir_fftconv(u, k_fwd, k_bwd) + u * skip
    return y * x0


def segsum(a):
    T = a.shape[-1]
    a_rep = jnp.broadcast_to(a[..., None], a.shape + (T,))
    a_rep = jnp.where(jnp.tril(jnp.ones((T, T), bool), -1), a_rep, 0.0)
    s = jnp.cumsum(a_rep, axis=-2)
    return jnp.where(jnp.tril(jnp.ones((T, T), bool)), s, -jnp.inf)


def ssd_scan(x, dt, A, Bh, Ch, init_state, want_y):
    Bsz, L, H, P = x.shape
    N = Bh.shape[-1]
    nc = L // SSD_CHUNK
    xd = (x.astype(jnp.float32) * dt[..., None]).reshape(Bsz, nc, SSD_CHUNK, H, P)
    a = (dt * A).reshape(Bsz, nc, SSD_CHUNK, H).transpose(0, 3, 1, 2)
    Bc = Bh.astype(jnp.float32).reshape(Bsz, nc, SSD_CHUNK, H, N)
    Cc = Ch.astype(jnp.float32).reshape(Bsz, nc, SSD_CHUNK, H, N)
    a_cum = jnp.cumsum(a, -1)
    decay_states = jnp.exp(a_cum[..., -1:] - a_cum)
    states = jnp.einsum('bclhn,bhcl,bclhp->bchpn', Bc, decay_states, xd)
    states = jnp.concatenate([init_state[:, None], states], 1)
    decay_chunk = jnp.exp(segsum(jnp.pad(a_cum[..., -1], ((0, 0), (0, 0), (1, 0)))))
    new_states = jnp.einsum('bhzc,bchpn->bzhpn', decay_chunk, states)
    prev_states, final = new_states[:, :-1], new_states[:, -1]
    if not want_y:
        return None, final
    Lmat = jnp.exp(segsum(a))
    y_diag = jnp.einsum('bclhn,bcshn,bhcls,bcshp->bclhp', Cc, Bc, Lmat, xd)
    y_off = jnp.einsum('bclhn,bchpn,bhcl->bclhp', Cc, prev_states, jnp.exp(a_cum))
    return (y_diag + y_off).reshape(Bsz, L, H, P), final


def _flip(t):
    return jnp.flip(t, axis=1)


def ssd_bidir(z, xbc, dt_raw, init_f, init_b, want_y, conv_w, conv_b, a_log, dt_bias, d_skip, norm_g):
    Bsz, L, _ = xbc.shape
    xbc = jax.nn.silu(dwconv(xbc, conv_w) + conv_b)
    xs = xbc[..., :GROUP_W].reshape(Bsz, L, SSD_HEADS, SSD_HEAD_DIM)
    bc = xbc[..., GROUP_W:].reshape(Bsz, L, 2, SSD_GROUPS, SSD_STATE)
    rep = SSD_HEADS // SSD_GROUPS
    Bh = jnp.repeat(bc[:, :, 0], rep, axis=2)
    Ch = jnp.repeat(bc[:, :, 1], rep, axis=2)
    dt = jax.nn.softplus(dt_raw.astype(jnp.float32).reshape(Bsz, L, 2, SSD_HEADS) + dt_bias.astype(jnp.float32))
    A = -jnp.exp(a_log.astype(jnp.float32))
    y_f, s_f = ssd_scan(xs, dt[:, :, 0], A[0], Bh, Ch, init_f, want_y)
    y_b, s_b = ssd_scan(_flip(xs), _flip(dt[:, :, 1]), A[1], _flip(Bh), _flip(Ch), init_b, want_y)
    if not want_y:
        return None, s_f, s_b
    y = y_f + _flip(y_b) + xs.astype(jnp.float32) * d_skip.astype(jnp.float32)[:, None]
    yg = (y.reshape(Bsz, L, GROUP_W) * jax.nn.silu(z.astype(jnp.float32)))
    yg = yg.reshape(Bsz, L, SSD_GROUPS, GROUP_W // SSD_GROUPS)
    yg = yg * lax.rsqrt(jnp.mean(jnp.square(yg), -1, keepdims=True) + LN_EPS)
    return (yg.reshape(Bsz, L, GROUP_W) * norm_g).astype(z.dtype), s_f, s_b


_HY_TB = 256
_HY_CB = 8
_LANES = 128
_SUBLANES = 8


def _hyena_filter_body(L, w1_ref, b1_ref, w2_ref, b2_ref, w3_ref, dec_ref, o_ref):
    G = GROUP_W
    hp = lax.Precision.HIGHEST
    p = lax.broadcasted_iota(jnp.int32, (2 * L, _LANES), 0)
    lane = lax.broadcasted_iota(jnp.int32, (2 * L, _LANES), 1)
    tn = jnp.abs(p - L).astype(jnp.float32) / L
    band = ((lane - 1) % HY_BANDS + 1).astype(jnp.float32)
    ang = 2.0 * math.pi * band * tn
    z = jnp.where(lane == 0, tn, jnp.where(lane <= HY_BANDS, jnp.sin(ang), jnp.cos(ang)))
    z = jnp.where(lane < HY_EMB, z, 0.0)
    h = jnp.sin(HY_SIN_FREQ * (jnp.dot(z, w1_ref[...], precision=hp, preferred_element_type=jnp.float32) + b1_ref[...]))
    h = jnp.sin(HY_SIN_FREQ * (jnp.dot(h, w2_ref[...], precision=hp, preferred_element_type=jnp.float32) + b2_ref[...]))
    k = jnp.dot(h, w3_ref[...], precision=hp, preferred_element_type=jnp.float32) * jnp.exp(-tn[:, 0:1] * dec_ref[...])
    kf, kb = k[:, 0:G], k[:, G:2 * G]
    n = lax.broadcasted_iota(jnp.int32, (2 * L, G), 0) - L
    nf = jnp.sum(jnp.where(n >= 0, jnp.abs(kf), 0.0), axis=0, keepdims=True) + 1e-6
    nb = jnp.sum(jnp.where(n <= 0, jnp.where(n > -L, jnp.abs(kb), 0.0), 0.0), axis=0, keepdims=True) + 1e-6
    rev = jnp.where(n > 0, kb / nb, jnp.where(n > -L, kf / nf, 0.0))
    o_ref[...] = rev.T


def _hyena_filters_rev(L, w1, b1, w2, b2, w3, decay):
    G = GROUP_W
    hp = _LANES - HY_HIDDEN
    w1p = jnp.pad(w1.astype(jnp.float32), ((0, _LANES - HY_EMB), (0, hp)))
    args = (w1p, jnp.pad(b1.reshape(1, -1), ((0, 0), (0, hp))), jnp.pad(w2, ((0, hp), (0, hp))),
            jnp.pad(b2.reshape(1, -1), ((0, 0), (0, hp))), jnp.pad(w3, ((0, hp), (0, 0))), decay.reshape(1, -1))
    return pl.pallas_call(
        functools.partial(_hyena_filter_body, L),
        out_shape=jax.ShapeDtypeStruct((G, 2 * L), jnp.float32),
        compiler_params=pltpu.CompilerParams(vmem_limit_bytes=_VMEM_LIMIT),
        name="hyena_filters",
    )(*[a.astype(jnp.float32) for a in args])


def _hyena_pre_body(L, p_ref, w_ref, b_ref, u_ref, ut_ref, x0_ref, xp_ref):
    G = GROUP_W
    W = 3 * G
    halo = jnp.zeros((_CONV_HALO, W), jnp.float32)
    xp_ref[0:_CONV_HALO, :] = halo
    xp_ref[_CONV_HALO + L:_CONV_HALO + L + _CONV_HALO, :] = halo
    xp_ref[_CONV_HALO:_CONV_HALO + L, :] = p_ref[0]
    pad = (HY_SHORT - 1) // 2
    tile = min(_CONV_TILE, L)
    for t in range(L // tile):
        win = xp_ref[t * tile:t * tile + tile + 2 * _CONV_HALO, :]
        acc = jnp.zeros((tile, W), jnp.float32) + b_ref[...]
        for k, tap in enumerate(_shifted_taps(win[_CONV_HALO - pad:], HY_SHORT, tile)):
            acc = acc + tap * w_ref[k:k + 1, :]
        rows = slice(t * tile, (t + 1) * tile)
        u = acc[:, 2 * G:3 * G] * acc[:, G:2 * G]
        x0_ref[0, rows, :] = acc[:, 0:G]
        u_ref[0, rows, :] = u
        ut_ref[0, :, rows] = u.T


def _hyena_pre(py, short_w, short_b):
    Bsz, L, W = py.shape
    G = GROUP_W
    f32 = jnp.float32
    return pl.pallas_call(
        functools.partial(_hyena_pre_body, L),
        grid=(Bsz,),
        in_specs=[pl.BlockSpec((1, L, W), lambda b: (b, 0, 0)),
                  pl.BlockSpec((HY_SHORT, W), lambda b: (0, 0)),
                  pl.BlockSpec((1, W), lambda b: (0, 0))],
        out_specs=[pl.BlockSpec((1, L, G), lambda b: (b, 0, 0)),
                   pl.BlockSpec((1, G, L), lambda b: (b, 0, 0)),
                   pl.BlockSpec((1, L, G), lambda b: (b, 0, 0))],
        out_shape=[jax.ShapeDtypeStruct((Bsz, L, G), f32), jax.ShapeDtypeStruct((Bsz, G, L), f32),
                   jax.ShapeDtypeStruct((Bsz, L, G), f32)],
        scratch_shapes=[pltpu.VMEM((L + 2 * _CONV_HALO, W), f32)],
        compiler_params=pltpu.CompilerParams(dimension_semantics=("arbitrary",), vmem_limit_bytes=_VMEM_LIMIT),
        name="hyena_pre",
    )(py, short_w, short_b.reshape(1, W))


def _toeplitz_tile(w8):
    TB = _HY_TB
    nq = 2 * TB // _LANES
    per = _LANES // _SUBLANES
    i = lax.broadcasted_iota(jnp.int32, (_SUBLANES, _LANES), 0)
    l = lax.broadcasted_iota(jnp.int32, (_SUBLANES, _LANES), 1)
    rolled = [[pltpu.roll(w8[:, q * _LANES:(q + 1) * _LANES], (_SUBLANES * k) % _LANES, 1, stride=1, stride_axis=0)
               for k in range(per)] for q in range(nq)]
    row_blocks = []
    for rg in range(TB // _SUBLANES):
        k = rg % per
        pieces = []
        for lg in range(TB // _LANES):
            o = TB + _LANES * lg - _SUBLANES * rg
            q, rho = divmod(o, _LANES)
            if rho == 0:
                pieces.append(jnp.where(l - i < 0, rolled[q - 1][k], rolled[q][k]))
            else:
                pieces.append(jnp.where(l + rho - i >= _LANES, rolled[q + 1][k], rolled[q][k]))
        row_blocks.append(jnp.concatenate(pieces, axis=1))
    return jnp.concatenate(row_blocks, axis=0).astype(jnp.bfloat16)


def _hyena_conv_body(L, Bsz, g_ref, u_ref, o_ref):
    TB = min(_HY_TB, L)
    nb = L // TB
    cols = Bsz * nb
    lane = lax.broadcasted_iota(jnp.int32, (TB, cols), 1) % nb

    def channel(ci, carry):
        u = u_ref[ci] if nb == 1 else u_ref[:, ci].reshape(cols, TB)
        u = u.astype(jnp.bfloat16)
        acc = jnp.zeros((TB, cols), jnp.float32)
        for d in range(-(nb - 1), nb):
            start = L - TB * d - TB
            w8 = jnp.broadcast_to(g_ref[pl.ds(ci, 1), start:start + 2 * TB], (_SUBLANES, 2 * TB))
            z = lax.dot_general(_toeplitz_tile(w8), u, _NT, preferred_element_type=jnp.float32)
            if d != 0:
                z = jnp.where((lane - d >= 0) & (lane - d < nb), pltpu.roll(z, d % cols, 1), 0.0)
            acc = acc + z
        if nb == 1:
            o_ref[ci] = acc.T
        else:
            o_ref[:, ci] = acc.T.reshape(Bsz, nb, TB)
        return carry

    lax.fori_loop(0, _HY_CB, channel, 0)


def _hyena_conv(g_rev, u_t):
    Bsz, G, L = u_t.shape
    TB = min(_HY_TB, L)
    nb = L // TB
    if nb == 1:
        blk = pl.BlockSpec((_HY_CB, Bsz, TB), lambda c: (c, 0, 0))
        operand, out_shape = jnp.swapaxes(u_t, 0, 1), (G, Bsz, TB)
    else:
        blk = pl.BlockSpec((Bsz, _HY_CB, nb, TB), lambda c: (0, c, 0, 0))
        operand, out_shape = u_t.reshape(Bsz, G, nb, TB), (Bsz, G, nb, TB)
    out = pl.pallas_call(
        functools.partial(_hyena_conv_body, L, Bsz),
        grid=(G // _HY_CB,),
        in_specs=[pl.BlockSpec((_HY_CB, 2 * L), lambda c: (c, 0)), blk],
        out_specs=blk,
        out_shape=jax.ShapeDtypeStruct(out_shape, jnp.float32),
        compiler_params=pltpu.CompilerParams(dimension_semantics=("arbitrary",), vmem_limit_bytes=_VMEM_LIMIT),
        name="hyena_conv",
    )(g_rev, operand)
    return jnp.swapaxes(out, 0, 1) if nb == 1 else out.reshape(Bsz, G, L)


def _hyena_post_body(yt_ref, u_ref, x0_ref, skip_ref, o_ref):
    o_ref[0] = (yt_ref[0].T + u_ref[0] * skip_ref[...]) * x0_ref[0]


def _hyena_post(y_t, u, x0, skip):
    Bsz, L, G = u.shape
    tok = pl.BlockSpec((1, L, G), lambda b: (b, 0, 0))
    return pl.pallas_call(
        _hyena_post_body,
        grid=(Bsz,),
        in_specs=[pl.BlockSpec((1, G, L), lambda b: (b, 0, 0)), tok, tok, pl.BlockSpec((1, G), lambda b: (0, 0))],
        out_specs=tok,
        out_shape=jax.ShapeDtypeStruct((Bsz, L, G), jnp.float32),
        compiler_params=pltpu.CompilerParams(dimension_semantics=("arbitrary",), vmem_limit_bytes=_VMEM_LIMIT),
        name="hyena_post",
    )(y_t, u, x0, skip.reshape(1, G))


def _hyena(py, short_w, short_b, w1, b1, w2, b2, w3, decay, skip):
    L = py.shape[1]
    g_rev = _hyena_filters_rev(L, w1, b1, w2, b2, w3, decay)
    u, u_t, x0 = _hyena_pre(py, short_w, short_b)
    return _hyena_post(_hyena_conv(g_rev, u_t), u, x0, skip)


def _split3_dot(a, b_bf16, dims=None):
    hi = a.astype(jnp.bfloat16)
    r1 = a - hi.astype(jnp.float32)
    mid = r1.astype(jnp.bfloat16)
    lo = (r1 - mid.astype(jnp.float32)).astype(jnp.bfloat16)
    out = 0.0
    for part in (hi, mid, lo):
        if dims is None:
            out = out + jnp.dot(part, b_bf16, preferred_element_type=jnp.float32)
        else:
            out = out + jnp.dot(b_bf16, part, preferred_element_type=jnp.float32)
    return out


def _softplus(x):
    return jnp.maximum(x, 0.0) + jnp.log(1.0 + jnp.exp(-jnp.abs(x)))


def _ssd_body(L, zx_ref, dtc_ref, dtr_ref, cw_ref, cb_ref, arow_ref, acol_ref, brow_ref, bcol_ref, dsk_ref,
              ng_ref, init_ref, o_ref, fin_ref, xp_ref, xc_ref, bt_ref, y_ref, ccol_ref, crow_ref, edec_ref,
              tot_ref):
    G = GROUP_W
    Q = SSD_CHUNK
    nc = L // Q
    P = SSD_HEAD_DIM
    N = SSD_STATE
    H = SSD_HEADS
    f32 = jnp.float32
    bf16 = jnp.bfloat16
    halo = jnp.zeros((_CONV_HALO, XBC_W), f32)
    xp_ref[0:_CONV_HALO, :] = halo
    xp_ref[_CONV_HALO + L:_CONV_HALO + L + _CONV_HALO, :] = halo
    xp_ref[_CONV_HALO:_CONV_HALO + L, :] = zx_ref[0, :, G:G + XBC_W]
    pad = (SSD_CONV - 1) // 2
    tile = min(_CONV_TILE, L)
    for t in range(L // tile):
        win = xp_ref[t * tile:t * tile + tile + 2 * _CONV_HALO, :]
        acc = jnp.zeros((tile, XBC_W), f32) + cb_ref[...]
        for k, tap in enumerate(_shifted_taps(win[_CONV_HALO - pad:], SSD_CONV, tile)):
            acc = acc + tap * cw_ref[k:k + 1, :]
        xc_ref[t * tile:(t + 1) * tile, :] = acc * jax.nn.sigmoid(acc)
    for c in range(nc):
        bt_ref[c] = xc_ref[c * Q:(c + 1) * Q, G:G + SSD_GROUPS * N].T
    dt_col = _softplus(dtc_ref[0] + brow_ref[...])
    a_col = dt_col * arow_ref[...]
    a_row = _softplus(dtr_ref[0] + bcol_ref[...]) * acol_ref[...]
    a_stack = jnp.concatenate([a_row[:, c * Q:(c + 1) * Q] for c in range(nc)], axis=0)
    ri = lax.broadcasted_iota(jnp.int32, (Q, Q), 0)
    ci = lax.broadcasted_iota(jnp.int32, (Q, Q), 1)
    one = lambda m: jnp.where(m, 1.0, 0.0).astype(bf16)
    tot_ref[...] = _split3_dot(a_stack, jnp.ones((Q, Q), bf16))

    def direction(d, y_store):
        fwd = d == 0
        m_col = one(ci <= ri) if fwd else one(ci >= ri)
        for c in range(nc):
            ccol_ref[c * Q:(c + 1) * Q, :] = _split3_dot(a_col[c * Q:(c + 1) * Q, :], m_col, dims="left")
        crow_ref[...] = _split3_dot(a_stack, one(ri <= ci) if fwd else one(ri >= ci))
        edec_ref[...] = _split3_dot(a_stack, one(ri > ci) if fwd else one(ri < ci))
        keep = (ri >= ci) if fwd else (ri <= ci)

        def chunk(step, states):
            c = step if fwd else nc - 1 - step
            r0 = pl.multiple_of(c * Q, Q)
            j0 = pl.multiple_of(c * 2 * H, 2 * H)
            xc = xc_ref[pl.ds(r0, Q), :]
            ccol = ccol_ref[pl.ds(r0, Q), :]
            crow = crow_ref[pl.ds(j0, 2 * H), :]
            edec = edec_ref[pl.ds(j0, 2 * H), :]
            tot = tot_ref[pl.ds(j0, 2 * H), :]
            dtc = _softplus(dtc_ref[0, pl.ds(r0, Q), :] + brow_ref[...])
            bt = bt_ref[c]
            new_states = []
            outs = []
            for g in range(SSD_GROUPS):
                cm = xc[:, G + SSD_GROUPS * N + g * N:G + SSD_GROUPS * N + (g + 1) * N].astype(bf16)
                bm = xc[:, G + g * N:G + (g + 1) * N].astype(bf16)
                cb = lax.dot_general(cm, bm, _NT, preferred_element_type=f32)
                for hh in range(H // SSD_GROUPS):
                    h = g * (H // SSD_GROUPS) + hh
                    j = d * H + h
                    col = jnp.broadcast_to(ccol[:, j:j + 1], (Q, Q))
                    lmat = jnp.exp(jnp.where(keep, col - crow[j:j + 1, :], _NEG))
                    xd = (xc[:, h * P:(h + 1) * P] * jnp.broadcast_to(dtc[:, j:j + 1], (Q, P))).astype(bf16)
                    st = states[h]
                    y = jnp.dot((cb * lmat).astype(bf16), xd, preferred_element_type=f32)
                    y = y + jnp.dot(cm, st.astype(bf16), preferred_element_type=f32) * jnp.exp(col[:, 0:P])
                    outs.append(y)
                    btd = (bt[g * N:(g + 1) * N, :] * jnp.exp(edec[j:j + 1, :])).astype(bf16)
                    new_states.append(jnp.exp(tot[j:j + 1, 0:P]) * st
                                      + jnp.dot(btd, xd, preferred_element_type=f32))
            y_store(r0, jnp.concatenate(outs, axis=-1))
            return tuple(new_states)

        init = tuple(init_ref[0, d, h] for h in range(H))
        final = lax.fori_loop(0, nc, chunk, init)
        for h in range(H):
            fin_ref[0, d, h] = final[h]

    def store_fwd(r0, y):
        y_ref[pl.ds(r0, Q), :] = y

    def store_bwd(r0, y):
        y_ref[pl.ds(r0, Q), :] += y

    direction(0, store_fwd)
    direction(1, store_bwd)
    gw = G // SSD_GROUPS
    for t in range(L // tile):
        rows = slice(t * tile, (t + 1) * tile)
        z = zx_ref[0, rows, 0:G]
        yg = (y_ref[rows, :] + xc_ref[rows, 0:G] * dsk_ref[...]) * (z * jax.nn.sigmoid(z))
        parts = []
        for g in range(SSD_GROUPS):
            v = yg[:, g * gw:(g + 1) * gw]
            parts.append(v * lax.rsqrt(jnp.mean(v * v, -1, keepdims=True) + LN_EPS))
        o_ref[0, rows, :] = jnp.concatenate(parts, axis=-1) * ng_ref[...]


def _ssd(pzx, pdt, init, conv_w, conv_b, a_log, dt_bias, d_skip, norm_g):
    Bsz, L, _ = pzx.shape
    G, H, Q = GROUP_W, SSD_HEADS, SSD_CHUNK
    lanes = pdt.shape[-1]
    nc = L // Q
    neg_a = -jnp.exp(a_log.astype(jnp.float32)).reshape(1, 2 * H)
    a_rowv = jnp.pad(neg_a, ((0, 0), (0, lanes - 2 * H)))
    b_rowv = jnp.pad(dt_bias.astype(jnp.float32).reshape(1, 2 * H), ((0, 0), (0, lanes - 2 * H)))
    dt_rows = jnp.swapaxes(pdt[:, :, :2 * H], 1, 2)
    d_lane = jnp.repeat(d_skip.astype(jnp.float32), SSD_HEAD_DIM).reshape(1, G)
    const = lambda shape: pl.BlockSpec(shape, lambda b: (0,) * len(shape))
    st_spec = pl.BlockSpec((1, 2, H, SSD_STATE, SSD_HEAD_DIM), lambda b: (b, 0, 0, 0, 0))
    f32 = jnp.float32
    return pl.pallas_call(
        functools.partial(_ssd_body, L),
        grid=(Bsz,),
        in_specs=[pl.BlockSpec((1, L, G + XBC_W), lambda b: (b, 0, 0)),
                  pl.BlockSpec((1, L, lanes), lambda b: (b, 0, 0)),
                  pl.BlockSpec((1, 2 * H, L), lambda b: (b, 0, 0)),
                  const((SSD_CONV, XBC_W)), const((1, XBC_W)),
                  const((1, lanes)), const((2 * H, 1)), const((1, lanes)), const((2 * H, 1)),
                  const((1, G)), const((1, G)), st_spec],
        out_specs=[pl.BlockSpec((1, L, G), lambda b: (b, 0, 0)), st_spec],
        out_shape=[jax.ShapeDtypeStruct((Bsz, L, G), f32),
                   jax.ShapeDtypeStruct((Bsz, 2, H, SSD_STATE, SSD_HEAD_DIM), f32)],
        scratch_shapes=[pltpu.VMEM((L + 2 * _CONV_HALO, XBC_W), f32),
                        pltpu.VMEM((L, XBC_W), f32),
                        pltpu.VMEM((nc, SSD_GROUPS * SSD_STATE, Q), f32),
                        pltpu.VMEM((L, G), f32),
                        pltpu.VMEM((L, lanes), f32),
                        pltpu.VMEM((nc * 2 * H, Q), f32),
                        pltpu.VMEM((nc * 2 * H, Q), f32),
                        pltpu.VMEM((nc * 2 * H, Q), f32)],
        compiler_params=pltpu.CompilerParams(dimension_semantics=("arbitrary",),
                                             vmem_limit_bytes=_VMEM_LIMIT),
        name="ssd",
    )(pzx, pdt, dt_rows, conv_w, conv_b.reshape(1, XBC_W), a_rowv, neg_a.reshape(2 * H, 1),
      b_rowv, dt_bias.astype(f32).reshape(2 * H, 1), d_lane, norm_g.reshape(1, G), init)


def _mixer_ln(x, xc, mod, mod_c, ctx_out, alpha, w_in, w_out, ln_g, ln_b, conf, rpb, hy, ssd):
    Bsz = x.shape[0]
    sh, sc, g = mod
    shc, scc, gc = mod_c
    pa, pb, py, pzx, pdt = _in_proj(x, sc, sh, w_in)
    ca, cb, cy, czx, cdt = _in_proj(xc, scc, shc, w_in)
    zero = jnp.zeros((Bsz, 2, SSD_HEADS, SSD_STATE, SSD_HEAD_DIM), jnp.float32)
    y_dc, ctx_states = _ssd(czx, cdt, zero, *ssd)
    y_d, _ = _ssd(pzx, pdt, ctx_states, *ssd)
    ys = [_conformer(pa, *conf), _na_attention(pb, cb, rpb), _hyena(py, *hy), y_d]
    x_new = _out_proj_ln(ys, w_out, x, g, ln_g, ln_b, alpha)
    if not ctx_out:
        return x_new, None
    ycs = [_conformer(ca, *conf), _ctx_attention(cb), _hyena(cy, *hy), y_dc]
    return x_new, _out_proj_ln(ycs, w_out, xc, gc, ln_g, ln_b, alpha)


_NEG = -1e30
_ROUTE_TT = 256
_DENSE_TT = 1024
_DENSE_PARTS = 4
_DENSE_EC = 1024
_NT = (((1,), (1,)), ((), ()))


def _bf16_round(x):
    return x.astype(jnp.bfloat16).astype(jnp.float32)


def _oddeven_sort_pairs(n):
    pairs = []
    p = 1
    while p < n:
        k = p
        while k >= 1:
            for j in range(k % p, n - k, 2 * k):
                for i in range(min(k, n - j - k)):
                    if (i + j) // (2 * p) == (i + j + k) // (2 * p):
                        pairs.append((i + j, i + j + k))
            k //= 2
        p *= 2
    return pairs


_SORT16 = _oddeven_sort_pairs(PEER_TOPK)


def _order_pair(vs, i, j):
    a, b = vs[i], vs[j]
    if b is None:
        return
    if a is None:
        vs[i], vs[j] = b, None
        return
    vs[i], vs[j] = jnp.maximum(a, b), jnp.minimum(a, b)


def _top16_replicated(vs):
    vs = list(vs) + [None] * (PEER_TOPK - len(vs))
    for i, j in _SORT16:
        _order_pair(vs, i, j)
    for shift in (4, 2, 1):
        other = [None if v is None else pltpu.roll(v, shift, 0) for v in vs]
        merged = []
        for k in range(PEER_TOPK):
            a, b = vs[k], other[PEER_TOPK - 1 - k]
            merged.append(b if a is None else a if b is None else jnp.maximum(a, b))
        vs = merged
        stride = PEER_TOPK // 2
        while stride >= 1:
            for i in range(PEER_TOPK):
                if (i // stride) % 2 == 0:
                    _order_pair(vs, i, i + stride)
            stride //= 2
    return vs


def _pack_sublanes(blocks):
    sub = lax.broadcasted_iota(jnp.int32, blocks[0].shape, 0)
    out = blocks[0]
    for r in range(1, len(blocks)):
        out = jnp.where(sub == r, blocks[r], out)
    return out


def _pair_candidates(r1, lo1, hi1, r2, lo2, hi2, op, fill):
    keep = lax.broadcasted_iota(jnp.int32, lo1.shape, 0) >= 4
    pieces = [op(r1[0], lo2), op(r1[0], hi2), op(r1[1], lo2), op(r1[2], lo2), op(r1[3], lo2), op(r2[0], hi1)]
    for b in range(3):
        pieces.append(jnp.where(keep, op(r2[b], lo1), fill))
    return pieces


def _peer_route_body(x_ref, sc_ref, sh_ref, wqT_ref, keys_ref, hm_ref, e1_ref, e2_ref, pthr_ref, qT_ref):
    hm = (x_ref[0] * (1.0 + sc_ref[0]) + sh_ref[0]).astype(jnp.bfloat16)
    hm_ref[0] = hm
    qT_ref[...] = lax.dot_general(wqT_ref[...], hm, _NT, preferred_element_type=jnp.float32)
    nblk = PEER_KEYS // 8

    def head(h, carry):
        base = pl.multiple_of(h * PEER_QDIM, PEER_QDIM)
        half_q = PEER_QDIM // 2
        s_both = []
        for p in range(2):
            qb = qT_ref[pl.ds(base + p * half_q, half_q), :].astype(jnp.bfloat16)
            s_both.append(jnp.dot(keys_ref[h, p], qb, preferred_element_type=jnp.float32))
        for half in range(_ROUTE_TT // 128):
            lanes = slice(half * 128, (half + 1) * 128)
            s1 = s_both[0][:, lanes]
            s2 = s_both[1][:, lanes]
            r1 = _top16_replicated([s1[8 * k:8 * k + 8] for k in range(nblk)])
            r2 = _top16_replicated([s2[8 * k:8 * k + 8] for k in range(nblk)])
            lo1, hi1 = _pack_sublanes(r1[:8]), _pack_sublanes(r1[8:])
            lo2, hi2 = _pack_sublanes(r2[:8]), _pack_sublanes(r2[8:])
            cand = _pair_candidates(r1, lo1, hi1, r2, lo2, hi2, lambda a, b: a + b, _NEG)
            cv = _top16_replicated(cand)
            top, thr = cv[0], cv[PEER_TOPK - 1]
            z = jnp.ones_like(top)
            for k in range(1, PEER_TOPK):
                z = z + jnp.exp(cv[k] - top)
            rz = 1.0 / z
            f1 = lambda v: _bf16_round(jnp.exp(v - r1[0]) * rz)
            f2 = lambda v: _bf16_round(jnp.exp(v - r2[0]))
            prod = _pair_candidates([f1(v) for v in r1[:4]], f1(lo1), f1(hi1), [f2(v) for v in r2[:3]], f2(lo2), f2(hi2),
                                    lambda a, b: _bf16_round(a * b), 0.0)
            low = None
            for cpiece, ppiece in zip(cand, prod):
                sel = jnp.where(cpiece >= thr, ppiece, 1e30)
                low = sel if low is None else jnp.minimum(low, sel)
            e1_ref[h, :, lanes] = jnp.exp(s1 - r1[0][0:1]) * rz[0:1]
            e2_ref[h, :, lanes] = (jnp.exp(s2 - r2[0][0:1])).astype(jnp.bfloat16)
            pthr_ref[h, :, lanes] = jnp.min(low, axis=0, keepdims=True)
        return carry

    lax.fori_loop(0, PEER_HEADS, head, 0)


def _peer_route(x, sc, sh, wqT, keys):
    Bsz, S, D = x.shape
    assert S % _ROUTE_TT == 0, S
    nt = S // _ROUTE_TT
    T = Bsz * S
    tab = jax.ShapeDtypeStruct((PEER_HEADS, PEER_KEYS, T), jnp.float32)
    return pl.pallas_call(
        _peer_route_body,
        grid=(Bsz, nt),
        in_specs=[pl.BlockSpec((1, _ROUTE_TT, D), lambda b, i: (b, i, 0)),
                  pl.BlockSpec((1, 1, D), lambda b, i: (b, 0, 0)),
                  pl.BlockSpec((1, 1, D), lambda b, i: (b, 0, 0)),
                  pl.BlockSpec(wqT.shape, lambda b, i: (0, 0)),
                  pl.BlockSpec(keys.shape, lambda b, i: (0, 0, 0, 0))],
        out_specs=[pl.BlockSpec((1, _ROUTE_TT, D), lambda b, i: (b, i, 0)),
                   pl.BlockSpec((PEER_HEADS, PEER_KEYS, _ROUTE_TT), lambda b, i: (0, 0, b * nt + i)),
                   pl.BlockSpec((PEER_HEADS, PEER_KEYS, _ROUTE_TT), lambda b, i: (0, 0, b * nt + i)),
                   pl.BlockSpec((PEER_HEADS, 1, _ROUTE_TT), lambda b, i: (0, 0, b * nt + i))],
        out_shape=[jax.ShapeDtypeStruct((Bsz, S, D), jnp.bfloat16), tab,
                   jax.ShapeDtypeStruct(tab.shape, jnp.bfloat16),
                   jax.ShapeDtypeStruct((PEER_HEADS, 1, T), jnp.float32)],
        scratch_shapes=[pltpu.VMEM((PEER_HEADS * PEER_QDIM, _ROUTE_TT), jnp.float32)],
        compiler_params=pltpu.CompilerParams(dimension_semantics=("arbitrary", "arbitrary"),
                                             vmem_limit_bytes=_VMEM_LIMIT),
        name="peer_route",
    )(x, sc, sh, wqT, keys)


_GELU_K = math.sqrt(2.0 / math.pi)


def _gelu_tanh(x):
    half = 0.5 * x
    return half + half * jnp.tanh(x * (_GELU_K + (_GELU_K * 0.044715) * (x * x)))


_PACK = 16


def _peer_dense_body(alpha, hm_ref, e1_ref, e2_ref, pthr_ref, u_ref, vt_ref, x_ref, g_ref, lng_ref, lnb_ref,
                     o_ref, acc_ref, wt_ref, e1b_ref, pthrb_ref):
    c = pl.program_id(2)
    jrows = 32
    bf16 = jnp.bfloat16
    tp = _DENSE_TT // _DENSE_PARTS

    @pl.when(c == 0)
    def _():
        acc_ref[...] = jnp.zeros_like(acc_ref)
        for h in range(PEER_HEADS):
            pthrb_ref[h] = jnp.broadcast_to(pthr_ref[h], (_PACK, _DENSE_TT)).astype(bf16)

    parts = [slice(t * tp, (t + 1) * tp) for t in range(_DENSE_PARTS)]

    def scores(toks):
        return lax.dot_general(u_ref[...], hm_ref[0, toks, :], _NT, preferred_element_type=jnp.float32)

    def build(toks, act):
        for ii in range(_DENSE_EC // PEER_KEYS):
            for h in range(PEER_HEADS):
                e1b_ref[ii % 2, h] = jnp.broadcast_to(e1_ref[h, ii:ii + 1, toks], (_PACK, tp)).astype(bf16)
            for jb in range(PEER_KEYS // jrows):
                r0 = ii * PEER_KEYS + jb * jrows
                gate = jnp.zeros((jrows // _PACK, _PACK, tp), bf16)
                for h in range(PEER_HEADS):
                    e2 = e2_ref[h, jb * jrows:(jb + 1) * jrows, toks].reshape(jrows // _PACK, _PACK, tp)
                    val = e2 * e1b_ref[ii % 2, h][None]
                    gate = gate + jnp.where(val >= pthrb_ref[h, :, toks][None], val, jnp.zeros_like(val))
                gel = _gelu_tanh(act[r0:r0 + jrows, :]).astype(bf16)
                wt_ref[r0:r0 + jrows, toks] = gate.reshape(jrows, tp) * gel
        acc_ref[:, toks] += jnp.dot(vt_ref[...], wt_ref[:, toks], preferred_element_type=jnp.float32)

    acts = {t: scores(parts[t]) for t in range(min(2, _DENSE_PARTS))}
    for t in range(_DENSE_PARTS):
        build(parts[t], acts.pop(t))
        if t + 2 < _DENSE_PARTS:
            acts[t + 2] = scores(parts[t + 2])

    @pl.when(c == pl.num_programs(2) - 1)
    def _():
        y = alpha * x_ref[0] + g_ref[0] * acc_ref[...].T
        mu = jnp.mean(y, -1, keepdims=True)
        yc = y - mu
        var = jnp.mean(yc * yc, -1, keepdims=True)
        o_ref[0] = yc * lax.rsqrt(var + LN_EPS) * lng_ref[...] + lnb_ref[...]


def _peer_dense(hm, e1, e2, pthr, u_bf, vt_bf, x, g, ln_g, ln_b, alpha):
    Bsz, S, D = x.shape
    assert S % _DENSE_TT == 0, S
    nt = S // _DENSE_TT
    nchunk = N_EXPERTS // _DENSE_EC
    rows_i = _DENSE_EC // PEER_KEYS
    return pl.pallas_call(
        functools.partial(_peer_dense_body, alpha),
        grid=(Bsz, nt, nchunk),
        in_specs=[pl.BlockSpec((1, _DENSE_TT, D), lambda b, i, c: (b, i, 0)),
                  pl.BlockSpec((PEER_HEADS, rows_i, _DENSE_TT), lambda b, i, c: (0, c, b * nt + i)),
                  pl.BlockSpec((PEER_HEADS, PEER_KEYS, _DENSE_TT), lambda b, i, c: (0, 0, b * nt + i)),
                  pl.BlockSpec((PEER_HEADS, 1, _DENSE_TT), lambda b, i, c: (0, 0, b * nt + i)),
                  pl.BlockSpec((_DENSE_EC, D), lambda b, i, c: (c, 0)),
                  pl.BlockSpec((D, _DENSE_EC), lambda b, i, c: (0, c)),
                  pl.BlockSpec((1, _DENSE_TT, D), lambda b, i, c: (b, i, 0)),
                  pl.BlockSpec((1, 1, D), lambda b, i, c: (b, 0, 0)),
                  pl.BlockSpec((1, D), lambda b, i, c: (0, 0)),
                  pl.BlockSpec((1, D), lambda b, i, c: (0, 0))],
        out_specs=pl.BlockSpec((1, _DENSE_TT, D), lambda b, i, c: (b, i, 0)),
        out_shape=jax.ShapeDtypeStruct((Bsz, S, D), jnp.float32),
        scratch_shapes=[pltpu.VMEM((D, _DENSE_TT), jnp.float32),
                        pltpu.VMEM((_DENSE_EC, _DENSE_TT), jnp.bfloat16),
                        pltpu.VMEM((2, PEER_HEADS, _PACK, _DENSE_TT // _DENSE_PARTS), jnp.bfloat16),
                        pltpu.VMEM((PEER_HEADS, _PACK, _DENSE_TT), jnp.bfloat16)],
        compiler_params=pltpu.CompilerParams(dimension_semantics=("arbitrary", "arbitrary", "arbitrary"),
                                             vmem_limit_bytes=_VMEM_LIMIT),
        name="peer_dense",
    )(hm, e1, e2, pthr, u_bf, vt_bf, x, g, ln_g.reshape(1, D), ln_b.reshape(1, D))


def _peer_weights(wq, sub_keys, u_tab, v_tab):
    return (wq.T.astype(jnp.bfloat16), sub_keys.astype(jnp.bfloat16),
            u_tab.astype(jnp.bfloat16), v_tab.T.astype(jnp.bfloat16))


def _peer_ln(x, sc, sh, g, pw, ln_g, ln_b, alpha):
    wqT, keys, u_bf, vt_bf = pw
    hm, e1, e2, pthr = _peer_route(x, sc, sh, wqT, keys)
    return _peer_dense(hm, e1, e2, pthr, u_bf, vt_bf, x, g, ln_g, ln_b, alpha)


def kernel(x, c, ctx, c_ctx, w_ada, b_ada, w_in, w_out, ln1_g, ln1_b, ln2_g, ln2_b,
           conf_dw_w, conf_dw_b, conf_norm_g, conf_norm_b, na_rpb, hy_short_w, hy_short_b,
           hy_w1, hy_b1, hy_w2, hy_b2, hy_w3, hy_decay, hy_bias, ssd_conv_w, ssd_conv_b,
           ssd_a_log, ssd_dt_bias, ssd_d, ssd_norm_g, peer_wq, peer_keys, peer_u, peer_v):
    alpha = (2.0 * DEPTH) ** 0.25
    s_c = jax.nn.silu(c)
    s_cc = jax.nn.silu(c_ctx)
    xc = ctx
    Bsz, Lc, D = ctx.shape
    for l in range(DEPTH):
        ctx_out = l < DEPTH - 1
        mod = (s_c @ w_ada[l] + b_ada[l])[:, None, :]
        mod_c = jnp.broadcast_to((s_cc @ w_ada[l] + b_ada[l])[None, None, :], mod.shape)
        sh1, sc1, g1, sh2, sc2, g2 = jnp.split(mod, 6, -1)
        sh1c, sc1c, g1c, sh2c, sc2c, g2c = jnp.split(mod_c, 6, -1)
        x, xc = _mixer_ln(
            x, xc, (sh1, sc1, g1), (sh1c, sc1c, g1c), ctx_out, alpha, w_in[l], w_out[l], ln1_g[l], ln1_b[l],
            (conf_dw_w[l], conf_dw_b[l], conf_norm_g[l], conf_norm_b[l]), na_rpb[l],
            (hy_short_w[l], hy_short_b[l], hy_w1[l], hy_b1[l], hy_w2[l], hy_b2[l], hy_w3[l], hy_decay[l], hy_bias[l]),
            (ssd_conv_w[l], ssd_conv_b[l], ssd_a_log[l], ssd_dt_bias[l], ssd_d[l], ssd_norm_g[l]))
        pw = _peer_weights(peer_wq[l], peer_keys[l], peer_u[l], peer_v[l])
        x = _peer_ln(x, sc2, sh2, g2, pw, ln2_g[l], ln2_b[l], alpha)
        if ctx_out:
            xc = _peer_ln(xc.reshape(1, Bsz * Lc, D), sc2c[:1], sh2c[:1], g2c[:1], pw, ln2_g[l], ln2_b[l],
                          alpha).reshape(Bsz, Lc, D)
    return x
```

```python
import functools
import math
import jax, jax.numpy as jnp
from jax import lax
from jax.experimental import pallas as pl
from jax.experimental.pallas import tpu as pltpu

D_MODEL = 1024
DEPTH = 2

GRID_W = 64
N_MIXERS = 4
GROUP_W = D_MODEL // N_MIXERS
D_MIX = N_MIXERS * GROUP_W
LN_EPS = 1e-5
CONF_K = 31
CONF_GROUPS = 4
NA_HEADS = 4
HEAD_DIM = GROUP_W // NA_HEADS
NA_KH = 8
NA_KW = 16
ROPE_BASE = 10000.0
HY_SHORT = 3
HY_BANDS = 16
HY_EMB = 1 + 2 * HY_BANDS
HY_HIDDEN = 64
HY_SIN_FREQ = 1.0
SSD_HEADS = 4
SSD_HEAD_DIM = GROUP_W // SSD_HEADS
SSD_GROUPS = 2
SSD_STATE = 64
SSD_CONV = 3
SSD_CHUNK = 128
XBC_W = GROUP_W + 2 * SSD_GROUPS * SSD_STATE
IN_COLS = 2 * GROUP_W + 3 * GROUP_W + 3 * GROUP_W + GROUP_W + XBC_W + 2 * SSD_HEADS
PEER_HEADS = 8
PEER_KEYS = 128
PEER_TOPK = 16
PEER_QDIM = 256
N_EXPERTS = PEER_KEYS * PEER_KEYS

_LANES = 128
_SUBLANES = 8
_VMEM_LIMIT = 56 * 1024 * 1024
_IN_PAD = -(-IN_COLS // _LANES) * _LANES


_IN_SPLITS = ((0, 2 * GROUP_W), (2 * GROUP_W, 5 * GROUP_W), (5 * GROUP_W, 8 * GROUP_W),
              (8 * GROUP_W, 9 * GROUP_W + XBC_W), (9 * GROUP_W + XBC_W, _IN_PAD))
_PROJ_TM = 512


def _in_proj_body(x_ref, sc_ref, sh_ref, w_ref, *o_refs):
    h = (x_ref[0] * (1.0 + sc_ref[0]) + sh_ref[0]).astype(jnp.bfloat16)
    for (lo, hi), o_ref in zip(_IN_SPLITS, o_refs):
        o_ref[0] = jnp.dot(h, w_ref[:, lo:hi], preferred_element_type=jnp.float32)


def _in_proj(x, sc, sh, w_in):
    Bsz, L, D = x.shape
    tm = min(_PROJ_TM, L)
    w = jnp.pad(w_in, ((0, 0), (0, _IN_PAD - IN_COLS))).astype(jnp.bfloat16)
    return pl.pallas_call(
        _in_proj_body,
        grid=(Bsz, L // tm),
        in_specs=[pl.BlockSpec((1, tm, D), lambda b, i: (b, i, 0)),
                  pl.BlockSpec((1, 1, D), lambda b, i: (b, 0, 0)),
                  pl.BlockSpec((1, 1, D), lambda b, i: (b, 0, 0)),
                  pl.BlockSpec((D, _IN_PAD), lambda b, i: (0, 0))],
        out_specs=[pl.BlockSpec((1, tm, hi - lo), lambda b, i: (b, i, 0)) for lo, hi in _IN_SPLITS],
        out_shape=[jax.ShapeDtypeStruct((Bsz, L, hi - lo), jnp.float32) for lo, hi in _IN_SPLITS],
        compiler_params=pltpu.CompilerParams(dimension_semantics=("arbitrary", "arbitrary"),
                                             vmem_limit_bytes=_VMEM_LIMIT),
        name="in_proj",
    )(x, sc, sh, w)


def _out_proj_ln_body(alpha, ya_ref, yb_ref, yc_ref, yd_ref, w_ref, x_ref, g_ref, lng_ref, lnb_ref, o_ref):
    y = 0.0
    for m, y_ref in enumerate((ya_ref, yb_ref, yc_ref, yd_ref)):
        y = y + jnp.dot(y_ref[0].astype(jnp.bfloat16), w_ref[m * GROUP_W:(m + 1) * GROUP_W, :],
                        preferred_element_type=jnp.float32)
    r = alpha * x_ref[0] + g_ref[0] * y
    mu = jnp.mean(r, -1, keepdims=True)
    rc = r - mu
    var = jnp.mean(rc * rc, -1, keepdims=True)
    o_ref[0] = rc * lax.rsqrt(var + LN_EPS) * lng_ref[...] + lnb_ref[...]


def _out_proj_ln(ys, w_out, x, g, ln_g, ln_b, alpha):
    Bsz, L, D = x.shape
    tm = min(_PROJ_TM, L)
    mix = pl.BlockSpec((1, tm, GROUP_W), lambda b, i: (b, i, 0))
    return pl.pallas_call(
        functools.partial(_out_proj_ln_body, alpha),
        grid=(Bsz, L // tm),
        in_specs=[mix, mix, mix, mix,
                  pl.BlockSpec((D_MIX, D), lambda b, i: (0, 0)),
                  pl.BlockSpec((1, tm, D), lambda b, i: (b, i, 0)),
                  pl.BlockSpec((1, 1, D), lambda b, i: (b, 0, 0)),
                  pl.BlockSpec((1, D), lambda b, i: (0, 0)),
                  pl.BlockSpec((1, D), lambda b, i: (0, 0))],
        out_specs=pl.BlockSpec((1, tm, D), lambda b, i: (b, i, 0)),
        out_shape=jax.ShapeDtypeStruct((Bsz, L, D), jnp.float32),
        compiler_params=pltpu.CompilerParams(dimension_semantics=("arbitrary", "arbitrary"),
                                             vmem_limit_bytes=_VMEM_LIMIT),
        name="out_proj_ln",
    )(*ys, w_out.astype(jnp.bfloat16), x, g, ln_g.reshape(1, D), ln_b.reshape(1, D))


def _shifted_taps(win, n_taps, rows):
    return [win[k:k + rows] for k in range(n_taps)]


_CONV_HALO = 16
_CONV_TILE = 256


def _group_mean_matrix(width, group):
    r = lax.broadcasted_iota(jnp.int32, (width, width), 0) // group
    c = lax.broadcasted_iota(jnp.int32, (width, width), 1) // group
    return jnp.where(r == c, 1.0 / group, 0.0).astype(jnp.float32)


def _conformer_body(L, p_ref, w_ref, b_ref, ng_ref, nb_ref, o_ref, u_ref):
    G = GROUP_W
    pad = (CONF_K - 1) // 2
    halo = jnp.zeros((_CONV_HALO, G), jnp.float32)
    u_ref[0:_CONV_HALO, :] = halo
    u_ref[_CONV_HALO + L:_CONV_HALO + L + _CONV_HALO, :] = halo
    u_ref[_CONV_HALO:_CONV_HALO + L, :] = p_ref[0, :, 0:G] * jax.nn.sigmoid(p_ref[0, :, G:2 * G])
    avg = _group_mean_matrix(G, G // CONF_GROUPS)
    tile = min(_CONV_TILE, L)
    for t in range(L // tile):
        win = u_ref[t * tile:t * tile + tile + 2 * _CONV_HALO, :]
        acc = jnp.zeros((tile, G), jnp.float32) + b_ref[...]
        for k, tap in enumerate(_shifted_taps(win[_CONV_HALO - pad:], CONF_K, tile)):
            acc = acc + tap * w_ref[k:k + 1, :]
        mu = jnp.dot(acc, avg, preferred_element_type=jnp.float32, precision=lax.Precision.HIGHEST)
        cen = acc - mu
        var = jnp.dot(cen * cen, avg, preferred_element_type=jnp.float32, precision=lax.Precision.HIGHEST)
        un = cen * lax.rsqrt(var + LN_EPS) * ng_ref[...] + nb_ref[...]
        o_ref[0, t * tile:(t + 1) * tile, :] = un * jax.nn.sigmoid(un)


def _conformer(pa, dw_w, dw_b, n_g, n_b):
    Bsz, L, _ = pa.shape
    G = GROUP_W
    vec = pl.BlockSpec((1, G), lambda b: (0, 0))
    return pl.pallas_call(
        functools.partial(_conformer_body, L),
        grid=(Bsz,),
        in_specs=[pl.BlockSpec((1, L, 2 * G), lambda b: (b, 0, 0)),
                  pl.BlockSpec((CONF_K, G), lambda b: (0, 0)), vec, vec, vec],
        out_specs=pl.BlockSpec((1, L, G), lambda b: (b, 0, 0)),
        out_shape=jax.ShapeDtypeStruct((Bsz, L, G), jnp.float32),
        scratch_shapes=[pltpu.VMEM((L + 2 * _CONV_HALO, G), jnp.float32)],
        compiler_params=pltpu.CompilerParams(dimension_semantics=("arbitrary",),
                                             vmem_limit_bytes=_VMEM_LIMIT),
        name="conformer",
    )(pa, dw_w, dw_b.reshape(1, G), n_g.reshape(1, G), n_b.reshape(1, G))


def axial_rope(rows, head_dim):
    n_f = head_dim // 4
    inv = ROPE_BASE ** (-jnp.arange(n_f, dtype=jnp.float32) / n_f)
    t = jnp.arange(rows * GRID_W)
    r = (t // GRID_W).astype(jnp.float32)
    col = (t % GRID_W).astype(jnp.float32)
    ang = jnp.concatenate([r[:, None] * inv, col[:, None] * inv], -1)
    return jnp.cos(ang), jnp.sin(ang)


_NA_MASK = -1e30


def _na_tables(rows, rpb):
    cos, sin = axial_rope(rows, HEAD_DIM)
    cos_f = jnp.tile(cos, (1, 2 * NA_HEADS))
    sin_s = jnp.tile(jnp.concatenate([-sin, sin], -1), (1, NA_HEADS))
    cq = jnp.arange(GRID_W)
    cs = jnp.clip(cq - NA_KW // 2, 0, GRID_W - NA_KW)
    col = jnp.arange(GRID_W)
    in_band = (col[None, :] >= cs[:, None]) & (col[None, :] < cs[:, None] + NA_KW)
    cb_idx = jnp.clip(col[None, :] - cq[:, None] + (NA_KW - 1), 0, 2 * NA_KW - 2)
    po = jnp.arange(NA_KH)
    rb_idx = jnp.arange(NA_KH)[None, :] - po[:, None] + (NA_KH - 1)
    bias = rpb[:, rb_idx][:, :, :, cb_idx]
    bias = jnp.where(in_band[None, None, None], bias, _NA_MASK)
    bias = bias.transpose(1, 0, 3, 2, 4).reshape(NA_KH, NA_HEADS, GRID_W, NA_KH * GRID_W)
    return cos_f, sin_s, bias


def _rope_lanes(x, cos_f, sin_s):
    n = x.shape[-1]
    hd = HEAD_DIM // 2
    first = (lax.broadcasted_iota(jnp.int32, x.shape, 1) % HEAD_DIM) < hd
    partner = jnp.where(first, pltpu.roll(x, n - hd, 1), pltpu.roll(x, hd, 1))
    return x * cos_f + partner * sin_s


def _softmax_pv(s_parts, v_parts):
    m = s_parts[0].max(-1, keepdims=True)
    for s in s_parts[1:]:
        m = jnp.maximum(m, s.max(-1, keepdims=True))
    l = 0.0
    o = 0.0
    for s, v in zip(s_parts, v_parts):
        p = jnp.exp(s - m)
        l = l + p.sum(-1, keepdims=True)
        o = o + jnp.dot(p.astype(jnp.bfloat16), v, preferred_element_type=jnp.float32)
    return o / l


_NA_ROWS = 4


def _na_body(rows, q_ref, k_ref, v_ref, kc_ref, vc_ref, cosq_ref, sinq_ref, cosk_ref, sink_ref, *rest):
    bias_refs, (o_ref, krot_ref, vbf_ref) = rest[:_NA_ROWS], rest[_NA_ROWS:]
    step = pl.program_id(1)
    scale = HEAD_DIM ** -0.5

    @pl.when(step == 0)
    def _():
        krot_ref[...] = _rope_lanes(k_ref[0], cosk_ref[...], sink_ref[...]).astype(jnp.bfloat16)
        vbf_ref[...] = v_ref[0].astype(jnp.bfloat16)

    win = NA_KH * GRID_W
    q = q_ref[0] * scale
    q_rot = _rope_lanes(q, cosq_ref[...], sinq_ref[...]).astype(jnp.bfloat16)
    q_plain = q.astype(jnp.bfloat16)
    kc = kc_ref[0].astype(jnp.bfloat16)
    vc = vc_ref[0].astype(jnp.bfloat16)
    for i in range(_NA_ROWS):
        r = step * _NA_ROWS + i
        rs = jnp.clip(r - NA_KH // 2, 0, rows - NA_KH)
        start = pl.multiple_of(rs * GRID_W, GRID_W)
        kw = krot_ref[pl.ds(start, win), :]
        vw = vbf_ref[pl.ds(start, win), :]
        qs = slice(i * GRID_W, (i + 1) * GRID_W)
        outs = []
        for h in range(NA_HEADS):
            hs = slice(h * HEAD_DIM, (h + 1) * HEAD_DIM)
            s_loc = lax.dot_general(q_rot[qs, hs], kw[:, hs], _NT, preferred_element_type=jnp.float32) + bias_refs[i][0, h]
            s_ctx = lax.dot_general(q_plain[qs, hs], kc[:, hs], _NT, preferred_element_type=jnp.float32)
            outs.append(_softmax_pv([s_loc, s_ctx], [vw[:, hs], vc[:, hs]]))
        o_ref[0, qs, :] = jnp.concatenate(outs, axis=-1)


def _na_attention(pb, cb, rpb):
    Bsz, S, _ = pb.shape
    Lc = cb.shape[1]
    rows = S // GRID_W
    assert rows % _NA_ROWS == 0, rows
    G = GROUP_W
    nq = _NA_ROWS * GRID_W
    cos_f, sin_s, bias = _na_tables(rows, rpb)

    def bias_spec(i):
        def idx(b, step):
            r = step * _NA_ROWS + i
            return (r - jnp.clip(r - NA_KH // 2, 0, rows - NA_KH), 0, 0, 0)
        return pl.BlockSpec((1, NA_HEADS, GRID_W, NA_KH * GRID_W), idx)

    return pl.pallas_call(
        functools.partial(_na_body, rows),
        grid=(Bsz, rows // _NA_ROWS),
        in_specs=[pl.BlockSpec((1, nq, G), lambda b, r: (b, r, 0)),
                  pl.BlockSpec((1, S, G), lambda b, r: (b, 0, 1)),
                  pl.BlockSpec((1, S, G), lambda b, r: (b, 0, 2)),
                  pl.BlockSpec((1, Lc, G), lambda b, r: (b, 0, 1)),
                  pl.BlockSpec((1, Lc, G), lambda b, r: (b, 0, 2)),
                  pl.BlockSpec((nq, G), lambda b, r: (r, 0)),
                  pl.BlockSpec((nq, G), lambda b, r: (r, 0)),
                  pl.BlockSpec((S, G), lambda b, r: (0, 0)),
                  pl.BlockSpec((S, G), lambda b, r: (0, 0))] + [bias_spec(i) for i in range(_NA_ROWS)],
        out_specs=pl.BlockSpec((1, nq, G), lambda b, r: (b, r, 0)),
        out_shape=jax.ShapeDtypeStruct((Bsz, S, G), jnp.float32),
        scratch_shapes=[pltpu.VMEM((S, G), jnp.bfloat16), pltpu.VMEM((S, G), jnp.bfloat16)],
        compiler_params=pltpu.CompilerParams(dimension_semantics=("arbitrary", "arbitrary"),
                                             vmem_limit_bytes=_VMEM_LIMIT),
        name="na_attention",
    )(pb, pb, pb, cb, cb, cos_f, sin_s, cos_f, sin_s, *([bias] * _NA_ROWS))


def _ctx_attn_body(q_ref, k_ref, v_ref, o_ref):
    q = (q_ref[0] * HEAD_DIM ** -0.5).astype(jnp.bfloat16)
    k = k_ref[0].astype(jnp.bfloat16)
    v = v_ref[0].astype(jnp.bfloat16)
    outs = []
    for h in range(NA_HEADS):
        hs = slice(h * HEAD_DIM, (h + 1) * HEAD_DIM)
        s = lax.dot_general(q[:, hs], k[:, hs], _NT, preferred_element_type=jnp.float32)
        outs.append(_softmax_pv([s], [v[:, hs]]))
    o_ref[0] = jnp.concatenate(outs, axis=-1)


def _ctx_attention(cb):
    Bsz, Lc, _ = cb.shape
    G = GROUP_W
    return pl.pallas_call(
        _ctx_attn_body,
        grid=(Bsz,),
        in_specs=[pl.BlockSpec((1, Lc, G), lambda b: (b, 0, 0)),
                  pl.BlockSpec((1, Lc, G), lambda b: (b, 0, 1)),
                  pl.BlockSpec((1, Lc, G), lambda b: (b, 0, 2))],
        out_specs=pl.BlockSpec((1, Lc, G), lambda b: (b, 0, 0)),
        out_shape=jax.ShapeDtypeStruct((Bsz, Lc, G), jnp.float32),
        compiler_params=pltpu.CompilerParams(dimension_semantics=("arbitrary",)),
        name="ctx_attention",
    )(cb, cb, cb)


_HY_TB = 256
_HY_CB = 8


def _hyena_filter_body(L, w1_ref, b1_ref, w2_ref, b2_ref, w3_ref, dec_ref, o_ref):
    G = GROUP_W
    hp = lax.Precision.HIGHEST
    p = lax.broadcasted_iota(jnp.int32, (2 * L, _LANES), 0)
    lane = lax.broadcasted_iota(jnp.int32, (2 * L, _LANES), 1)
    tn = jnp.abs(p - L).astype(jnp.float32) / L
    band = ((lane - 1) % HY_BANDS + 1).astype(jnp.float32)
    ang = 2.0 * math.pi * band * tn
    z = jnp.where(lane == 0, tn, jnp.where(lane <= HY_BANDS, jnp.sin(ang), jnp.cos(ang)))
    z = jnp.where(lane < HY_EMB, z, 0.0)
    h = jnp.sin(HY_SIN_FREQ * (jnp.dot(z, w1_ref[...], precision=hp, preferred_element_type=jnp.float32) + b1_ref[...]))
    h = jnp.sin(HY_SIN_FREQ * (jnp.dot(h, w2_ref[...], precision=hp, preferred_element_type=jnp.float32) + b2_ref[...]))
    k = jnp.dot(h, w3_ref[...], precision=hp, preferred_element_type=jnp.float32) * jnp.exp(-tn[:, 0:1] * dec_ref[...])
    kf, kb = k[:, 0:G], k[:, G:2 * G]
    n = lax.broadcasted_iota(jnp.int32, (2 * L, G), 0) - L
    nf = jnp.sum(jnp.where(n >= 0, jnp.abs(kf), 0.0), axis=0, keepdims=True) + 1e-6
    nb = jnp.sum(jnp.where(n <= 0, jnp.where(n > -L, jnp.abs(kb), 0.0), 0.0), axis=0, keepdims=True) + 1e-6
    rev = jnp.where(n > 0, kb / nb, jnp.where(n > -L, kf / nf, 0.0))
    o_ref[...] = rev.T


def _hyena_filters_rev(L, w1, b1, w2, b2, w3, decay):
    G = GROUP_W
    hp = _LANES - HY_HIDDEN
    w1p = jnp.pad(w1.astype(jnp.float32), ((0, _LANES - HY_EMB), (0, hp)))
    args = (w1p, jnp.pad(b1.reshape(1, -1), ((0, 0), (0, hp))), jnp.pad(w2, ((0, hp), (0, hp))),
            jnp.pad(b2.reshape(1, -1), ((0, 0), (0, hp))), jnp.pad(w3, ((0, hp), (0, 0))), decay.reshape(1, -1))
    return pl.pallas_call(
        functools.partial(_hyena_filter_body, L),
        out_shape=jax.ShapeDtypeStruct((G, 2 * L), jnp.float32),
        compiler_params=pltpu.CompilerParams(vmem_limit_bytes=_VMEM_LIMIT),
        name="hyena_filters",
    )(*[a.astype(jnp.float32) for a in args])


def _hyena_pre_body(L, p_ref, w_ref, b_ref, u_ref, ut_ref, x0_ref, xp_ref):
    G = GROUP_W
    W = 3 * G
    halo = jnp.zeros((_CONV_HALO, W), jnp.float32)
    xp_ref[0:_CONV_HALO, :] = halo
    xp_ref[_CONV_HALO + L:_CONV_HALO + L + _CONV_HALO, :] = halo
    xp_ref[_CONV_HALO:_CONV_HALO + L, :] = p_ref[0]
    pad = (HY_SHORT - 1) // 2
    tile = min(_CONV_TILE, L)
    for t in range(L // tile):
        win = xp_ref[t * tile:t * tile + tile + 2 * _CONV_HALO, :]
        acc = jnp.zeros((tile, W), jnp.float32) + b_ref[...]
        for k, tap in enumerate(_shifted_taps(win[_CONV_HALO - pad:], HY_SHORT, tile)):
            acc = acc + tap * w_ref[k:k + 1, :]
        rows = slice(t * tile, (t + 1) * tile)
        u = acc[:, 2 * G:3 * G] * acc[:, G:2 * G]
        x0_ref[0, rows, :] = acc[:, 0:G]
        u_ref[0, rows, :] = u
        ut_ref[0, :, rows] = u.T


def _hyena_pre(py, short_w, short_b):
    Bsz, L, W = py.shape
    G = GROUP_W
    f32 = jnp.float32
    return pl.pallas_call(
        functools.partial(_hyena_pre_body, L),
        grid=(Bsz,),
        in_specs=[pl.BlockSpec((1, L, W), lambda b: (b, 0, 0)),
                  pl.BlockSpec((HY_SHORT, W), lambda b: (0, 0)),
                  pl.BlockSpec((1, W), lambda b: (0, 0))],
        out_specs=[pl.BlockSpec((1, L, G), lambda b: (b, 0, 0)),
                   pl.BlockSpec((1, G, L), lambda b: (b, 0, 0)),
                   pl.BlockSpec((1, L, G), lambda b: (b, 0, 0))],
        out_shape=[jax.ShapeDtypeStruct((Bsz, L, G), f32), jax.ShapeDtypeStruct((Bsz, G, L), f32),
                   jax.ShapeDtypeStruct((Bsz, L, G), f32)],
        scratch_shapes=[pltpu.VMEM((L + 2 * _CONV_HALO, W), f32)],
        compiler_params=pltpu.CompilerParams(dimension_semantics=("arbitrary",), vmem_limit_bytes=_VMEM_LIMIT),
        name="hyena_pre",
    )(py, short_w, short_b.reshape(1, W))


def _toeplitz_tile(w8):
    TB = _HY_TB
    nq = 2 * TB // _LANES
    per = _LANES // _SUBLANES
    i = lax.broadcasted_iota(jnp.int32, (_SUBLANES, _LANES), 0)
    l = lax.broadcasted_iota(jnp.int32, (_SUBLANES, _LANES), 1)
    rolled = [[pltpu.roll(w8[:, q * _LANES:(q + 1) * _LANES], (_SUBLANES * k) % _LANES, 1, stride=1, stride_axis=0)
               for k in range(per)] for q in range(nq)]
    row_blocks = []
    for rg in range(TB // _SUBLANES):
        k = rg % per
        pieces = []
        for lg in range(TB // _LANES):
            o = TB + _LANES * lg - _SUBLANES * rg
            q, rho = divmod(o, _LANES)
            if rho == 0:
                pieces.append(jnp.where(l - i < 0, rolled[q - 1][k], rolled[q][k]))
            else:
                pieces.append(jnp.where(l + rho - i >= _LANES, rolled[q + 1][k], rolled[q][k]))
        row_blocks.append(jnp.concatenate(pieces, axis=1))
    return jnp.concatenate(row_blocks, axis=0).astype(jnp.bfloat16)


def _hyena_conv_body(L, Bsz, g_ref, u_ref, o_ref):
    TB = min(_HY_TB, L)
    nb = L // TB
    cols = Bsz * nb
    lane = lax.broadcasted_iota(jnp.int32, (TB, cols), 1) % nb

    def channel(ci, carry):
        u = u_ref[ci] if nb == 1 else u_ref[:, ci].reshape(cols, TB)
        u = u.astype(jnp.bfloat16)
        acc = jnp.zeros((TB, cols), jnp.float32)
        for d in range(-(nb - 1), nb):
            start = L - TB * d - TB
            w8 = jnp.broadcast_to(g_ref[pl.ds(ci, 1), start:start + 2 * TB], (_SUBLANES, 2 * TB))
            z = lax.dot_general(_toeplitz_tile(w8), u, _NT, preferred_element_type=jnp.float32)
            if d != 0:
                z = jnp.where((lane - d >= 0) & (lane - d < nb), pltpu.roll(z, d % cols, 1), 0.0)
            acc = acc + z
        if nb == 1:
            o_ref[ci] = acc.T
        else:
            o_ref[:, ci] = acc.T.reshape(Bsz, nb, TB)
        return carry

    lax.fori_loop(0, _HY_CB, channel, 0)


def _hyena_conv(g_rev, u_t):
    Bsz, G, L = u_t.shape
    TB = min(_HY_TB, L)
    nb = L // TB
    if nb == 1:
        blk = pl.BlockSpec((_HY_CB, Bsz, TB), lambda c: (c, 0, 0))
        operand, out_shape = jnp.swapaxes(u_t, 0, 1), (G, Bsz, TB)
    else:
        blk = pl.BlockSpec((Bsz, _HY_CB, nb, TB), lambda c: (0, c, 0, 0))
        operand, out_shape = u_t.reshape(Bsz, G, nb, TB), (Bsz, G, nb, TB)
    out = pl.pallas_call(
        functools.partial(_hyena_conv_body, L, Bsz),
        grid=(G // _HY_CB,),
        in_specs=[pl.BlockSpec((_HY_CB, 2 * L), lambda c: (c, 0)), blk],
        out_specs=blk,
        out_shape=jax.ShapeDtypeStruct(out_shape, jnp.float32),
        compiler_params=pltpu.CompilerParams(dimension_semantics=("arbitrary",), vmem_limit_bytes=_VMEM_LIMIT),
        name="hyena_conv",
    )(g_rev, operand)
    return jnp.swapaxes(out, 0, 1) if nb == 1 else out.reshape(Bsz, G, L)


def _hyena_post_body(yt_ref, u_ref, x0_ref, skip_ref, o_ref):
    o_ref[0] = (yt_ref[0].T + u_ref[0] * skip_ref[...]) * x0_ref[0]


def _hyena_post(y_t, u, x0, skip):
    Bsz, L, G = u.shape
    tok = pl.BlockSpec((1, L, G), lambda b: (b, 0, 0))
    return pl.pallas_call(
        _hyena_post_body,
        grid=(Bsz,),
        in_specs=[pl.BlockSpec((1, G, L), lambda b: (b, 0, 0)), tok, tok, pl.BlockSpec((1, G), lambda b: (0, 0))],
        out_specs=tok,
        out_shape=jax.ShapeDtypeStruct((Bsz, L, G), jnp.float32),
        compiler_params=pltpu.CompilerParams(dimension_semantics=("arbitrary",), vmem_limit_bytes=_VMEM_LIMIT),
        name="hyena_post",
    )(y_t, u, x0, skip.reshape(1, G))


def _hyena(py, short_w, short_b, w1, b1, w2, b2, w3, decay, skip):
    L = py.shape[1]
    g_rev = _hyena_filters_rev(L, w1, b1, w2, b2, w3, decay)
    u, u_t, x0 = _hyena_pre(py, short_w, short_b)
    return _hyena_post(_hyena_conv(g_rev, u_t), u, x0, skip)


def _split3_dot(a, b_bf16, dims=None):
    hi = a.astype(jnp.bfloat16)
    r1 = a - hi.astype(jnp.float32)
    mid = r1.astype(jnp.bfloat16)
    lo = (r1 - mid.astype(jnp.float32)).astype(jnp.bfloat16)
    out = 0.0
    for part in (hi, mid, lo):
        if dims is None:
            out = out + jnp.dot(part, b_bf16, preferred_element_type=jnp.float32)
        else:
            out = out + jnp.dot(b_bf16, part, preferred_element_type=jnp.float32)
    return out


def _softplus(x):
    return jnp.maximum(x, 0.0) + jnp.log(1.0 + jnp.exp(-jnp.abs(x)))


def _ssd_body(L, zx_ref, dtc_ref, dtr_ref, cw_ref, cb_ref, arow_ref, acol_ref, brow_ref, bcol_ref, dsk_ref,
              ng_ref, init_ref, o_ref, fin_ref, xp_ref, xc_ref, bt_ref, y_ref, ccol_ref, crow_ref, edec_ref,
              tot_ref):
    G = GROUP_W
    Q = SSD_CHUNK
    nc = L // Q
    P = SSD_HEAD_DIM
    N = SSD_STATE
    H = SSD_HEADS
    f32 = jnp.float32
    bf16 = jnp.bfloat16
    halo = jnp.zeros((_CONV_HALO, XBC_W), f32)
    xp_ref[0:_CONV_HALO, :] = halo
    xp_ref[_CONV_HALO + L:_CONV_HALO + L + _CONV_HALO, :] = halo
    xp_ref[_CONV_HALO:_CONV_HALO + L, :] = zx_ref[0, :, G:G + XBC_W]
    pad = (SSD_CONV - 1) // 2
    tile = min(_CONV_TILE, L)
    for t in range(L // tile):
        win = xp_ref[t * tile:t * tile + tile + 2 * _CONV_HALO, :]
        acc = jnp.zeros((tile, XBC_W), f32) + cb_ref[...]
        for k, tap in enumerate(_shifted_taps(win[_CONV_HALO - pad:], SSD_CONV, tile)):
            acc = acc + tap * cw_ref[k:k + 1, :]
        xc_ref[t * tile:(t + 1) * tile, :] = acc * jax.nn.sigmoid(acc)
    for c in range(nc):
        bt_ref[c] = xc_ref[c * Q:(c + 1) * Q, G:G + SSD_GROUPS * N].T
    dt_col = _softplus(dtc_ref[0] + brow_ref[...])
    a_col = dt_col * arow_ref[...]
    a_row = _softplus(dtr_ref[0] + bcol_ref[...]) * acol_ref[...]
    a_stack = jnp.concatenate([a_row[:, c * Q:(c + 1) * Q] for c in range(nc)], axis=0)
    ri = lax.broadcasted_iota(jnp.int32, (Q, Q), 0)
    ci = lax.broadcasted_iota(jnp.int32, (Q, Q), 1)
    one = lambda m: jnp.where(m, 1.0, 0.0).astype(bf16)
    tot_ref[...] = _split3_dot(a_stack, jnp.ones((Q, Q), bf16))

    def direction(d, y_store):
        fwd = d == 0
        m_col = one(ci <= ri) if fwd else one(ci >= ri)
        for c in range(nc):
            ccol_ref[c * Q:(c + 1) * Q, :] = _split3_dot(a_col[c * Q:(c + 1) * Q, :], m_col, dims="left")
        crow_ref[...] = _split3_dot(a_stack, one(ri <= ci) if fwd else one(ri >= ci))
        edec_ref[...] = _split3_dot(a_stack, one(ri > ci) if fwd else one(ri < ci))
        keep = (ri >= ci) if fwd else (ri <= ci)

        def chunk(step, states):
            c = step if fwd else nc - 1 - step
            r0 = pl.multiple_of(c * Q, Q)
            j0 = pl.multiple_of(c * 2 * H, 2 * H)
            xc = xc_ref[pl.ds(r0, Q), :]
            ccol = ccol_ref[pl.ds(r0, Q), :]
            crow = crow_ref[pl.ds(j0, 2 * H), :]
            edec = edec_ref[pl.ds(j0, 2 * H), :]
            tot = tot_ref[pl.ds(j0, 2 * H), :]
            dtc = _softplus(dtc_ref[0, pl.ds(r0, Q), :] + brow_ref[...])
            bt = bt_ref[c]
            new_states = []
            outs = []
            for g in range(SSD_GROUPS):
                cm = xc[:, G + SSD_GROUPS * N + g * N:G + SSD_GROUPS * N + (g + 1) * N].astype(bf16)
                bm = xc[:, G + g * N:G + (g + 1) * N].astype(bf16)
                cb = lax.dot_general(cm, bm, _NT, preferred_element_type=f32)
                for hh in range(H // SSD_GROUPS):
                    h = g * (H // SSD_GROUPS) + hh
                    j = d * H + h
                    col = jnp.broadcast_to(ccol[:, j:j + 1], (Q, Q))
                    lmat = jnp.exp(jnp.where(keep, col - crow[j:j + 1, :], _NEG))
                    xd = (xc[:, h * P:(h + 1) * P] * jnp.broadcast_to(dtc[:, j:j + 1], (Q, P))).astype(bf16)
                    st = states[h]
                    y = jnp.dot((cb * lmat).astype(bf16), xd, preferred_element_type=f32)
                    y = y + jnp.dot(cm, st.astype(bf16), preferred_element_type=f32) * jnp.exp(col[:, 0:P])
                    outs.append(y)
                    btd = (bt[g * N:(g + 1) * N, :] * jnp.exp(edec[j:j + 1, :])).astype(bf16)
                    new_states.append(jnp.exp(tot[j:j + 1, 0:P]) * st
                                      + jnp.dot(btd, xd, preferred_element_type=f32))
            y_store(r0, jnp.concatenate(outs, axis=-1))
            return tuple(new_states)

        init = tuple(init_ref[0, d, h] for h in range(H))
        final = lax.fori_loop(0, nc, chunk, init)
        for h in range(H):
            fin_ref[0, d, h] = final[h]

    def store_fwd(r0, y):
        y_ref[pl.ds(r0, Q), :] = y

    def store_bwd(r0, y):
        y_ref[pl.ds(r0, Q), :] += y

    direction(0, store_fwd)
    direction(1, store_bwd)
    gw = G // SSD_GROUPS
    for t in range(L // tile):
        rows = slice(t * tile, (t + 1) * tile)
        z = zx_ref[0, rows, 0:G]
        yg = (y_ref[rows, :] + xc_ref[rows, 0:G] * dsk_ref[...]) * (z * jax.nn.sigmoid(z))
        parts = []
        for g in range(SSD_GROUPS):
            v = yg[:, g * gw:(g + 1) * gw]
            parts.append(v * lax.rsqrt(jnp.mean(v * v, -1, keepdims=True) + LN_EPS))
        o_ref[0, rows, :] = jnp.concatenate(parts, axis=-1) * ng_ref[...]


def _ssd(pzx, pdt, init, conv_w, conv_b, a_log, dt_bias, d_skip, norm_g):
    Bsz, L, _ = pzx.shape
    G, H, Q = GROUP_W, SSD_HEADS, SSD_CHUNK
    lanes = pdt.shape[-1]
    nc = L // Q
    neg_a = -jnp.exp(a_log.astype(jnp.float32)).reshape(1, 2 * H)
    a_rowv = jnp.pad(neg_a, ((0, 0), (0, lanes - 2 * H)))
    b_rowv = jnp.pad(dt_bias.astype(jnp.float32).reshape(1, 2 * H), ((0, 0), (0, lanes - 2 * H)))
    dt_rows = jnp.swapaxes(pdt[:, :, :2 * H], 1, 2)
    d_lane = jnp.repeat(d_skip.astype(jnp.float32), SSD_HEAD_DIM).reshape(1, G)
    const = lambda shape: pl.BlockSpec(shape, lambda b: (0,) * len(shape))
    st_spec = pl.BlockSpec((1, 2, H, SSD_STATE, SSD_HEAD_DIM), lambda b: (b, 0, 0, 0, 0))
    f32 = jnp.float32
    return pl.pallas_call(
        functools.partial(_ssd_body, L),
        grid=(Bsz,),
        in_specs=[pl.BlockSpec((1, L, G + XBC_W), lambda b: (b, 0, 0)),
                  pl.BlockSpec((1, L, lanes), lambda b: (b, 0, 0)),
                  pl.BlockSpec((1, 2 * H, L), lambda b: (b, 0, 0)),
                  const((SSD_CONV, XBC_W)), const((1, XBC_W)),
                  const((1, lanes)), const((2 * H, 1)), const((1, lanes)), const((2 * H, 1)),
                  const((1, G)), const((1, G)), st_spec],
        out_specs=[pl.BlockSpec((1, L, G), lambda b: (b, 0, 0)), st_spec],
        out_shape=[jax.ShapeDtypeStruct((Bsz, L, G), f32),
                   jax.ShapeDtypeStruct((Bsz, 2, H, SSD_STATE, SSD_HEAD_DIM), f32)],
        scratch_shapes=[pltpu.VMEM((L + 2 * _CONV_HALO, XBC_W), f32),
                        pltpu.VMEM((L, XBC_W), f32),
                        pltpu.VMEM((nc, SSD_GROUPS * SSD_STATE, Q), f32),
                        pltpu.VMEM((L, G), f32),
                        pltpu.VMEM((L, lanes), f32),
                        pltpu.VMEM((nc * 2 * H, Q), f32),
                        pltpu.VMEM((nc * 2 * H, Q), f32),
                        pltpu.VMEM((nc * 2 * H, Q), f32)],
        compiler_params=pltpu.CompilerParams(dimension_semantics=("arbitrary",),
                                             vmem_limit_bytes=_VMEM_LIMIT),
        name="ssd",
    )(pzx, pdt, dt_rows, conv_w, conv_b.reshape(1, XBC_W), a_rowv, neg_a.reshape(2 * H, 1),
      b_rowv, dt_bias.astype(f32).reshape(2 * H, 1), d_lane, norm_g.reshape(1, G), init)


def _mixer_ln(x, xc, mod, mod_c, ctx_out, alpha, w_in, w_out, ln_g, ln_b, conf, rpb, hy, ssd):
    Bsz = x.shape[0]
    sh, sc, g = mod
    shc, scc, gc = mod_c
    pa, pb, py, pzx, pdt = _in_proj(x, sc, sh, w_in)
    ca, cb, cy, czx, cdt = _in_proj(xc, scc, shc, w_in)
    zero = jnp.zeros((Bsz, 2, SSD_HEADS, SSD_STATE, SSD_HEAD_DIM), jnp.float32)
    y_dc, ctx_states = _ssd(czx, cdt, zero, *ssd)
    y_d, _ = _ssd(pzx, pdt, ctx_states, *ssd)
    ys = [_conformer(pa, *conf), _na_attention(pb, cb, rpb), _hyena(py, *hy), y_d]
    x_new = _out_proj_ln(ys, w_out, x, g, ln_g, ln_b, alpha)
    if not ctx_out:
        return x_new, None
    ycs = [_conformer(ca, *conf), _ctx_attention(cb), _hyena(cy, *hy), y_dc]
    return x_new, _out_proj_ln(ycs, w_out, xc, gc, ln_g, ln_b, alpha)


_NEG = -1e30
_ROUTE_TT = 256
_DENSE_TT = 1024
_DENSE_PARTS = 4
_DENSE_EC = 1024
_NT = (((1,), (1,)), ((), ()))


def _bf16_round(x):
    return x.astype(jnp.bfloat16).astype(jnp.float32)


def _oddeven_sort_pairs(n):
    pairs = []
    p = 1
    while p < n:
        k = p
        while k >= 1:
            for j in range(k % p, n - k, 2 * k):
                for i in range(min(k, n - j - k)):
                    if (i + j) // (2 * p) == (i + j + k) // (2 * p):
                        pairs.append((i + j, i + j + k))
            k //= 2
        p *= 2
    return pairs


_SORT16 = _oddeven_sort_pairs(PEER_TOPK)


def _order_pair(vs, i, j):
    a, b = vs[i], vs[j]
    if b is None:
        return
    if a is None:
        vs[i], vs[j] = b, None
        return
    vs[i], vs[j] = jnp.maximum(a, b), jnp.minimum(a, b)


def _top16_replicated(vs):
    vs = list(vs) + [None] * (PEER_TOPK - len(vs))
    for i, j in _SORT16:
        _order_pair(vs, i, j)
    for shift in (4, 2, 1):
        other = [None if v is None else pltpu.roll(v, shift, 0) for v in vs]
        merged = []
        for k in range(PEER_TOPK):
            a, b = vs[k], other[PEER_TOPK - 1 - k]
            merged.append(b if a is None else a if b is None else jnp.maximum(a, b))
        vs = merged
        stride = PEER_TOPK // 2
        while stride >= 1:
            for i in range(PEER_TOPK):
                if (i // stride) % 2 == 0:
                    _order_pair(vs, i, i + stride)
            stride //= 2
    return vs


def _pack_sublanes(blocks):
    sub = lax.broadcasted_iota(jnp.int32, blocks[0].shape, 0)
    out = blocks[0]
    for r in range(1, len(blocks)):
        out = jnp.where(sub == r, blocks[r], out)
    return out


def _pair_candidates(r1, lo1, hi1, r2, lo2, hi2, op, fill):
    keep = lax.broadcasted_iota(jnp.int32, lo1.shape, 0) >= 4
    pieces = [op(r1[0], lo2), op(r1[0], hi2), op(r1[1], lo2), op(r1[2], lo2), op(r1[3], lo2), op(r2[0], hi1)]
    for b in range(3):
        pieces.append(jnp.where(keep, op(r2[b], lo1), fill))
    return pieces


def _peer_route_body(x_ref, sc_ref, sh_ref, wqT_ref, keys_ref, hm_ref, e1_ref, e2_ref, pthr_ref, qT_ref):
    hm = (x_ref[0] * (1.0 + sc_ref[0]) + sh_ref[0]).astype(jnp.bfloat16)
    hm_ref[0] = hm
    qT_ref[...] = lax.dot_general(wqT_ref[...], hm, _NT, preferred_element_type=jnp.float32)
    nblk = PEER_KEYS // 8

    def head(h, carry):
        base = pl.multiple_of(h * PEER_QDIM, PEER_QDIM)
        half_q = PEER_QDIM // 2
        s_both = []
        for p in range(2):
            qb = qT_ref[pl.ds(base + p * half_q, half_q), :].astype(jnp.bfloat16)
            s_both.append(jnp.dot(keys_ref[h, p], qb, preferred_element_type=jnp.float32))
        for half in range(_ROUTE_TT // 128):
            lanes = slice(half * 128, (half + 1) * 128)
            s1 = s_both[0][:, lanes]
            s2 = s_both[1][:, lanes]
            r1 = _top16_replicated([s1[8 * k:8 * k + 8] for k in range(nblk)])
            r2 = _top16_replicated([s2[8 * k:8 * k + 8] for k in range(nblk)])
            lo1, hi1 = _pack_sublanes(r1[:8]), _pack_sublanes(r1[8:])
            lo2, hi2 = _pack_sublanes(r2[:8]), _pack_sublanes(r2[8:])
            cand = _pair_candidates(r1, lo1, hi1, r2, lo2, hi2, lambda a, b: a + b, _NEG)
            cv = _top16_replicated(cand)
            top, thr = cv[0], cv[PEER_TOPK - 1]
            z = jnp.ones_like(top)
            for k in range(1, PEER_TOPK):
                z = z + jnp.exp(cv[k] - top)
            rz = 1.0 / z
            f1 = lambda v: _bf16_round(jnp.exp(v - r1[0]) * rz)
            f2 = lambda v: _bf16_round(jnp.exp(v - r2[0]))
            prod = _pair_candidates([f1(v) for v in r1[:4]], f1(lo1), f1(hi1), [f2(v) for v in r2[:3]], f2(lo2), f2(hi2),
                                    lambda a, b: _bf16_round(a * b), 0.0)
            low = None
            for cpiece, ppiece in zip(cand, prod):
                sel = jnp.where(cpiece >= thr, ppiece, 1e30)
                low = sel if low is None else jnp.minimum(low, sel)
            e1_ref[h, :, lanes] = jnp.exp(s1 - r1[0][0:1]) * rz[0:1]
            e2_ref[h, :, lanes] = (jnp.exp(s2 - r2[0][0:1])).astype(jnp.bfloat16)
            pthr_ref[h, :, lanes] = jnp.min(low, axis=0, keepdims=True)
        return carry

    lax.fori_loop(0, PEER_HEADS, head, 0)


def _peer_route(x, sc, sh, wqT, keys):
    Bsz, S, D = x.shape
    assert S % _ROUTE_TT == 0, S
    nt = S // _ROUTE_TT
    T = Bsz * S
    tab = jax.ShapeDtypeStruct((PEER_HEADS, PEER_KEYS, T), jnp.float32)
    return pl.pallas_call(
        _peer_route_body,
        grid=(Bsz, nt),
        in_specs=[pl.BlockSpec((1, _ROUTE_TT, D), lambda b, i: (b, i, 0)),
                  pl.BlockSpec((1, 1, D), lambda b, i: (b, 0, 0)),
                  pl.BlockSpec((1, 1, D), lambda b, i: (b, 0, 0)),
                  pl.BlockSpec(wqT.shape, lambda b, i: (0, 0)),
                  pl.BlockSpec(keys.shape, lambda b, i: (0, 0, 0, 0))],
        out_specs=[pl.BlockSpec((1, _ROUTE_TT, D), lambda b, i: (b, i, 0)),
                   pl.BlockSpec((PEER_HEADS, PEER_KEYS, _ROUTE_TT), lambda b, i: (0, 0, b * nt + i)),
                   pl.BlockSpec((PEER_HEADS, PEER_KEYS, _ROUTE_TT), lambda b, i: (0, 0, b * nt + i)),
                   pl.BlockSpec((PEER_HEADS, 1, _ROUTE_TT), lambda b, i: (0, 0, b * nt + i))],
        out_shape=[jax.ShapeDtypeStruct((Bsz, S, D), jnp.bfloat16), tab,
                   jax.ShapeDtypeStruct(tab.shape, jnp.bfloat16),
                   jax.ShapeDtypeStruct((PEER_HEADS, 1, T), jnp.float32)],
        scratch_shapes=[pltpu.VMEM((PEER_HEADS * PEER_QDIM, _ROUTE_TT), jnp.float32)],
        compiler_params=pltpu.CompilerParams(dimension_semantics=("arbitrary", "arbitrary"),
                                             vmem_limit_bytes=_VMEM_LIMIT),
        name="peer_route",
    )(x, sc, sh, wqT, keys)


_GELU_K = math.sqrt(2.0 / math.pi)


def _gelu_tanh(x):
    half = 0.5 * x
    return half + half * jnp.tanh(x * (_GELU_K + (_GELU_K * 0.044715) * (x * x)))


_PACK = 16


def _peer_dense_body(alpha, hm_ref, e1_ref, e2_ref, pthr_ref, u_ref, vt_ref, x_ref, g_ref, lng_ref, lnb_ref,
                     o_ref, acc_ref, wt_ref, e1b_ref, pthrb_ref):
    c = pl.program_id(2)
    jrows = 32
    bf16 = jnp.bfloat16
    tp = _DENSE_TT // _DENSE_PARTS

    @pl.when(c == 0)
    def _():
        acc_ref[...] = jnp.zeros_like(acc_ref)
        for h in range(PEER_HEADS):
            pthrb_ref[h] = jnp.broadcast_to(pthr_ref[h], (_PACK, _DENSE_TT)).astype(bf16)

    parts = [slice(t * tp, (t + 1) * tp) for t in range(_DENSE_PARTS)]

    def scores(toks):
        return lax.dot_general(u_ref[...], hm_ref[0, toks, :], _NT, preferred_element_type=jnp.float32)

    def build(toks, act):
        for ii in range(_DENSE_EC // PEER_KEYS):
            for h in range(PEER_HEADS):
                e1b_ref[ii % 2, h] = jnp.broadcast_to(e1_ref[h, ii:ii + 1, toks], (_PACK, tp)).astype(bf16)
            for jb in range(PEER_KEYS // jrows):
                r0 = ii * PEER_KEYS + jb * jrows
                gate = jnp.zeros((jrows // _PACK, _PACK, tp), bf16)
                for h in range(PEER_HEADS):
                    e2 = e2_ref[h, jb * jrows:(jb + 1) * jrows, toks].reshape(jrows // _PACK, _PACK, tp)
                    val = e2 * e1b_ref[ii % 2, h][None]
                    gate = gate + jnp.where(val >= pthrb_ref[h, :, toks][None], val, jnp.zeros_like(val))
                gel = _gelu_tanh(act[r0:r0 + jrows, :]).astype(bf16)
                wt_ref[r0:r0 + jrows, toks] = gate.reshape(jrows, tp) * gel
        acc_ref[:, toks] += jnp.dot(vt_ref[...], wt_ref[:, toks], preferred_element_type=jnp.float32)

    acts = {t: scores(parts[t]) for t in range(min(2, _DENSE_PARTS))}
    for t in range(_DENSE_PARTS):
        build(parts[t], acts.pop(t))
        if t + 2 < _DENSE_PARTS:
            acts[t + 2] = scores(parts[t + 2])

    @pl.when(c == pl.num_programs(2) - 1)
    def _():
        y = alpha * x_ref[0] + g_ref[0] * acc_ref[...].T
        mu = jnp.mean(y, -1, keepdims=True)
        yc = y - mu
        var = jnp.mean(yc * yc, -1, keepdims=True)
        o_ref[0] = yc * lax.rsqrt(var + LN_EPS) * lng_ref[...] + lnb_ref[...]


def _peer_dense(hm, e1, e2, pthr, u_bf, vt_bf, x, g, ln_g, ln_b, alpha):
    Bsz, S, D = x.shape
    assert S % _DENSE_TT == 0, S
    nt = S // _DENSE_TT
    nchunk = N_EXPERTS // _DENSE_EC
    rows_i = _DENSE_EC // PEER_KEYS
    return pl.pallas_call(
        functools.partial(_peer_dense_body, alpha),
        grid=(Bsz, nt, nchunk),
        in_specs=[pl.BlockSpec((1, _DENSE_TT, D), lambda b, i, c: (b, i, 0)),
                  pl.BlockSpec((PEER_HEADS, rows_i, _DENSE_TT), lambda b, i, c: (0, c, b * nt + i)),
                  pl.BlockSpec((PEER_HEADS, PEER_KEYS, _DENSE_TT), lambda b, i, c: (0, 0, b * nt + i)),
                  pl.BlockSpec((PEER_HEADS, 1, _DENSE_TT), lambda b, i, c: (0, 0, b * nt + i)),
                  pl.BlockSpec((_DENSE_EC, D), lambda b, i, c: (c, 0)),
                  pl.BlockSpec((D, _DENSE_EC), lambda b, i, c: (0, c)),
                  pl.BlockSpec((1, _DENSE_TT, D), lambda b, i, c: (b, i, 0)),
                  pl.BlockSpec((1, 1, D), lambda b, i, c: (b, 0, 0)),
                  pl.BlockSpec((1, D), lambda b, i, c: (0, 0)),
                  pl.BlockSpec((1, D), lambda b, i, c: (0, 0))],
        out_specs=pl.BlockSpec((1, _DENSE_TT, D), lambda b, i, c: (b, i, 0)),
        out_shape=jax.ShapeDtypeStruct((Bsz, S, D), jnp.float32),
        scratch_shapes=[pltpu.VMEM((D, _DENSE_TT), jnp.float32),
                        pltpu.VMEM((_DENSE_EC, _DENSE_TT), jnp.bfloat16),
                        pltpu.VMEM((2, PEER_HEADS, _PACK, _DENSE_TT // _DENSE_PARTS), jnp.bfloat16),
                        pltpu.VMEM((PEER_HEADS, _PACK, _DENSE_TT), jnp.bfloat16)],
        compiler_params=pltpu.CompilerParams(dimension_semantics=("arbitrary", "arbitrary", "arbitrary"),
                                             vmem_limit_bytes=_VMEM_LIMIT),
        name="peer_dense",
    )(hm, e1, e2, pthr, u_bf, vt_bf, x, g, ln_g.reshape(1, D), ln_b.reshape(1, D))


def _peer_weights(wq, sub_keys, u_tab, v_tab):
    return (wq.T.astype(jnp.bfloat16), sub_keys.astype(jnp.bfloat16),
            u_tab.astype(jnp.bfloat16), v_tab.T.astype(jnp.bfloat16))


def _peer_ln(x, sc, sh, g, pw, ln_g, ln_b, alpha):
    wqT, keys, u_bf, vt_bf = pw
    hm, e1, e2, pthr = _peer_route(x, sc, sh, wqT, keys)
    return _peer_dense(hm, e1, e2, pthr, u_bf, vt_bf, x, g, ln_g, ln_b, alpha)


def kernel(x, c, ctx, c_ctx, w_ada, b_ada, w_in, w_out, ln1_g, ln1_b, ln2_g, ln2_b,
           conf_dw_w, conf_dw_b, conf_norm_g, conf_norm_b, na_rpb, hy_short_w, hy_short_b,
           hy_w1, hy_b1, hy_w2, hy_b2, hy_w3, hy_decay, hy_bias, ssd_conv_w, ssd_conv_b,
           ssd_a_log, ssd_dt_bias, ssd_d, ssd_norm_g, peer_wq, peer_keys, peer_u, peer_v):
    alpha = (2.0 * DEPTH) ** 0.25
    s_c = jax.nn.silu(c)
    s_cc = jax.nn.silu(c_ctx)
    xc = ctx
    Bsz, Lc, D = ctx.shape
    for l in range(DEPTH):
        ctx_out = l < DEPTH - 1
        mod = (s_c @ w_ada[l] + b_ada[l])[:, None, :]
        mod_c = jnp.broadcast_to((s_cc @ w_ada[l] + b_ada[l])[None, None, :], mod.shape)
        sh1, sc1, g1, sh2, sc2, g2 = jnp.split(mod, 6, -1)
        sh1c, sc1c, g1c, sh2c, sc2c, g2c = jnp.split(mod_c, 6, -1)
        x, xc = _mixer_ln(
            x, xc, (sh1, sc1, g1), (sh1c, sc1c, g1c), ctx_out, alpha, w_in[l], w_out[l], ln1_g[l], ln1_b[l],
            (conf_dw_w[l], conf_dw_b[l], conf_norm_g[l], conf_norm_b[l]), na_rpb[l],
            (hy_short_w[l], hy_short_b[l], hy_w1[l], hy_b1[l], hy_w2[l], hy_b2[l], hy_w3[l], hy_decay[l], hy_bias[l]),
            (ssd_conv_w[l], ssd_conv_b[l], ssd_a_log[l], ssd_dt_bias[l], ssd_d[l], ssd_norm_g[l]))
        pw = _peer_weights(peer_wq[l], peer_keys[l], peer_u[l], peer_v[l])
        x = _peer_ln(x, sc2, sh2, g2, pw, ln2_g[l], ln2_b[l], alpha)
        if ctx_out:
            xc = _peer_ln(xc.reshape(1, Bsz * Lc, D), sc2c[:1], sh2c[:1], g2c[:1], pw, ln2_g[l], ln2_b[l],
                          alpha).reshape(Bsz, Lc, D)
    return x
```

```python
import functools
import math
import jax, jax.numpy as jnp
from jax import lax
from jax.experimental import pallas as pl
from jax.experimental.pallas import tpu as pltpu

D_MODEL = 1024
DEPTH = 2

GRID_W = 64
N_MIXERS = 4
GROUP_W = D_MODEL // N_MIXERS
D_MIX = N_MIXERS * GROUP_W
LN_EPS = 1e-5
CONF_K = 31
CONF_GROUPS = 4
NA_HEADS = 4
HEAD_DIM = GROUP_W // NA_HEADS
NA_KH = 8
NA_KW = 16
ROPE_BASE = 10000.0
HY_SHORT = 3
HY_BANDS = 16
HY_EMB = 1 + 2 * HY_BANDS
HY_HIDDEN = 64
HY_SIN_FREQ = 1.0
SSD_HEADS = 4
SSD_HEAD_DIM = GROUP_W // SSD_HEADS
SSD_GROUPS = 2
SSD_STATE = 64
SSD_CONV = 3
SSD_CHUNK = 128
XBC_W = GROUP_W + 2 * SSD_GROUPS * SSD_STATE
IN_COLS = 2 * GROUP_W + 3 * GROUP_W + 3 * GROUP_W + GROUP_W + XBC_W + 2 * SSD_HEADS
PEER_HEADS = 8
PEER_KEYS = 128
PEER_TOPK = 16
PEER_QDIM = 256
N_EXPERTS = PEER_KEYS * PEER_KEYS

_LANES = 128
_SUBLANES = 8
_VMEM_LIMIT = 56 * 1024 * 1024
_IN_PAD = -(-IN_COLS // _LANES) * _LANES


_IN_SPLITS = ((0, 2 * GROUP_W), (2 * GROUP_W, 5 * GROUP_W), (5 * GROUP_W, 8 * GROUP_W),
              (8 * GROUP_W, 9 * GROUP_W + XBC_W), (9 * GROUP_W + XBC_W, _IN_PAD))
_PROJ_TM = 512


def _in_proj_body(x_ref, sc_ref, sh_ref, w_ref, *o_refs):
    h = (x_ref[0] * (1.0 + sc_ref[0]) + sh_ref[0]).astype(jnp.bfloat16)
    for (lo, hi), o_ref in zip(_IN_SPLITS, o_refs):
        o_ref[0] = jnp.dot(h, w_ref[:, lo:hi], preferred_element_type=jnp.float32)


def _in_proj(x, sc, sh, w_in):
    Bsz, L, D = x.shape
    tm = min(_PROJ_TM, L)
    w = jnp.pad(w_in, ((0, 0), (0, _IN_PAD - IN_COLS))).astype(jnp.bfloat16)
    return pl.pallas_call(
        _in_proj_body,
        grid=(Bsz, L // tm),
        in_specs=[pl.BlockSpec((1, tm, D), lambda b, i: (b, i, 0)),
                  pl.BlockSpec((1, 1, D), lambda b, i: (b, 0, 0)),
                  pl.BlockSpec((1, 1, D), lambda b, i: (b, 0, 0)),
                  pl.BlockSpec((D, _IN_PAD), lambda b, i: (0, 0))],
        out_specs=[pl.BlockSpec((1, tm, hi - lo), lambda b, i: (b, i, 0)) for lo, hi in _IN_SPLITS],
        out_shape=[jax.ShapeDtypeStruct((Bsz, L, hi - lo), jnp.float32) for lo, hi in _IN_SPLITS],
        compiler_params=pltpu.CompilerParams(dimension_semantics=("arbitrary", "arbitrary"),
                                             vmem_limit_bytes=_VMEM_LIMIT),
        name="in_proj",
    )(x, sc, sh, w)


def _out_proj_ln_body(alpha, ya_ref, yb_ref, yc_ref, yd_ref, w_ref, x_ref, g_ref, lng_ref, lnb_ref, o_ref):
    y = 0.0
    for m, y_ref in enumerate((ya_ref, yb_ref, yc_ref, yd_ref)):
        y = y + jnp.dot(y_ref[0].astype(jnp.bfloat16), w_ref[m * GROUP_W:(m + 1) * GROUP_W, :],
                        preferred_element_type=jnp.float32)
    r = alpha * x_ref[0] + g_ref[0] * y
    mu = jnp.mean(r, -1, keepdims=True)
    rc = r - mu
    var = jnp.mean(rc * rc, -1, keepdims=True)
    o_ref[0] = rc * lax.rsqrt(var + LN_EPS) * lng_ref[...] + lnb_ref[...]


def _out_proj_ln(ys, w_out, x, g, ln_g, ln_b, alpha):
    Bsz, L, D = x.shape
    tm = min(_PROJ_TM, L)
    mix = pl.BlockSpec((1, tm, GROUP_W), lambda b, i: (b, i, 0))
    return pl.pallas_call(
        functools.partial(_out_proj_ln_body, alpha),
        grid=(Bsz, L // tm),
        in_specs=[mix, mix, mix, mix,
                  pl.BlockSpec((D_MIX, D), lambda b, i: (0, 0)),
                  pl.BlockSpec((1, tm, D), lambda b, i: (b, i, 0)),
                  pl.BlockSpec((1, 1, D), lambda b, i: (b, 0, 0)),
                  pl.BlockSpec((1, D), lambda b, i: (0, 0)),
                  pl.BlockSpec((1, D), lambda b, i: (0, 0))],
        out_specs=pl.BlockSpec((1, tm, D), lambda b, i: (b, i, 0)),
        out_shape=jax.ShapeDtypeStruct((Bsz, L, D), jnp.float32),
        compiler_params=pltpu.CompilerParams(dimension_semantics=("arbitrary", "arbitrary"),
                                             vmem_limit_bytes=_VMEM_LIMIT),
        name="out_proj_ln",
    )(*ys, w_out.astype(jnp.bfloat16), x, g, ln_g.reshape(1, D), ln_b.reshape(1, D))


def _shifted_taps(win, first, n_taps, rows):
    n = win.shape[0]
    rolled = {0: win}
    taps = []
    for k in range(n_taps):
        r = (first + k) % _SUBLANES
        if r not in rolled:
            rolled[r] = pltpu.roll(win, n - r, 0)
        base = first + k - r
        taps.append(rolled[r][base:base + rows])
    return taps


_CONV_HALO = 16
_CONV_TILE = 256


def _group_mean_matrix(width, group):
    r = lax.broadcasted_iota(jnp.int32, (width, width), 0) // group
    c = lax.broadcasted_iota(jnp.int32, (width, width), 1) // group
    return jnp.where(r == c, 1.0 / group, 0.0).astype(jnp.float32)


def _conformer_body(L, p_ref, w_ref, b_ref, ng_ref, nb_ref, o_ref, u_ref):
    G = GROUP_W
    pad = (CONF_K - 1) // 2
    halo = jnp.zeros((_CONV_HALO, G), jnp.float32)
    u_ref[0:_CONV_HALO, :] = halo
    u_ref[_CONV_HALO + L:_CONV_HALO + L + _CONV_HALO, :] = halo
    u_ref[_CONV_HALO:_CONV_HALO + L, :] = p_ref[0, :, 0:G] * jax.nn.sigmoid(p_ref[0, :, G:2 * G])
    avg = _group_mean_matrix(G, G // CONF_GROUPS)
    tile = min(_CONV_TILE, L)
    for t in range(L // tile):
        win = u_ref[t * tile:t * tile + tile + 2 * _CONV_HALO, :]
        acc = jnp.zeros((tile, G), jnp.float32) + b_ref[...]
        for k, tap in enumerate(_shifted_taps(win, _CONV_HALO - pad, CONF_K, tile)):
            acc = acc + tap * w_ref[k:k + 1, :]
        mu = jnp.dot(acc, avg, preferred_element_type=jnp.float32, precision=lax.Precision.HIGHEST)
        cen = acc - mu
        var = jnp.dot(cen * cen, avg, preferred_element_type=jnp.float32, precision=lax.Precision.HIGHEST)
        un = cen * lax.rsqrt(var + LN_EPS) * ng_ref[...] + nb_ref[...]
        o_ref[0, t * tile:(t + 1) * tile, :] = un * jax.nn.sigmoid(un)


def _conformer(pa, dw_w, dw_b, n_g, n_b):
    Bsz, L, _ = pa.shape
    G = GROUP_W
    vec = pl.BlockSpec((1, G), lambda b: (0, 0))
    return pl.pallas_call(
        functools.partial(_conformer_body, L),
        grid=(Bsz,),
        in_specs=[pl.BlockSpec((1, L, 2 * G), lambda b: (b, 0, 0)),
                  pl.BlockSpec((CONF_K, G), lambda b: (0, 0)), vec, vec, vec],
        out_specs=pl.BlockSpec((1, L, G), lambda b: (b, 0, 0)),
        out_shape=jax.ShapeDtypeStruct((Bsz, L, G), jnp.float32),
        scratch_shapes=[pltpu.VMEM((L + 2 * _CONV_HALO, G), jnp.float32)],
        compiler_params=pltpu.CompilerParams(dimension_semantics=("arbitrary",),
                                             vmem_limit_bytes=_VMEM_LIMIT),
        name="conformer",
    )(pa, dw_w, dw_b.reshape(1, G), n_g.reshape(1, G), n_b.reshape(1, G))


def axial_rope(rows, head_dim):
    n_f = head_dim // 4
    inv = ROPE_BASE ** (-jnp.arange(n_f, dtype=jnp.float32) / n_f)
    t = jnp.arange(rows * GRID_W)
    r = (t // GRID_W).astype(jnp.float32)
    col = (t % GRID_W).astype(jnp.float32)
    ang = jnp.concatenate([r[:, None] * inv, col[:, None] * inv], -1)
    return jnp.cos(ang), jnp.sin(ang)


_NA_MASK = -1e30


def _na_tables(rows, rpb):
    cos, sin = axial_rope(rows, HEAD_DIM)
    cos_f = jnp.tile(cos, (1, 2 * NA_HEADS))
    sin_s = jnp.tile(jnp.concatenate([-sin, sin], -1), (1, NA_HEADS))
    cq = jnp.arange(GRID_W)
    cs = jnp.clip(cq - NA_KW // 2, 0, GRID_W - NA_KW)
    col = jnp.arange(GRID_W)
    in_band = (col[None, :] >= cs[:, None]) & (col[None, :] < cs[:, None] + NA_KW)
    cb_idx = jnp.clip(col[None, :] - cq[:, None] + (NA_KW - 1), 0, 2 * NA_KW - 2)
    po = jnp.arange(NA_KH)
    rb_idx = jnp.arange(NA_KH)[None, :] - po[:, None] + (NA_KH - 1)
    bias = rpb[:, rb_idx][:, :, :, cb_idx]
    bias = jnp.where(in_band[None, None, None], bias, _NA_MASK)
    bias = bias.transpose(1, 0, 3, 2, 4).reshape(NA_KH, NA_HEADS, GRID_W, NA_KH * GRID_W)
    return cos_f, sin_s, bias


def _rope_lanes(x, cos_f, sin_s):
    n = x.shape[-1]
    hd = HEAD_DIM // 2
    first = (lax.broadcasted_iota(jnp.int32, x.shape, 1) % HEAD_DIM) < hd
    partner = jnp.where(first, pltpu.roll(x, n - hd, 1), pltpu.roll(x, hd, 1))
    return x * cos_f + partner * sin_s


def _softmax_pv(s_parts, v_parts):
    m = s_parts[0].max(-1, keepdims=True)
    for s in s_parts[1:]:
        m = jnp.maximum(m, s.max(-1, keepdims=True))
    l = 0.0
    o = 0.0
    for s, v in zip(s_parts, v_parts):
        p = jnp.exp(s - m)
        l = l + p.sum(-1, keepdims=True)
        o = o + jnp.dot(p.astype(jnp.bfloat16), v, preferred_element_type=jnp.float32)
    return o / l


_NA_ROWS = 4


def _na_body(rows, q_ref, k_ref, v_ref, kc_ref, vc_ref, cosq_ref, sinq_ref, cosk_ref, sink_ref, *rest):
    bias_refs, (o_ref, krot_ref, vbf_ref) = rest[:_NA_ROWS], rest[_NA_ROWS:]
    step = pl.program_id(1)
    scale = HEAD_DIM ** -0.5

    @pl.when(step == 0)
    def _():
        krot_ref[...] = _rope_lanes(k_ref[0], cosk_ref[...], sink_ref[...]).astype(jnp.bfloat16)
        vbf_ref[...] = v_ref[0].astype(jnp.bfloat16)

    win = NA_KH * GRID_W
    q = q_ref[0] * scale
    q_rot = _rope_lanes(q, cosq_ref[...], sinq_ref[...]).astype(jnp.bfloat16)
    q_plain = q.astype(jnp.bfloat16)
    kc = kc_ref[0].astype(jnp.bfloat16)
    vc = vc_ref[0].astype(jnp.bfloat16)
    for i in range(_NA_ROWS):
        r = step * _NA_ROWS + i
        rs = jnp.clip(r - NA_KH // 2, 0, rows - NA_KH)
        start = pl.multiple_of(rs * GRID_W, GRID_W)
        kw = krot_ref[pl.ds(start, win), :]
        vw = vbf_ref[pl.ds(start, win), :]
        qs = slice(i * GRID_W, (i + 1) * GRID_W)
        outs = []
        for h in range(NA_HEADS):
            hs = slice(h * HEAD_DIM, (h + 1) * HEAD_DIM)
            s_loc = lax.dot_general(q_rot[qs, hs], kw[:, hs], _NT, preferred_element_type=jnp.float32) + bias_refs[i][0, h]
            s_ctx = lax.dot_general(q_plain[qs, hs], kc[:, hs], _NT, preferred_element_type=jnp.float32)
            outs.append(_softmax_pv([s_loc, s_ctx], [vw[:, hs], vc[:, hs]]))
        o_ref[0, qs, :] = jnp.concatenate(outs, axis=-1)


def _na_attention(pb, cb, rpb):
    Bsz, S, _ = pb.shape
    Lc = cb.shape[1]
    rows = S // GRID_W
    assert rows % _NA_ROWS == 0, rows
    G = GROUP_W
    nq = _NA_ROWS * GRID_W
    cos_f, sin_s, bias = _na_tables(rows, rpb)

    def bias_spec(i):
        def idx(b, step):
            r = step * _NA_ROWS + i
            return (r - jnp.clip(r - NA_KH // 2, 0, rows - NA_KH), 0, 0, 0)
        return pl.BlockSpec((1, NA_HEADS, GRID_W, NA_KH * GRID_W), idx)

    return pl.pallas_call(
        functools.partial(_na_body, rows),
        grid=(Bsz, rows // _NA_ROWS),
        in_specs=[pl.BlockSpec((1, nq, G), lambda b, r: (b, r, 0)),
                  pl.BlockSpec((1, S, G), lambda b, r: (b, 0, 1)),
                  pl.BlockSpec((1, S, G), lambda b, r: (b, 0, 2)),
                  pl.BlockSpec((1, Lc, G), lambda b, r: (b, 0, 1)),
                  pl.BlockSpec((1, Lc, G), lambda b, r: (b, 0, 2)),
                  pl.BlockSpec((nq, G), lambda b, r: (r, 0)),
                  pl.BlockSpec((nq, G), lambda b, r: (r, 0)),
                  pl.BlockSpec((S, G), lambda b, r: (0, 0)),
                  pl.BlockSpec((S, G), lambda b, r: (0, 0))] + [bias_spec(i) for i in range(_NA_ROWS)],
        out_specs=pl.BlockSpec((1, nq, G), lambda b, r: (b, r, 0)),
        out_shape=jax.ShapeDtypeStruct((Bsz, S, G), jnp.float32),
        scratch_shapes=[pltpu.VMEM((S, G), jnp.bfloat16), pltpu.VMEM((S, G), jnp.bfloat16)],
        compiler_params=pltpu.CompilerParams(dimension_semantics=("arbitrary", "arbitrary"),
                                             vmem_limit_bytes=_VMEM_LIMIT),
        name="na_attention",
    )(pb, pb, pb, cb, cb, cos_f, sin_s, cos_f, sin_s, *([bias] * _NA_ROWS))


def _ctx_attn_body(q_ref, k_ref, v_ref, o_ref):
    q = (q_ref[0] * HEAD_DIM ** -0.5).astype(jnp.bfloat16)
    k = k_ref[0].astype(jnp.bfloat16)
    v = v_ref[0].astype(jnp.bfloat16)
    outs = []
    for h in range(NA_HEADS):
        hs = slice(h * HEAD_DIM, (h + 1) * HEAD_DIM)
        s = lax.dot_general(q[:, hs], k[:, hs], _NT, preferred_element_type=jnp.float32)
        outs.append(_softmax_pv([s], [v[:, hs]]))
    o_ref[0] = jnp.concatenate(outs, axis=-1)


def _ctx_attention(cb):
    Bsz, Lc, _ = cb.shape
    G = GROUP_W
    return pl.pallas_call(
        _ctx_attn_body,
        grid=(Bsz,),
        in_specs=[pl.BlockSpec((1, Lc, G), lambda b: (b, 0, 0)),
                  pl.BlockSpec((1, Lc, G), lambda b: (b, 0, 1)),
                  pl.BlockSpec((1, Lc, G), lambda b: (b, 0, 2))],
        out_specs=pl.BlockSpec((1, Lc, G), lambda b: (b, 0, 0)),
        out_shape=jax.ShapeDtypeStruct((Bsz, Lc, G), jnp.float32),
        compiler_params=pltpu.CompilerParams(dimension_semantics=("arbitrary",)),
        name="ctx_attention",
    )(cb, cb, cb)


_HY_TB = 256
_HY_CB = 8


def _hyena_filter_body(L, w1_ref, b1_ref, w2_ref, b2_ref, w3_ref, dec_ref, o_ref):
    G = GROUP_W
    hp = lax.Precision.HIGHEST
    p = lax.broadcasted_iota(jnp.int32, (2 * L, _LANES), 0)
    lane = lax.broadcasted_iota(jnp.int32, (2 * L, _LANES), 1)
    tn = jnp.abs(p - L).astype(jnp.float32) / L
    band = ((lane - 1) % HY_BANDS + 1).astype(jnp.float32)
    ang = 2.0 * math.pi * band * tn
    z = jnp.where(lane == 0, tn, jnp.where(lane <= HY_BANDS, jnp.sin(ang), jnp.cos(ang)))
    z = jnp.where(lane < HY_EMB, z, 0.0)
    h = jnp.sin(HY_SIN_FREQ * (jnp.dot(z, w1_ref[...], precision=hp, preferred_element_type=jnp.float32) + b1_ref[...]))
    h = jnp.sin(HY_SIN_FREQ * (jnp.dot(h, w2_ref[...], precision=hp, preferred_element_type=jnp.float32) + b2_ref[...]))
    k = jnp.dot(h, w3_ref[...], precision=hp, preferred_element_type=jnp.float32) * jnp.exp(-tn[:, 0:1] * dec_ref[...])
    kf, kb = k[:, 0:G], k[:, G:2 * G]
    n = lax.broadcasted_iota(jnp.int32, (2 * L, G), 0) - L
    nf = jnp.sum(jnp.where(n >= 0, jnp.abs(kf), 0.0), axis=0, keepdims=True) + 1e-6
    nb = jnp.sum(jnp.where(n <= 0, jnp.where(n > -L, jnp.abs(kb), 0.0), 0.0), axis=0, keepdims=True) + 1e-6
    rev = jnp.where(n > 0, kb / nb, jnp.where(n > -L, kf / nf, 0.0))
    o_ref[...] = rev.T


def _hyena_filters_rev(L, w1, b1, w2, b2, w3, decay):
    G = GROUP_W
    hp = _LANES - HY_HIDDEN
    w1p = jnp.pad(w1.astype(jnp.float32), ((0, _LANES - HY_EMB), (0, hp)))
    args = (w1p, jnp.pad(b1.reshape(1, -1), ((0, 0), (0, hp))), jnp.pad(w2, ((0, hp), (0, hp))),
            jnp.pad(b2.reshape(1, -1), ((0, 0), (0, hp))), jnp.pad(w3, ((0, hp), (0, 0))), decay.reshape(1, -1))
    return pl.pallas_call(
        functools.partial(_hyena_filter_body, L),
        out_shape=jax.ShapeDtypeStruct((G, 2 * L), jnp.float32),
        compiler_params=pltpu.CompilerParams(vmem_limit_bytes=_VMEM_LIMIT),
        name="hyena_filters",
    )(*[a.astype(jnp.float32) for a in args])


def _hyena_pre_body(L, p_ref, w_ref, b_ref, u_ref, ut_ref, x0_ref, xp_ref):
    G = GROUP_W
    W = 3 * G
    halo = jnp.zeros((_CONV_HALO, W), jnp.float32)
    xp_ref[0:_CONV_HALO, :] = halo
    xp_ref[_CONV_HALO + L:_CONV_HALO + L + _CONV_HALO, :] = halo
    xp_ref[_CONV_HALO:_CONV_HALO + L, :] = p_ref[0]
    pad = (HY_SHORT - 1) // 2
    tile = min(_CONV_TILE, L)
    for t in range(L // tile):
        win = xp_ref[t * tile:t * tile + tile + 2 * _CONV_HALO, :]
        acc = jnp.zeros((tile, W), jnp.float32) + b_ref[...]
        for k, tap in enumerate(_shifted_taps(win, _CONV_HALO - pad, HY_SHORT, tile)):
            acc = acc + tap * w_ref[k:k + 1, :]
        rows = slice(t * tile, (t + 1) * tile)
        u = acc[:, 2 * G:3 * G] * acc[:, G:2 * G]
        x0_ref[0, rows, :] = acc[:, 0:G]
        u_ref[0, rows, :] = u
        ut_ref[0, :, rows] = u.T


def _hyena_pre(py, short_w, short_b):
    Bsz, L, W = py.shape
    G = GROUP_W
    f32 = jnp.float32
    return pl.pallas_call(
        functools.partial(_hyena_pre_body, L),
        grid=(Bsz,),
        in_specs=[pl.BlockSpec((1, L, W), lambda b: (b, 0, 0)),
                  pl.BlockSpec((HY_SHORT, W), lambda b: (0, 0)),
                  pl.BlockSpec((1, W), lambda b: (0, 0))],
        out_specs=[pl.BlockSpec((1, L, G), lambda b: (b, 0, 0)),
                   pl.BlockSpec((1, G, L), lambda b: (b, 0, 0)),
                   pl.BlockSpec((1, L, G), lambda b: (b, 0, 0))],
        out_shape=[jax.ShapeDtypeStruct((Bsz, L, G), f32), jax.ShapeDtypeStruct((Bsz, G, L), f32),
                   jax.ShapeDtypeStruct((Bsz, L, G), f32)],
        scratch_shapes=[pltpu.VMEM((L + 2 * _CONV_HALO, W), f32)],
        compiler_params=pltpu.CompilerParams(dimension_semantics=("arbitrary",), vmem_limit_bytes=_VMEM_LIMIT),
        name="hyena_pre",
    )(py, short_w, short_b.reshape(1, W))


def _toeplitz_tile(rolled, q0):
    TB = _HY_TB
    per = _LANES // _SUBLANES
    i = lax.broadcasted_iota(jnp.int32, (_SUBLANES, _LANES), 0)
    l = lax.broadcasted_iota(jnp.int32, (_SUBLANES, _LANES), 1)
    row_blocks = []
    for rg in range(TB // _SUBLANES):
        k = rg % per
        pieces = []
        for lg in range(TB // _LANES):
            o = TB + _LANES * lg - _SUBLANES * rg
            q, rho = divmod(o, _LANES)
            if rho == 0:
                pieces.append(jnp.where(l - i < 0, rolled(q0 + q - 1, k), rolled(q0 + q, k)))
            else:
                pieces.append(jnp.where(l + rho - i >= _LANES, rolled(q0 + q + 1, k), rolled(q0 + q, k)))
        row_blocks.append(jnp.concatenate(pieces, axis=1))
    return jnp.concatenate(row_blocks, axis=0).astype(jnp.bfloat16)


def _hyena_conv_body(L, Bsz, g_ref, u_ref, o_ref):
    TB = min(_HY_TB, L)
    nb = L // TB
    cols = Bsz * nb
    lane = lax.broadcasted_iota(jnp.int32, (TB, cols), 1) % nb

    def channel(ci, carry):
        u = u_ref[ci] if nb == 1 else u_ref[:, ci].reshape(cols, TB)
        u = u.astype(jnp.bfloat16)
        acc = jnp.zeros((TB, cols), jnp.float32)
        cache = {}
        g_row = g_ref[pl.ds(ci, 1), :]

        def rolled(q, k):
            if (q, k) not in cache:
                blk = jnp.broadcast_to(g_row[:, q * _LANES:(q + 1) * _LANES], (_SUBLANES, _LANES))
                cache[q, k] = pltpu.roll(blk, (_SUBLANES * k) % _LANES, 1, stride=1, stride_axis=0)
            return cache[q, k]

        for d in range(-(nb - 1), nb):
            start = L - TB * d - TB
            tile = _toeplitz_tile(rolled, start // _LANES)
            z = lax.dot_general(tile, u, _NT, preferred_element_type=jnp.float32)
            if d != 0:
                z = jnp.where((lane - d >= 0) & (lane - d < nb), pltpu.roll(z, d % cols, 1), 0.0)
            acc = acc + z
        if nb == 1:
            o_ref[ci] = acc.T
        else:
            o_ref[:, ci] = acc.T.reshape(Bsz, nb, TB)
        return carry

    lax.fori_loop(0, _HY_CB, channel, 0)


def _hyena_conv(g_rev, u_t):
    Bsz, G, L = u_t.shape
    TB = min(_HY_TB, L)
    nb = L // TB
    if nb == 1:
        blk = pl.BlockSpec((_HY_CB, Bsz, TB), lambda c: (c, 0, 0))
        operand, out_shape = jnp.swapaxes(u_t, 0, 1), (G, Bsz, TB)
    else:
        blk = pl.BlockSpec((Bsz, _HY_CB, nb, TB), lambda c: (0, c, 0, 0))
        operand, out_shape = u_t.reshape(Bsz, G, nb, TB), (Bsz, G, nb, TB)
    out = pl.pallas_call(
        functools.partial(_hyena_conv_body, L, Bsz),
        grid=(G // _HY_CB,),
        in_specs=[pl.BlockSpec((_HY_CB, 2 * L), lambda c: (c, 0)), blk],
        out_specs=blk,
        out_shape=jax.ShapeDtypeStruct(out_shape, jnp.float32),
        compiler_params=pltpu.CompilerParams(dimension_semantics=("arbitrary",), vmem_limit_bytes=_VMEM_LIMIT),
        name="hyena_conv",
    )(g_rev, operand)
    return jnp.swapaxes(out, 0, 1) if nb == 1 else out.reshape(Bsz, G, L)


def _hyena_post_body(yt_ref, u_ref, x0_ref, skip_ref, o_ref):
    o_ref[0] = (yt_ref[0].T + u_ref[0] * skip_ref[...]) * x0_ref[0]


def _hyena_post(y_t, u, x0, skip):
    Bsz, L, G = u.shape
    tok = pl.BlockSpec((1, L, G), lambda b: (b, 0, 0))
    return pl.pallas_call(
        _hyena_post_body,
        grid=(Bsz,),
        in_specs=[pl.BlockSpec((1, G, L), lambda b: (b, 0, 0)), tok, tok, pl.BlockSpec((1, G), lambda b: (0, 0))],
        out_specs=tok,
        out_shape=jax.ShapeDtypeStruct((Bsz, L, G), jnp.float32),
        compiler_params=pltpu.CompilerParams(dimension_semantics=("arbitrary",), vmem_limit_bytes=_VMEM_LIMIT),
        name="hyena_post",
    )(y_t, u, x0, skip.reshape(1, G))


def _hyena(py, short_w, short_b, w1, b1, w2, b2, w3, decay, skip):
    L = py.shape[1]
    g_rev = _hyena_filters_rev(L, w1, b1, w2, b2, w3, decay)
    u, u_t, x0 = _hyena_pre(py, short_w, short_b)
    return _hyena_post(_hyena_conv(g_rev, u_t), u, x0, skip)


def _split3_dot(a, b_bf16, dims=None):
    hi = a.astype(jnp.bfloat16)
    r1 = a - hi.astype(jnp.float32)
    mid = r1.astype(jnp.bfloat16)
    lo = (r1 - mid.astype(jnp.float32)).astype(jnp.bfloat16)
    out = 0.0
    for part in (hi, mid, lo):
        if dims is None:
            out = out + jnp.dot(part, b_bf16, preferred_element_type=jnp.float32)
        else:
            out = out + jnp.dot(b_bf16, part, preferred_element_type=jnp.float32)
    return out


def _softplus(x):
    return jnp.maximum(x, 0.0) + jnp.log(1.0 + jnp.exp(-jnp.abs(x)))


def _ssd_body(L, zx_ref, dtc_ref, dtr_ref, cw_ref, cb_ref, arow_ref, acol_ref, brow_ref, bcol_ref, dsk_ref,
              ng_ref, init_ref, o_ref, fin_ref, xp_ref, xc_ref, bt_ref, y_ref, ccol_ref, crow_ref, edec_ref,
              tot_ref):
    G = GROUP_W
    Q = SSD_CHUNK
    nc = L // Q
    P = SSD_HEAD_DIM
    N = SSD_STATE
    H = SSD_HEADS
    f32 = jnp.float32
    bf16 = jnp.bfloat16
    halo = jnp.zeros((_CONV_HALO, XBC_W), f32)
    xp_ref[0:_CONV_HALO, :] = halo
    xp_ref[_CONV_HALO + L:_CONV_HALO + L + _CONV_HALO, :] = halo
    xp_ref[_CONV_HALO:_CONV_HALO + L, :] = zx_ref[0, :, G:G + XBC_W]
    pad = (SSD_CONV - 1) // 2
    tile = min(_CONV_TILE, L)
    for t in range(L // tile):
        win = xp_ref[t * tile:t * tile + tile + 2 * _CONV_HALO, :]
        acc = jnp.zeros((tile, XBC_W), f32) + cb_ref[...]
        for k, tap in enumerate(_shifted_taps(win, _CONV_HALO - pad, SSD_CONV, tile)):
            acc = acc + tap * cw_ref[k:k + 1, :]
        xc_ref[t * tile:(t + 1) * tile, :] = acc * jax.nn.sigmoid(acc)
    for c in range(nc):
        bt_ref[c] = xc_ref[c * Q:(c + 1) * Q, G:G + SSD_GROUPS * N].T
    dt_col = _softplus(dtc_ref[0] + brow_ref[...])
    a_col = dt_col * arow_ref[...]
    a_row = _softplus(dtr_ref[0] + bcol_ref[...]) * acol_ref[...]
    a_stack = jnp.concatenate([a_row[:, c * Q:(c + 1) * Q] for c in range(nc)], axis=0)
    ri = lax.broadcasted_iota(jnp.int32, (Q, Q), 0)
    ci = lax.broadcasted_iota(jnp.int32, (Q, Q), 1)
    one = lambda m: jnp.where(m, 1.0, 0.0).astype(bf16)
    tot_ref[...] = _split3_dot(a_stack, jnp.ones((Q, Q), bf16))

    def direction(d, y_store):
        fwd = d == 0
        m_col = one(ci <= ri) if fwd else one(ci >= ri)
        for c in range(nc):
            ccol_ref[c * Q:(c + 1) * Q, :] = _split3_dot(a_col[c * Q:(c + 1) * Q, :], m_col, dims="left")
        crow_ref[...] = _split3_dot(a_stack, one(ri <= ci) if fwd else one(ri >= ci))
        edec_ref[...] = _split3_dot(a_stack, one(ri > ci) if fwd else one(ri < ci))
        keep = (ri >= ci) if fwd else (ri <= ci)

        def chunk(step, states):
            c = step if fwd else nc - 1 - step
            r0 = pl.multiple_of(c * Q, Q)
            j0 = pl.multiple_of(c * 2 * H, 2 * H)
            xc = xc_ref[pl.ds(r0, Q), :]
            ccol = ccol_ref[pl.ds(r0, Q), :]
            crow = crow_ref[pl.ds(j0, 2 * H), :]
            edec = edec_ref[pl.ds(j0, 2 * H), :]
            tot = tot_ref[pl.ds(j0, 2 * H), :]
            dtc = _softplus(dtc_ref[0, pl.ds(r0, Q), :] + brow_ref[...])
            bt = bt_ref[c]
            new_states = []
            outs = []
            for g in range(SSD_GROUPS):
                cm = xc[:, G + SSD_GROUPS * N + g * N:G + SSD_GROUPS * N + (g + 1) * N].astype(bf16)
                bm = xc[:, G + g * N:G + (g + 1) * N].astype(bf16)
                cb = lax.dot_general(cm, bm, _NT, preferred_element_type=f32)
                for hh in range(H // SSD_GROUPS):
                    h = g * (H // SSD_GROUPS) + hh
                    j = d * H + h
                    col = jnp.broadcast_to(ccol[:, j:j + 1], (Q, Q))
                    lmat = jnp.exp(jnp.where(keep, col - crow[j:j + 1, :], _NEG))
                    xd = (xc[:, h * P:(h + 1) * P] * jnp.broadcast_to(dtc[:, j:j + 1], (Q, P))).astype(bf16)
                    st = states[h]
                    y = jnp.dot((cb * lmat).astype(bf16), xd, preferred_element_type=f32)
                    y = y + jnp.dot(cm, st.astype(bf16), preferred_element_type=f32) * jnp.exp(col[:, 0:P])
                    outs.append(y)
                    btd = (bt[g * N:(g + 1) * N, :] * jnp.exp(edec[j:j + 1, :])).astype(bf16)
                    new_states.append(jnp.exp(tot[j:j + 1, 0:P]) * st
                                      + jnp.dot(btd, xd, preferred_element_type=f32))
            y_store(r0, jnp.concatenate(outs, axis=-1))
            return tuple(new_states)

        init = tuple(init_ref[0, d, h] for h in range(H))
        final = lax.fori_loop(0, nc, chunk, init)
        for h in range(H):
            fin_ref[0, d, h] = final[h]

    def store_fwd(r0, y):
        y_ref[pl.ds(r0, Q), :] = y

    def store_bwd(r0, y):
        y_ref[pl.ds(r0, Q), :] += y

    direction(0, store_fwd)
    direction(1, store_bwd)
    gw = G // SSD_GROUPS
    for t in range(L // tile):
        rows = slice(t * tile, (t + 1) * tile)
        z = zx_ref[0, rows, 0:G]
        yg = (y_ref[rows, :] + xc_ref[rows, 0:G] * dsk_ref[...]) * (z * jax.nn.sigmoid(z))
        parts = []
        for g in range(SSD_GROUPS):
            v = yg[:, g * gw:(g + 1) * gw]
            parts.append(v * lax.rsqrt(jnp.mean(v * v, -1, keepdims=True) + LN_EPS))
        o_ref[0, rows, :] = jnp.concatenate(parts, axis=-1) * ng_ref[...]


def _ssd(pzx, pdt, init, conv_w, conv_b, a_log, dt_bias, d_skip, norm_g):
    Bsz, L, _ = pzx.shape
    G, H, Q = GROUP_W, SSD_HEADS, SSD_CHUNK
    lanes = pdt.shape[-1]
    nc = L // Q
    neg_a = -jnp.exp(a_log.astype(jnp.float32)).reshape(1, 2 * H)
    a_rowv = jnp.pad(neg_a, ((0, 0), (0, lanes - 2 * H)))
    b_rowv = jnp.pad(dt_bias.astype(jnp.float32).reshape(1, 2 * H), ((0, 0), (0, lanes - 2 * H)))
    dt_rows = jnp.swapaxes(pdt[:, :, :2 * H], 1, 2)
    d_lane = jnp.repeat(d_skip.astype(jnp.float32), SSD_HEAD_DIM).reshape(1, G)
    const = lambda shape: pl.BlockSpec(shape, lambda b: (0,) * len(shape))
    st_spec = pl.BlockSpec((1, 2, H, SSD_STATE, SSD_HEAD_DIM), lambda b: (b, 0, 0, 0, 0))
    f32 = jnp.float32
    return pl.pallas_call(
        functools.partial(_ssd_body, L),
        grid=(Bsz,),
        in_specs=[pl.BlockSpec((1, L, G + XBC_W), lambda b: (b, 0, 0)),
                  pl.BlockSpec((1, L, lanes), lambda b: (b, 0, 0)),
                  pl.BlockSpec((1, 2 * H, L), lambda b: (b, 0, 0)),
                  const((SSD_CONV, XBC_W)), const((1, XBC_W)),
                  const((1, lanes)), const((2 * H, 1)), const((1, lanes)), const((2 * H, 1)),
                  const((1, G)), const((1, G)), st_spec],
        out_specs=[pl.BlockSpec((1, L, G), lambda b: (b, 0, 0)), st_spec],
        out_shape=[jax.ShapeDtypeStruct((Bsz, L, G), f32),
                   jax.ShapeDtypeStruct((Bsz, 2, H, SSD_STATE, SSD_HEAD_DIM), f32)],
        scratch_shapes=[pltpu.VMEM((L + 2 * _CONV_HALO, XBC_W), f32),
                        pltpu.VMEM((L, XBC_W), f32),
                        pltpu.VMEM((nc, SSD_GROUPS * SSD_STATE, Q), f32),
                        pltpu.VMEM((L, G), f32),
                        pltpu.VMEM((L, lanes), f32),
                        pltpu.VMEM((nc * 2 * H, Q), f32),
                        pltpu.VMEM((nc * 2 * H, Q), f32),
                        pltpu.VMEM((nc * 2 * H, Q), f32)],
        compiler_params=pltpu.CompilerParams(dimension_semantics=("arbitrary",),
                                             vmem_limit_bytes=_VMEM_LIMIT),
        name="ssd",
    )(pzx, pdt, dt_rows, conv_w, conv_b.reshape(1, XBC_W), a_rowv, neg_a.reshape(2 * H, 1),
      b_rowv, dt_bias.astype(f32).reshape(2 * H, 1), d_lane, norm_g.reshape(1, G), init)


def _mixer_ln(x, xc, mod, mod_c, ctx_out, alpha, w_in, w_out, ln_g, ln_b, conf, rpb, hy, ssd):
    Bsz = x.shape[0]
    sh, sc, g = mod
    shc, scc, gc = mod_c
    pa, pb, py, pzx, pdt = _in_proj(x, sc, sh, w_in)
    ca, cb, cy, czx, cdt = _in_proj(xc, scc, shc, w_in)
    zero = jnp.zeros((Bsz, 2, SSD_HEADS, SSD_STATE, SSD_HEAD_DIM), jnp.float32)
    y_dc, ctx_states = _ssd(czx, cdt, zero, *ssd)
    y_d, _ = _ssd(pzx, pdt, ctx_states, *ssd)
    ys = [_conformer(pa, *conf), _na_attention(pb, cb, rpb), _hyena(py, *hy), y_d]
    x_new = _out_proj_ln(ys, w_out, x, g, ln_g, ln_b, alpha)
    if not ctx_out:
        return x_new, None
    ycs = [_conformer(ca, *conf), _ctx_attention(cb), _hyena(cy, *hy), y_dc]
    return x_new, _out_proj_ln(ycs, w_out, xc, gc, ln_g, ln_b, alpha)


_NEG = -1e30
_ROUTE_TT = 256
_DENSE_TT = 1024
_DENSE_PARTS = 4
_DENSE_EC = 1024
_NT = (((1,), (1,)), ((), ()))


def _bf16_round(x):
    return x.astype(jnp.bfloat16).astype(jnp.float32)


def _oddeven_sort_pairs(n):
    pairs = []
    p = 1
    while p < n:
        k = p
        while k >= 1:
            for j in range(k % p, n - k, 2 * k):
                for i in range(min(k, n - j - k)):
                    if (i + j) // (2 * p) == (i + j + k) // (2 * p):
                        pairs.append((i + j, i + j + k))
            k //= 2
        p *= 2
    return pairs


_SORT16 = _oddeven_sort_pairs(PEER_TOPK)


def _order_pair(vs, i, j):
    a, b = vs[i], vs[j]
    if b is None:
        return
    if a is None:
        vs[i], vs[j] = b, None
        return
    vs[i], vs[j] = jnp.maximum(a, b), jnp.minimum(a, b)


def _top16_replicated(vs):
    vs = list(vs) + [None] * (PEER_TOPK - len(vs))
    for i, j in _SORT16:
        _order_pair(vs, i, j)
    for shift in (4, 2, 1):
        other = [None if v is None else pltpu.roll(v, shift, 0) for v in vs]
        merged = []
        for k in range(PEER_TOPK):
            a, b = vs[k], other[PEER_TOPK - 1 - k]
            merged.append(b if a is None else a if b is None else jnp.maximum(a, b))
        vs = merged
        stride = PEER_TOPK // 2
        while stride >= 1:
            for i in range(PEER_TOPK):
                if (i // stride) % 2 == 0:
                    _order_pair(vs, i, i + stride)
            stride //= 2
    return vs


def _pack_sublanes(blocks):
    sub = lax.broadcasted_iota(jnp.int32, blocks[0].shape, 0)
    out = blocks[0]
    for r in range(1, len(blocks)):
        out = jnp.where(sub == r, blocks[r], out)
    return out


def _pair_candidates(r1, lo1, hi1, r2, lo2, hi2, op, fill):
    keep = lax.broadcasted_iota(jnp.int32, lo1.shape, 0) >= 4
    pieces = [op(r1[0], lo2), op(r1[0], hi2), op(r1[1], lo2), op(r1[2], lo2), op(r1[3], lo2), op(r2[0], hi1)]
    for b in range(3):
        pieces.append(jnp.where(keep, op(r2[b], lo1), fill))
    return pieces


def _peer_route_body(x_ref, sc_ref, sh_ref, wqT_ref, keys_ref, hm_ref, e1_ref, e2_ref, pthr_ref, qT_ref):
    hm = (x_ref[0] * (1.0 + sc_ref[0]) + sh_ref[0]).astype(jnp.bfloat16)
    hm_ref[0] = hm
    qT_ref[...] = lax.dot_general(wqT_ref[...], hm, _NT, preferred_element_type=jnp.float32)
    nblk = PEER_KEYS // 8

    def head(h, carry):
        base = pl.multiple_of(h * PEER_QDIM, PEER_QDIM)
        half_q = PEER_QDIM // 2
        s_both = []
        for p in range(2):
            qb = qT_ref[pl.ds(base + p * half_q, half_q), :].astype(jnp.bfloat16)
            s_both.append(jnp.dot(keys_ref[h, p], qb, preferred_element_type=jnp.float32))
        for half in range(_ROUTE_TT // 128):
            lanes = slice(half * 128, (half + 1) * 128)
            s1 = s_both[0][:, lanes]
            s2 = s_both[1][:, lanes]
            r1 = _top16_replicated([s1[8 * k:8 * k + 8] for k in range(nblk)])
            r2 = _top16_replicated([s2[8 * k:8 * k + 8] for k in range(nblk)])
            lo1, hi1 = _pack_sublanes(r1[:8]), _pack_sublanes(r1[8:])
            lo2, hi2 = _pack_sublanes(r2[:8]), _pack_sublanes(r2[8:])
            cand = _pair_candidates(r1, lo1, hi1, r2, lo2, hi2, lambda a, b: a + b, _NEG)
            cv = _top16_replicated(cand)
            top, thr = cv[0], cv[PEER_TOPK - 1]
            z = jnp.ones_like(top)
            for k in range(1, PEER_TOPK):
                z = z + jnp.exp(cv[k] - top)
            rz = 1.0 / z
            f1 = lambda v: _bf16_round(jnp.exp(v - r1[0]) * rz)
            f2 = lambda v: _bf16_round(jnp.exp(v - r2[0]))
            prod = _pair_candidates([f1(v) for v in r1[:4]], f1(lo1), f1(hi1), [f2(v) for v in r2[:3]], f2(lo2), f2(hi2),
                                    lambda a, b: _bf16_round(a * b), 0.0)
            low = None
            for cpiece, ppiece in zip(cand, prod):
                sel = jnp.where(cpiece >= thr, ppiece, 1e30)
                low = sel if low is None else jnp.minimum(low, sel)
            e1_ref[h, :, lanes] = jnp.exp(s1 - r1[0][0:1]) * rz[0:1]
            e2_ref[h, :, lanes] = (jnp.exp(s2 - r2[0][0:1])).astype(jnp.bfloat16)
            pthr_ref[h, :, lanes] = jnp.min(low, axis=0, keepdims=True)
        return carry

    lax.fori_loop(0, PEER_HEADS, head, 0)


def _peer_route(x, sc, sh, wqT, keys):
    Bsz, S, D = x.shape
    assert S % _ROUTE_TT == 0, S
    nt = S // _ROUTE_TT
    T = Bsz * S
    tab = jax.ShapeDtypeStruct((PEER_HEADS, PEER_KEYS, T), jnp.float32)
    return pl.pallas_call(
        _peer_route_body,
        grid=(Bsz, nt),
        in_specs=[pl.BlockSpec((1, _ROUTE_TT, D), lambda b, i: (b, i, 0)),
                  pl.BlockSpec((1, 1, D), lambda b, i: (b, 0, 0)),
                  pl.BlockSpec((1, 1, D), lambda b, i: (b, 0, 0)),
                  pl.BlockSpec(wqT.shape, lambda b, i: (0, 0)),
                  pl.BlockSpec(keys.shape, lambda b, i: (0, 0, 0, 0))],
        out_specs=[pl.BlockSpec((1, _ROUTE_TT, D), lambda b, i: (b, i, 0)),
                   pl.BlockSpec((PEER_HEADS, PEER_KEYS, _ROUTE_TT), lambda b, i: (0, 0, b * nt + i)),
                   pl.BlockSpec((PEER_HEADS, PEER_KEYS, _ROUTE_TT), lambda b, i: (0, 0, b * nt + i)),
                   pl.BlockSpec((PEER_HEADS, 1, _ROUTE_TT), lambda b, i: (0, 0, b * nt + i))],
        out_shape=[jax.ShapeDtypeStruct((Bsz, S, D), jnp.bfloat16), tab,
                   jax.ShapeDtypeStruct(tab.shape, jnp.bfloat16),
                   jax.ShapeDtypeStruct((PEER_HEADS, 1, T), jnp.float32)],
        scratch_shapes=[pltpu.VMEM((PEER_HEADS * PEER_QDIM, _ROUTE_TT), jnp.float32)],
        compiler_params=pltpu.CompilerParams(dimension_semantics=("arbitrary", "arbitrary"),
                                             vmem_limit_bytes=_VMEM_LIMIT),
        name="peer_route",
    )(x, sc, sh, wqT, keys)


_GELU_K = math.sqrt(2.0 / math.pi)


def _gelu_tanh(x):
    half = 0.5 * x
    return half + half * jnp.tanh(x * (_GELU_K + (_GELU_K * 0.044715) * (x * x)))


_PACK = 16


def _peer_dense_body(alpha, hm_ref, e1_ref, e2_ref, pthr_ref, u_ref, vt_ref, x_ref, g_ref, lng_ref, lnb_ref,
                     o_ref, acc_ref, wt_ref, e1b_ref, pthrb_ref):
    c = pl.program_id(2)
    jrows = 32
    bf16 = jnp.bfloat16
    tp = _DENSE_TT // _DENSE_PARTS

    @pl.when(c == 0)
    def _():
        acc_ref[...] = jnp.zeros_like(acc_ref)
        for h in range(PEER_HEADS):
            pthrb_ref[h] = jnp.broadcast_to(pthr_ref[h], (_PACK, _DENSE_TT)).astype(bf16)

    parts = [slice(t * tp, (t + 1) * tp) for t in range(_DENSE_PARTS)]

    def scores(toks):
        return lax.dot_general(u_ref[...], hm_ref[0, toks, :], _NT, preferred_element_type=jnp.float32)

    def build(toks, act):
        for ii in range(_DENSE_EC // PEER_KEYS):
            for h in range(PEER_HEADS):
                e1b_ref[ii % 2, h] = jnp.broadcast_to(e1_ref[h, ii:ii + 1, toks], (_PACK, tp)).astype(bf16)
            for jb in range(PEER_KEYS // jrows):
                r0 = ii * PEER_KEYS + jb * jrows
                gate = jnp.zeros((jrows // _PACK, _PACK, tp), bf16)
                for h in range(PEER_HEADS):
                    e2 = e2_ref[h, jb * jrows:(jb + 1) * jrows, toks].reshape(jrows // _PACK, _PACK, tp)
                    val = e2 * e1b_ref[ii % 2, h][None]
                    gate = gate + jnp.where(val >= pthrb_ref[h, :, toks][None], val, jnp.zeros_like(val))
                gel = _gelu_tanh(act[r0:r0 + jrows, :]).astype(bf16)
                wt_ref[r0:r0 + jrows, toks] = gate.reshape(jrows, tp) * gel
        acc_ref[:, toks] += jnp.dot(vt_ref[...], wt_ref[:, toks], preferred_element_type=jnp.float32)

    acts = {t: scores(parts[t]) for t in range(min(2, _DENSE_PARTS))}
    for t in range(_DENSE_PARTS):
        build(parts[t], acts.pop(t))
        if t + 2 < _DENSE_PARTS:
            acts[t + 2] = scores(parts[t + 2])

    @pl.when(c == pl.num_programs(2) - 1)
    def _():
        y = alpha * x_ref[0] + g_ref[0] * acc_ref[...].T
        mu = jnp.mean(y, -1, keepdims=True)
        yc = y - mu
        var = jnp.mean(yc * yc, -1, keepdims=True)
        o_ref[0] = yc * lax.rsqrt(var + LN_EPS) * lng_ref[...] + lnb_ref[...]


def _peer_dense(hm, e1, e2, pthr, u_bf, vt_bf, x, g, ln_g, ln_b, alpha):
    Bsz, S, D = x.shape
    assert S % _DENSE_TT == 0, S
    nt = S // _DENSE_TT
    nchunk = N_EXPERTS // _DENSE_EC
    rows_i = _DENSE_EC // PEER_KEYS
    return pl.pallas_call(
        functools.partial(_peer_dense_body, alpha),
        grid=(Bsz, nt, nchunk),
        in_specs=[pl.BlockSpec((1, _DENSE_TT, D), lambda b, i, c: (b, i, 0)),
                  pl.BlockSpec((PEER_HEADS, rows_i, _DENSE_TT), lambda b, i, c: (0, c, b * nt + i)),
                  pl.BlockSpec((PEER_HEADS, PEER_KEYS, _DENSE_TT), lambda b, i, c: (0, 0, b * nt + i)),
                  pl.BlockSpec((PEER_HEADS, 1, _DENSE_TT), lambda b, i, c: (0, 0, b * nt + i)),
                  pl.BlockSpec((_DENSE_EC, D), lambda b, i, c: (c, 0)),
                  pl.BlockSpec((D, _DENSE_EC), lambda b, i, c: (0, c)),
                  pl.BlockSpec((1, _DENSE_TT, D), lambda b, i, c: (b, i, 0)),
                  pl.BlockSpec((1, 1, D), lambda b, i, c: (b, 0, 0)),
                  pl.BlockSpec((1, D), lambda b, i, c: (0, 0)),
                  pl.BlockSpec((1, D), lambda b, i, c: (0, 0))],
        out_specs=pl.BlockSpec((1, _DENSE_TT, D), lambda b, i, c: (b, i, 0)),
        out_shape=jax.ShapeDtypeStruct((Bsz, S, D), jnp.float32),
        scratch_shapes=[pltpu.VMEM((D, _DENSE_TT), jnp.float32),
                        pltpu.VMEM((_DENSE_EC, _DENSE_TT), jnp.bfloat16),
                        pltpu.VMEM((2, PEER_HEADS, _PACK, _DENSE_TT // _DENSE_PARTS), jnp.bfloat16),
                        pltpu.VMEM((PEER_HEADS, _PACK, _DENSE_TT), jnp.bfloat16)],
        compiler_params=pltpu.CompilerParams(dimension_semantics=("arbitrary", "arbitrary", "arbitrary"),
                                             vmem_limit_bytes=_VMEM_LIMIT),
        name="peer_dense",
    )(hm, e1, e2, pthr, u_bf, vt_bf, x, g, ln_g.reshape(1, D), ln_b.reshape(1, D))


def _peer_weights(wq, sub_keys, u_tab, v_tab):
    return (wq.T.astype(jnp.bfloat16), sub_keys.astype(jnp.bfloat16),
            u_tab.astype(jnp.bfloat16), v_tab.T.astype(jnp.bfloat16))


def _peer_ln(x, sc, sh, g, pw, ln_g, ln_b, alpha):
    wqT, keys, u_bf, vt_bf = pw
    hm, e1, e2, pthr = _peer_route(x, sc, sh, wqT, keys)
    return _peer_dense(hm, e1, e2, pthr, u_bf, vt_bf, x, g, ln_g, ln_b, alpha)


def kernel(x, c, ctx, c_ctx, w_ada, b_ada, w_in, w_out, ln1_g, ln1_b, ln2_g, ln2_b,
           conf_dw_w, conf_dw_b, conf_norm_g, conf_norm_b, na_rpb, hy_short_w, hy_short_b,
           hy_w1, hy_b1, hy_w2, hy_b2, hy_w3, hy_decay, hy_bias, ssd_conv_w, ssd_conv_b,
           ssd_a_log, ssd_dt_bias, ssd_d, ssd_norm_g, peer_wq, peer_keys, peer_u, peer_v):
    alpha = (2.0 * DEPTH) ** 0.25
    s_c = jax.nn.silu(c)
    s_cc = jax.nn.silu(c_ctx)
    xc = ctx
    Bsz, Lc, D = ctx.shape
    for l in range(DEPTH):
        ctx_out = l < DEPTH - 1
        mod = (s_c @ w_ada[l] + b_ada[l])[:, None, :]
        mod_c = jnp.broadcast_to((s_cc @ w_ada[l] + b_ada[l])[None, None, :], mod.shape)
        sh1, sc1, g1, sh2, sc2, g2 = jnp.split(mod, 6, -1)
        sh1c, sc1c, g1c, sh2c, sc2c, g2c = jnp.split(mod_c, 6, -1)
        x, xc = _mixer_ln(
            x, xc, (sh1, sc1, g1), (sh1c, sc1c, g1c), ctx_out, alpha, w_in[l], w_out[l], ln1_g[l], ln1_b[l],
            (conf_dw_w[l], conf_dw_b[l], conf_norm_g[l], conf_norm_b[l]), na_rpb[l],
            (hy_short_w[l], hy_short_b[l], hy_w1[l], hy_b1[l], hy_w2[l], hy_b2[l], hy_w3[l], hy_decay[l], hy_bias[l]),
            (ssd_conv_w[l], ssd_conv_b[l], ssd_a_log[l], ssd_dt_bias[l], ssd_d[l], ssd_norm_g[l]))
        pw = _peer_weights(peer_wq[l], peer_keys[l], peer_u[l], peer_v[l])
        x = _peer_ln(x, sc2, sh2, g2, pw, ln2_g[l], ln2_b[l], alpha)
        if ctx_out:
            xc = _peer_ln(xc.reshape(1, Bsz * Lc, D), sc2c[:1], sh2c[:1], g2c[:1], pw, ln2_g[l], ln2_b[l],
                          alpha).reshape(Bsz, Lc, D)
    return x
```

```python
import functools
import math
import jax, jax.numpy as jnp
from jax import lax
from jax.experimental import pallas as pl
from jax.experimental.pallas import tpu as pltpu

D_MODEL = 1024
DEPTH = 2

GRID_W = 64
N_MIXERS = 4
GROUP_W = D_MODEL // N_MIXERS
D_MIX = N_MIXERS * GROUP_W
LN_EPS = 1e-5
CONF_K = 31
CONF_GROUPS = 4
NA_HEADS = 4
HEAD_DIM = GROUP_W // NA_HEADS
NA_KH = 8
NA_KW = 16
ROPE_BASE = 10000.0
HY_SHORT = 3
HY_BANDS = 16
HY_EMB = 1 + 2 * HY_BANDS
HY_HIDDEN = 64
HY_SIN_FREQ = 1.0
SSD_HEADS = 4
SSD_HEAD_DIM = GROUP_W // SSD_HEADS
SSD_GROUPS = 2
SSD_STATE = 64
SSD_CONV = 3
SSD_CHUNK = 128
XBC_W = GROUP_W + 2 * SSD_GROUPS * SSD_STATE
IN_COLS = 2 * GROUP_W + 3 * GROUP_W + 3 * GROUP_W + GROUP_W + XBC_W + 2 * SSD_HEADS
PEER_HEADS = 8
PEER_KEYS = 128
PEER_TOPK = 16
PEER_QDIM = 256
N_EXPERTS = PEER_KEYS * PEER_KEYS

_LANES = 128
_SUBLANES = 8
_VMEM_LIMIT = 56 * 1024 * 1024
_IN_PAD = -(-IN_COLS // _LANES) * _LANES


_IN_SPLITS = ((0, 2 * GROUP_W), (2 * GROUP_W, 5 * GROUP_W), (5 * GROUP_W, 8 * GROUP_W),
              (8 * GROUP_W, 9 * GROUP_W + XBC_W), (9 * GROUP_W + XBC_W, _IN_PAD))
_PROJ_TM = 512


def _in_proj_body(x_ref, sc_ref, sh_ref, w_ref, *o_refs):
    h = (x_ref[0] * (1.0 + sc_ref[0]) + sh_ref[0]).astype(jnp.bfloat16)
    for (lo, hi), o_ref in zip(_IN_SPLITS, o_refs):
        o_ref[0] = jnp.dot(h, w_ref[:, lo:hi], preferred_element_type=jnp.float32)


def _in_proj(x, sc, sh, w_in):
    Bsz, L, D = x.shape
    tm = min(_PROJ_TM, L)
    w = jnp.pad(w_in, ((0, 0), (0, _IN_PAD - IN_COLS))).astype(jnp.bfloat16)
    return pl.pallas_call(
        _in_proj_body,
        grid=(Bsz, L // tm),
        in_specs=[pl.BlockSpec((1, tm, D), lambda b, i: (b, i, 0)),
                  pl.BlockSpec((1, 1, D), lambda b, i: (b, 0, 0)),
                  pl.BlockSpec((1, 1, D), lambda b, i: (b, 0, 0)),
                  pl.BlockSpec((D, _IN_PAD), lambda b, i: (0, 0))],
        out_specs=[pl.BlockSpec((1, tm, hi - lo), lambda b, i: (b, i, 0)) for lo, hi in _IN_SPLITS],
        out_shape=[jax.ShapeDtypeStruct((Bsz, L, hi - lo), jnp.float32) for lo, hi in _IN_SPLITS],
        compiler_params=pltpu.CompilerParams(dimension_semantics=("arbitrary", "arbitrary"),
                                             vmem_limit_bytes=_VMEM_LIMIT),
        name="in_proj",
    )(x, sc, sh, w)


def _out_proj_ln_body(alpha, ya_ref, yb_ref, yc_ref, yd_ref, w_ref, x_ref, g_ref, lng_ref, lnb_ref, o_ref):
    y = 0.0
    for m, y_ref in enumerate((ya_ref, yb_ref, yc_ref, yd_ref)):
        y = y + jnp.dot(y_ref[0].astype(jnp.bfloat16), w_ref[m * GROUP_W:(m + 1) * GROUP_W, :],
                        preferred_element_type=jnp.float32)
    r = alpha * x_ref[0] + g_ref[0] * y
    mu = jnp.mean(r, -1, keepdims=True)
    rc = r - mu
    var = jnp.mean(rc * rc, -1, keepdims=True)
    o_ref[0] = rc * lax.rsqrt(var + LN_EPS) * lng_ref[...] + lnb_ref[...]


def _out_proj_ln(ys, w_out, x, g, ln_g, ln_b, alpha):
    Bsz, L, D = x.shape
    tm = min(_PROJ_TM, L)
    mix = pl.BlockSpec((1, tm, GROUP_W), lambda b, i: (b, i, 0))
    return pl.pallas_call(
        functools.partial(_out_proj_ln_body, alpha),
        grid=(Bsz, L // tm),
        in_specs=[mix, mix, mix, mix,
                  pl.BlockSpec((D_MIX, D), lambda b, i: (0, 0)),
                  pl.BlockSpec((1, tm, D), lambda b, i: (b, i, 0)),
                  pl.BlockSpec((1, 1, D), lambda b, i: (b, 0, 0)),
                  pl.BlockSpec((1, D), lambda b, i: (0, 0)),
                  pl.BlockSpec((1, D), lambda b, i: (0, 0))],
        out_specs=pl.BlockSpec((1, tm, D), lambda b, i: (b, i, 0)),
        out_shape=jax.ShapeDtypeStruct((Bsz, L, D), jnp.float32),
        compiler_params=pltpu.CompilerParams(dimension_semantics=("arbitrary", "arbitrary"),
                                             vmem_limit_bytes=_VMEM_LIMIT),
        name="out_proj_ln",
    )(*ys, w_out.astype(jnp.bfloat16), x, g, ln_g.reshape(1, D), ln_b.reshape(1, D))


def _shifted_taps(win, first, n_taps, rows):
    n = win.shape[0]
    rolled = {0: win}
    taps = []
    for k in range(n_taps):
        r = (first + k) % _SUBLANES
        if r not in rolled:
            rolled[r] = pltpu.roll(win, n - r, 0)
        base = first + k - r
        taps.append(rolled[r][base:base + rows])
    return taps


_CONV_HALO = 16
_CONV_TILE = 256


def _group_mean_matrix(width, group):
    r = lax.broadcasted_iota(jnp.int32, (width, width), 0) // group
    c = lax.broadcasted_iota(jnp.int32, (width, width), 1) // group
    return jnp.where(r == c, 1.0 / group, 0.0).astype(jnp.float32)


def _conformer_body(L, p_ref, w_ref, b_ref, ng_ref, nb_ref, o_ref, u_ref):
    G = GROUP_W
    pad = (CONF_K - 1) // 2
    halo = jnp.zeros((_CONV_HALO, G), jnp.float32)
    u_ref[0:_CONV_HALO, :] = halo
    u_ref[_CONV_HALO + L:_CONV_HALO + L + _CONV_HALO, :] = halo
    u_ref[_CONV_HALO:_CONV_HALO + L, :] = p_ref[0, :, 0:G] * jax.nn.sigmoid(p_ref[0, :, G:2 * G])
    avg = _group_mean_matrix(G, G // CONF_GROUPS)
    tile = min(_CONV_TILE, L)
    for t in range(L // tile):
        win = u_ref[t * tile:t * tile + tile + 2 * _CONV_HALO, :]
        acc = jnp.zeros((tile, G), jnp.float32) + b_ref[...]
        for k, tap in enumerate(_shifted_taps(win, _CONV_HALO - pad, CONF_K, tile)):
            acc = acc + tap * w_ref[k:k + 1, :]
        mu = jnp.dot(acc, avg, preferred_element_type=jnp.float32, precision=lax.Precision.HIGHEST)
        cen = acc - mu
        var = jnp.dot(cen * cen, avg, preferred_element_type=jnp.float32, precision=lax.Precision.HIGHEST)
        un = cen * lax.rsqrt(var + LN_EPS) * ng_ref[...] + nb_ref[...]
        o_ref[0, t * tile:(t + 1) * tile, :] = un * jax.nn.sigmoid(un)


def _conformer(pa, dw_w, dw_b, n_g, n_b):
    Bsz, L, _ = pa.shape
    G = GROUP_W
    vec = pl.BlockSpec((1, G), lambda b: (0, 0))
    return pl.pallas_call(
        functools.partial(_conformer_body, L),
        grid=(Bsz,),
        in_specs=[pl.BlockSpec((1, L, 2 * G), lambda b: (b, 0, 0)),
                  pl.BlockSpec((CONF_K, G), lambda b: (0, 0)), vec, vec, vec],
        out_specs=pl.BlockSpec((1, L, G), lambda b: (b, 0, 0)),
        out_shape=jax.ShapeDtypeStruct((Bsz, L, G), jnp.float32),
        scratch_shapes=[pltpu.VMEM((L + 2 * _CONV_HALO, G), jnp.float32)],
        compiler_params=pltpu.CompilerParams(dimension_semantics=("arbitrary",),
                                             vmem_limit_bytes=_VMEM_LIMIT),
        name="conformer",
    )(pa, dw_w, dw_b.reshape(1, G), n_g.reshape(1, G), n_b.reshape(1, G))


def axial_rope(rows, head_dim):
    n_f = head_dim // 4
    inv = ROPE_BASE ** (-jnp.arange(n_f, dtype=jnp.float32) / n_f)
    t = jnp.arange(rows * GRID_W)
    r = (t // GRID_W).astype(jnp.float32)
    col = (t % GRID_W).astype(jnp.float32)
    ang = jnp.concatenate([r[:, None] * inv, col[:, None] * inv], -1)
    return jnp.cos(ang), jnp.sin(ang)


_NA_MASK = -1e30


def _na_tables(rows, rpb):
    cos, sin = axial_rope(rows, HEAD_DIM)
    cos_f = jnp.tile(cos, (1, 2 * NA_HEADS))
    sin_s = jnp.tile(jnp.concatenate([-sin, sin], -1), (1, NA_HEADS))
    cq = jnp.arange(GRID_W)
    cs = jnp.clip(cq - NA_KW // 2, 0, GRID_W - NA_KW)
    col = jnp.arange(GRID_W)
    in_band = (col[None, :] >= cs[:, None]) & (col[None, :] < cs[:, None] + NA_KW)
    cb_idx = jnp.clip(col[None, :] - cq[:, None] + (NA_KW - 1), 0, 2 * NA_KW - 2)
    po = jnp.arange(NA_KH)
    rb_idx = jnp.arange(NA_KH)[None, :] - po[:, None] + (NA_KH - 1)
    bias = rpb[:, rb_idx][:, :, :, cb_idx]
    bias = jnp.where(in_band[None, None, None], bias, _NA_MASK)
    bias = bias.transpose(1, 0, 3, 2, 4).reshape(NA_KH, NA_HEADS, GRID_W, NA_KH * GRID_W)
    return cos_f, sin_s, bias


def _rope_lanes(x, cos_f, sin_s):
    n = x.shape[-1]
    hd = HEAD_DIM // 2
    first = (lax.broadcasted_iota(jnp.int32, x.shape, 1) % HEAD_DIM) < hd
    partner = jnp.where(first, pltpu.roll(x, n - hd, 1), pltpu.roll(x, hd, 1))
    return x * cos_f + partner * sin_s


def _softmax_pv(s_parts, v_parts):
    m = s_parts[0].max(-1, keepdims=True)
    for s in s_parts[1:]:
        m = jnp.maximum(m, s.max(-1, keepdims=True))
    l = 0.0
    o = 0.0
    for s, v in zip(s_parts, v_parts):
        p = jnp.exp(s - m)
        l = l + p.sum(-1, keepdims=True)
        o = o + jnp.dot(p.astype(jnp.bfloat16), v, preferred_element_type=jnp.float32)
    return o / l


_NA_ROWS = 4


def _na_body(rows, q_ref, k_ref, v_ref, kc_ref, vc_ref, cosq_ref, sinq_ref, cosk_ref, sink_ref, *rest):
    bias_refs, (o_ref, krot_ref, vbf_ref) = rest[:_NA_ROWS], rest[_NA_ROWS:]
    step = pl.program_id(1)
    scale = HEAD_DIM ** -0.5

    @pl.when(step == 0)
    def _():
        krot_ref[...] = _rope_lanes(k_ref[0], cosk_ref[...], sink_ref[...]).astype(jnp.bfloat16)
        vbf_ref[...] = v_ref[0].astype(jnp.bfloat16)

    win = NA_KH * GRID_W
    q = q_ref[0] * scale
    q_rot = _rope_lanes(q, cosq_ref[...], sinq_ref[...]).astype(jnp.bfloat16)
    q_plain = q.astype(jnp.bfloat16)
    kc = kc_ref[0].astype(jnp.bfloat16)
    vc = vc_ref[0].astype(jnp.bfloat16)
    for i in range(_NA_ROWS):
        r = step * _NA_ROWS + i
        rs = jnp.clip(r - NA_KH // 2, 0, rows - NA_KH)
        start = pl.multiple_of(rs * GRID_W, GRID_W)
        kw = krot_ref[pl.ds(start, win), :]
        vw = vbf_ref[pl.ds(start, win), :]
        qs = slice(i * GRID_W, (i + 1) * GRID_W)
        outs = []
        for h in range(NA_HEADS):
            hs = slice(h * HEAD_DIM, (h + 1) * HEAD_DIM)
            s_loc = lax.dot_general(q_rot[qs, hs], kw[:, hs], _NT, preferred_element_type=jnp.float32) + bias_refs[i][0, h]
            s_ctx = lax.dot_general(q_plain[qs, hs], kc[:, hs], _NT, preferred_element_type=jnp.float32)
            outs.append(_softmax_pv([s_loc, s_ctx], [vw[:, hs], vc[:, hs]]))
        o_ref[0, qs, :] = jnp.concatenate(outs, axis=-1)


def _na_attention(pb, cb, rpb):
    Bsz, S, _ = pb.shape
    Lc = cb.shape[1]
    rows = S // GRID_W
    assert rows % _NA_ROWS == 0, rows
    G = GROUP_W
    nq = _NA_ROWS * GRID_W
    cos_f, sin_s, bias = _na_tables(rows, rpb)

    def bias_spec(i):
        def idx(b, step):
            r = step * _NA_ROWS + i
            return (r - jnp.clip(r - NA_KH // 2, 0, rows - NA_KH), 0, 0, 0)
        return pl.BlockSpec((1, NA_HEADS, GRID_W, NA_KH * GRID_W), idx)

    return pl.pallas_call(
        functools.partial(_na_body, rows),
        grid=(Bsz, rows // _NA_ROWS),
        in_specs=[pl.BlockSpec((1, nq, G), lambda b, r: (b, r, 0)),
                  pl.BlockSpec((1, S, G), lambda b, r: (b, 0, 1)),
                  pl.BlockSpec((1, S, G), lambda b, r: (b, 0, 2)),
                  pl.BlockSpec((1, Lc, G), lambda b, r: (b, 0, 1)),
                  pl.BlockSpec((1, Lc, G), lambda b, r: (b, 0, 2)),
                  pl.BlockSpec((nq, G), lambda b, r: (r, 0)),
                  pl.BlockSpec((nq, G), lambda b, r: (r, 0)),
                  pl.BlockSpec((S, G), lambda b, r: (0, 0)),
                  pl.BlockSpec((S, G), lambda b, r: (0, 0))] + [bias_spec(i) for i in range(_NA_ROWS)],
        out_specs=pl.BlockSpec((1, nq, G), lambda b, r: (b, r, 0)),
        out_shape=jax.ShapeDtypeStruct((Bsz, S, G), jnp.float32),
        scratch_shapes=[pltpu.VMEM((S, G), jnp.bfloat16), pltpu.VMEM((S, G), jnp.bfloat16)],
        compiler_params=pltpu.CompilerParams(dimension_semantics=("arbitrary", "arbitrary"),
                                             vmem_limit_bytes=_VMEM_LIMIT),
        name="na_attention",
    )(pb, pb, pb, cb, cb, cos_f, sin_s, cos_f, sin_s, *([bias] * _NA_ROWS))


def _ctx_attn_body(q_ref, k_ref, v_ref, o_ref):
    q = (q_ref[0] * HEAD_DIM ** -0.5).astype(jnp.bfloat16)
    k = k_ref[0].astype(jnp.bfloat16)
    v = v_ref[0].astype(jnp.bfloat16)
    outs = []
    for h in range(NA_HEADS):
        hs = slice(h * HEAD_DIM, (h + 1) * HEAD_DIM)
        s = lax.dot_general(q[:, hs], k[:, hs], _NT, preferred_element_type=jnp.float32)
        outs.append(_softmax_pv([s], [v[:, hs]]))
    o_ref[0] = jnp.concatenate(outs, axis=-1)


def _ctx_attention(cb):
    Bsz, Lc, _ = cb.shape
    G = GROUP_W
    return pl.pallas_call(
        _ctx_attn_body,
        grid=(Bsz,),
        in_specs=[pl.BlockSpec((1, Lc, G), lambda b: (b, 0, 0)),
                  pl.BlockSpec((1, Lc, G), lambda b: (b, 0, 1)),
                  pl.BlockSpec((1, Lc, G), lambda b: (b, 0, 2))],
        out_specs=pl.BlockSpec((1, Lc, G), lambda b: (b, 0, 0)),
        out_shape=jax.ShapeDtypeStruct((Bsz, Lc, G), jnp.float32),
        compiler_params=pltpu.CompilerParams(dimension_semantics=("arbitrary",)),
        name="ctx_attention",
    )(cb, cb, cb)


_HY_TB = 256
_HY_CB = 8


def _hyena_filter_body(L, w1_ref, b1_ref, w2_ref, b2_ref, w3_ref, dec_ref, o_ref):
    G = GROUP_W
    hp = lax.Precision.HIGHEST
    p = lax.broadcasted_iota(jnp.int32, (2 * L, _LANES), 0)
    lane = lax.broadcasted_iota(jnp.int32, (2 * L, _LANES), 1)
    tn = jnp.abs(p - L).astype(jnp.float32) / L
    band = ((lane - 1) % HY_BANDS + 1).astype(jnp.float32)
    ang = 2.0 * math.pi * band * tn
    z = jnp.where(lane == 0, tn, jnp.where(lane <= HY_BANDS, jnp.sin(ang), jnp.cos(ang)))
    z = jnp.where(lane < HY_EMB, z, 0.0)
    h = jnp.sin(HY_SIN_FREQ * (jnp.dot(z, w1_ref[...], precision=hp, preferred_element_type=jnp.float32) + b1_ref[...]))
    h = jnp.sin(HY_SIN_FREQ * (jnp.dot(h, w2_ref[...], precision=hp, preferred_element_type=jnp.float32) + b2_ref[...]))
    k = jnp.dot(h, w3_ref[...], precision=hp, preferred_element_type=jnp.float32) * jnp.exp(-tn[:, 0:1] * dec_ref[...])
    kf, kb = k[:, 0:G], k[:, G:2 * G]
    n = lax.broadcasted_iota(jnp.int32, (2 * L, G), 0) - L
    nf = jnp.sum(jnp.where(n >= 0, jnp.abs(kf), 0.0), axis=0, keepdims=True) + 1e-6
    nb = jnp.sum(jnp.where(n <= 0, jnp.where(n > -L, jnp.abs(kb), 0.0), 0.0), axis=0, keepdims=True) + 1e-6
    rev = jnp.where(n > 0, kb / nb, jnp.where(n > -L, kf / nf, 0.0))
    o_ref[...] = rev.T


def _hyena_filters_rev(L, w1, b1, w2, b2, w3, decay):
    G = GROUP_W
    hp = _LANES - HY_HIDDEN
    w1p = jnp.pad(w1.astype(jnp.float32), ((0, _LANES - HY_EMB), (0, hp)))
    args = (w1p, jnp.pad(b1.reshape(1, -1), ((0, 0), (0, hp))), jnp.pad(w2, ((0, hp), (0, hp))),
            jnp.pad(b2.reshape(1, -1), ((0, 0), (0, hp))), jnp.pad(w3, ((0, hp), (0, 0))), decay.reshape(1, -1))
    return pl.pallas_call(
        functools.partial(_hyena_filter_body, L),
        out_shape=jax.ShapeDtypeStruct((G, 2 * L), jnp.float32),
        compiler_params=pltpu.CompilerParams(vmem_limit_bytes=_VMEM_LIMIT),
        name="hyena_filters",
    )(*[a.astype(jnp.float32) for a in args])


def _hyena_pre_body(L, p_ref, w_ref, b_ref, u_ref, ut_ref, x0_ref, xp_ref):
    G = GROUP_W
    W = 3 * G
    halo = jnp.zeros((_CONV_HALO, W), jnp.float32)
    xp_ref[0:_CONV_HALO, :] = halo
    xp_ref[_CONV_HALO + L:_CONV_HALO + L + _CONV_HALO, :] = halo
    xp_ref[_CONV_HALO:_CONV_HALO + L, :] = p_ref[0]
    pad = (HY_SHORT - 1) // 2
    tile = min(_CONV_TILE, L)
    for t in range(L // tile):
        win = xp_ref[t * tile:t * tile + tile + 2 * _CONV_HALO, :]
        acc = jnp.zeros((tile, W), jnp.float32) + b_ref[...]
        for k, tap in enumerate(_shifted_taps(win, _CONV_HALO - pad, HY_SHORT, tile)):
            acc = acc + tap * w_ref[k:k + 1, :]
        rows = slice(t * tile, (t + 1) * tile)
        u = acc[:, 2 * G:3 * G] * acc[:, G:2 * G]
        x0_ref[0, rows, :] = acc[:, 0:G]
        u_ref[0, rows, :] = u
        ut_ref[0, :, rows] = u.T


def _hyena_pre(py, short_w, short_b):
    Bsz, L, W = py.shape
    G = GROUP_W
    f32 = jnp.float32
    return pl.pallas_call(
        functools.partial(_hyena_pre_body, L),
        grid=(Bsz,),
        in_specs=[pl.BlockSpec((1, L, W), lambda b: (b, 0, 0)),
                  pl.BlockSpec((HY_SHORT, W), lambda b: (0, 0)),
                  pl.BlockSpec((1, W), lambda b: (0, 0))],
        out_specs=[pl.BlockSpec((1, L, G), lambda b: (b, 0, 0)),
                   pl.BlockSpec((1, G, L), lambda b: (b, 0, 0)),
                   pl.BlockSpec((1, L, G), lambda b: (b, 0, 0))],
        out_shape=[jax.ShapeDtypeStruct((Bsz, L, G), f32), jax.ShapeDtypeStruct((Bsz, G, L), f32),
                   jax.ShapeDtypeStruct((Bsz, L, G), f32)],
        scratch_shapes=[pltpu.VMEM((L + 2 * _CONV_HALO, W), f32)],
        compiler_params=pltpu.CompilerParams(dimension_semantics=("arbitrary",), vmem_limit_bytes=_VMEM_LIMIT),
        name="hyena_pre",
    )(py, short_w, short_b.reshape(1, W))


def _toeplitz_tile(rolled, q0):
    TB = _HY_TB
    per = _LANES // _SUBLANES
    i = lax.broadcasted_iota(jnp.int32, (_SUBLANES, _LANES), 0)
    l = lax.broadcasted_iota(jnp.int32, (_SUBLANES, _LANES), 1)
    row_blocks = []
    for rg in range(TB // _SUBLANES):
        k = rg % per
        pieces = []
        for lg in range(TB // _LANES):
            o = TB + _LANES * lg - _SUBLANES * rg
            q, rho = divmod(o, _LANES)
            if rho == 0:
                pieces.append(jnp.where(l - i < 0, rolled(q0 + q - 1, k), rolled(q0 + q, k)))
            else:
                pieces.append(jnp.where(l + rho - i >= _LANES, rolled(q0 + q + 1, k), rolled(q0 + q, k)))
        row_blocks.append(jnp.concatenate(pieces, axis=1))
    return jnp.concatenate(row_blocks, axis=0).astype(jnp.bfloat16)


def _hyena_conv_body(L, Bsz, g_ref, u_ref, o_ref):
    TB = min(_HY_TB, L)
    nb = L // TB
    cols = Bsz * nb
    lane = lax.broadcasted_iota(jnp.int32, (TB, cols), 1) % nb

    def channel(ci, carry):
        u = u_ref[ci] if nb == 1 else u_ref[:, ci].reshape(cols, TB)
        u = u.astype(jnp.bfloat16)
        acc = jnp.zeros((TB, cols), jnp.float32)
        cache = {}
        g_row = g_ref[pl.ds(ci, 1), :]

        def rolled(q, k):
            if (q, k) not in cache:
                blk = jnp.broadcast_to(g_row[:, q * _LANES:(q + 1) * _LANES], (_SUBLANES, _LANES))
                cache[q, k] = pltpu.roll(blk, (_SUBLANES * k) % _LANES, 1, stride=1, stride_axis=0)
            return cache[q, k]

        for d in range(-(nb - 1), nb):
            start = L - TB * d - TB
            tile = _toeplitz_tile(rolled, start // _LANES)
            z = lax.dot_general(tile, u, _NT, preferred_element_type=jnp.float32)
            if d != 0:
                z = jnp.where((lane - d >= 0) & (lane - d < nb), pltpu.roll(z, d % cols, 1), 0.0)
            acc = acc + z
        if nb == 1:
            o_ref[ci] = acc.T
        else:
            o_ref[:, ci] = acc.T.reshape(Bsz, nb, TB)
        return carry

    lax.fori_loop(0, _HY_CB, channel, 0)


def _hyena_conv(g_rev, u_t):
    Bsz, G, L = u_t.shape
    TB = min(_HY_TB, L)
    nb = L // TB
    if nb == 1:
        blk = pl.BlockSpec((_HY_CB, Bsz, TB), lambda c: (c, 0, 0))
        operand, out_shape = jnp.swapaxes(u_t, 0, 1), (G, Bsz, TB)
    else:
        blk = pl.BlockSpec((Bsz, _HY_CB, nb, TB), lambda c: (0, c, 0, 0))
        operand, out_shape = u_t.reshape(Bsz, G, nb, TB), (Bsz, G, nb, TB)
    out = pl.pallas_call(
        functools.partial(_hyena_conv_body, L, Bsz),
        grid=(G // _HY_CB,),
        in_specs=[pl.BlockSpec((_HY_CB, 2 * L), lambda c: (c, 0)), blk],
        out_specs=blk,
        out_shape=jax.ShapeDtypeStruct(out_shape, jnp.float32),
        compiler_params=pltpu.CompilerParams(dimension_semantics=("arbitrary",), vmem_limit_bytes=_VMEM_LIMIT),
        name="hyena_conv",
    )(g_rev, operand)
    return jnp.swapaxes(out, 0, 1) if nb == 1 else out.reshape(Bsz, G, L)


def _hyena_post_body(yt_ref, u_ref, x0_ref, skip_ref, o_ref):
    o_ref[0] = (yt_ref[0].T + u_ref[0] * skip_ref[...]) * x0_ref[0]


def _hyena_post(y_t, u, x0, skip):
    Bsz, L, G = u.shape
    tok = pl.BlockSpec((1, L, G), lambda b: (b, 0, 0))
    return pl.pallas_call(
        _hyena_post_body,
        grid=(Bsz,),
        in_specs=[pl.BlockSpec((1, G, L), lambda b: (b, 0, 0)), tok, tok, pl.BlockSpec((1, G), lambda b: (0, 0))],
        out_specs=tok,
        out_shape=jax.ShapeDtypeStruct((Bsz, L, G), jnp.float32),
        compiler_params=pltpu.CompilerParams(dimension_semantics=("arbitrary",), vmem_limit_bytes=_VMEM_LIMIT),
        name="hyena_post",
    )(y_t, u, x0, skip.reshape(1, G))


def _hyena(py, short_w, short_b, w1, b1, w2, b2, w3, decay, skip):
    L = py.shape[1]
    g_rev = _hyena_filters_rev(L, w1, b1, w2, b2, w3, decay)
    u, u_t, x0 = _hyena_pre(py, short_w, short_b)
    return _hyena_post(_hyena_conv(g_rev, u_t), u, x0, skip)


def _split3_dot(a, b_bf16, dims=None):
    hi = a.astype(jnp.bfloat16)
    r1 = a - hi.astype(jnp.float32)
    mid = r1.astype(jnp.bfloat16)
    lo = (r1 - mid.astype(jnp.float32)).astype(jnp.bfloat16)
    out = 0.0
    for part in (hi, mid, lo):
        if dims is None:
            out = out + jnp.dot(part, b_bf16, preferred_element_type=jnp.float32)
        else:
            out = out + jnp.dot(b_bf16, part, preferred_element_type=jnp.float32)
    return out


def _softplus(x):
    return jnp.maximum(x, 0.0) + jnp.log(1.0 + jnp.exp(-jnp.abs(x)))


def _ssd_body(L, zx_ref, dtc_ref, dtr_ref, cw_ref, cb_ref, arow_ref, acol_ref, brow_ref, bcol_ref, dsk_ref,
              ng_ref, init_ref, o_ref, fin_ref, xp_ref, xc_ref, bt_ref, y_ref, ccol_ref, crow_ref, edec_ref,
              tot_ref):
    G = GROUP_W
    Q = SSD_CHUNK
    nc = L // Q
    P = SSD_HEAD_DIM
    N = SSD_STATE
    H = SSD_HEADS
    f32 = jnp.float32
    bf16 = jnp.bfloat16
    halo = jnp.zeros((_CONV_HALO, XBC_W), f32)
    xp_ref[0:_CONV_HALO, :] = halo
    xp_ref[_CONV_HALO + L:_CONV_HALO + L + _CONV_HALO, :] = halo
    xp_ref[_CONV_HALO:_CONV_HALO + L, :] = zx_ref[0, :, G:G + XBC_W]
    pad = (SSD_CONV - 1) // 2
    tile = min(_CONV_TILE, L)
    for t in range(L // tile):
        win = xp_ref[t * tile:t * tile + tile + 2 * _CONV_HALO, :]
        acc = jnp.zeros((tile, XBC_W), f32) + cb_ref[...]
        for k, tap in enumerate(_shifted_taps(win, _CONV_HALO - pad, SSD_CONV, tile)):
            acc = acc + tap * cw_ref[k:k + 1, :]
        xc_ref[t * tile:(t + 1) * tile, :] = acc * jax.nn.sigmoid(acc)
    for c in range(nc):
        bt_ref[c] = xc_ref[c * Q:(c + 1) * Q, G:G + SSD_GROUPS * N].T
    dt_col = _softplus(dtc_ref[0] + brow_ref[...])
    a_col = dt_col * arow_ref[...]
    a_row = _softplus(dtr_ref[0] + bcol_ref[...]) * acol_ref[...]
    a_stack = jnp.concatenate([a_row[:, c * Q:(c + 1) * Q] for c in range(nc)], axis=0)
    ri = lax.broadcasted_iota(jnp.int32, (Q, Q), 0)
    ci = lax.broadcasted_iota(jnp.int32, (Q, Q), 1)
    one = lambda m: jnp.where(m, 1.0, 0.0).astype(bf16)
    tot_ref[...] = _split3_dot(a_stack, jnp.ones((Q, Q), bf16))

    def direction(d, y_store):
        fwd = d == 0
        m_col = one(ci <= ri) if fwd else one(ci >= ri)
        for c in range(nc):
            ccol_ref[c * Q:(c + 1) * Q, :] = _split3_dot(a_col[c * Q:(c + 1) * Q, :], m_col, dims="left")
        crow_ref[...] = _split3_dot(a_stack, one(ri <= ci) if fwd else one(ri >= ci))
        edec_ref[...] = _split3_dot(a_stack, one(ri > ci) if fwd else one(ri < ci))
        keep = (ri >= ci) if fwd else (ri <= ci)

        def chunk(step, states):
            c = step if fwd else nc - 1 - step
            r0 = pl.multiple_of(c * Q, Q)
            j0 = pl.multiple_of(c * 2 * H, 2 * H)
            xc = xc_ref[pl.ds(r0, Q), :]
            ccol = ccol_ref[pl.ds(r0, Q), :]
            crow = crow_ref[pl.ds(j0, 2 * H), :]
            edec = edec_ref[pl.ds(j0, 2 * H), :]
            tot = tot_ref[pl.ds(j0, 2 * H), :]
            dtc = _softplus(dtc_ref[0, pl.ds(r0, Q), :] + brow_ref[...])
            bt = bt_ref[c]
            new_states = []
            outs = []
            for g in range(SSD_GROUPS):
                cm = xc[:, G + SSD_GROUPS * N + g * N:G + SSD_GROUPS * N + (g + 1) * N].astype(bf16)
                bm = xc[:, G + g * N:G + (g + 1) * N].astype(bf16)
                cb = lax.dot_general(cm, bm, _NT, preferred_element_type=f32)
                for hh in range(H // SSD_GROUPS):
                    h = g * (H // SSD_GROUPS) + hh
                    j = d * H + h
                    col = jnp.broadcast_to(ccol[:, j:j + 1], (Q, Q))
                    lmat = jnp.exp(jnp.where(keep, col - crow[j:j + 1, :], _NEG))
                    xd = (xc[:, h * P:(h + 1) * P] * jnp.broadcast_to(dtc[:, j:j + 1], (Q, P))).astype(bf16)
                    st = states[h]
                    y = jnp.dot((cb * lmat).astype(bf16), xd, preferred_element_type=f32)
                    y = y + jnp.dot(cm, st.astype(bf16), preferred_element_type=f32) * jnp.exp(col[:, 0:P])
                    outs.append(y)
                    btd = (bt[g * N:(g + 1) * N, :] * jnp.exp(edec[j:j + 1, :])).astype(bf16)
                    new_states.append(jnp.exp(tot[j:j + 1, 0:P]) * st
                                      + jnp.dot(btd, xd, preferred_element_type=f32))
            y_store(r0, jnp.concatenate(outs, axis=-1))
            return tuple(new_states)

        init = tuple(init_ref[0, d, h] for h in range(H))
        final = lax.fori_loop(0, nc, chunk, init, unroll=min(4, nc))
        for h in range(H):
            fin_ref[0, d, h] = final[h]

    def store_fwd(r0, y):
        y_ref[pl.ds(r0, Q), :] = y

    def store_bwd(r0, y):
        y_ref[pl.ds(r0, Q), :] += y

    direction(0, store_fwd)
    direction(1, store_bwd)
    gw = G // SSD_GROUPS
    for t in range(L // tile):
        rows = slice(t * tile, (t + 1) * tile)
        z = zx_ref[0, rows, 0:G]
        yg = (y_ref[rows, :] + xc_ref[rows, 0:G] * dsk_ref[...]) * (z * jax.nn.sigmoid(z))
        parts = []
        for g in range(SSD_GROUPS):
            v = yg[:, g * gw:(g + 1) * gw]
            parts.append(v * lax.rsqrt(jnp.mean(v * v, -1, keepdims=True) + LN_EPS))
        o_ref[0, rows, :] = jnp.concatenate(parts, axis=-1) * ng_ref[...]


def _ssd(pzx, pdt, init, conv_w, conv_b, a_log, dt_bias, d_skip, norm_g):
    Bsz, L, _ = pzx.shape
    G, H, Q = GROUP_W, SSD_HEADS, SSD_CHUNK
    lanes = pdt.shape[-1]
    nc = L // Q
    neg_a = -jnp.exp(a_log.astype(jnp.float32)).reshape(1, 2 * H)
    a_rowv = jnp.pad(neg_a, ((0, 0), (0, lanes - 2 * H)))
    b_rowv = jnp.pad(dt_bias.astype(jnp.float32).reshape(1, 2 * H), ((0, 0), (0, lanes - 2 * H)))
    dt_rows = jnp.swapaxes(pdt[:, :, :2 * H], 1, 2)
    d_lane = jnp.repeat(d_skip.astype(jnp.float32), SSD_HEAD_DIM).reshape(1, G)
    const = lambda shape: pl.BlockSpec(shape, lambda b: (0,) * len(shape))
    st_spec = pl.BlockSpec((1, 2, H, SSD_STATE, SSD_HEAD_DIM), lambda b: (b, 0, 0, 0, 0))
    f32 = jnp.float32
    return pl.pallas_call(
        functools.partial(_ssd_body, L),
        grid=(Bsz,),
        in_specs=[pl.BlockSpec((1, L, G + XBC_W), lambda b: (b, 0, 0)),
                  pl.BlockSpec((1, L, lanes), lambda b: (b, 0, 0)),
                  pl.BlockSpec((1, 2 * H, L), lambda b: (b, 0, 0)),
                  const((SSD_CONV, XBC_W)), const((1, XBC_W)),
                  const((1, lanes)), const((2 * H, 1)), const((1, lanes)), const((2 * H, 1)),
                  const((1, G)), const((1, G)), st_spec],
        out_specs=[pl.BlockSpec((1, L, G), lambda b: (b, 0, 0)), st_spec],
        out_shape=[jax.ShapeDtypeStruct((Bsz, L, G), f32),
                   jax.ShapeDtypeStruct((Bsz, 2, H, SSD_STATE, SSD_HEAD_DIM), f32)],
        scratch_shapes=[pltpu.VMEM((L + 2 * _CONV_HALO, XBC_W), f32),
                        pltpu.VMEM((L, XBC_W), f32),
                        pltpu.VMEM((nc, SSD_GROUPS * SSD_STATE, Q), f32),
                        pltpu.VMEM((L, G), f32),
                        pltpu.VMEM((L, lanes), f32),
                        pltpu.VMEM((nc * 2 * H, Q), f32),
                        pltpu.VMEM((nc * 2 * H, Q), f32),
                        pltpu.VMEM((nc * 2 * H, Q), f32)],
        compiler_params=pltpu.CompilerParams(dimension_semantics=("arbitrary",),
                                             vmem_limit_bytes=_VMEM_LIMIT),
        name="ssd",
    )(pzx, pdt, dt_rows, conv_w, conv_b.reshape(1, XBC_W), a_rowv, neg_a.reshape(2 * H, 1),
      b_rowv, dt_bias.astype(f32).reshape(2 * H, 1), d_lane, norm_g.reshape(1, G), init)


def _mixer_ln(x, xc, mod, mod_c, ctx_out, alpha, w_in, w_out, ln_g, ln_b, conf, rpb, hy, ssd):
    Bsz = x.shape[0]
    sh, sc, g = mod
    shc, scc, gc = mod_c
    pa, pb, py, pzx, pdt = _in_proj(x, sc, sh, w_in)
    ca, cb, cy, czx, cdt = _in_proj(xc, scc, shc, w_in)
    zero = jnp.zeros((Bsz, 2, SSD_HEADS, SSD_STATE, SSD_HEAD_DIM), jnp.float32)
    y_dc, ctx_states = _ssd(czx, cdt, zero, *ssd)
    y_d, _ = _ssd(pzx, pdt, ctx_states, *ssd)
    ys = [_conformer(pa, *conf), _na_attention(pb, cb, rpb), _hyena(py, *hy), y_d]
    x_new = _out_proj_ln(ys, w_out, x, g, ln_g, ln_b, alpha)
    if not ctx_out:
        return x_new, None
    ycs = [_conformer(ca, *conf), _ctx_attention(cb), _hyena(cy, *hy), y_dc]
    return x_new, _out_proj_ln(ycs, w_out, xc, gc, ln_g, ln_b, alpha)


_NEG = -1e30
_ROUTE_TT = 256
_DENSE_TT = 1024
_DENSE_PARTS = 4
_DENSE_EC = 1024
_NT = (((1,), (1,)), ((), ()))


def _bf16_round(x):
    return x.astype(jnp.bfloat16).astype(jnp.float32)


def _oddeven_sort_pairs(n):
    pairs = []
    p = 1
    while p < n:
        k = p
        while k >= 1:
            for j in range(k % p, n - k, 2 * k):
                for i in range(min(k, n - j - k)):
                    if (i + j) // (2 * p) == (i + j + k) // (2 * p):
                        pairs.append((i + j, i + j + k))
            k //= 2
        p *= 2
    return pairs


_SORT16 = _oddeven_sort_pairs(PEER_TOPK)


def _order_pair(vs, i, j):
    a, b = vs[i], vs[j]
    if b is None:
        return
    if a is None:
        vs[i], vs[j] = b, None
        return
    vs[i], vs[j] = jnp.maximum(a, b), jnp.minimum(a, b)


def _top16_replicated(vs):
    vs = list(vs) + [None] * (PEER_TOPK - len(vs))
    for i, j in _SORT16:
        _order_pair(vs, i, j)
    for shift in (4, 2, 1):
        other = [None if v is None else pltpu.roll(v, shift, 0) for v in vs]
        merged = []
        for k in range(PEER_TOPK):
            a, b = vs[k], other[PEER_TOPK - 1 - k]
            merged.append(b if a is None else a if b is None else jnp.maximum(a, b))
        vs = merged
        stride = PEER_TOPK // 2
        while stride >= 1:
            for i in range(PEER_TOPK):
                if (i // stride) % 2 == 0:
                    _order_pair(vs, i, i + stride)
            stride //= 2
    return vs


def _pack_sublanes(blocks):
    sub = lax.broadcasted_iota(jnp.int32, blocks[0].shape, 0)
    out = blocks[0]
    for r in range(1, len(blocks)):
        out = jnp.where(sub == r, blocks[r], out)
    return out


def _pair_candidates(r1, lo1, hi1, r2, lo2, hi2, op, fill):
    keep = lax.broadcasted_iota(jnp.int32, lo1.shape, 0) >= 4
    pieces = [op(r1[0], lo2), op(r1[0], hi2), op(r1[1], lo2), op(r1[2], lo2), op(r1[3], lo2), op(r2[0], hi1)]
    for b in range(3):
        pieces.append(jnp.where(keep, op(r2[b], lo1), fill))
    return pieces


def _peer_route_body(x_ref, sc_ref, sh_ref, wqT_ref, keys_ref, hm_ref, e1_ref, e2_ref, pthr_ref, qT_ref):
    hm = (x_ref[0] * (1.0 + sc_ref[0]) + sh_ref[0]).astype(jnp.bfloat16)
    hm_ref[0] = hm
    qT_ref[...] = lax.dot_general(wqT_ref[...], hm, _NT, preferred_element_type=jnp.float32)
    nblk = PEER_KEYS // 8

    def head(h, carry):
        base = pl.multiple_of(h * PEER_QDIM, PEER_QDIM)
        half_q = PEER_QDIM // 2
        s_both = []
        for p in range(2):
            qb = qT_ref[pl.ds(base + p * half_q, half_q), :].astype(jnp.bfloat16)
            s_both.append(jnp.dot(keys_ref[h, p], qb, preferred_element_type=jnp.float32))
        for half in range(_ROUTE_TT // 128):
            lanes = slice(half * 128, (half + 1) * 128)
            s1 = s_both[0][:, lanes]
            s2 = s_both[1][:, lanes]
            r1 = _top16_replicated([s1[8 * k:8 * k + 8] for k in range(nblk)])
            r2 = _top16_replicated([s2[8 * k:8 * k + 8] for k in range(nblk)])
            lo1, hi1 = _pack_sublanes(r1[:8]), _pack_sublanes(r1[8:])
            lo2, hi2 = _pack_sublanes(r2[:8]), _pack_sublanes(r2[8:])
            cand = _pair_candidates(r1, lo1, hi1, r2, lo2, hi2, lambda a, b: a + b, _NEG)
            cv = _top16_replicated(cand)
            top, thr = cv[0], cv[PEER_TOPK - 1]
            z = jnp.ones_like(top)
            for k in range(1, PEER_TOPK):
                z = z + jnp.exp(cv[k] - top)
            rz = 1.0 / z
            f1 = lambda v: _bf16_round(jnp.exp(v - r1[0]) * rz)
            f2 = lambda v: _bf16_round(jnp.exp(v - r2[0]))
            prod = _pair_candidates([f1(v) for v in r1[:4]], f1(lo1), f1(hi1), [f2(v) for v in r2[:3]], f2(lo2), f2(hi2),
                                    lambda a, b: _bf16_round(a * b), 0.0)
            low = None
            for cpiece, ppiece in zip(cand, prod):
                sel = jnp.where(cpiece >= thr, ppiece, 1e30)
                low = sel if low is None else jnp.minimum(low, sel)
            e1_ref[h, :, lanes] = jnp.exp(s1 - r1[0][0:1]) * rz[0:1]
            e2_ref[h, :, lanes] = (jnp.exp(s2 - r2[0][0:1])).astype(jnp.bfloat16)
            pthr_ref[h, :, lanes] = jnp.min(low, axis=0, keepdims=True)
        return carry

    lax.fori_loop(0, PEER_HEADS, head, 0, unroll=2)


def _peer_route(x, sc, sh, wqT, keys):
    Bsz, S, D = x.shape
    assert S % _ROUTE_TT == 0, S
    nt = S // _ROUTE_TT
    T = Bsz * S
    tab = jax.ShapeDtypeStruct((PEER_HEADS, PEER_KEYS, T), jnp.float32)
    return pl.pallas_call(
        _peer_route_body,
        grid=(Bsz, nt),
        in_specs=[pl.BlockSpec((1, _ROUTE_TT, D), lambda b, i: (b, i, 0)),
                  pl.BlockSpec((1, 1, D), lambda b, i: (b, 0, 0)),
                  pl.BlockSpec((1, 1, D), lambda b, i: (b, 0, 0)),
                  pl.BlockSpec(wqT.shape, lambda b, i: (0, 0)),
                  pl.BlockSpec(keys.shape, lambda b, i: (0, 0, 0, 0))],
        out_specs=[pl.BlockSpec((1, _ROUTE_TT, D), lambda b, i: (b, i, 0)),
                   pl.BlockSpec((PEER_HEADS, PEER_KEYS, _ROUTE_TT), lambda b, i: (0, 0, b * nt + i)),
                   pl.BlockSpec((PEER_HEADS, PEER_KEYS, _ROUTE_TT), lambda b, i: (0, 0, b * nt + i)),
                   pl.BlockSpec((PEER_HEADS, 1, _ROUTE_TT), lambda b, i: (0, 0, b * nt + i))],
        out_shape=[jax.ShapeDtypeStruct((Bsz, S, D), jnp.bfloat16), tab,
                   jax.ShapeDtypeStruct(tab.shape, jnp.bfloat16),
                   jax.ShapeDtypeStruct((PEER_HEADS, 1, T), jnp.float32)],
        scratch_shapes=[pltpu.VMEM((PEER_HEADS * PEER_QDIM, _ROUTE_TT), jnp.float32)],
        compiler_params=pltpu.CompilerParams(dimension_semantics=("arbitrary", "arbitrary"),
                                             vmem_limit_bytes=_VMEM_LIMIT),
        name="peer_route",
    )(x, sc, sh, wqT, keys)


_GELU_K = math.sqrt(2.0 / math.pi)


def _gelu_tanh(x):
    half = 0.5 * x
    return half + half * jnp.tanh(x * (_GELU_K + (_GELU_K * 0.044715) * (x * x)))


_PACK = 16


def _peer_dense_body(alpha, hm_ref, e1_ref, e2_ref, pthr_ref, u_ref, vt_ref, x_ref, g_ref, lng_ref, lnb_ref,
                     o_ref, acc_ref, wt_ref, e1b_ref, pthrb_ref):
    c = pl.program_id(2)
    jrows = 32
    bf16 = jnp.bfloat16
    tp = _DENSE_TT // _DENSE_PARTS

    @pl.when(c == 0)
    def _():
        acc_ref[...] = jnp.zeros_like(acc_ref)
        for h in range(PEER_HEADS):
            pthrb_ref[h] = jnp.broadcast_to(pthr_ref[h], (_PACK, _DENSE_TT)).astype(bf16)

    parts = [slice(t * tp, (t + 1) * tp) for t in range(_DENSE_PARTS)]

    def scores(toks):
        return lax.dot_general(u_ref[...], hm_ref[0, toks, :], _NT, preferred_element_type=jnp.float32)

    def build(toks, act):
        for ii in range(_DENSE_EC // PEER_KEYS):
            for h in range(PEER_HEADS):
                e1b_ref[ii % 2, h] = jnp.broadcast_to(e1_ref[h, ii:ii + 1, toks], (_PACK, tp)).astype(bf16)
            for jb in range(PEER_KEYS // jrows):
                r0 = ii * PEER_KEYS + jb * jrows
                gate = jnp.zeros((jrows // _PACK, _PACK, tp), bf16)
                for h in range(PEER_HEADS):
                    e2 = e2_ref[h, jb * jrows:(jb + 1) * jrows, toks].reshape(jrows // _PACK, _PACK, tp)
                    val = e2 * e1b_ref[ii % 2, h][None]
                    gate = gate + jnp.where(val >= pthrb_ref[h, :, toks][None], val, jnp.zeros_like(val))
                gel = _gelu_tanh(act[r0:r0 + jrows, :]).astype(bf16)
                wt_ref[r0:r0 + jrows, toks] = gate.reshape(jrows, tp) * gel
        acc_ref[:, toks] += jnp.dot(vt_ref[...], wt_ref[:, toks], preferred_element_type=jnp.float32)

    acts = {t: scores(parts[t]) for t in range(min(2, _DENSE_PARTS))}
    for t in range(_DENSE_PARTS):
        build(parts[t], acts.pop(t))
        if t + 2 < _DENSE_PARTS:
            acts[t + 2] = scores(parts[t + 2])

    @pl.when(c == pl.num_programs(2) - 1)
    def _():
        y = alpha * x_ref[0] + g_ref[0] * acc_ref[...].T
        mu = jnp.mean(y, -1, keepdims=True)
        yc = y - mu
        var = jnp.mean(yc * yc, -1, keepdims=True)
        o_ref[0] = yc * lax.rsqrt(var + LN_EPS) * lng_ref[...] + lnb_ref[...]


def _peer_dense(hm, e1, e2, pthr, u_bf, vt_bf, x, g, ln_g, ln_b, alpha):
    Bsz, S, D = x.shape
    assert S % _DENSE_TT == 0, S
    nt = S // _DENSE_TT
    nchunk = N_EXPERTS // _DENSE_EC
    rows_i = _DENSE_EC // PEER_KEYS
    return pl.pallas_call(
        functools.partial(_peer_dense_body, alpha),
        grid=(Bsz, nt, nchunk),
        in_specs=[pl.BlockSpec((1, _DENSE_TT, D), lambda b, i, c: (b, i, 0)),
                  pl.BlockSpec((PEER_HEADS, rows_i, _DENSE_TT), lambda b, i, c: (0, c, b * nt + i)),
                  pl.BlockSpec((PEER_HEADS, PEER_KEYS, _DENSE_TT), lambda b, i, c: (0, 0, b * nt + i)),
                  pl.BlockSpec((PEER_HEADS, 1, _DENSE_TT), lambda b, i, c: (0, 0, b * nt + i)),
                  pl.BlockSpec((_DENSE_EC, D), lambda b, i, c: (c, 0)),
                  pl.BlockSpec((D, _DENSE_EC), lambda b, i, c: (0, c)),
                  pl.BlockSpec((1, _DENSE_TT, D), lambda b, i, c: (b, i, 0)),
                  pl.BlockSpec((1, 1, D), lambda b, i, c: (b, 0, 0)),
                  pl.BlockSpec((1, D), lambda b, i, c: (0, 0)),
                  pl.BlockSpec((1, D), lambda b, i, c: (0, 0))],
        out_specs=pl.BlockSpec((1, _DENSE_TT, D), lambda b, i, c: (b, i, 0)),
        out_shape=jax.ShapeDtypeStruct((Bsz, S, D), jnp.float32),
        scratch_shapes=[pltpu.VMEM((D, _DENSE_TT), jnp.float32),
                        pltpu.VMEM((_DENSE_EC, _DENSE_TT), jnp.bfloat16),
                        pltpu.VMEM((2, PEER_HEADS, _PACK, _DENSE_TT // _DENSE_PARTS), jnp.bfloat16),
                        pltpu.VMEM((PEER_HEADS, _PACK, _DENSE_TT), jnp.bfloat16)],
        compiler_params=pltpu.CompilerParams(dimension_semantics=("arbitrary", "arbitrary", "arbitrary"),
                                             vmem_limit_bytes=_VMEM_LIMIT),
        name="peer_dense",
    )(hm, e1, e2, pthr, u_bf, vt_bf, x, g, ln_g.reshape(1, D), ln_b.reshape(1, D))


def _peer_weights(wq, sub_keys, u_tab, v_tab):
    return (wq.T.astype(jnp.bfloat16), sub_keys.astype(jnp.bfloat16),
            u_tab.astype(jnp.bfloat16), v_tab.T.astype(jnp.bfloat16))


def _peer_ln(x, sc, sh, g, pw, ln_g, ln_b, alpha):
    wqT, keys, u_bf, vt_bf = pw
    hm, e1, e2, pthr = _peer_route(x, sc, sh, wqT, keys)
    return _peer_dense(hm, e1, e2, pthr, u_bf, vt_bf, x, g, ln_g, ln_b, alpha)


def kernel(x, c, ctx, c_ctx, w_ada, b_ada, w_in, w_out, ln1_g, ln1_b, ln2_g, ln2_b,
           conf_dw_w, conf_dw_b, conf_norm_g, conf_norm_b, na_rpb, hy_short_w, hy_short_b,
           hy_w1, hy_b1, hy_w2, hy_b2, hy_w3, hy_decay, hy_bias, ssd_conv_w, ssd_conv_b,
           ssd_a_log, ssd_dt_bias, ssd_d, ssd_norm_g, peer_wq, peer_keys, peer_u, peer_v):
    alpha = (2.0 * DEPTH) ** 0.25
    s_c = jax.nn.silu(c)
    s_cc = jax.nn.silu(c_ctx)
    xc = ctx
    Bsz, Lc, D = ctx.shape
    for l in range(DEPTH):
        ctx_out = l < DEPTH - 1
        mod = (s_c @ w_ada[l] + b_ada[l])[:, None, :]
        mod_c = jnp.broadcast_to((s_cc @ w_ada[l] + b_ada[l])[None, None, :], mod.shape)
        sh1, sc1, g1, sh2, sc2, g2 = jnp.split(mod, 6, -1)
        sh1c, sc1c, g1c, sh2c, sc2c, g2c = jnp.split(mod_c, 6, -1)
        x, xc = _mixer_ln(
            x, xc, (sh1, sc1, g1), (sh1c, sc1c, g1c), ctx_out, alpha, w_in[l], w_out[l], ln1_g[l], ln1_b[l],
            (conf_dw_w[l], conf_dw_b[l], conf_norm_g[l], conf_norm_b[l]), na_rpb[l],
            (hy_short_w[l], hy_short_b[l], hy_w1[l], hy_b1[l], hy_w2[l], hy_b2[l], hy_w3[l], hy_decay[l], hy_bias[l]),
            (ssd_conv_w[l], ssd_conv_b[l], ssd_a_log[l], ssd_dt_bias[l], ssd_d[l], ssd_norm_g[l]))
        pw = _peer_weights(peer_wq[l], peer_keys[l], peer_u[l], peer_v[l])
        x = _peer_ln(x, sc2, sh2, g2, pw, ln2_g[l], ln2_b[l], alpha)
        if ctx_out:
            xc = _peer_ln(xc.reshape(1, Bsz * Lc, D), sc2c[:1], sh2c[:1], g2c[:1], pw, ln2_g[l], ln2_b[l],
                          alpha).reshape(Bsz, Lc, D)
    return x
```

```python
import functools
import math
import jax, jax.numpy as jnp
from jax import lax
from jax.experimental import pallas as pl
from jax.experimental.pallas import tpu as pltpu

D_MODEL = 1024
DEPTH = 2

GRID_W = 64
N_MIXERS = 4
GROUP_W = D_MODEL // N_MIXERS
D_MIX = N_MIXERS * GROUP_W
LN_EPS = 1e-5
CONF_K = 31
CONF_GROUPS = 4
NA_HEADS = 4
HEAD_DIM = GROUP_W // NA_HEADS
NA_KH = 8
NA_KW = 16
ROPE_BASE = 10000.0
HY_SHORT = 3
HY_BANDS = 16
HY_EMB = 1 + 2 * HY_BANDS
HY_HIDDEN = 64
HY_SIN_FREQ = 1.0
SSD_HEADS = 4
SSD_HEAD_DIM = GROUP_W // SSD_HEADS
SSD_GROUPS = 2
SSD_STATE = 64
SSD_CONV = 3
SSD_CHUNK = 128
XBC_W = GROUP_W + 2 * SSD_GROUPS * SSD_STATE
IN_COLS = 2 * GROUP_W + 3 * GROUP_W + 3 * GROUP_W + GROUP_W + XBC_W + 2 * SSD_HEADS
PEER_HEADS = 8
PEER_KEYS = 128
PEER_TOPK = 16
PEER_QDIM = 256
N_EXPERTS = PEER_KEYS * PEER_KEYS

_LANES = 128
_SUBLANES = 8
_VMEM_LIMIT = 56 * 1024 * 1024
_IN_PAD = -(-IN_COLS // _LANES) * _LANES


_IN_SPLITS = ((0, 2 * GROUP_W), (2 * GROUP_W, 5 * GROUP_W), (5 * GROUP_W, 8 * GROUP_W),
              (8 * GROUP_W, 9 * GROUP_W + XBC_W), (9 * GROUP_W + XBC_W, _IN_PAD))
_PROJ_TM = 512


def _in_proj_body(x_ref, sc_ref, sh_ref, w_ref, *o_refs):
    h = (x_ref[0] * (1.0 + sc_ref[0]) + sh_ref[0]).astype(jnp.bfloat16)
    for (lo, hi), o_ref in zip(_IN_SPLITS, o_refs):
        o_ref[0] = jnp.dot(h, w_ref[:, lo:hi], preferred_element_type=jnp.float32)


def _in_proj(x, sc, sh, w_in):
    Bsz, L, D = x.shape
    tm = min(_PROJ_TM, L)
    w = jnp.pad(w_in, ((0, 0), (0, _IN_PAD - IN_COLS))).astype(jnp.bfloat16)
    return pl.pallas_call(
        _in_proj_body,
        grid=(Bsz, L // tm),
        in_specs=[pl.BlockSpec((1, tm, D), lambda b, i: (b, i, 0)),
                  pl.BlockSpec((1, 1, D), lambda b, i: (b, 0, 0)),
                  pl.BlockSpec((1, 1, D), lambda b, i: (b, 0, 0)),
                  pl.BlockSpec((D, _IN_PAD), lambda b, i: (0, 0))],
        out_specs=[pl.BlockSpec((1, tm, hi - lo), lambda b, i: (b, i, 0)) for lo, hi in _IN_SPLITS],
        out_shape=[jax.ShapeDtypeStruct((Bsz, L, hi - lo), jnp.float32) for lo, hi in _IN_SPLITS],
        compiler_params=pltpu.CompilerParams(dimension_semantics=("arbitrary", "arbitrary"),
                                             vmem_limit_bytes=_VMEM_LIMIT),
        name="in_proj",
    )(x, sc, sh, w)


def _out_proj_ln_body(alpha, ya_ref, yb_ref, yc_ref, yd_ref, w_ref, x_ref, g_ref, lng_ref, lnb_ref, o_ref):
    y = 0.0
    for m, y_ref in enumerate((ya_ref, yb_ref, yc_ref, yd_ref)):
        y = y + jnp.dot(y_ref[0].astype(jnp.bfloat16), w_ref[m * GROUP_W:(m + 1) * GROUP_W, :],
                        preferred_element_type=jnp.float32)
    r = alpha * x_ref[0] + g_ref[0] * y
    mu = jnp.mean(r, -1, keepdims=True)
    rc = r - mu
    var = jnp.mean(rc * rc, -1, keepdims=True)
    o_ref[0] = rc * lax.rsqrt(var + LN_EPS) * lng_ref[...] + lnb_ref[...]


def _out_proj_ln(ys, w_out, x, g, ln_g, ln_b, alpha):
    Bsz, L, D = x.shape
    tm = min(_PROJ_TM, L)
    mix = pl.BlockSpec((1, tm, GROUP_W), lambda b, i: (b, i, 0))
    return pl.pallas_call(
        functools.partial(_out_proj_ln_body, alpha),
        grid=(Bsz, L // tm),
        in_specs=[mix, mix, mix, mix,
                  pl.BlockSpec((D_MIX, D), lambda b, i: (0, 0)),
                  pl.BlockSpec((1, tm, D), lambda b, i: (b, i, 0)),
                  pl.BlockSpec((1, 1, D), lambda b, i: (b, 0, 0)),
                  pl.BlockSpec((1, D), lambda b, i: (0, 0)),
                  pl.BlockSpec((1, D), lambda b, i: (0, 0))],
        out_specs=pl.BlockSpec((1, tm, D), lambda b, i: (b, i, 0)),
        out_shape=jax.ShapeDtypeStruct((Bsz, L, D), jnp.float32),
        compiler_params=pltpu.CompilerParams(dimension_semantics=("arbitrary", "arbitrary"),
                                             vmem_limit_bytes=_VMEM_LIMIT),
        name="out_proj_ln",
    )(*ys, w_out.astype(jnp.bfloat16), x, g, ln_g.reshape(1, D), ln_b.reshape(1, D))


def _shifted_taps(win, first, n_taps, rows):
    n = win.shape[0]
    rolled = {0: win}
    taps = []
    for k in range(n_taps):
        r = (first + k) % _SUBLANES
        if r not in rolled:
            rolled[r] = pltpu.roll(win, n - r, 0)
        base = first + k - r
        taps.append(rolled[r][base:base + rows])
    return taps


_CONV_HALO = 16
_CONV_TILE = 256


def _group_mean_matrix(width, group):
    r = lax.broadcasted_iota(jnp.int32, (width, width), 0) // group
    c = lax.broadcasted_iota(jnp.int32, (width, width), 1) // group
    return jnp.where(r == c, 1.0 / group, 0.0).astype(jnp.float32)


def _conformer_body(L, p_ref, w_ref, b_ref, ng_ref, nb_ref, o_ref, u_ref):
    G = GROUP_W
    pad = (CONF_K - 1) // 2
    halo = jnp.zeros((_CONV_HALO, G), jnp.float32)
    u_ref[0:_CONV_HALO, :] = halo
    u_ref[_CONV_HALO + L:_CONV_HALO + L + _CONV_HALO, :] = halo
    u_ref[_CONV_HALO:_CONV_HALO + L, :] = p_ref[0, :, 0:G] * jax.nn.sigmoid(p_ref[0, :, G:2 * G])
    avg = _group_mean_matrix(G, G // CONF_GROUPS)
    tile = min(_CONV_TILE, L)
    for t in range(L // tile):
        win = u_ref[t * tile:t * tile + tile + 2 * _CONV_HALO, :]
        acc = jnp.zeros((tile, G), jnp.float32) + b_ref[...]
        for k, tap in enumerate(_shifted_taps(win, _CONV_HALO - pad, CONF_K, tile)):
            acc = acc + tap * w_ref[k:k + 1, :]
        mu = jnp.dot(acc, avg, preferred_element_type=jnp.float32, precision=lax.Precision.HIGHEST)
        cen = acc - mu
        var = jnp.dot(cen * cen, avg, preferred_element_type=jnp.float32, precision=lax.Precision.HIGHEST)
        un = cen * lax.rsqrt(var + LN_EPS) * ng_ref[...] + nb_ref[...]
        o_ref[0, t * tile:(t + 1) * tile, :] = un * jax.nn.sigmoid(un)


def _conformer(pa, dw_w, dw_b, n_g, n_b):
    Bsz, L, _ = pa.shape
    G = GROUP_W
    vec = pl.BlockSpec((1, G), lambda b: (0, 0))
    return pl.pallas_call(
        functools.partial(_conformer_body, L),
        grid=(Bsz,),
        in_specs=[pl.BlockSpec((1, L, 2 * G), lambda b: (b, 0, 0)),
                  pl.BlockSpec((CONF_K, G), lambda b: (0, 0)), vec, vec, vec],
        out_specs=pl.BlockSpec((1, L, G), lambda b: (b, 0, 0)),
        out_shape=jax.ShapeDtypeStruct((Bsz, L, G), jnp.float32),
        scratch_shapes=[pltpu.VMEM((L + 2 * _CONV_HALO, G), jnp.float32)],
        compiler_params=pltpu.CompilerParams(dimension_semantics=("arbitrary",),
                                             vmem_limit_bytes=_VMEM_LIMIT),
        name="conformer",
    )(pa, dw_w, dw_b.reshape(1, G), n_g.reshape(1, G), n_b.reshape(1, G))


def axial_rope(rows, head_dim):
    n_f = head_dim // 4
    inv = ROPE_BASE ** (-jnp.arange(n_f, dtype=jnp.float32) / n_f)
    t = jnp.arange(rows * GRID_W)
    r = (t // GRID_W).astype(jnp.float32)
    col = (t % GRID_W).astype(jnp.float32)
    ang = jnp.concatenate([r[:, None] * inv, col[:, None] * inv], -1)
    return jnp.cos(ang), jnp.sin(ang)


_NA_MASK = -1e30


def _na_tables(rows, rpb):
    cos, sin = axial_rope(rows, HEAD_DIM)
    cos_f = jnp.tile(cos, (1, 2 * NA_HEADS))
    sin_s = jnp.tile(jnp.concatenate([-sin, sin], -1), (1, NA_HEADS))
    cq = jnp.arange(GRID_W)
    cs = jnp.clip(cq - NA_KW // 2, 0, GRID_W - NA_KW)
    col = jnp.arange(GRID_W)
    in_band = (col[None, :] >= cs[:, None]) & (col[None, :] < cs[:, None] + NA_KW)
    cb_idx = jnp.clip(col[None, :] - cq[:, None] + (NA_KW - 1), 0, 2 * NA_KW - 2)
    po = jnp.arange(NA_KH)
    rb_idx = jnp.arange(NA_KH)[None, :] - po[:, None] + (NA_KH - 1)
    bias = rpb[:, rb_idx][:, :, :, cb_idx]
    bias = jnp.where(in_band[None, None, None], bias, _NA_MASK)
    bias = bias.transpose(1, 0, 3, 2, 4).reshape(NA_KH, NA_HEADS, GRID_W, NA_KH * GRID_W)
    return cos_f, sin_s, bias


def _rope_lanes(x, cos_f, sin_s):
    n = x.shape[-1]
    hd = HEAD_DIM // 2
    first = (lax.broadcasted_iota(jnp.int32, x.shape, 1) % HEAD_DIM) < hd
    partner = jnp.where(first, pltpu.roll(x, n - hd, 1), pltpu.roll(x, hd, 1))
    return x * cos_f + partner * sin_s


def _softmax_pv(s_parts, v_parts):
    m = s_parts[0].max(-1, keepdims=True)
    for s in s_parts[1:]:
        m = jnp.maximum(m, s.max(-1, keepdims=True))
    l = 0.0
    o = 0.0
    for s, v in zip(s_parts, v_parts):
        p = jnp.exp(s - m)
        l = l + p.sum(-1, keepdims=True)
        o = o + jnp.dot(p.astype(jnp.bfloat16), v, preferred_element_type=jnp.float32)
    return o / l


_NA_ROWS = 4


def _na_body(rows, q_ref, k_ref, v_ref, kc_ref, vc_ref, cosq_ref, sinq_ref, cosk_ref, sink_ref, *rest):
    bias_refs, (o_ref, krot_ref, vbf_ref) = rest[:_NA_ROWS], rest[_NA_ROWS:]
    step = pl.program_id(1)
    scale = HEAD_DIM ** -0.5

    @pl.when(step == 0)
    def _():
        krot_ref[...] = _rope_lanes(k_ref[0], cosk_ref[...], sink_ref[...]).astype(jnp.bfloat16)
        vbf_ref[...] = v_ref[0].astype(jnp.bfloat16)

    win = NA_KH * GRID_W
    q = q_ref[0] * scale
    q_rot = _rope_lanes(q, cosq_ref[...], sinq_ref[...]).astype(jnp.bfloat16)
    q_plain = q.astype(jnp.bfloat16)
    kc = kc_ref[0].astype(jnp.bfloat16)
    vc = vc_ref[0].astype(jnp.bfloat16)
    for i in range(_NA_ROWS):
        r = step * _NA_ROWS + i
        rs = jnp.clip(r - NA_KH // 2, 0, rows - NA_KH)
        start = pl.multiple_of(rs * GRID_W, GRID_W)
        kw = krot_ref[pl.ds(start, win), :]
        vw = vbf_ref[pl.ds(start, win), :]
        qs = slice(i * GRID_W, (i + 1) * GRID_W)
        outs = []
        for h in range(NA_HEADS):
            hs = slice(h * HEAD_DIM, (h + 1) * HEAD_DIM)
            s_loc = lax.dot_general(q_rot[qs, hs], kw[:, hs], _NT, preferred_element_type=jnp.float32) + bias_refs[i][0, h]
            s_ctx = lax.dot_general(q_plain[qs, hs], kc[:, hs], _NT, preferred_element_type=jnp.float32)
            outs.append(_softmax_pv([s_loc, s_ctx], [vw[:, hs], vc[:, hs]]))
        o_ref[0, qs, :] = jnp.concatenate(outs, axis=-1)


def _na_attention(pb, cb, rpb):
    Bsz, S, _ = pb.shape
    Lc = cb.shape[1]
    rows = S // GRID_W
    assert rows % _NA_ROWS == 0, rows
    G = GROUP_W
    nq = _NA_ROWS * GRID_W
    cos_f, sin_s, bias = _na_tables(rows, rpb)

    def bias_spec(i):
        def idx(b, step):
            r = step * _NA_ROWS + i
            return (r - jnp.clip(r - NA_KH // 2, 0, rows - NA_KH), 0, 0, 0)
        return pl.BlockSpec((1, NA_HEADS, GRID_W, NA_KH * GRID_W), idx)

    return pl.pallas_call(
        functools.partial(_na_body, rows),
        grid=(Bsz, rows // _NA_ROWS),
        in_specs=[pl.BlockSpec((1, nq, G), lambda b, r: (b, r, 0)),
                  pl.BlockSpec((1, S, G), lambda b, r: (b, 0, 1)),
                  pl.BlockSpec((1, S, G), lambda b, r: (b, 0, 2)),
                  pl.BlockSpec((1, Lc, G), lambda b, r: (b, 0, 1)),
                  pl.BlockSpec((1, Lc, G), lambda b, r: (b, 0, 2)),
                  pl.BlockSpec((nq, G), lambda b, r: (r, 0)),
                  pl.BlockSpec((nq, G), lambda b, r: (r, 0)),
                  pl.BlockSpec((S, G), lambda b, r: (0, 0)),
                  pl.BlockSpec((S, G), lambda b, r: (0, 0))] + [bias_spec(i) for i in range(_NA_ROWS)],
        out_specs=pl.BlockSpec((1, nq, G), lambda b, r: (b, r, 0)),
        out_shape=jax.ShapeDtypeStruct((Bsz, S, G), jnp.float32),
        scratch_shapes=[pltpu.VMEM((S, G), jnp.bfloat16), pltpu.VMEM((S, G), jnp.bfloat16)],
        compiler_params=pltpu.CompilerParams(dimension_semantics=("arbitrary", "arbitrary"),
                                             vmem_limit_bytes=_VMEM_LIMIT),
        name="na_attention",
    )(pb, pb, pb, cb, cb, cos_f, sin_s, cos_f, sin_s, *([bias] * _NA_ROWS))


def _ctx_attn_body(q_ref, k_ref, v_ref, o_ref):
    q = (q_ref[0] * HEAD_DIM ** -0.5).astype(jnp.bfloat16)
    k = k_ref[0].astype(jnp.bfloat16)
    v = v_ref[0].astype(jnp.bfloat16)
    outs = []
    for h in range(NA_HEADS):
        hs = slice(h * HEAD_DIM, (h + 1) * HEAD_DIM)
        s = lax.dot_general(q[:, hs], k[:, hs], _NT, preferred_element_type=jnp.float32)
        outs.append(_softmax_pv([s], [v[:, hs]]))
    o_ref[0] = jnp.concatenate(outs, axis=-1)


def _ctx_attention(cb):
    Bsz, Lc, _ = cb.shape
    G = GROUP_W
    return pl.pallas_call(
        _ctx_attn_body,
        grid=(Bsz,),
        in_specs=[pl.BlockSpec((1, Lc, G), lambda b: (b, 0, 0)),
                  pl.BlockSpec((1, Lc, G), lambda b: (b, 0, 1)),
                  pl.BlockSpec((1, Lc, G), lambda b: (b, 0, 2))],
        out_specs=pl.BlockSpec((1, Lc, G), lambda b: (b, 0, 0)),
        out_shape=jax.ShapeDtypeStruct((Bsz, Lc, G), jnp.float32),
        compiler_params=pltpu.CompilerParams(dimension_semantics=("arbitrary",)),
        name="ctx_attention",
    )(cb, cb, cb)


_HY_TB = 256
_HY_CB = 8


def _hyena_filter_body(L, w1_ref, b1_ref, w2_ref, b2_ref, w3_ref, dec_ref, o_ref):
    G = GROUP_W
    hp = lax.Precision.HIGHEST
    p = lax.broadcasted_iota(jnp.int32, (2 * L, _LANES), 0)
    lane = lax.broadcasted_iota(jnp.int32, (2 * L, _LANES), 1)
    tn = jnp.abs(p - L).astype(jnp.float32) / L
    band = ((lane - 1) % HY_BANDS + 1).astype(jnp.float32)
    ang = 2.0 * math.pi * band * tn
    z = jnp.where(lane == 0, tn, jnp.where(lane <= HY_BANDS, jnp.sin(ang), jnp.cos(ang)))
    z = jnp.where(lane < HY_EMB, z, 0.0)
    h = jnp.sin(HY_SIN_FREQ * (jnp.dot(z, w1_ref[...], precision=hp, preferred_element_type=jnp.float32) + b1_ref[...]))
    h = jnp.sin(HY_SIN_FREQ * (jnp.dot(h, w2_ref[...], precision=hp, preferred_element_type=jnp.float32) + b2_ref[...]))
    k = jnp.dot(h, w3_ref[...], precision=hp, preferred_element_type=jnp.float32) * jnp.exp(-tn[:, 0:1] * dec_ref[...])
    kf, kb = k[:, 0:G], k[:, G:2 * G]
    n = lax.broadcasted_iota(jnp.int32, (2 * L, G), 0) - L
    nf = jnp.sum(jnp.where(n >= 0, jnp.abs(kf), 0.0), axis=0, keepdims=True) + 1e-6
    nb = jnp.sum(jnp.where(n <= 0, jnp.where(n > -L, jnp.abs(kb), 0.0), 0.0), axis=0, keepdims=True) + 1e-6
    rev = jnp.where(n > 0, kb / nb, jnp.where(n > -L, kf / nf, 0.0))
    o_ref[...] = rev.T


def _hyena_filters_rev(L, w1, b1, w2, b2, w3, decay):
    G = GROUP_W
    hp = _LANES - HY_HIDDEN
    w1p = jnp.pad(w1.astype(jnp.float32), ((0, _LANES - HY_EMB), (0, hp)))
    args = (w1p, jnp.pad(b1.reshape(1, -1), ((0, 0), (0, hp))), jnp.pad(w2, ((0, hp), (0, hp))),
            jnp.pad(b2.reshape(1, -1), ((0, 0), (0, hp))), jnp.pad(w3, ((0, hp), (0, 0))), decay.reshape(1, -1))
    return pl.pallas_call(
        functools.partial(_hyena_filter_body, L),
        out_shape=jax.ShapeDtypeStruct((G, 2 * L), jnp.float32),
        compiler_params=pltpu.CompilerParams(vmem_limit_bytes=_VMEM_LIMIT),
        name="hyena_filters",
    )(*[a.astype(jnp.float32) for a in args])


def _hyena_pre_body(L, p_ref, w_ref, b_ref, u_ref, ut_ref, x0_ref, xp_ref):
    G = GROUP_W
    W = 3 * G
    halo = jnp.zeros((_CONV_HALO, W), jnp.float32)
    xp_ref[0:_CONV_HALO, :] = halo
    xp_ref[_CONV_HALO + L:_CONV_HALO + L + _CONV_HALO, :] = halo
    xp_ref[_CONV_HALO:_CONV_HALO + L, :] = p_ref[0]
    pad = (HY_SHORT - 1) // 2
    tile = min(_CONV_TILE, L)
    for t in range(L // tile):
        win = xp_ref[t * tile:t * tile + tile + 2 * _CONV_HALO, :]
        acc = jnp.zeros((tile, W), jnp.float32) + b_ref[...]
        for k, tap in enumerate(_shifted_taps(win, _CONV_HALO - pad, HY_SHORT, tile)):
            acc = acc + tap * w_ref[k:k + 1, :]
        rows = slice(t * tile, (t + 1) * tile)
        u = acc[:, 2 * G:3 * G] * acc[:, G:2 * G]
        x0_ref[0, rows, :] = acc[:, 0:G]
        u_ref[0, rows, :] = u
        ut_ref[0, :, t, :] = u.T


def _hyena_pre(py, short_w, short_b):
    Bsz, L, W = py.shape
    G = GROUP_W
    f32 = jnp.float32
    TB = min(_HY_TB, L)
    assert TB == min(_CONV_TILE, L)
    return pl.pallas_call(
        functools.partial(_hyena_pre_body, L),
        grid=(Bsz,),
        in_specs=[pl.BlockSpec((1, L, W), lambda b: (b, 0, 0)),
                  pl.BlockSpec((HY_SHORT, W), lambda b: (0, 0)),
                  pl.BlockSpec((1, W), lambda b: (0, 0))],
        out_specs=[pl.BlockSpec((1, L, G), lambda b: (b, 0, 0)),
                   pl.BlockSpec((1, G, L // TB, TB), lambda b: (b, 0, 0, 0)),
                   pl.BlockSpec((1, L, G), lambda b: (b, 0, 0))],
        out_shape=[jax.ShapeDtypeStruct((Bsz, L, G), f32), jax.ShapeDtypeStruct((Bsz, G, L // TB, TB), f32),
                   jax.ShapeDtypeStruct((Bsz, L, G), f32)],
        scratch_shapes=[pltpu.VMEM((L + 2 * _CONV_HALO, W), f32)],
        compiler_params=pltpu.CompilerParams(dimension_semantics=("arbitrary",), vmem_limit_bytes=_VMEM_LIMIT),
        name="hyena_pre",
    )(py, short_w, short_b.reshape(1, W))


def _toeplitz_tile(rolled, q0):
    TB = _HY_TB
    per = _LANES // _SUBLANES
    i = lax.broadcasted_iota(jnp.int32, (_SUBLANES, _LANES), 0)
    l = lax.broadcasted_iota(jnp.int32, (_SUBLANES, _LANES), 1)
    row_blocks = []
    for rg in range(TB // _SUBLANES):
        k = rg % per
        pieces = []
        for lg in range(TB // _LANES):
            o = TB + _LANES * lg - _SUBLANES * rg
            q, rho = divmod(o, _LANES)
            if rho == 0:
                pieces.append(jnp.where(l - i < 0, rolled(q0 + q - 1, k), rolled(q0 + q, k)))
            else:
                pieces.append(jnp.where(l + rho - i >= _LANES, rolled(q0 + q + 1, k), rolled(q0 + q, k)))
        row_blocks.append(jnp.concatenate(pieces, axis=1))
    return jnp.concatenate(row_blocks, axis=0).astype(jnp.bfloat16)


def _hyena_conv_body(L, Bsz, g_ref, u_ref, o_ref):
    TB = min(_HY_TB, L)
    nb = L // TB
    cols = Bsz * nb
    lane = lax.broadcasted_iota(jnp.int32, (TB, cols), 1) % nb

    def channel(ci, carry):
        u = u_ref[ci] if nb == 1 else u_ref[:, ci].reshape(cols, TB)
        u = u.astype(jnp.bfloat16)
        acc = jnp.zeros((TB, cols), jnp.float32)
        cache = {}
        g_row = g_ref[pl.ds(ci, 1), :]

        def rolled(q, k):
            if (q, k) not in cache:
                blk = jnp.broadcast_to(g_row[:, q * _LANES:(q + 1) * _LANES], (_SUBLANES, _LANES))
                cache[q, k] = pltpu.roll(blk, (_SUBLANES * k) % _LANES, 1, stride=1, stride_axis=0)
            return cache[q, k]

        for d in range(-(nb - 1), nb):
            start = L - TB * d - TB
            tile = _toeplitz_tile(rolled, start // _LANES)
            z = lax.dot_general(tile, u, _NT, preferred_element_type=jnp.float32)
            if d != 0:
                z = jnp.where((lane - d >= 0) & (lane - d < nb), pltpu.roll(z, d % cols, 1), 0.0)
            acc = acc + z
        if nb == 1:
            o_ref[ci] = acc.T
        else:
            o_ref[:, ci] = acc.T.reshape(Bsz, nb, TB)
        return carry

    lax.fori_loop(0, _HY_CB, channel, 0)


def _hyena_conv(g_rev, u_t):
    Bsz, G, nb, TB = u_t.shape
    L = nb * TB
    if nb == 1:
        blk = pl.BlockSpec((_HY_CB, Bsz, TB), lambda c: (c, 0, 0))
        operand, out_shape = jnp.swapaxes(u_t.reshape(Bsz, G, TB), 0, 1), (G, Bsz, TB)
    else:
        blk = pl.BlockSpec((Bsz, _HY_CB, nb, TB), lambda c: (0, c, 0, 0))
        operand, out_shape = u_t, (Bsz, G, nb, TB)
    out = pl.pallas_call(
        functools.partial(_hyena_conv_body, L, Bsz),
        grid=(G // _HY_CB,),
        in_specs=[pl.BlockSpec((_HY_CB, 2 * L), lambda c: (c, 0)), blk],
        out_specs=blk,
        out_shape=jax.ShapeDtypeStruct(out_shape, jnp.float32),
        compiler_params=pltpu.CompilerParams(dimension_semantics=("arbitrary",), vmem_limit_bytes=_VMEM_LIMIT),
        name="hyena_conv",
    )(g_rev, operand)
    return jnp.swapaxes(out, 0, 1).reshape(Bsz, G, nb, TB) if nb == 1 else out


def _hyena_post_body(yt_ref, u_ref, x0_ref, skip_ref, o_ref):
    nb, TB = yt_ref.shape[2], yt_ref.shape[3]
    for t in range(nb):
        rows = slice(t * TB, (t + 1) * TB)
        o_ref[0, rows, :] = (yt_ref[0, :, t, :].T + u_ref[0, rows, :] * skip_ref[...]) * x0_ref[0, rows, :]


def _hyena_post(y_t, u, x0, skip):
    Bsz, L, G = u.shape
    tok = pl.BlockSpec((1, L, G), lambda b: (b, 0, 0))
    return pl.pallas_call(
        _hyena_post_body,
        grid=(Bsz,),
        in_specs=[pl.BlockSpec((1,) + y_t.shape[1:], lambda b: (b, 0, 0, 0)), tok, tok,
                  pl.BlockSpec((1, G), lambda b: (0, 0))],
        out_specs=tok,
        out_shape=jax.ShapeDtypeStruct((Bsz, L, G), jnp.float32),
        compiler_params=pltpu.CompilerParams(dimension_semantics=("arbitrary",), vmem_limit_bytes=_VMEM_LIMIT),
        name="hyena_post",
    )(y_t, u, x0, skip.reshape(1, G))


def _hyena(py, short_w, short_b, w1, b1, w2, b2, w3, decay, skip):
    L = py.shape[1]
    g_rev = _hyena_filters_rev(L, w1, b1, w2, b2, w3, decay)
    u, u_t, x0 = _hyena_pre(py, short_w, short_b)
    return _hyena_post(_hyena_conv(g_rev, u_t), u, x0, skip)


def _split3_dot(a, b_bf16, dims=None):
    hi = a.astype(jnp.bfloat16)
    r1 = a - hi.astype(jnp.float32)
    mid = r1.astype(jnp.bfloat16)
    lo = (r1 - mid.astype(jnp.float32)).astype(jnp.bfloat16)
    out = 0.0
    for part in (hi, mid, lo):
        if dims is None:
            out = out + jnp.dot(part, b_bf16, preferred_element_type=jnp.float32)
        else:
            out = out + jnp.dot(b_bf16, part, preferred_element_type=jnp.float32)
    return out


def _softplus(x):
    return jnp.maximum(x, 0.0) + jnp.log(1.0 + jnp.exp(-jnp.abs(x)))


def _ssd_body(L, zx_ref, dtc_ref, cw_ref, cb_ref, arow_ref, acol_ref, brow_ref, bcol_ref, dsk_ref,
              ng_ref, init_ref, o_ref, fin_ref, xp_ref, xc_ref, bt_ref, y_ref, ccol_ref, crow_ref, edec_ref,
              tot_ref):
    G = GROUP_W
    Q = SSD_CHUNK
    nc = L // Q
    P = SSD_HEAD_DIM
    N = SSD_STATE
    H = SSD_HEADS
    f32 = jnp.float32
    bf16 = jnp.bfloat16
    halo = jnp.zeros((_CONV_HALO, XBC_W), f32)
    xp_ref[0:_CONV_HALO, :] = halo
    xp_ref[_CONV_HALO + L:_CONV_HALO + L + _CONV_HALO, :] = halo
    xp_ref[_CONV_HALO:_CONV_HALO + L, :] = zx_ref[0, :, G:G + XBC_W]
    pad = (SSD_CONV - 1) // 2
    tile = min(_CONV_TILE, L)
    for t in range(L // tile):
        win = xp_ref[t * tile:t * tile + tile + 2 * _CONV_HALO, :]
        acc = jnp.zeros((tile, XBC_W), f32) + cb_ref[...]
        for k, tap in enumerate(_shifted_taps(win, _CONV_HALO - pad, SSD_CONV, tile)):
            acc = acc + tap * cw_ref[k:k + 1, :]
        xc_ref[t * tile:(t + 1) * tile, :] = acc * jax.nn.sigmoid(acc)
    for c in range(nc):
        bt_ref[c] = xc_ref[c * Q:(c + 1) * Q, G:G + SSD_GROUPS * N].T
    dt_col = _softplus(dtc_ref[0] + brow_ref[...])
    a_col = dt_col * arow_ref[...]
    dt_rows = dtc_ref[0].T[0:2 * H]
    a_row = _softplus(dt_rows + bcol_ref[...]) * acol_ref[...]
    a_stack = jnp.concatenate([a_row[:, c * Q:(c + 1) * Q] for c in range(nc)], axis=0)
    ri = lax.broadcasted_iota(jnp.int32, (Q, Q), 0)
    ci = lax.broadcasted_iota(jnp.int32, (Q, Q), 1)
    one = lambda m: jnp.where(m, 1.0, 0.0).astype(bf16)
    tot_ref[...] = _split3_dot(a_stack, jnp.ones((Q, Q), bf16))

    def direction(d, y_store):
        fwd = d == 0
        m_col = one(ci <= ri) if fwd else one(ci >= ri)
        for c in range(nc):
            ccol_ref[c * Q:(c + 1) * Q, :] = _split3_dot(a_col[c * Q:(c + 1) * Q, :], m_col, dims="left")
        crow_ref[...] = _split3_dot(a_stack, one(ri <= ci) if fwd else one(ri >= ci))
        edec_ref[...] = _split3_dot(a_stack, one(ri > ci) if fwd else one(ri < ci))
        keep = (ri >= ci) if fwd else (ri <= ci)

        def chunk(step, states):
            c = step if fwd else nc - 1 - step
            r0 = pl.multiple_of(c * Q, Q)
            j0 = pl.multiple_of(c * 2 * H, 2 * H)
            xc = xc_ref[pl.ds(r0, Q), :]
            ccol = ccol_ref[pl.ds(r0, Q), :]
            crow = crow_ref[pl.ds(j0, 2 * H), :]
            edec = edec_ref[pl.ds(j0, 2 * H), :]
            tot = tot_ref[pl.ds(j0, 2 * H), :]
            dtc = _softplus(dtc_ref[0, pl.ds(r0, Q), :] + brow_ref[...])
            bt = bt_ref[c]
            new_states = []
            outs = []
            for g in range(SSD_GROUPS):
                cm = xc[:, G + SSD_GROUPS * N + g * N:G + SSD_GROUPS * N + (g + 1) * N].astype(bf16)
                bm = xc[:, G + g * N:G + (g + 1) * N].astype(bf16)
                cb = lax.dot_general(cm, bm, _NT, preferred_element_type=f32)
                for hh in range(H // SSD_GROUPS):
                    h = g * (H // SSD_GROUPS) + hh
                    j = d * H + h
                    col = jnp.broadcast_to(ccol[:, j:j + 1], (Q, Q))
                    lmat = jnp.exp(jnp.where(keep, col - crow[j:j + 1, :], _NEG))
                    xd = (xc[:, h * P:(h + 1) * P] * jnp.broadcast_to(dtc[:, j:j + 1], (Q, P))).astype(bf16)
                    st = states[h]
                    y = jnp.dot((cb * lmat).astype(bf16), xd, preferred_element_type=f32)
                    y = y + jnp.dot(cm, st.astype(bf16), preferred_element_type=f32) * jnp.exp(col[:, 0:P])
                    outs.append(y)
                    btd = (bt[g * N:(g + 1) * N, :] * jnp.exp(edec[j:j + 1, :])).astype(bf16)
                    new_states.append(jnp.exp(tot[j:j + 1, 0:P]) * st
                                      + jnp.dot(btd, xd, preferred_element_type=f32))
            y_store(r0, jnp.concatenate(outs, axis=-1))
            return tuple(new_states)

        init = tuple(init_ref[0, d, h] for h in range(H))
        final = lax.fori_loop(0, nc, chunk, init, unroll=min(4, nc))
        for h in range(H):
            fin_ref[0, d, h] = final[h]

    def store_fwd(r0, y):
        y_ref[pl.ds(r0, Q), :] = y

    def store_bwd(r0, y):
        y_ref[pl.ds(r0, Q), :] += y

    direction(0, store_fwd)
    direction(1, store_bwd)
    gw = G // SSD_GROUPS
    for t in range(L // tile):
        rows = slice(t * tile, (t + 1) * tile)
        z = zx_ref[0, rows, 0:G]
        yg = (y_ref[rows, :] + xc_ref[rows, 0:G] * dsk_ref[...]) * (z * jax.nn.sigmoid(z))
        parts = []
        for g in range(SSD_GROUPS):
            v = yg[:, g * gw:(g + 1) * gw]
            parts.append(v * lax.rsqrt(jnp.mean(v * v, -1, keepdims=True) + LN_EPS))
        o_ref[0, rows, :] = jnp.concatenate(parts, axis=-1) * ng_ref[...]


def _ssd(pzx, pdt, init, conv_w, conv_b, a_log, dt_bias, d_skip, norm_g):
    Bsz, L, _ = pzx.shape
    G, H, Q = GROUP_W, SSD_HEADS, SSD_CHUNK
    lanes = pdt.shape[-1]
    nc = L // Q
    neg_a = -jnp.exp(a_log.astype(jnp.float32)).reshape(1, 2 * H)
    a_rowv = jnp.pad(neg_a, ((0, 0), (0, lanes - 2 * H)))
    b_rowv = jnp.pad(dt_bias.astype(jnp.float32).reshape(1, 2 * H), ((0, 0), (0, lanes - 2 * H)))
    d_lane = jnp.repeat(d_skip.astype(jnp.float32), SSD_HEAD_DIM).reshape(1, G)
    const = lambda shape: pl.BlockSpec(shape, lambda b: (0,) * len(shape))
    st_spec = pl.BlockSpec((1, 2, H, SSD_STATE, SSD_HEAD_DIM), lambda b: (b, 0, 0, 0, 0))
    f32 = jnp.float32
    return pl.pallas_call(
        functools.partial(_ssd_body, L),
        grid=(Bsz,),
        in_specs=[pl.BlockSpec((1, L, G + XBC_W), lambda b: (b, 0, 0)),
                  pl.BlockSpec((1, L, lanes), lambda b: (b, 0, 0)),
                  const((SSD_CONV, XBC_W)), const((1, XBC_W)),
                  const((1, lanes)), const((2 * H, 1)), const((1, lanes)), const((2 * H, 1)),
                  const((1, G)), const((1, G)), st_spec],
        out_specs=[pl.BlockSpec((1, L, G), lambda b: (b, 0, 0)), st_spec],
        out_shape=[jax.ShapeDtypeStruct((Bsz, L, G), f32),
                   jax.ShapeDtypeStruct((Bsz, 2, H, SSD_STATE, SSD_HEAD_DIM), f32)],
        scratch_shapes=[pltpu.VMEM((L + 2 * _CONV_HALO, XBC_W), f32),
                        pltpu.VMEM((L, XBC_W), f32),
                        pltpu.VMEM((nc, SSD_GROUPS * SSD_STATE, Q), f32),
                        pltpu.VMEM((L, G), f32),
                        pltpu.VMEM((L, lanes), f32),
                        pltpu.VMEM((nc * 2 * H, Q), f32),
                        pltpu.VMEM((nc * 2 * H, Q), f32),
                        pltpu.VMEM((nc * 2 * H, Q), f32)],
        compiler_params=pltpu.CompilerParams(dimension_semantics=("arbitrary",),
                                             vmem_limit_bytes=_VMEM_LIMIT),
        name="ssd",
    )(pzx, pdt, conv_w, conv_b.reshape(1, XBC_W), a_rowv, neg_a.reshape(2 * H, 1),
      b_rowv, dt_bias.astype(f32).reshape(2 * H, 1), d_lane, norm_g.reshape(1, G), init)


def _mixer_ln(x, xc, mod, mod_c, ctx_out, alpha, w_in, w_out, ln_g, ln_b, conf, rpb, hy, ssd):
    Bsz = x.shape[0]
    sh, sc, g = mod
    shc, scc, gc = mod_c
    pa, pb, py, pzx, pdt = _in_proj(x, sc, sh, w_in)
    ca, cb, cy, czx, cdt = _in_proj(xc, scc, shc, w_in)
    zero = jnp.zeros((Bsz, 2, SSD_HEADS, SSD_STATE, SSD_HEAD_DIM), jnp.float32)
    y_dc, ctx_states = _ssd(czx, cdt, zero, *ssd)
    y_d, _ = _ssd(pzx, pdt, ctx_states, *ssd)
    ys = [_conformer(pa, *conf), _na_attention(pb, cb, rpb), _hyena(py, *hy), y_d]
    x_new = _out_proj_ln(ys, w_out, x, g, ln_g, ln_b, alpha)
    if not ctx_out:
        return x_new, None
    ycs = [_conformer(ca, *conf), _ctx_attention(cb), _hyena(cy, *hy), y_dc]
    return x_new, _out_proj_ln(ycs, w_out, xc, gc, ln_g, ln_b, alpha)


_NEG = -1e30
_ROUTE_TT = 256
_DENSE_TT = 1024
_DENSE_PARTS = 4
_DENSE_EC = 1024
_NT = (((1,), (1,)), ((), ()))


def _bf16_round(x):
    return x.astype(jnp.bfloat16).astype(jnp.float32)


def _oddeven_sort_pairs(n):
    pairs = []
    p = 1
    while p < n:
        k = p
        while k >= 1:
            for j in range(k % p, n - k, 2 * k):
                for i in range(min(k, n - j - k)):
                    if (i + j) // (2 * p) == (i + j + k) // (2 * p):
                        pairs.append((i + j, i + j + k))
            k //= 2
        p *= 2
    return pairs


_SORT16 = _oddeven_sort_pairs(PEER_TOPK)


def _order_pair(vs, i, j):
    a, b = vs[i], vs[j]
    if b is None:
        return
    if a is None:
        vs[i], vs[j] = b, None
        return
    vs[i], vs[j] = jnp.maximum(a, b), jnp.minimum(a, b)


def _top16_replicated(vs):
    vs = list(vs) + [None] * (PEER_TOPK - len(vs))
    for i, j in _SORT16:
        _order_pair(vs, i, j)
    for shift in (4, 2, 1):
        other = [None if v is None else pltpu.roll(v, shift, 0) for v in vs]
        merged = []
        for k in range(PEER_TOPK):
            a, b = vs[k], other[PEER_TOPK - 1 - k]
            merged.append(b if a is None else a if b is None else jnp.maximum(a, b))
        vs = merged
        stride = PEER_TOPK // 2
        while stride >= 1:
            for i in range(PEER_TOPK):
                if (i // stride) % 2 == 0:
                    _order_pair(vs, i, i + stride)
            stride //= 2
    return vs


def _pack_sublanes(blocks):
    sub = lax.broadcasted_iota(jnp.int32, blocks[0].shape, 0)
    out = blocks[0]
    for r in range(1, len(blocks)):
        out = jnp.where(sub == r, blocks[r], out)
    return out


def _pair_candidates(r1, lo1, hi1, r2, lo2, hi2, op, fill):
    keep = lax.broadcasted_iota(jnp.int32, lo1.shape, 0) >= 4
    pieces = [op(r1[0], lo2), op(r1[0], hi2), op(r1[1], lo2), op(r1[2], lo2), op(r1[3], lo2), op(r2[0], hi1)]
    for b in range(3):
        pieces.append(jnp.where(keep, op(r2[b], lo1), fill))
    return pieces


def _peer_route_body(x_ref, sc_ref, sh_ref, wqT_ref, keys_ref, hm_ref, e1_ref, e2_ref, pthr_ref, qT_ref):
    hm = (x_ref[0] * (1.0 + sc_ref[0]) + sh_ref[0]).astype(jnp.bfloat16)
    hm_ref[0] = hm
    qT_ref[...] = lax.dot_general(wqT_ref[...], hm, _NT, preferred_element_type=jnp.float32)
    nblk = PEER_KEYS // 8

    def head(h, carry):
        base = pl.multiple_of(h * PEER_QDIM, PEER_QDIM)
        half_q = PEER_QDIM // 2
        s_both = []
        for p in range(2):
            qb = qT_ref[pl.ds(base + p * half_q, half_q), :].astype(jnp.bfloat16)
            s_both.append(jnp.dot(keys_ref[h, p], qb, preferred_element_type=jnp.float32))
        for half in range(_ROUTE_TT // 128):
            lanes = slice(half * 128, (half + 1) * 128)
            s1 = s_both[0][:, lanes]
            s2 = s_both[1][:, lanes]
            r1 = _top16_replicated([s1[8 * k:8 * k + 8] for k in range(nblk)])
            r2 = _top16_replicated([s2[8 * k:8 * k + 8] for k in range(nblk)])
            lo1, hi1 = _pack_sublanes(r1[:8]), _pack_sublanes(r1[8:])
            lo2, hi2 = _pack_sublanes(r2[:8]), _pack_sublanes(r2[8:])
            cand = _pair_candidates(r1, lo1, hi1, r2, lo2, hi2, lambda a, b: a + b, _NEG)
            cv = _top16_replicated(cand)
            top, thr = cv[0], cv[PEER_TOPK - 1]
            z = jnp.ones_like(top)
            for k in range(1, PEER_TOPK):
                z = z + jnp.exp(cv[k] - top)
            rz = 1.0 / z
            f1 = lambda v: _bf16_round(jnp.exp(v - r1[0]) * rz)
            f2 = lambda v: _bf16_round(jnp.exp(v - r2[0]))
            prod = _pair_candidates([f1(v) for v in r1[:4]], f1(lo1), f1(hi1), [f2(v) for v in r2[:3]], f2(lo2), f2(hi2),
                                    lambda a, b: _bf16_round(a * b), 0.0)
            low = None
            for cpiece, ppiece in zip(cand, prod):
                sel = jnp.where(cpiece >= thr, ppiece, 1e30)
                low = sel if low is None else jnp.minimum(low, sel)
            e1_ref[h, :, lanes] = jnp.exp(s1 - r1[0][0:1]) * rz[0:1]
            e2_ref[h, :, lanes] = (jnp.exp(s2 - r2[0][0:1])).astype(jnp.bfloat16)
            pthr_ref[h, :, lanes] = jnp.min(low, axis=0, keepdims=True)
        return carry

    lax.fori_loop(0, PEER_HEADS, head, 0, unroll=2)


def _peer_route(x, sc, sh, wqT, keys):
    Bsz, S, D = x.shape
    assert S % _ROUTE_TT == 0, S
    nt = S // _ROUTE_TT
    T = Bsz * S
    tab = jax.ShapeDtypeStruct((PEER_HEADS, PEER_KEYS, T), jnp.float32)
    return pl.pallas_call(
        _peer_route_body,
        grid=(Bsz, nt),
        in_specs=[pl.BlockSpec((1, _ROUTE_TT, D), lambda b, i: (b, i, 0)),
                  pl.BlockSpec((1, 1, D), lambda b, i: (b, 0, 0)),
                  pl.BlockSpec((1, 1, D), lambda b, i: (b, 0, 0)),
                  pl.BlockSpec(wqT.shape, lambda b, i: (0, 0)),
                  pl.BlockSpec(keys.shape, lambda b, i: (0, 0, 0, 0))],
        out_specs=[pl.BlockSpec((1, _ROUTE_TT, D), lambda b, i: (b, i, 0)),
                   pl.BlockSpec((PEER_HEADS, PEER_KEYS, _ROUTE_TT), lambda b, i: (0, 0, b * nt + i)),
                   pl.BlockSpec((PEER_HEADS, PEER_KEYS, _ROUTE_TT), lambda b, i: (0, 0, b * nt + i)),
                   pl.BlockSpec((PEER_HEADS, 1, _ROUTE_TT), lambda b, i: (0, 0, b * nt + i))],
        out_shape=[jax.ShapeDtypeStruct((Bsz, S, D), jnp.bfloat16), tab,
                   jax.ShapeDtypeStruct(tab.shape, jnp.bfloat16),
                   jax.ShapeDtypeStruct((PEER_HEADS, 1, T), jnp.float32)],
        scratch_shapes=[pltpu.VMEM((PEER_HEADS * PEER_QDIM, _ROUTE_TT), jnp.float32)],
        compiler_params=pltpu.CompilerParams(dimension_semantics=("arbitrary", "arbitrary"),
                                             vmem_limit_bytes=_VMEM_LIMIT),
        name="peer_route",
    )(x, sc, sh, wqT, keys)


_GELU_K = math.sqrt(2.0 / math.pi)


def _gelu_tanh(x):
    half = 0.5 * x
    return half + half * jnp.tanh(x * (_GELU_K + (_GELU_K * 0.044715) * (x * x)))


_PACK = 16


def _peer_dense_body(alpha, hm_ref, e1_ref, e2_ref, pthr_ref, u_ref, vt_ref, x_ref, g_ref, lng_ref, lnb_ref,
                     o_ref, acc_ref, wt_ref, e1b_ref, pthrb_ref):
    c = pl.program_id(2)
    jrows = 32
    bf16 = jnp.bfloat16
    tp = _DENSE_TT // _DENSE_PARTS

    @pl.when(c == 0)
    def _():
        acc_ref[...] = jnp.zeros_like(acc_ref)
        for h in range(PEER_HEADS):
            pthrb_ref[h] = jnp.broadcast_to(pthr_ref[h], (_PACK, _DENSE_TT)).astype(bf16)

    parts = [slice(t * tp, (t + 1) * tp) for t in range(_DENSE_PARTS)]

    def scores(toks):
        return lax.dot_general(u_ref[...], hm_ref[0, toks, :], _NT, preferred_element_type=jnp.float32)

    def build(toks, act):
        for ii in range(_DENSE_EC // PEER_KEYS):
            for h in range(PEER_HEADS):
                e1b_ref[ii % 2, h] = jnp.broadcast_to(e1_ref[h, ii:ii + 1, toks], (_PACK, tp)).astype(bf16)
            for jb in range(PEER_KEYS // jrows):
                r0 = ii * PEER_KEYS + jb * jrows
                gate = jnp.zeros((jrows // _PACK, _PACK, tp), bf16)
                for h in range(PEER_HEADS):
                    e2 = e2_ref[h, jb * jrows:(jb + 1) * jrows, toks].reshape(jrows // _PACK, _PACK, tp)
                    val = e2 * e1b_ref[ii % 2, h][None]
                    gate = gate + jnp.where(val >= pthrb_ref[h, :, toks][None], val, jnp.zeros_like(val))
                gel = _gelu_tanh(act[r0:r0 + jrows, :]).astype(bf16)
                wt_ref[r0:r0 + jrows, toks] = gate.reshape(jrows, tp) * gel
        acc_ref[:, toks] += jnp.dot(vt_ref[...], wt_ref[:, toks], preferred_element_type=jnp.float32)

    acts = {t: scores(parts[t]) for t in range(min(2, _DENSE_PARTS))}
    for t in range(_DENSE_PARTS):
        build(parts[t], acts.pop(t))
        if t + 2 < _DENSE_PARTS:
            acts[t + 2] = scores(parts[t + 2])

    @pl.when(c == pl.num_programs(2) - 1)
    def _():
        y = alpha * x_ref[0] + g_ref[0] * acc_ref[...].T
        mu = jnp.mean(y, -1, keepdims=True)
        yc = y - mu
        var = jnp.mean(yc * yc, -1, keepdims=True)
        o_ref[0] = yc * lax.rsqrt(var + LN_EPS) * lng_ref[...] + lnb_ref[...]


def _peer_dense(hm, e1, e2, pthr, u_bf, vt_bf, x, g, ln_g, ln_b, alpha):
    Bsz, S, D = x.shape
    assert S % _DENSE_TT == 0, S
    nt = S // _DENSE_TT
    nchunk = N_EXPERTS // _DENSE_EC
    rows_i = _DENSE_EC // PEER_KEYS
    return pl.pallas_call(
        functools.partial(_peer_dense_body, alpha),
        grid=(Bsz, nt, nchunk),
        in_specs=[pl.BlockSpec((1, _DENSE_TT, D), lambda b, i, c: (b, i, 0)),
                  pl.BlockSpec((PEER_HEADS, rows_i, _DENSE_TT), lambda b, i, c: (0, c, b * nt + i)),
                  pl.BlockSpec((PEER_HEADS, PEER_KEYS, _DENSE_TT), lambda b, i, c: (0, 0, b * nt + i)),
                  pl.BlockSpec((PEER_HEADS, 1, _DENSE_TT), lambda b, i, c: (0, 0, b * nt + i)),
                  pl.BlockSpec((_DENSE_EC, D), lambda b, i, c: (c, 0)),
                  pl.BlockSpec((D, _DENSE_EC), lambda b, i, c: (0, c)),
                  pl.BlockSpec((1, _DENSE_TT, D), lambda b, i, c: (b, i, 0)),
                  pl.BlockSpec((1, 1, D), lambda b, i, c: (b, 0, 0)),
                  pl.BlockSpec((1, D), lambda b, i, c: (0, 0)),
                  pl.BlockSpec((1, D), lambda b, i, c: (0, 0))],
        out_specs=pl.BlockSpec((1, _DENSE_TT, D), lambda b, i, c: (b, i, 0)),
        out_shape=jax.ShapeDtypeStruct((Bsz, S, D), jnp.float32),
        scratch_shapes=[pltpu.VMEM((D, _DENSE_TT), jnp.float32),
                        pltpu.VMEM((_DENSE_EC, _DENSE_TT), jnp.bfloat16),
                        pltpu.VMEM((2, PEER_HEADS, _PACK, _DENSE_TT // _DENSE_PARTS), jnp.bfloat16),
                        pltpu.VMEM((PEER_HEADS, _PACK, _DENSE_TT), jnp.bfloat16)],
        compiler_params=pltpu.CompilerParams(dimension_semantics=("arbitrary", "arbitrary", "arbitrary"),
                                             vmem_limit_bytes=_VMEM_LIMIT),
        name="peer_dense",
    )(hm, e1, e2, pthr, u_bf, vt_bf, x, g, ln_g.reshape(1, D), ln_b.reshape(1, D))


def _peer_weights(wq, sub_keys, u_tab, v_tab):
    return (wq.T.astype(jnp.bfloat16), sub_keys.astype(jnp.bfloat16),
            u_tab.astype(jnp.bfloat16), v_tab.T.astype(jnp.bfloat16))


def _peer_ln(x, sc, sh, g, pw, ln_g, ln_b, alpha):
    wqT, keys, u_bf, vt_bf = pw
    hm, e1, e2, pthr = _peer_route(x, sc, sh, wqT, keys)
    return _peer_dense(hm, e1, e2, pthr, u_bf, vt_bf, x, g, ln_g, ln_b, alpha)


def kernel(x, c, ctx, c_ctx, w_ada, b_ada, w_in, w_out, ln1_g, ln1_b, ln2_g, ln2_b,
           conf_dw_w, conf_dw_b, conf_norm_g, conf_norm_b, na_rpb, hy_short_w, hy_short_b,
           hy_w1, hy_b1, hy_w2, hy_b2, hy_w3, hy_decay, hy_bias, ssd_conv_w, ssd_conv_b,
           ssd_a_log, ssd_dt_bias, ssd_d, ssd_norm_g, peer_wq, peer_keys, peer_u, peer_v):
    alpha = (2.0 * DEPTH) ** 0.25
    s_c = jax.nn.silu(c)
    s_cc = jax.nn.silu(c_ctx)
    xc = ctx
    Bsz, Lc, D = ctx.shape
    for l in range(DEPTH):
        ctx_out = l < DEPTH - 1
        mod = (s_c @ w_ada[l] + b_ada[l])[:, None, :]
        mod_c = jnp.broadcast_to((s_cc @ w_ada[l] + b_ada[l])[None, None, :], mod.shape)
        sh1, sc1, g1, sh2, sc2, g2 = jnp.split(mod, 6, -1)
        sh1c, sc1c, g1c, sh2c, sc2c, g2c = jnp.split(mod_c, 6, -1)
        x, xc = _mixer_ln(
            x, xc, (sh1, sc1, g1), (sh1c, sc1c, g1c), ctx_out, alpha, w_in[l], w_out[l], ln1_g[l], ln1_b[l],
            (conf_dw_w[l], conf_dw_b[l], conf_norm_g[l], conf_norm_b[l]), na_rpb[l],
            (hy_short_w[l], hy_short_b[l], hy_w1[l], hy_b1[l], hy_w2[l], hy_b2[l], hy_w3[l], hy_decay[l], hy_bias[l]),
            (ssd_conv_w[l], ssd_conv_b[l], ssd_a_log[l], ssd_dt_bias[l], ssd_d[l], ssd_norm_g[l]))
        pw = _peer_weights(peer_wq[l], peer_keys[l], peer_u[l], peer_v[l])
        x = _peer_ln(x, sc2, sh2, g2, pw, ln2_g[l], ln2_b[l], alpha)
        if ctx_out:
            xc = _peer_ln(xc.reshape(1, Bsz * Lc, D), sc2c[:1], sh2c[:1], g2c[:1], pw, ln2_g[l], ln2_b[l],
                          alpha).reshape(Bsz, Lc, D)
    return x
```

```python
import functools
import math
import jax, jax.numpy as jnp
from jax import lax
from jax.experimental import pallas as pl
from jax.experimental.pallas import tpu as pltpu

D_MODEL = 1024
DEPTH = 2

GRID_W = 64
N_MIXERS = 4
GROUP_W = D_MODEL // N_MIXERS
D_MIX = N_MIXERS * GROUP_W
LN_EPS = 1e-5
CONF_K = 31
CONF_GROUPS = 4
NA_HEADS = 4
HEAD_DIM = GROUP_W // NA_HEADS
NA_KH = 8
NA_KW = 16
ROPE_BASE = 10000.0
HY_SHORT = 3
HY_BANDS = 16
HY_EMB = 1 + 2 * HY_BANDS
HY_HIDDEN = 64
HY_SIN_FREQ = 1.0
SSD_HEADS = 4
SSD_HEAD_DIM = GROUP_W // SSD_HEADS
SSD_GROUPS = 2
SSD_STATE = 64
SSD_CONV = 3
SSD_CHUNK = 128
XBC_W = GROUP_W + 2 * SSD_GROUPS * SSD_STATE
IN_COLS = 2 * GROUP_W + 3 * GROUP_W + 3 * GROUP_W + GROUP_W + XBC_W + 2 * SSD_HEADS
PEER_HEADS = 8
PEER_KEYS = 128
PEER_TOPK = 16
PEER_QDIM = 256
N_EXPERTS = PEER_KEYS * PEER_KEYS

_LANES = 128
_SUBLANES = 8
_VMEM_LIMIT = 56 * 1024 * 1024
_IN_PAD = -(-IN_COLS // _LANES) * _LANES


_IN_SPLITS = ((0, 2 * GROUP_W), (2 * GROUP_W, 5 * GROUP_W), (5 * GROUP_W, 8 * GROUP_W),
              (8 * GROUP_W, 9 * GROUP_W + XBC_W), (9 * GROUP_W + XBC_W, _IN_PAD))
_PROJ_TM = 512


def _in_proj_body(x_ref, sc_ref, sh_ref, w_ref, *o_refs):
    h = (x_ref[0] * (1.0 + sc_ref[0]) + sh_ref[0]).astype(jnp.bfloat16)
    for (lo, hi), o_ref in zip(_IN_SPLITS, o_refs):
        o_ref[0] = jnp.dot(h, w_ref[:, lo:hi], preferred_element_type=jnp.float32)


def _in_proj(x, sc, sh, w_in):
    Bsz, L, D = x.shape
    tm = min(_PROJ_TM, L)
    w = jnp.pad(w_in, ((0, 0), (0, _IN_PAD - IN_COLS))).astype(jnp.bfloat16)
    return pl.pallas_call(
        _in_proj_body,
        grid=(Bsz, L // tm),
        in_specs=[pl.BlockSpec((1, tm, D), lambda b, i: (b, i, 0)),
                  pl.BlockSpec((1, 1, D), lambda b, i: (b, 0, 0)),
                  pl.BlockSpec((1, 1, D), lambda b, i: (b, 0, 0)),
                  pl.BlockSpec((D, _IN_PAD), lambda b, i: (0, 0))],
        out_specs=[pl.BlockSpec((1, tm, hi - lo), lambda b, i: (b, i, 0)) for lo, hi in _IN_SPLITS],
        out_shape=[jax.ShapeDtypeStruct((Bsz, L, hi - lo), jnp.float32) for lo, hi in _IN_SPLITS],
        compiler_params=pltpu.CompilerParams(dimension_semantics=("arbitrary", "arbitrary"),
                                             vmem_limit_bytes=_VMEM_LIMIT),
        name="in_proj",
    )(x, sc, sh, w)


def _out_proj_ln_body(alpha, ya_ref, yb_ref, yc_ref, yd_ref, w_ref, x_ref, g_ref, lng_ref, lnb_ref, o_ref):
    y = 0.0
    for m, y_ref in enumerate((ya_ref, yb_ref, yc_ref, yd_ref)):
        y = y + jnp.dot(y_ref[0].astype(jnp.bfloat16), w_ref[m * GROUP_W:(m + 1) * GROUP_W, :],
                        preferred_element_type=jnp.float32)
    r = alpha * x_ref[0] + g_ref[0] * y
    mu = jnp.mean(r, -1, keepdims=True)
    rc = r - mu
    var = jnp.mean(rc * rc, -1, keepdims=True)
    o_ref[0] = rc * lax.rsqrt(var + LN_EPS) * lng_ref[...] + lnb_ref[...]


def _out_proj_ln(ys, w_out, x, g, ln_g, ln_b, alpha):
    Bsz, L, D = x.shape
    tm = min(_PROJ_TM, L)
    mix = pl.BlockSpec((1, tm, GROUP_W), lambda b, i: (b, i, 0))
    return pl.pallas_call(
        functools.partial(_out_proj_ln_body, alpha),
        grid=(Bsz, L // tm),
        in_specs=[mix, mix, mix, mix,
                  pl.BlockSpec((D_MIX, D), lambda b, i: (0, 0)),
                  pl.BlockSpec((1, tm, D), lambda b, i: (b, i, 0)),
                  pl.BlockSpec((1, 1, D), lambda b, i: (b, 0, 0)),
                  pl.BlockSpec((1, D), lambda b, i: (0, 0)),
                  pl.BlockSpec((1, D), lambda b, i: (0, 0))],
        out_specs=pl.BlockSpec((1, tm, D), lambda b, i: (b, i, 0)),
        out_shape=jax.ShapeDtypeStruct((Bsz, L, D), jnp.float32),
        compiler_params=pltpu.CompilerParams(dimension_semantics=("arbitrary", "arbitrary"),
                                             vmem_limit_bytes=_VMEM_LIMIT),
        name="out_proj_ln",
    )(*ys, w_out.astype(jnp.bfloat16), x, g, ln_g.reshape(1, D), ln_b.reshape(1, D))


def _shifted_taps(win, first, n_taps, rows):
    n = win.shape[0]
    rolled = {0: win}
    taps = []
    for k in range(n_taps):
        r = (first + k) % _SUBLANES
        if r not in rolled:
            rolled[r] = pltpu.roll(win, n - r, 0)
        base = first + k - r
        taps.append(rolled[r][base:base + rows])
    return taps


_CONV_HALO = 16
_CONV_TILE = 256


def _group_mean_matrix(width, group):
    r = lax.broadcasted_iota(jnp.int32, (width, width), 0) // group
    c = lax.broadcasted_iota(jnp.int32, (width, width), 1) // group
    return jnp.where(r == c, 1.0 / group, 0.0).astype(jnp.float32)


def _conformer_body(L, p_ref, w_ref, b_ref, ng_ref, nb_ref, o_ref, u_ref):
    G = GROUP_W
    pad = (CONF_K - 1) // 2
    halo = jnp.zeros((_CONV_HALO, G), jnp.float32)
    u_ref[0:_CONV_HALO, :] = halo
    u_ref[_CONV_HALO + L:_CONV_HALO + L + _CONV_HALO, :] = halo
    u_ref[_CONV_HALO:_CONV_HALO + L, :] = p_ref[0, :, 0:G] * jax.nn.sigmoid(p_ref[0, :, G:2 * G])
    avg = _group_mean_matrix(G, G // CONF_GROUPS)
    tile = min(_CONV_TILE, L)
    for t in range(L // tile):
        win = u_ref[t * tile:t * tile + tile + 2 * _CONV_HALO, :]
        acc = jnp.zeros((tile, G), jnp.float32) + b_ref[...]
        for k, tap in enumerate(_shifted_taps(win, _CONV_HALO - pad, CONF_K, tile)):
            acc = acc + tap * w_ref[k:k + 1, :]
        mu = jnp.dot(acc, avg, preferred_element_type=jnp.float32, precision=lax.Precision.HIGHEST)
        cen = acc - mu
        var = jnp.dot(cen * cen, avg, preferred_element_type=jnp.float32, precision=lax.Precision.HIGHEST)
        un = cen * lax.rsqrt(var + LN_EPS) * ng_ref[...] + nb_ref[...]
        o_ref[0, t * tile:(t + 1) * tile, :] = un * jax.nn.sigmoid(un)


def _conformer(pa, dw_w, dw_b, n_g, n_b):
    Bsz, L, _ = pa.shape
    G = GROUP_W
    vec = pl.BlockSpec((1, G), lambda b: (0, 0))
    return pl.pallas_call(
        functools.partial(_conformer_body, L),
        grid=(Bsz,),
        in_specs=[pl.BlockSpec((1, L, 2 * G), lambda b: (b, 0, 0)),
                  pl.BlockSpec((CONF_K, G), lambda b: (0, 0)), vec, vec, vec],
        out_specs=pl.BlockSpec((1, L, G), lambda b: (b, 0, 0)),
        out_shape=jax.ShapeDtypeStruct((Bsz, L, G), jnp.float32),
        scratch_shapes=[pltpu.VMEM((L + 2 * _CONV_HALO, G), jnp.float32)],
        compiler_params=pltpu.CompilerParams(dimension_semantics=("arbitrary",),
                                             vmem_limit_bytes=_VMEM_LIMIT),
        name="conformer",
    )(pa, dw_w, dw_b.reshape(1, G), n_g.reshape(1, G), n_b.reshape(1, G))


def axial_rope(rows, head_dim):
    n_f = head_dim // 4
    inv = ROPE_BASE ** (-jnp.arange(n_f, dtype=jnp.float32) / n_f)
    t = jnp.arange(rows * GRID_W)
    r = (t // GRID_W).astype(jnp.float32)
    col = (t % GRID_W).astype(jnp.float32)
    ang = jnp.concatenate([r[:, None] * inv, col[:, None] * inv], -1)
    return jnp.cos(ang), jnp.sin(ang)


_NA_MASK = -1e30


def _na_tables(rows, rpb):
    cos, sin = axial_rope(rows, HEAD_DIM)
    cos_f = jnp.tile(cos, (1, 2 * NA_HEADS))
    sin_s = jnp.tile(jnp.concatenate([-sin, sin], -1), (1, NA_HEADS))
    cq = jnp.arange(GRID_W)
    cs = jnp.clip(cq - NA_KW // 2, 0, GRID_W - NA_KW)
    col = jnp.arange(GRID_W)
    in_band = (col[None, :] >= cs[:, None]) & (col[None, :] < cs[:, None] + NA_KW)
    cb_idx = jnp.clip(col[None, :] - cq[:, None] + (NA_KW - 1), 0, 2 * NA_KW - 2)
    po = jnp.arange(NA_KH)
    rb_idx = jnp.arange(NA_KH)[None, :] - po[:, None] + (NA_KH - 1)
    bias = rpb[:, rb_idx][:, :, :, cb_idx]
    bias = jnp.where(in_band[None, None, None], bias, _NA_MASK)
    bias = bias.transpose(1, 0, 3, 2, 4).reshape(NA_KH, NA_HEADS, GRID_W, NA_KH * GRID_W)
    return cos_f, sin_s, bias


def _rope_lanes(x, cos_f, sin_s):
    n = x.shape[-1]
    hd = HEAD_DIM // 2
    first = (lax.broadcasted_iota(jnp.int32, x.shape, 1) % HEAD_DIM) < hd
    partner = jnp.where(first, pltpu.roll(x, n - hd, 1), pltpu.roll(x, hd, 1))
    return x * cos_f + partner * sin_s


def _softmax_pv(s_parts, v_parts):
    m = s_parts[0].max(-1, keepdims=True)
    for s in s_parts[1:]:
        m = jnp.maximum(m, s.max(-1, keepdims=True))
    l = 0.0
    o = 0.0
    for s, v in zip(s_parts, v_parts):
        p = jnp.exp(s - m)
        l = l + p.sum(-1, keepdims=True)
        o = o + jnp.dot(p.astype(jnp.bfloat16), v, preferred_element_type=jnp.float32)
    return o / l


_NA_ROWS = 4


def _na_body(rows, q_ref, k_ref, v_ref, kc_ref, vc_ref, cosq_ref, sinq_ref, cosk_ref, sink_ref, *rest):
    bias_refs, (o_ref, krot_ref, vbf_ref) = rest[:_NA_ROWS], rest[_NA_ROWS:]
    step = pl.program_id(1)
    scale = HEAD_DIM ** -0.5

    @pl.when(step == 0)
    def _():
        krot_ref[...] = _rope_lanes(k_ref[0], cosk_ref[...], sink_ref[...]).astype(jnp.bfloat16)
        vbf_ref[...] = v_ref[0].astype(jnp.bfloat16)

    win = NA_KH * GRID_W
    q = q_ref[0] * scale
    q_rot = _rope_lanes(q, cosq_ref[...], sinq_ref[...]).astype(jnp.bfloat16)
    q_plain = q.astype(jnp.bfloat16)
    kc = kc_ref[0].astype(jnp.bfloat16)
    vc = vc_ref[0].astype(jnp.bfloat16)
    for i in range(_NA_ROWS):
        r = step * _NA_ROWS + i
        rs = jnp.clip(r - NA_KH // 2, 0, rows - NA_KH)
        start = pl.multiple_of(rs * GRID_W, GRID_W)
        kw = krot_ref[pl.ds(start, win), :]
        vw = vbf_ref[pl.ds(start, win), :]
        qs = slice(i * GRID_W, (i + 1) * GRID_W)
        outs = []
        for h in range(NA_HEADS):
            hs = slice(h * HEAD_DIM, (h + 1) * HEAD_DIM)
            s_loc = lax.dot_general(q_rot[qs, hs], kw[:, hs], _NT, preferred_element_type=jnp.float32) + bias_refs[i][0, h]
            s_ctx = lax.dot_general(q_plain[qs, hs], kc[:, hs], _NT, preferred_element_type=jnp.float32)
            outs.append(_softmax_pv([s_loc, s_ctx], [vw[:, hs], vc[:, hs]]))
        o_ref[0, qs, :] = jnp.concatenate(outs, axis=-1)


def _na_attention(pb, cb, rpb):
    Bsz, S, _ = pb.shape
    Lc = cb.shape[1]
    rows = S // GRID_W
    assert rows % _NA_ROWS == 0, rows
    G = GROUP_W
    nq = _NA_ROWS * GRID_W
    cos_f, sin_s, bias = _na_tables(rows, rpb)

    def bias_spec(i):
        def idx(b, step):
            r = step * _NA_ROWS + i
            return (r - jnp.clip(r - NA_KH // 2, 0, rows - NA_KH), 0, 0, 0)
        return pl.BlockSpec((1, NA_HEADS, GRID_W, NA_KH * GRID_W), idx)

    return pl.pallas_call(
        functools.partial(_na_body, rows),
        grid=(Bsz, rows // _NA_ROWS),
        in_specs=[pl.BlockSpec((1, nq, G), lambda b, r: (b, r, 0)),
                  pl.BlockSpec((1, S, G), lambda b, r: (b, 0, 1)),
                  pl.BlockSpec((1, S, G), lambda b, r: (b, 0, 2)),
                  pl.BlockSpec((1, Lc, G), lambda b, r: (b, 0, 1)),
                  pl.BlockSpec((1, Lc, G), lambda b, r: (b, 0, 2)),
                  pl.BlockSpec((nq, G), lambda b, r: (r, 0)),
                  pl.BlockSpec((nq, G), lambda b, r: (r, 0)),
                  pl.BlockSpec((S, G), lambda b, r: (0, 0)),
                  pl.BlockSpec((S, G), lambda b, r: (0, 0))] + [bias_spec(i) for i in range(_NA_ROWS)],
        out_specs=pl.BlockSpec((1, nq, G), lambda b, r: (b, r, 0)),
        out_shape=jax.ShapeDtypeStruct((Bsz, S, G), jnp.float32),
        scratch_shapes=[pltpu.VMEM((S, G), jnp.bfloat16), pltpu.VMEM((S, G), jnp.bfloat16)],
        compiler_params=pltpu.CompilerParams(dimension_semantics=("arbitrary", "arbitrary"),
                                             vmem_limit_bytes=_VMEM_LIMIT),
        name="na_attention",
    )(pb, pb, pb, cb, cb, cos_f, sin_s, cos_f, sin_s, *([bias] * _NA_ROWS))


def _ctx_attn_body(q_ref, k_ref, v_ref, o_ref):
    q = (q_ref[0] * HEAD_DIM ** -0.5).astype(jnp.bfloat16)
    k = k_ref[0].astype(jnp.bfloat16)
    v = v_ref[0].astype(jnp.bfloat16)
    outs = []
    for h in range(NA_HEADS):
        hs = slice(h * HEAD_DIM, (h + 1) * HEAD_DIM)
        s = lax.dot_general(q[:, hs], k[:, hs], _NT, preferred_element_type=jnp.float32)
        outs.append(_softmax_pv([s], [v[:, hs]]))
    o_ref[0] = jnp.concatenate(outs, axis=-1)


def _ctx_attention(cb):
    Bsz, Lc, _ = cb.shape
    G = GROUP_W
    return pl.pallas_call(
        _ctx_attn_body,
        grid=(Bsz,),
        in_specs=[pl.BlockSpec((1, Lc, G), lambda b: (b, 0, 0)),
                  pl.BlockSpec((1, Lc, G), lambda b: (b, 0, 1)),
                  pl.BlockSpec((1, Lc, G), lambda b: (b, 0, 2))],
        out_specs=pl.BlockSpec((1, Lc, G), lambda b: (b, 0, 0)),
        out_shape=jax.ShapeDtypeStruct((Bsz, Lc, G), jnp.float32),
        compiler_params=pltpu.CompilerParams(dimension_semantics=("arbitrary",)),
        name="ctx_attention",
    )(cb, cb, cb)


_HY_TB = 256
_HY_CB = 8


def _hyena_filter_body(L, w1_ref, b1_ref, w2_ref, b2_ref, w3_ref, dec_ref, o_ref):
    G = GROUP_W
    hp = lax.Precision.HIGHEST
    p = lax.broadcasted_iota(jnp.int32, (2 * L, _LANES), 0)
    lane = lax.broadcasted_iota(jnp.int32, (2 * L, _LANES), 1)
    tn = jnp.abs(p - L).astype(jnp.float32) / L
    band = ((lane - 1) % HY_BANDS + 1).astype(jnp.float32)
    ang = 2.0 * math.pi * band * tn
    z = jnp.where(lane == 0, tn, jnp.where(lane <= HY_BANDS, jnp.sin(ang), jnp.cos(ang)))
    z = jnp.where(lane < HY_EMB, z, 0.0)
    h = jnp.sin(HY_SIN_FREQ * (jnp.dot(z, w1_ref[...], precision=hp, preferred_element_type=jnp.float32) + b1_ref[...]))
    h = jnp.sin(HY_SIN_FREQ * (jnp.dot(h, w2_ref[...], precision=hp, preferred_element_type=jnp.float32) + b2_ref[...]))
    k = jnp.dot(h, w3_ref[...], precision=hp, preferred_element_type=jnp.float32) * jnp.exp(-tn[:, 0:1] * dec_ref[...])
    kf, kb = k[:, 0:G], k[:, G:2 * G]
    n = lax.broadcasted_iota(jnp.int32, (2 * L, G), 0) - L
    nf = jnp.sum(jnp.where(n >= 0, jnp.abs(kf), 0.0), axis=0, keepdims=True) + 1e-6
    nb = jnp.sum(jnp.where(n <= 0, jnp.where(n > -L, jnp.abs(kb), 0.0), 0.0), axis=0, keepdims=True) + 1e-6
    rev = jnp.where(n > 0, kb / nb, jnp.where(n > -L, kf / nf, 0.0))
    o_ref[...] = rev.T


def _hyena_filters_rev(L, w1, b1, w2, b2, w3, decay):
    G = GROUP_W
    hp = _LANES - HY_HIDDEN
    w1p = jnp.pad(w1.astype(jnp.float32), ((0, _LANES - HY_EMB), (0, hp)))
    args = (w1p, jnp.pad(b1.reshape(1, -1), ((0, 0), (0, hp))), jnp.pad(w2, ((0, hp), (0, hp))),
            jnp.pad(b2.reshape(1, -1), ((0, 0), (0, hp))), jnp.pad(w3, ((0, hp), (0, 0))), decay.reshape(1, -1))
    return pl.pallas_call(
        functools.partial(_hyena_filter_body, L),
        out_shape=jax.ShapeDtypeStruct((G, 2 * L), jnp.float32),
        compiler_params=pltpu.CompilerParams(vmem_limit_bytes=_VMEM_LIMIT),
        name="hyena_filters",
    )(*[a.astype(jnp.float32) for a in args])


def _hyena_pre_body(L, p_ref, w_ref, b_ref, u_ref, ut_ref, x0_ref, xp_ref):
    G = GROUP_W
    W = 3 * G
    halo = jnp.zeros((_CONV_HALO, W), jnp.float32)
    xp_ref[0:_CONV_HALO, :] = halo
    xp_ref[_CONV_HALO + L:_CONV_HALO + L + _CONV_HALO, :] = halo
    xp_ref[_CONV_HALO:_CONV_HALO + L, :] = p_ref[0]
    pad = (HY_SHORT - 1) // 2
    tile = min(_CONV_TILE, L)
    for t in range(L // tile):
        win = xp_ref[t * tile:t * tile + tile + 2 * _CONV_HALO, :]
        acc = jnp.zeros((tile, W), jnp.float32) + b_ref[...]
        for k, tap in enumerate(_shifted_taps(win, _CONV_HALO - pad, HY_SHORT, tile)):
            acc = acc + tap * w_ref[k:k + 1, :]
        rows = slice(t * tile, (t + 1) * tile)
        u = acc[:, 2 * G:3 * G] * acc[:, G:2 * G]
        x0_ref[0, rows, :] = acc[:, 0:G]
        u_ref[0, rows, :] = u
        ut_ref[0, :, t, :] = u.T


def _hyena_pre(py, short_w, short_b):
    Bsz, L, W = py.shape
    G = GROUP_W
    f32 = jnp.float32
    TB = min(_HY_TB, L)
    assert TB == min(_CONV_TILE, L)
    return pl.pallas_call(
        functools.partial(_hyena_pre_body, L),
        grid=(Bsz,),
        in_specs=[pl.BlockSpec((1, L, W), lambda b: (b, 0, 0)),
                  pl.BlockSpec((HY_SHORT, W), lambda b: (0, 0)),
                  pl.BlockSpec((1, W), lambda b: (0, 0))],
        out_specs=[pl.BlockSpec((1, L, G), lambda b: (b, 0, 0)),
                   pl.BlockSpec((1, G, L // TB, TB), lambda b: (b, 0, 0, 0)),
                   pl.BlockSpec((1, L, G), lambda b: (b, 0, 0))],
        out_shape=[jax.ShapeDtypeStruct((Bsz, L, G), f32), jax.ShapeDtypeStruct((Bsz, G, L // TB, TB), f32),
                   jax.ShapeDtypeStruct((Bsz, L, G), f32)],
        scratch_shapes=[pltpu.VMEM((L + 2 * _CONV_HALO, W), f32)],
        compiler_params=pltpu.CompilerParams(dimension_semantics=("arbitrary",), vmem_limit_bytes=_VMEM_LIMIT),
        name="hyena_pre",
    )(py, short_w, short_b.reshape(1, W))


def _toeplitz_tile(rolled, q0):
    TB = _HY_TB
    per = _LANES // _SUBLANES
    i = lax.broadcasted_iota(jnp.int32, (_SUBLANES, _LANES), 0)
    l = lax.broadcasted_iota(jnp.int32, (_SUBLANES, _LANES), 1)
    row_blocks = []
    for rg in range(TB // _SUBLANES):
        k = rg % per
        pieces = []
        for lg in range(TB // _LANES):
            o = TB + _LANES * lg - _SUBLANES * rg
            q, rho = divmod(o, _LANES)
            if rho == 0:
                pieces.append(jnp.where(l - i < 0, rolled(q0 + q - 1, k), rolled(q0 + q, k)))
            else:
                pieces.append(jnp.where(l + rho - i >= _LANES, rolled(q0 + q + 1, k), rolled(q0 + q, k)))
        row_blocks.append(jnp.concatenate(pieces, axis=1))
    return jnp.concatenate(row_blocks, axis=0).astype(jnp.bfloat16)


def _hyena_conv_body(L, Bsz, g_ref, u_ref, o_ref):
    TB = min(_HY_TB, L)
    nb = L // TB
    cols = Bsz * nb
    lane = lax.broadcasted_iota(jnp.int32, (TB, cols), 1) % nb

    def channel(ci, carry):
        u = u_ref[ci] if nb == 1 else u_ref[:, ci].reshape(cols, TB)
        u = u.astype(jnp.bfloat16)
        acc = jnp.zeros((TB, cols), jnp.float32)
        cache = {}
        g_row = g_ref[pl.ds(ci, 1), :]

        def rolled(q, k):
            if (q, k) not in cache:
                blk = jnp.broadcast_to(g_row[:, q * _LANES:(q + 1) * _LANES], (_SUBLANES, _LANES))
                cache[q, k] = pltpu.roll(blk, (_SUBLANES * k) % _LANES, 1, stride=1, stride_axis=0)
            return cache[q, k]

        for d in range(-(nb - 1), nb):
            start = L - TB * d - TB
            tile = _toeplitz_tile(rolled, start // _LANES)
            z = lax.dot_general(tile, u, _NT, preferred_element_type=jnp.float32)
            if d != 0:
                z = jnp.where((lane - d >= 0) & (lane - d < nb), pltpu.roll(z, d % cols, 1), 0.0)
            acc = acc + z
        if nb == 1:
            o_ref[ci] = acc.T
        else:
            o_ref[:, ci] = acc.T.reshape(Bsz, nb, TB)
        return carry

    lax.fori_loop(0, _HY_CB, channel, 0)


def _hyena_conv(g_rev, u_t):
    Bsz, G, nb, TB = u_t.shape
    L = nb * TB
    if nb == 1:
        blk = pl.BlockSpec((_HY_CB, Bsz, TB), lambda c: (c, 0, 0))
        operand, out_shape = jnp.swapaxes(u_t.reshape(Bsz, G, TB), 0, 1), (G, Bsz, TB)
    else:
        blk = pl.BlockSpec((Bsz, _HY_CB, nb, TB), lambda c: (0, c, 0, 0))
        operand, out_shape = u_t, (Bsz, G, nb, TB)
    out = pl.pallas_call(
        functools.partial(_hyena_conv_body, L, Bsz),
        grid=(G // _HY_CB,),
        in_specs=[pl.BlockSpec((_HY_CB, 2 * L), lambda c: (c, 0)), blk],
        out_specs=blk,
        out_shape=jax.ShapeDtypeStruct(out_shape, jnp.float32),
        compiler_params=pltpu.CompilerParams(dimension_semantics=("arbitrary",), vmem_limit_bytes=_VMEM_LIMIT),
        name="hyena_conv",
    )(g_rev, operand)
    return jnp.swapaxes(out, 0, 1).reshape(Bsz, G, nb, TB) if nb == 1 else out


def _hyena_post_body(yt_ref, u_ref, x0_ref, skip_ref, o_ref):
    nb, TB = yt_ref.shape[2], yt_ref.shape[3]
    for t in range(nb):
        rows = slice(t * TB, (t + 1) * TB)
        o_ref[0, rows, :] = (yt_ref[0, :, t, :].T + u_ref[0, rows, :] * skip_ref[...]) * x0_ref[0, rows, :]


def _hyena_post(y_t, u, x0, skip):
    Bsz, L, G = u.shape
    tok = pl.BlockSpec((1, L, G), lambda b: (b, 0, 0))
    return pl.pallas_call(
        _hyena_post_body,
        grid=(Bsz,),
        in_specs=[pl.BlockSpec((1,) + y_t.shape[1:], lambda b: (b, 0, 0, 0)), tok, tok,
                  pl.BlockSpec((1, G), lambda b: (0, 0))],
        out_specs=tok,
        out_shape=jax.ShapeDtypeStruct((Bsz, L, G), jnp.float32),
        compiler_params=pltpu.CompilerParams(dimension_semantics=("arbitrary",), vmem_limit_bytes=_VMEM_LIMIT),
        name="hyena_post",
    )(y_t, u, x0, skip.reshape(1, G))


def _hyena(py, short_w, short_b, w1, b1, w2, b2, w3, decay, skip):
    L = py.shape[1]
    g_rev = _hyena_filters_rev(L, w1, b1, w2, b2, w3, decay)
    u, u_t, x0 = _hyena_pre(py, short_w, short_b)
    return _hyena_post(_hyena_conv(g_rev, u_t), u, x0, skip)


def _split3_dot(a, b_bf16, dims=None):
    hi = a.astype(jnp.bfloat16)
    r1 = a - hi.astype(jnp.float32)
    mid = r1.astype(jnp.bfloat16)
    lo = (r1 - mid.astype(jnp.float32)).astype(jnp.bfloat16)
    out = 0.0
    for part in (hi, mid, lo):
        if dims is None:
            out = out + jnp.dot(part, b_bf16, preferred_element_type=jnp.float32)
        else:
            out = out + jnp.dot(b_bf16, part, preferred_element_type=jnp.float32)
    return out


def _softplus(x):
    return jnp.maximum(x, 0.0) + jnp.log(1.0 + jnp.exp(-jnp.abs(x)))


def _ssd_body(L, zx_ref, dtc_ref, cw_ref, cb_ref, arow_ref, acol_ref, brow_ref, bcol_ref, dsk_ref,
              ng_ref, init_ref, o_ref, fin_ref, xp_ref, xc_ref, bt_ref, y_ref, ccol_ref, crow_ref, edec_ref,
              tot_ref):
    G = GROUP_W
    Q = SSD_CHUNK
    nc = L // Q
    P = SSD_HEAD_DIM
    N = SSD_STATE
    H = SSD_HEADS
    f32 = jnp.float32
    bf16 = jnp.bfloat16
    halo = jnp.zeros((_CONV_HALO, XBC_W), f32)
    xp_ref[0:_CONV_HALO, :] = halo
    xp_ref[_CONV_HALO + L:_CONV_HALO + L + _CONV_HALO, :] = halo
    xp_ref[_CONV_HALO:_CONV_HALO + L, :] = zx_ref[0, :, G:G + XBC_W]
    pad = (SSD_CONV - 1) // 2
    tile = min(_CONV_TILE, L)
    for t in range(L // tile):
        win = xp_ref[t * tile:t * tile + tile + 2 * _CONV_HALO, :]
        acc = jnp.zeros((tile, XBC_W), f32) + cb_ref[...]
        for k, tap in enumerate(_shifted_taps(win, _CONV_HALO - pad, SSD_CONV, tile)):
            acc = acc + tap * cw_ref[k:k + 1, :]
        xc_ref[t * tile:(t + 1) * tile, :] = acc * jax.nn.sigmoid(acc)
    for c in range(nc):
        bt_ref[c] = xc_ref[c * Q:(c + 1) * Q, G:G + SSD_GROUPS * N].T
    dt_col = _softplus(dtc_ref[0] + brow_ref[...])
    a_col = dt_col * arow_ref[...]
    dt_rows = dtc_ref[0].T[0:2 * H]
    a_row = _softplus(dt_rows + bcol_ref[...]) * acol_ref[...]
    a_stack = jnp.concatenate([a_row[:, c * Q:(c + 1) * Q] for c in range(nc)], axis=0)
    ri = lax.broadcasted_iota(jnp.int32, (Q, Q), 0)
    ci = lax.broadcasted_iota(jnp.int32, (Q, Q), 1)
    one = lambda m: jnp.where(m, 1.0, 0.0).astype(bf16)
    tot_ref[...] = _split3_dot(a_stack, jnp.ones((Q, Q), bf16))

    def direction(d, y_store):
        fwd = d == 0
        m_col = one(ci <= ri) if fwd else one(ci >= ri)
        for c in range(nc):
            ccol_ref[c * Q:(c + 1) * Q, :] = _split3_dot(a_col[c * Q:(c + 1) * Q, :], m_col, dims="left")
        crow_ref[...] = _split3_dot(a_stack, one(ri <= ci) if fwd else one(ri >= ci))
        edec_ref[...] = _split3_dot(a_stack, one(ri > ci) if fwd else one(ri < ci))
        keep = (ri >= ci) if fwd else (ri <= ci)

        def chunk(step, states):
            c = step if fwd else nc - 1 - step
            r0 = pl.multiple_of(c * Q, Q)
            j0 = pl.multiple_of(c * 2 * H, 2 * H)
            xc = xc_ref[pl.ds(r0, Q), :]
            ccol = ccol_ref[pl.ds(r0, Q), :]
            crow = crow_ref[pl.ds(j0, 2 * H), :]
            edec = edec_ref[pl.ds(j0, 2 * H), :]
            tot = tot_ref[pl.ds(j0, 2 * H), :]
            dtc = _softplus(dtc_ref[0, pl.ds(r0, Q), :] + brow_ref[...])
            bt = bt_ref[c]
            new_states = []
            outs = []
            for g in range(SSD_GROUPS):
                cm = xc[:, G + SSD_GROUPS * N + g * N:G + SSD_GROUPS * N + (g + 1) * N].astype(bf16)
                bm = xc[:, G + g * N:G + (g + 1) * N].astype(bf16)
                cb = lax.dot_general(cm, bm, _NT, preferred_element_type=f32)
                for hh in range(H // SSD_GROUPS):
                    h = g * (H // SSD_GROUPS) + hh
                    j = d * H + h
                    col = jnp.broadcast_to(ccol[:, j:j + 1], (Q, Q))
                    lmat = jnp.exp(jnp.where(keep, col - crow[j:j + 1, :], _NEG))
                    xd = (xc[:, h * P:(h + 1) * P] * jnp.broadcast_to(dtc[:, j:j + 1], (Q, P))).astype(bf16)
                    st = states[h]
                    y = jnp.dot((cb * lmat).astype(bf16), xd, preferred_element_type=f32)
                    y = y + jnp.dot(cm, st.astype(bf16), preferred_element_type=f32) * jnp.exp(col[:, 0:P])
                    outs.append(y)
                    btd = (bt[g * N:(g + 1) * N, :] * jnp.exp(edec[j:j + 1, :])).astype(bf16)
                    new_states.append(jnp.exp(tot[j:j + 1, 0:P]) * st
                                      + jnp.dot(btd, xd, preferred_element_type=f32))
            y_store(r0, jnp.concatenate(outs, axis=-1))
            return tuple(new_states)

        init = tuple(init_ref[0, d, h] for h in range(H))
        final = lax.fori_loop(0, nc, chunk, init, unroll=min(8, nc))
        for h in range(H):
            fin_ref[0, d, h] = final[h]

    def store_fwd(r0, y):
        y_ref[pl.ds(r0, Q), :] = y

    def store_bwd(r0, y):
        y_ref[pl.ds(r0, Q), :] += y

    direction(0, store_fwd)
    direction(1, store_bwd)
    gw = G // SSD_GROUPS
    for t in range(L // tile):
        rows = slice(t * tile, (t + 1) * tile)
        z = zx_ref[0, rows, 0:G]
        yg = (y_ref[rows, :] + xc_ref[rows, 0:G] * dsk_ref[...]) * (z * jax.nn.sigmoid(z))
        parts = []
        for g in range(SSD_GROUPS):
            v = yg[:, g * gw:(g + 1) * gw]
            parts.append(v * lax.rsqrt(jnp.mean(v * v, -1, keepdims=True) + LN_EPS))
        o_ref[0, rows, :] = jnp.concatenate(parts, axis=-1) * ng_ref[...]


def _ssd(pzx, pdt, init, conv_w, conv_b, a_log, dt_bias, d_skip, norm_g):
    Bsz, L, _ = pzx.shape
    G, H, Q = GROUP_W, SSD_HEADS, SSD_CHUNK
    lanes = pdt.shape[-1]
    nc = L // Q
    neg_a = -jnp.exp(a_log.astype(jnp.float32)).reshape(1, 2 * H)
    a_rowv = jnp.pad(neg_a, ((0, 0), (0, lanes - 2 * H)))
    b_rowv = jnp.pad(dt_bias.astype(jnp.float32).reshape(1, 2 * H), ((0, 0), (0, lanes - 2 * H)))
    d_lane = jnp.repeat(d_skip.astype(jnp.float32), SSD_HEAD_DIM).reshape(1, G)
    const = lambda shape: pl.BlockSpec(shape, lambda b: (0,) * len(shape))
    st_spec = pl.BlockSpec((1, 2, H, SSD_STATE, SSD_HEAD_DIM), lambda b: (b, 0, 0, 0, 0))
    f32 = jnp.float32
    return pl.pallas_call(
        functools.partial(_ssd_body, L),
        grid=(Bsz,),
        in_specs=[pl.BlockSpec((1, L, G + XBC_W), lambda b: (b, 0, 0)),
                  pl.BlockSpec((1, L, lanes), lambda b: (b, 0, 0)),
                  const((SSD_CONV, XBC_W)), const((1, XBC_W)),
                  const((1, lanes)), const((2 * H, 1)), const((1, lanes)), const((2 * H, 1)),
                  const((1, G)), const((1, G)), st_spec],
        out_specs=[pl.BlockSpec((1, L, G), lambda b: (b, 0, 0)), st_spec],
        out_shape=[jax.ShapeDtypeStruct((Bsz, L, G), f32),
                   jax.ShapeDtypeStruct((Bsz, 2, H, SSD_STATE, SSD_HEAD_DIM), f32)],
        scratch_shapes=[pltpu.VMEM((L + 2 * _CONV_HALO, XBC_W), f32),
                        pltpu.VMEM((L, XBC_W), f32),
                        pltpu.VMEM((nc, SSD_GROUPS * SSD_STATE, Q), f32),
                        pltpu.VMEM((L, G), f32),
                        pltpu.VMEM((L, lanes), f32),
                        pltpu.VMEM((nc * 2 * H, Q), f32),
                        pltpu.VMEM((nc * 2 * H, Q), f32),
                        pltpu.VMEM((nc * 2 * H, Q), f32)],
        compiler_params=pltpu.CompilerParams(dimension_semantics=("arbitrary",),
                                             vmem_limit_bytes=_VMEM_LIMIT),
        name="ssd",
    )(pzx, pdt, conv_w, conv_b.reshape(1, XBC_W), a_rowv, neg_a.reshape(2 * H, 1),
      b_rowv, dt_bias.astype(f32).reshape(2 * H, 1), d_lane, norm_g.reshape(1, G), init)


def _mixer_ln(x, xc, mod, mod_c, ctx_out, alpha, w_in, w_out, ln_g, ln_b, conf, rpb, hy, ssd):
    Bsz = x.shape[0]
    sh, sc, g = mod
    shc, scc, gc = mod_c
    pa, pb, py, pzx, pdt = _in_proj(x, sc, sh, w_in)
    ca, cb, cy, czx, cdt = _in_proj(xc, scc, shc, w_in)
    zero = jnp.zeros((Bsz, 2, SSD_HEADS, SSD_STATE, SSD_HEAD_DIM), jnp.float32)
    y_dc, ctx_states = _ssd(czx, cdt, zero, *ssd)
    y_d, _ = _ssd(pzx, pdt, ctx_states, *ssd)
    ys = [_conformer(pa, *conf), _na_attention(pb, cb, rpb), _hyena(py, *hy), y_d]
    x_new = _out_proj_ln(ys, w_out, x, g, ln_g, ln_b, alpha)
    if not ctx_out:
        return x_new, None
    ycs = [_conformer(ca, *conf), _ctx_attention(cb), _hyena(cy, *hy), y_dc]
    return x_new, _out_proj_ln(ycs, w_out, xc, gc, ln_g, ln_b, alpha)


_NEG = -1e30
_ROUTE_TT = 256
_DENSE_TT = 1024
_DENSE_PARTS = 4
_DENSE_EC = 1024
_NT = (((1,), (1,)), ((), ()))


def _bf16_round(x):
    return x.astype(jnp.bfloat16).astype(jnp.float32)


def _oddeven_sort_pairs(n):
    pairs = []
    p = 1
    while p < n:
        k = p
        while k >= 1:
            for j in range(k % p, n - k, 2 * k):
                for i in range(min(k, n - j - k)):
                    if (i + j) // (2 * p) == (i + j + k) // (2 * p):
                        pairs.append((i + j, i + j + k))
            k //= 2
        p *= 2
    return pairs


_SORT16 = _oddeven_sort_pairs(PEER_TOPK)


def _order_pair(vs, i, j):
    a, b = vs[i], vs[j]
    if b is None:
        return
    if a is None:
        vs[i], vs[j] = b, None
        return
    vs[i], vs[j] = jnp.maximum(a, b), jnp.minimum(a, b)


def _top16_replicated(vs):
    vs = list(vs) + [None] * (PEER_TOPK - len(vs))
    for i, j in _SORT16:
        _order_pair(vs, i, j)
    for shift in (4, 2, 1):
        other = [None if v is None else pltpu.roll(v, shift, 0) for v in vs]
        merged = []
        for k in range(PEER_TOPK):
            a, b = vs[k], other[PEER_TOPK - 1 - k]
            merged.append(b if a is None else a if b is None else jnp.maximum(a, b))
        vs = merged
        stride = PEER_TOPK // 2
        while stride >= 1:
            for i in range(PEER_TOPK):
                if (i // stride) % 2 == 0:
                    _order_pair(vs, i, i + stride)
            stride //= 2
    return vs


def _pack_sublanes(blocks):
    sub = lax.broadcasted_iota(jnp.int32, blocks[0].shape, 0)
    out = blocks[0]
    for r in range(1, len(blocks)):
        out = jnp.where(sub == r, blocks[r], out)
    return out


def _pair_candidates(r1, lo1, hi1, r2, lo2, hi2, op, fill):
    keep = lax.broadcasted_iota(jnp.int32, lo1.shape, 0) >= 4
    pieces = [op(r1[0], lo2), op(r1[0], hi2), op(r1[1], lo2), op(r1[2], lo2), op(r1[3], lo2), op(r2[0], hi1)]
    for b in range(3):
        pieces.append(jnp.where(keep, op(r2[b], lo1), fill))
    return pieces


def _peer_route_body(x_ref, sc_ref, sh_ref, wqT_ref, keys_ref, hm_ref, e1_ref, e2_ref, pthr_ref, qT_ref):
    hm = (x_ref[0] * (1.0 + sc_ref[0]) + sh_ref[0]).astype(jnp.bfloat16)
    hm_ref[0] = hm
    qT_ref[...] = lax.dot_general(wqT_ref[...], hm, _NT, preferred_element_type=jnp.float32)
    nblk = PEER_KEYS // 8

    def head(h, carry):
        base = pl.multiple_of(h * PEER_QDIM, PEER_QDIM)
        half_q = PEER_QDIM // 2
        s_both = []
        for p in range(2):
            qb = qT_ref[pl.ds(base + p * half_q, half_q), :].astype(jnp.bfloat16)
            s_both.append(jnp.dot(keys_ref[h, p], qb, preferred_element_type=jnp.float32))
        for half in range(_ROUTE_TT // 128):
            lanes = slice(half * 128, (half + 1) * 128)
            s1 = s_both[0][:, lanes]
            s2 = s_both[1][:, lanes]
            r1 = _top16_replicated([s1[8 * k:8 * k + 8] for k in range(nblk)])
            r2 = _top16_replicated([s2[8 * k:8 * k + 8] for k in range(nblk)])
            lo1, hi1 = _pack_sublanes(r1[:8]), _pack_sublanes(r1[8:])
            lo2, hi2 = _pack_sublanes(r2[:8]), _pack_sublanes(r2[8:])
            cand = _pair_candidates(r1, lo1, hi1, r2, lo2, hi2, lambda a, b: a + b, _NEG)
            cv = _top16_replicated(cand)
            top, thr = cv[0], cv[PEER_TOPK - 1]
            z = jnp.ones_like(top)
            for k in range(1, PEER_TOPK):
                z = z + jnp.exp(cv[k] - top)
            rz = 1.0 / z
            f1 = lambda v: _bf16_round(jnp.exp(v - r1[0]) * rz)
            f2 = lambda v: _bf16_round(jnp.exp(v - r2[0]))
            prod = _pair_candidates([f1(v) for v in r1[:4]], f1(lo1), f1(hi1), [f2(v) for v in r2[:3]], f2(lo2), f2(hi2),
                                    lambda a, b: _bf16_round(a * b), 0.0)
            low = None
            for cpiece, ppiece in zip(cand, prod):
                sel = jnp.where(cpiece >= thr, ppiece, 1e30)
                low = sel if low is None else jnp.minimum(low, sel)
            e1_ref[h, :, lanes] = jnp.exp(s1 - r1[0][0:1]) * rz[0:1]
            e2_ref[h, :, lanes] = (jnp.exp(s2 - r2[0][0:1])).astype(jnp.bfloat16)
            pthr_ref[h, :, lanes] = jnp.min(low, axis=0, keepdims=True)
        return carry

    lax.fori_loop(0, PEER_HEADS, head, 0, unroll=4)


def _peer_route(x, sc, sh, wqT, keys):
    Bsz, S, D = x.shape
    assert S % _ROUTE_TT == 0, S
    nt = S // _ROUTE_TT
    T = Bsz * S
    tab = jax.ShapeDtypeStruct((PEER_HEADS, PEER_KEYS, T), jnp.float32)
    return pl.pallas_call(
        _peer_route_body,
        grid=(Bsz, nt),
        in_specs=[pl.BlockSpec((1, _ROUTE_TT, D), lambda b, i: (b, i, 0)),
                  pl.BlockSpec((1, 1, D), lambda b, i: (b, 0, 0)),
                  pl.BlockSpec((1, 1, D), lambda b, i: (b, 0, 0)),
                  pl.BlockSpec(wqT.shape, lambda b, i: (0, 0)),
                  pl.BlockSpec(keys.shape, lambda b, i: (0, 0, 0, 0))],
        out_specs=[pl.BlockSpec((1, _ROUTE_TT, D), lambda b, i: (b, i, 0)),
                   pl.BlockSpec((PEER_HEADS, PEER_KEYS, _ROUTE_TT), lambda b, i: (0, 0, b * nt + i)),
                   pl.BlockSpec((PEER_HEADS, PEER_KEYS, _ROUTE_TT), lambda b, i: (0, 0, b * nt + i)),
                   pl.BlockSpec((PEER_HEADS, 1, _ROUTE_TT), lambda b, i: (0, 0, b * nt + i))],
        out_shape=[jax.ShapeDtypeStruct((Bsz, S, D), jnp.bfloat16), tab,
                   jax.ShapeDtypeStruct(tab.shape, jnp.bfloat16),
                   jax.ShapeDtypeStruct((PEER_HEADS, 1, T), jnp.float32)],
        scratch_shapes=[pltpu.VMEM((PEER_HEADS * PEER_QDIM, _ROUTE_TT), jnp.float32)],
        compiler_params=pltpu.CompilerParams(dimension_semantics=("arbitrary", "arbitrary"),
                                             vmem_limit_bytes=_VMEM_LIMIT),
        name="peer_route",
    )(x, sc, sh, wqT, keys)


_GELU_K = math.sqrt(2.0 / math.pi)


def _gelu_tanh(x):
    half = 0.5 * x
    return half + half * jnp.tanh(x * (_GELU_K + (_GELU_K * 0.044715) * (x * x)))


_PACK = 16


def _peer_dense_body(alpha, hm_ref, e1_ref, e2_ref, pthr_ref, u_ref, vt_ref, x_ref, g_ref, lng_ref, lnb_ref,
                     o_ref, acc_ref, wt_ref, e1b_ref, pthrb_ref):
    c = pl.program_id(2)
    jrows = 32
    bf16 = jnp.bfloat16
    tp = _DENSE_TT // _DENSE_PARTS

    @pl.when(c == 0)
    def _():
        acc_ref[...] = jnp.zeros_like(acc_ref)
        for h in range(PEER_HEADS):
            pthrb_ref[h] = jnp.broadcast_to(pthr_ref[h], (_PACK, _DENSE_TT)).astype(bf16)

    parts = [slice(t * tp, (t + 1) * tp) for t in range(_DENSE_PARTS)]

    def scores(toks):
        return lax.dot_general(u_ref[...], hm_ref[0, toks, :], _NT, preferred_element_type=jnp.float32)

    def build(toks, act):
        for ii in range(_DENSE_EC // PEER_KEYS):
            for h in range(PEER_HEADS):
                e1b_ref[ii % 2, h] = jnp.broadcast_to(e1_ref[h, ii:ii + 1, toks], (_PACK, tp)).astype(bf16)
            for jb in range(PEER_KEYS // jrows):
                r0 = ii * PEER_KEYS + jb * jrows
                gate = jnp.zeros((jrows // _PACK, _PACK, tp), bf16)
                for h in range(PEER_HEADS):
                    e2 = e2_ref[h, jb * jrows:(jb + 1) * jrows, toks].reshape(jrows // _PACK, _PACK, tp)
                    val = e2 * e1b_ref[ii % 2, h][None]
                    gate = gate + jnp.where(val >= pthrb_ref[h, :, toks][None], val, jnp.zeros_like(val))
                gel = _gelu_tanh(act[r0:r0 + jrows, :]).astype(bf16)
                wt_ref[r0:r0 + jrows, toks] = gate.reshape(jrows, tp) * gel
        acc_ref[:, toks] += jnp.dot(vt_ref[...], wt_ref[:, toks], preferred_element_type=jnp.float32)

    acts = {t: scores(parts[t]) for t in range(min(2, _DENSE_PARTS))}
    for t in range(_DENSE_PARTS):
        build(parts[t], acts.pop(t))
        if t + 2 < _DENSE_PARTS:
            acts[t + 2] = scores(parts[t + 2])

    @pl.when(c == pl.num_programs(2) - 1)
    def _():
        y = alpha * x_ref[0] + g_ref[0] * acc_ref[...].T
        mu = jnp.mean(y, -1, keepdims=True)
        yc = y - mu
        var = jnp.mean(yc * yc, -1, keepdims=True)
        o_ref[0] = yc * lax.rsqrt(var + LN_EPS) * lng_ref[...] + lnb_ref[...]


def _peer_dense(hm, e1, e2, pthr, u_bf, vt_bf, layer, x, g, ln_g, ln_b, alpha):
    Bsz, S, D = x.shape
    assert S % _DENSE_TT == 0, S
    nt = S // _DENSE_TT
    nchunk = N_EXPERTS // _DENSE_EC
    rows_i = _DENSE_EC // PEER_KEYS
    return pl.pallas_call(
        functools.partial(_peer_dense_body, alpha),
        grid=(Bsz, nt, nchunk),
        in_specs=[pl.BlockSpec((1, _DENSE_TT, D), lambda b, i, c: (b, i, 0)),
                  pl.BlockSpec((PEER_HEADS, rows_i, _DENSE_TT), lambda b, i, c: (0, c, b * nt + i)),
                  pl.BlockSpec((PEER_HEADS, PEER_KEYS, _DENSE_TT), lambda b, i, c: (0, 0, b * nt + i)),
                  pl.BlockSpec((PEER_HEADS, 1, _DENSE_TT), lambda b, i, c: (0, 0, b * nt + i)),
                  pl.BlockSpec((None, _DENSE_EC, D), lambda b, i, c: (layer, c, 0)),
                  pl.BlockSpec((None, D, _DENSE_EC), lambda b, i, c: (layer, 0, c)),
                  pl.BlockSpec((1, _DENSE_TT, D), lambda b, i, c: (b, i, 0)),
                  pl.BlockSpec((1, 1, D), lambda b, i, c: (b, 0, 0)),
                  pl.BlockSpec((1, D), lambda b, i, c: (0, 0)),
                  pl.BlockSpec((1, D), lambda b, i, c: (0, 0))],
        out_specs=pl.BlockSpec((1, _DENSE_TT, D), lambda b, i, c: (b, i, 0)),
        out_shape=jax.ShapeDtypeStruct((Bsz, S, D), jnp.float32),
        scratch_shapes=[pltpu.VMEM((D, _DENSE_TT), jnp.float32),
                        pltpu.VMEM((_DENSE_EC, _DENSE_TT), jnp.bfloat16),
                        pltpu.VMEM((2, PEER_HEADS, _PACK, _DENSE_TT // _DENSE_PARTS), jnp.bfloat16),
                        pltpu.VMEM((PEER_HEADS, _PACK, _DENSE_TT), jnp.bfloat16)],
        compiler_params=pltpu.CompilerParams(dimension_semantics=("arbitrary", "arbitrary", "arbitrary"),
                                             vmem_limit_bytes=_VMEM_LIMIT),
        name="peer_dense",
    )(hm, e1, e2, pthr, u_bf, vt_bf, x, g, ln_g.reshape(1, D), ln_b.reshape(1, D))


def _peer_tables(u_tab, v_tab):
    return u_tab.astype(jnp.bfloat16), jnp.swapaxes(v_tab, 1, 2).astype(jnp.bfloat16)


def _peer_ln(x, sc, sh, g, wq, sub_keys, tables, layer, ln_g, ln_b, alpha):
    hm, e1, e2, pthr = _peer_route(x, sc, sh, wq.T.astype(jnp.bfloat16), sub_keys.astype(jnp.bfloat16))
    return _peer_dense(hm, e1, e2, pthr, tables[0], tables[1], layer, x, g, ln_g, ln_b, alpha)


def kernel(x, c, ctx, c_ctx, w_ada, b_ada, w_in, w_out, ln1_g, ln1_b, ln2_g, ln2_b,
           conf_dw_w, conf_dw_b, conf_norm_g, conf_norm_b, na_rpb, hy_short_w, hy_short_b,
           hy_w1, hy_b1, hy_w2, hy_b2, hy_w3, hy_decay, hy_bias, ssd_conv_w, ssd_conv_b,
           ssd_a_log, ssd_dt_bias, ssd_d, ssd_norm_g, peer_wq, peer_keys, peer_u, peer_v):
    alpha = (2.0 * DEPTH) ** 0.25
    s_c = jax.nn.silu(c)
    s_cc = jax.nn.silu(c_ctx)
    xc = ctx
    Bsz, Lc, D = ctx.shape
    tables = _peer_tables(peer_u, peer_v)
    for l in range(DEPTH):
        ctx_out = l < DEPTH - 1
        mod = (s_c @ w_ada[l] + b_ada[l])[:, None, :]
        mod_c = jnp.broadcast_to((s_cc @ w_ada[l] + b_ada[l])[None, None, :], mod.shape)
        sh1, sc1, g1, sh2, sc2, g2 = jnp.split(mod, 6, -1)
        sh1c, sc1c, g1c, sh2c, sc2c, g2c = jnp.split(mod_c, 6, -1)
        x, xc = _mixer_ln(
            x, xc, (sh1, sc1, g1), (sh1c, sc1c, g1c), ctx_out, alpha, w_in[l], w_out[l], ln1_g[l], ln1_b[l],
            (conf_dw_w[l], conf_dw_b[l], conf_norm_g[l], conf_norm_b[l]), na_rpb[l],
            (hy_short_w[l], hy_short_b[l], hy_w1[l], hy_b1[l], hy_w2[l], hy_b2[l], hy_w3[l], hy_decay[l], hy_bias[l]),
            (ssd_conv_w[l], ssd_conv_b[l], ssd_a_log[l], ssd_dt_bias[l], ssd_d[l], ssd_norm_g[l]))
        x = _peer_ln(x, sc2, sh2, g2, peer_wq[l], peer_keys[l], tables, l, ln2_g[l], ln2_b[l], alpha)
        if ctx_out:
            xc = _peer_ln(xc.reshape(1, Bsz * Lc, D), sc2c[:1], sh2c[:1], g2c[:1], peer_wq[l], peer_keys[l], tables, l,
                          ln2_g[l], ln2_b[l], alpha).reshape(Bsz, Lc, D)
    return x
```

```python
import functools
import math
import jax, jax.numpy as jnp
from jax import lax
from jax.experimental import pallas as pl
from jax.experimental.pallas import tpu as pltpu

D_MODEL = 1024
DEPTH = 2

GRID_W = 64
N_MIXERS = 4
GROUP_W = D_MODEL // N_MIXERS
D_MIX = N_MIXERS * GROUP_W
LN_EPS = 1e-5
CONF_K = 31
CONF_GROUPS = 4
NA_HEADS = 4
HEAD_DIM = GROUP_W // NA_HEADS
NA_KH = 8
NA_KW = 16
ROPE_BASE = 10000.0
HY_SHORT = 3
HY_BANDS = 16
HY_EMB = 1 + 2 * HY_BANDS
HY_HIDDEN = 64
HY_SIN_FREQ = 1.0
SSD_HEADS = 4
SSD_HEAD_DIM = GROUP_W // SSD_HEADS
SSD_GROUPS = 2
SSD_STATE = 64
SSD_CONV = 3
SSD_CHUNK = 128
XBC_W = GROUP_W + 2 * SSD_GROUPS * SSD_STATE
IN_COLS = 2 * GROUP_W + 3 * GROUP_W + 3 * GROUP_W + GROUP_W + XBC_W + 2 * SSD_HEADS
PEER_HEADS = 8
PEER_KEYS = 128
PEER_TOPK = 16
PEER_QDIM = 256
N_EXPERTS = PEER_KEYS * PEER_KEYS

_LANES = 128
_SUBLANES = 8
_VMEM_LIMIT = 56 * 1024 * 1024
_IN_PAD = -(-IN_COLS // _LANES) * _LANES


_ADA_TN = 1536


def _ada_mod_body(c_ref, w_ref, b_ref, o_ref):
    s = c_ref[...]
    s = (s * jax.nn.sigmoid(s)).astype(jnp.bfloat16)
    o_ref[...] = jnp.dot(s, w_ref[...].astype(jnp.bfloat16), preferred_element_type=jnp.float32) + b_ref[...]


def _ada_mod(cond, w, b):
    R, D = cond.shape
    N = w.shape[1]
    assert N % _ADA_TN == 0, N
    return pl.pallas_call(
        _ada_mod_body,
        grid=(N // _ADA_TN,),
        in_specs=[pl.BlockSpec((R, D), lambda j: (0, 0)),
                  pl.BlockSpec((D, _ADA_TN), lambda j: (0, j)),
                  pl.BlockSpec((1, _ADA_TN), lambda j: (0, j))],
        out_specs=pl.BlockSpec((R, _ADA_TN), lambda j: (0, j)),
        out_shape=jax.ShapeDtypeStruct((R, N), jnp.float32),
        compiler_params=pltpu.CompilerParams(dimension_semantics=("arbitrary",), vmem_limit_bytes=_VMEM_LIMIT),
        name="ada_mod",
    )(cond, w, b.reshape(1, N))


_IN_SPLITS = ((0, 2 * GROUP_W), (2 * GROUP_W, 5 * GROUP_W), (5 * GROUP_W, 8 * GROUP_W),
              (8 * GROUP_W, 9 * GROUP_W + XBC_W), (9 * GROUP_W + XBC_W, _IN_PAD))
_PROJ_TM = 512


def _in_proj_body(x_ref, sc_ref, sh_ref, w_ref, *o_refs):
    h = (x_ref[0] * (1.0 + sc_ref[0]) + sh_ref[0]).astype(jnp.bfloat16)
    for (lo, hi), o_ref in zip(_IN_SPLITS, o_refs):
        o_ref[0] = jnp.dot(h, w_ref[:, lo:hi], preferred_element_type=jnp.float32)


def _in_proj(x, sc, sh, w_in):
    Bsz, L, D = x.shape
    tm = min(_PROJ_TM, L)
    w = jnp.pad(w_in, ((0, 0), (0, _IN_PAD - IN_COLS))).astype(jnp.bfloat16)
    return pl.pallas_call(
        _in_proj_body,
        grid=(Bsz, L // tm),
        in_specs=[pl.BlockSpec((1, tm, D), lambda b, i: (b, i, 0)),
                  pl.BlockSpec((1, 1, D), lambda b, i: (b, 0, 0)),
                  pl.BlockSpec((1, 1, D), lambda b, i: (b, 0, 0)),
                  pl.BlockSpec((D, _IN_PAD), lambda b, i: (0, 0))],
        out_specs=[pl.BlockSpec((1, tm, hi - lo), lambda b, i: (b, i, 0)) for lo, hi in _IN_SPLITS],
        out_shape=[jax.ShapeDtypeStruct((Bsz, L, hi - lo), jnp.float32) for lo, hi in _IN_SPLITS],
        compiler_params=pltpu.CompilerParams(dimension_semantics=("arbitrary", "arbitrary"),
                                             vmem_limit_bytes=_VMEM_LIMIT),
        name="in_proj",
    )(x, sc, sh, w)


def _out_proj_ln_body(alpha, ya_ref, yb_ref, yc_ref, yd_ref, w_ref, x_ref, g_ref, lng_ref, lnb_ref, o_ref):
    y = 0.0
    for m, y_ref in enumerate((ya_ref, yb_ref, yc_ref, yd_ref)):
        y = y + jnp.dot(y_ref[0].astype(jnp.bfloat16), w_ref[m * GROUP_W:(m + 1) * GROUP_W, :],
                        preferred_element_type=jnp.float32)
    r = alpha * x_ref[0] + g_ref[0] * y
    mu = jnp.mean(r, -1, keepdims=True)
    rc = r - mu
    var = jnp.mean(rc * rc, -1, keepdims=True)
    o_ref[0] = rc * lax.rsqrt(var + LN_EPS) * lng_ref[...] + lnb_ref[...]


def _out_proj_ln(ys, w_out, x, g, ln_g, ln_b, alpha):
    Bsz, L, D = x.shape
    tm = min(_PROJ_TM, L)
    mix = pl.BlockSpec((1, tm, GROUP_W), lambda b, i: (b, i, 0))
    return pl.pallas_call(
        functools.partial(_out_proj_ln_body, alpha),
        grid=(Bsz, L // tm),
        in_specs=[mix, mix, mix, mix,
                  pl.BlockSpec((D_MIX, D), lambda b, i: (0, 0)),
                  pl.BlockSpec((1, tm, D), lambda b, i: (b, i, 0)),
                  pl.BlockSpec((1, 1, D), lambda b, i: (b, 0, 0)),
                  pl.BlockSpec((1, D), lambda b, i: (0, 0)),
                  pl.BlockSpec((1, D), lambda b, i: (0, 0))],
        out_specs=pl.BlockSpec((1, tm, D), lambda b, i: (b, i, 0)),
        out_shape=jax.ShapeDtypeStruct((Bsz, L, D), jnp.float32),
        compiler_params=pltpu.CompilerParams(dimension_semantics=("arbitrary", "arbitrary"),
                                             vmem_limit_bytes=_VMEM_LIMIT),
        name="out_proj_ln",
    )(*ys, w_out.astype(jnp.bfloat16), x, g, ln_g.reshape(1, D), ln_b.reshape(1, D))


def _shifted_taps(win, first, n_taps, rows):
    n = win.shape[0]
    rolled = {0: win}
    taps = []
    for k in range(n_taps):
        r = (first + k) % _SUBLANES
        if r not in rolled:
            rolled[r] = pltpu.roll(win, n - r, 0)
        base = first + k - r
        taps.append(rolled[r][base:base + rows])
    return taps


_CONV_HALO = 16
_CONV_TILE = 256


def _group_mean_matrix(width, group):
    r = lax.broadcasted_iota(jnp.int32, (width, width), 0) // group
    c = lax.broadcasted_iota(jnp.int32, (width, width), 1) // group
    return jnp.where(r == c, 1.0 / group, 0.0).astype(jnp.float32)


def _conformer_body(L, p_ref, w_ref, b_ref, ng_ref, nb_ref, o_ref, u_ref):
    G = GROUP_W
    pad = (CONF_K - 1) // 2
    halo = jnp.zeros((_CONV_HALO, G), jnp.float32)
    u_ref[0:_CONV_HALO, :] = halo
    u_ref[_CONV_HALO + L:_CONV_HALO + L + _CONV_HALO, :] = halo
    u_ref[_CONV_HALO:_CONV_HALO + L, :] = p_ref[0, :, 0:G] * jax.nn.sigmoid(p_ref[0, :, G:2 * G])
    avg = _group_mean_matrix(G, G // CONF_GROUPS)
    tile = min(_CONV_TILE, L)
    for t in range(L // tile):
        win = u_ref[t * tile:t * tile + tile + 2 * _CONV_HALO, :]
        acc = jnp.zeros((tile, G), jnp.float32) + b_ref[...]
        for k, tap in enumerate(_shifted_taps(win, _CONV_HALO - pad, CONF_K, tile)):
            acc = acc + tap * w_ref[k:k + 1, :]
        mu = jnp.dot(acc, avg, preferred_element_type=jnp.float32, precision=lax.Precision.HIGHEST)
        cen = acc - mu
        var = jnp.dot(cen * cen, avg, preferred_element_type=jnp.float32, precision=lax.Precision.HIGHEST)
        un = cen * lax.rsqrt(var + LN_EPS) * ng_ref[...] + nb_ref[...]
        o_ref[0, t * tile:(t + 1) * tile, :] = un * jax.nn.sigmoid(un)


def _conformer(pa, dw_w, dw_b, n_g, n_b):
    Bsz, L, _ = pa.shape
    G = GROUP_W
    vec = pl.BlockSpec((1, G), lambda b: (0, 0))
    return pl.pallas_call(
        functools.partial(_conformer_body, L),
        grid=(Bsz,),
        in_specs=[pl.BlockSpec((1, L, 2 * G), lambda b: (b, 0, 0)),
                  pl.BlockSpec((CONF_K, G), lambda b: (0, 0)), vec, vec, vec],
        out_specs=pl.BlockSpec((1, L, G), lambda b: (b, 0, 0)),
        out_shape=jax.ShapeDtypeStruct((Bsz, L, G), jnp.float32),
        scratch_shapes=[pltpu.VMEM((L + 2 * _CONV_HALO, G), jnp.float32)],
        compiler_params=pltpu.CompilerParams(dimension_semantics=("arbitrary",),
                                             vmem_limit_bytes=_VMEM_LIMIT),
        name="conformer",
    )(pa, dw_w, dw_b.reshape(1, G), n_g.reshape(1, G), n_b.reshape(1, G))


def axial_rope(rows, head_dim):
    n_f = head_dim // 4
    inv = ROPE_BASE ** (-jnp.arange(n_f, dtype=jnp.float32) / n_f)
    t = jnp.arange(rows * GRID_W)
    r = (t // GRID_W).astype(jnp.float32)
    col = (t % GRID_W).astype(jnp.float32)
    ang = jnp.concatenate([r[:, None] * inv, col[:, None] * inv], -1)
    return jnp.cos(ang), jnp.sin(ang)


_NA_MASK = -1e30


def _na_tables(rows, rpb):
    cos, sin = axial_rope(rows, HEAD_DIM)
    cos_f = jnp.tile(cos, (1, 2 * NA_HEADS))
    sin_s = jnp.tile(jnp.concatenate([-sin, sin], -1), (1, NA_HEADS))
    cq = jnp.arange(GRID_W)
    cs = jnp.clip(cq - NA_KW // 2, 0, GRID_W - NA_KW)
    col = jnp.arange(GRID_W)
    in_band = (col[None, :] >= cs[:, None]) & (col[None, :] < cs[:, None] + NA_KW)
    cb_idx = jnp.clip(col[None, :] - cq[:, None] + (NA_KW - 1), 0, 2 * NA_KW - 2)
    po = jnp.arange(NA_KH)
    rb_idx = jnp.arange(NA_KH)[None, :] - po[:, None] + (NA_KH - 1)
    bias = rpb[:, rb_idx][:, :, :, cb_idx]
    bias = jnp.where(in_band[None, None, None], bias, _NA_MASK)
    bias = bias.transpose(1, 0, 3, 2, 4).reshape(NA_KH, NA_HEADS, GRID_W, NA_KH * GRID_W)
    return cos_f, sin_s, bias


def _rope_lanes(x, cos_f, sin_s):
    n = x.shape[-1]
    hd = HEAD_DIM // 2
    first = (lax.broadcasted_iota(jnp.int32, x.shape, 1) % HEAD_DIM) < hd
    partner = jnp.where(first, pltpu.roll(x, n - hd, 1), pltpu.roll(x, hd, 1))
    return x * cos_f + partner * sin_s


def _softmax_pv(s_parts, v_parts):
    m = s_parts[0].max(-1, keepdims=True)
    for s in s_parts[1:]:
        m = jnp.maximum(m, s.max(-1, keepdims=True))
    l = 0.0
    o = 0.0
    for s, v in zip(s_parts, v_parts):
        p = jnp.exp(s - m)
        l = l + p.sum(-1, keepdims=True)
        o = o + jnp.dot(p.astype(jnp.bfloat16), v, preferred_element_type=jnp.float32)
    return o / l


_NA_ROWS = 4


def _na_body(rows, q_ref, k_ref, v_ref, kc_ref, vc_ref, cosq_ref, sinq_ref, cosk_ref, sink_ref, *rest):
    bias_refs, (o_ref, krot_ref, vbf_ref) = rest[:_NA_ROWS], rest[_NA_ROWS:]
    step = pl.program_id(1)
    scale = HEAD_DIM ** -0.5

    @pl.when(step == 0)
    def _():
        krot_ref[...] = _rope_lanes(k_ref[0], cosk_ref[...], sink_ref[...]).astype(jnp.bfloat16)
        vbf_ref[...] = v_ref[0].astype(jnp.bfloat16)

    win = NA_KH * GRID_W
    q = q_ref[0] * scale
    q_rot = _rope_lanes(q, cosq_ref[...], sinq_ref[...]).astype(jnp.bfloat16)
    q_plain = q.astype(jnp.bfloat16)
    kc = kc_ref[0].astype(jnp.bfloat16)
    vc = vc_ref[0].astype(jnp.bfloat16)
    for i in range(_NA_ROWS):
        r = step * _NA_ROWS + i
        rs = jnp.clip(r - NA_KH // 2, 0, rows - NA_KH)
        start = pl.multiple_of(rs * GRID_W, GRID_W)
        kw = krot_ref[pl.ds(start, win), :]
        vw = vbf_ref[pl.ds(start, win), :]
        qs = slice(i * GRID_W, (i + 1) * GRID_W)
        outs = []
        for h in range(NA_HEADS):
            hs = slice(h * HEAD_DIM, (h + 1) * HEAD_DIM)
            s_loc = lax.dot_general(q_rot[qs, hs], kw[:, hs], _NT, preferred_element_type=jnp.float32) + bias_refs[i][0, h]
            s_ctx = lax.dot_general(q_plain[qs, hs], kc[:, hs], _NT, preferred_element_type=jnp.float32)
            outs.append(_softmax_pv([s_loc, s_ctx], [vw[:, hs], vc[:, hs]]))
        o_ref[0, qs, :] = jnp.concatenate(outs, axis=-1)


def _na_attention(pb, cb, rpb):
    Bsz, S, _ = pb.shape
    Lc = cb.shape[1]
    rows = S // GRID_W
    assert rows % _NA_ROWS == 0, rows
    G = GROUP_W
    nq = _NA_ROWS * GRID_W
    cos_f, sin_s, bias = _na_tables(rows, rpb)

    def bias_spec(i):
        def idx(b, step):
            r = step * _NA_ROWS + i
            return (r - jnp.clip(r - NA_KH // 2, 0, rows - NA_KH), 0, 0, 0)
        return pl.BlockSpec((1, NA_HEADS, GRID_W, NA_KH * GRID_W), idx)

    return pl.pallas_call(
        functools.partial(_na_body, rows),
        grid=(Bsz, rows // _NA_ROWS),
        in_specs=[pl.BlockSpec((1, nq, G), lambda b, r: (b, r, 0)),
                  pl.BlockSpec((1, S, G), lambda b, r: (b, 0, 1)),
                  pl.BlockSpec((1, S, G), lambda b, r: (b, 0, 2)),
                  pl.BlockSpec((1, Lc, G), lambda b, r: (b, 0, 1)),
                  pl.BlockSpec((1, Lc, G), lambda b, r: (b, 0, 2)),
                  pl.BlockSpec((nq, G), lambda b, r: (r, 0)),
                  pl.BlockSpec((nq, G), lambda b, r: (r, 0)),
                  pl.BlockSpec((S, G), lambda b, r: (0, 0)),
                  pl.BlockSpec((S, G), lambda b, r: (0, 0))] + [bias_spec(i) for i in range(_NA_ROWS)],
        out_specs=pl.BlockSpec((1, nq, G), lambda b, r: (b, r, 0)),
        out_shape=jax.ShapeDtypeStruct((Bsz, S, G), jnp.float32),
        scratch_shapes=[pltpu.VMEM((S, G), jnp.bfloat16), pltpu.VMEM((S, G), jnp.bfloat16)],
        compiler_params=pltpu.CompilerParams(dimension_semantics=("arbitrary", "arbitrary"),
                                             vmem_limit_bytes=_VMEM_LIMIT),
        name="na_attention",
    )(pb, pb, pb, cb, cb, cos_f, sin_s, cos_f, sin_s, *([bias] * _NA_ROWS))


def _ctx_attn_body(q_ref, k_ref, v_ref, o_ref):
    q = (q_ref[0] * HEAD_DIM ** -0.5).astype(jnp.bfloat16)
    k = k_ref[0].astype(jnp.bfloat16)
    v = v_ref[0].astype(jnp.bfloat16)
    outs = []
    for h in range(NA_HEADS):
        hs = slice(h * HEAD_DIM, (h + 1) * HEAD_DIM)
        s = lax.dot_general(q[:, hs], k[:, hs], _NT, preferred_element_type=jnp.float32)
        outs.append(_softmax_pv([s], [v[:, hs]]))
    o_ref[0] = jnp.concatenate(outs, axis=-1)


def _ctx_attention(cb):
    Bsz, Lc, _ = cb.shape
    G = GROUP_W
    return pl.pallas_call(
        _ctx_attn_body,
        grid=(Bsz,),
        in_specs=[pl.BlockSpec((1, Lc, G), lambda b: (b, 0, 0)),
                  pl.BlockSpec((1, Lc, G), lambda b: (b, 0, 1)),
                  pl.BlockSpec((1, Lc, G), lambda b: (b, 0, 2))],
        out_specs=pl.BlockSpec((1, Lc, G), lambda b: (b, 0, 0)),
        out_shape=jax.ShapeDtypeStruct((Bsz, Lc, G), jnp.float32),
        compiler_params=pltpu.CompilerParams(dimension_semantics=("arbitrary",)),
        name="ctx_attention",
    )(cb, cb, cb)


_HY_TB = 256
_HY_CB = 8


def _hyena_filter_body(L, w1_ref, b1_ref, w2_ref, b2_ref, w3_ref, dec_ref, o_ref):
    G = GROUP_W
    hp = lax.Precision.HIGHEST
    p = lax.broadcasted_iota(jnp.int32, (2 * L, _LANES), 0)
    lane = lax.broadcasted_iota(jnp.int32, (2 * L, _LANES), 1)
    tn = jnp.abs(p - L).astype(jnp.float32) / L
    band = ((lane - 1) % HY_BANDS + 1).astype(jnp.float32)
    ang = 2.0 * math.pi * band * tn
    z = jnp.where(lane == 0, tn, jnp.where(lane <= HY_BANDS, jnp.sin(ang), jnp.cos(ang)))
    z = jnp.where(lane < HY_EMB, z, 0.0)
    h = jnp.sin(HY_SIN_FREQ * (jnp.dot(z, w1_ref[...], precision=hp, preferred_element_type=jnp.float32) + b1_ref[...]))
    h = jnp.sin(HY_SIN_FREQ * (jnp.dot(h, w2_ref[...], precision=hp, preferred_element_type=jnp.float32) + b2_ref[...]))
    k = jnp.dot(h, w3_ref[...], precision=hp, preferred_element_type=jnp.float32) * jnp.exp(-tn[:, 0:1] * dec_ref[...])
    kf, kb = k[:, 0:G], k[:, G:2 * G]
    n = lax.broadcasted_iota(jnp.int32, (2 * L, G), 0) - L
    nf = jnp.sum(jnp.where(n >= 0, jnp.abs(kf), 0.0), axis=0, keepdims=True) + 1e-6
    nb = jnp.sum(jnp.where(n <= 0, jnp.where(n > -L, jnp.abs(kb), 0.0), 0.0), axis=0, keepdims=True) + 1e-6
    rev = jnp.where(n > 0, kb / nb, jnp.where(n > -L, kf / nf, 0.0))
    o_ref[...] = rev.T


def _hyena_filters_rev(L, w1, b1, w2, b2, w3, decay):
    G = GROUP_W
    hp = _LANES - HY_HIDDEN
    w1p = jnp.pad(w1.astype(jnp.float32), ((0, _LANES - HY_EMB), (0, hp)))
    args = (w1p, jnp.pad(b1.reshape(1, -1), ((0, 0), (0, hp))), jnp.pad(w2, ((0, hp), (0, hp))),
            jnp.pad(b2.reshape(1, -1), ((0, 0), (0, hp))), jnp.pad(w3, ((0, hp), (0, 0))), decay.reshape(1, -1))
    return pl.pallas_call(
        functools.partial(_hyena_filter_body, L),
        out_shape=jax.ShapeDtypeStruct((G, 2 * L), jnp.float32),
        compiler_params=pltpu.CompilerParams(vmem_limit_bytes=_VMEM_LIMIT),
        name="hyena_filters",
    )(*[a.astype(jnp.float32) for a in args])


def _hyena_pre_body(L, p_ref, w_ref, b_ref, u_ref, ut_ref, x0_ref, xp_ref):
    G = GROUP_W
    W = 3 * G
    halo = jnp.zeros((_CONV_HALO, W), jnp.float32)
    xp_ref[0:_CONV_HALO, :] = halo
    xp_ref[_CONV_HALO + L:_CONV_HALO + L + _CONV_HALO, :] = halo
    xp_ref[_CONV_HALO:_CONV_HALO + L, :] = p_ref[0]
    pad = (HY_SHORT - 1) // 2
    tile = min(_CONV_TILE, L)
    for t in range(L // tile):
        win = xp_ref[t * tile:t * tile + tile + 2 * _CONV_HALO, :]
        acc = jnp.zeros((tile, W), jnp.float32) + b_ref[...]
        for k, tap in enumerate(_shifted_taps(win, _CONV_HALO - pad, HY_SHORT, tile)):
            acc = acc + tap * w_ref[k:k + 1, :]
        rows = slice(t * tile, (t + 1) * tile)
        u = acc[:, 2 * G:3 * G] * acc[:, G:2 * G]
        x0_ref[0, rows, :] = acc[:, 0:G]
        u_ref[0, rows, :] = u
        ut_ref[0, :, t, :] = u.T


def _hyena_pre(py, short_w, short_b):
    Bsz, L, W = py.shape
    G = GROUP_W
    f32 = jnp.float32
    TB = min(_HY_TB, L)
    assert TB == min(_CONV_TILE, L)
    return pl.pallas_call(
        functools.partial(_hyena_pre_body, L),
        grid=(Bsz,),
        in_specs=[pl.BlockSpec((1, L, W), lambda b: (b, 0, 0)),
                  pl.BlockSpec((HY_SHORT, W), lambda b: (0, 0)),
                  pl.BlockSpec((1, W), lambda b: (0, 0))],
        out_specs=[pl.BlockSpec((1, L, G), lambda b: (b, 0, 0)),
                   pl.BlockSpec((1, G, L // TB, TB), lambda b: (b, 0, 0, 0)),
                   pl.BlockSpec((1, L, G), lambda b: (b, 0, 0))],
        out_shape=[jax.ShapeDtypeStruct((Bsz, L, G), f32), jax.ShapeDtypeStruct((Bsz, G, L // TB, TB), f32),
                   jax.ShapeDtypeStruct((Bsz, L, G), f32)],
        scratch_shapes=[pltpu.VMEM((L + 2 * _CONV_HALO, W), f32)],
        compiler_params=pltpu.CompilerParams(dimension_semantics=("arbitrary",), vmem_limit_bytes=_VMEM_LIMIT),
        name="hyena_pre",
    )(py, short_w, short_b.reshape(1, W))


def _toeplitz_tile(rolled, q0):
    TB = _HY_TB
    per = _LANES // _SUBLANES
    i = lax.broadcasted_iota(jnp.int32, (_SUBLANES, _LANES), 0)
    l = lax.broadcasted_iota(jnp.int32, (_SUBLANES, _LANES), 1)
    row_blocks = []
    for rg in range(TB // _SUBLANES):
        k = rg % per
        pieces = []
        for lg in range(TB // _LANES):
            o = TB + _LANES * lg - _SUBLANES * rg
            q, rho = divmod(o, _LANES)
            if rho == 0:
                pieces.append(jnp.where(l - i < 0, rolled(q0 + q - 1, k), rolled(q0 + q, k)))
            else:
                pieces.append(jnp.where(l + rho - i >= _LANES, rolled(q0 + q + 1, k), rolled(q0 + q, k)))
        row_blocks.append(jnp.concatenate(pieces, axis=1))
    return jnp.concatenate(row_blocks, axis=0).astype(jnp.bfloat16)


def _hyena_conv_body(L, Bsz, g_ref, u_ref, o_ref):
    TB = min(_HY_TB, L)
    nb = L // TB
    cols = Bsz * nb
    lane = lax.broadcasted_iota(jnp.int32, (TB, cols), 1) % nb

    def channel(ci, carry):
        u = u_ref[ci] if nb == 1 else u_ref[:, ci].reshape(cols, TB)
        u = u.astype(jnp.bfloat16)
        acc = jnp.zeros((TB, cols), jnp.float32)
        cache = {}
        g_row = g_ref[pl.ds(ci, 1), :]

        def rolled(q, k):
            if (q, k) not in cache:
                blk = jnp.broadcast_to(g_row[:, q * _LANES:(q + 1) * _LANES], (_SUBLANES, _LANES))
                cache[q, k] = pltpu.roll(blk, (_SUBLANES * k) % _LANES, 1, stride=1, stride_axis=0)
            return cache[q, k]

        for d in range(-(nb - 1), nb):
            start = L - TB * d - TB
            tile = _toeplitz_tile(rolled, start // _LANES)
            z = lax.dot_general(tile, u, _NT, preferred_element_type=jnp.float32)
            if d != 0:
                z = jnp.where((lane - d >= 0) & (lane - d < nb), pltpu.roll(z, d % cols, 1), 0.0)
            acc = acc + z
        if nb == 1:
            o_ref[ci] = acc.T
        else:
            o_ref[:, ci] = acc.T.reshape(Bsz, nb, TB)
        return carry

    lax.fori_loop(0, _HY_CB, channel, 0)


def _hyena_conv(g_rev, u_t):
    Bsz, G, nb, TB = u_t.shape
    L = nb * TB
    if nb == 1:
        blk = pl.BlockSpec((_HY_CB, Bsz, TB), lambda c: (c, 0, 0))
        operand, out_shape = jnp.swapaxes(u_t.reshape(Bsz, G, TB), 0, 1), (G, Bsz, TB)
    else:
        blk = pl.BlockSpec((Bsz, _HY_CB, nb, TB), lambda c: (0, c, 0, 0))
        operand, out_shape = u_t, (Bsz, G, nb, TB)
    out = pl.pallas_call(
        functools.partial(_hyena_conv_body, L, Bsz),
        grid=(G // _HY_CB,),
        in_specs=[pl.BlockSpec((_HY_CB, 2 * L), lambda c: (c, 0)), blk],
        out_specs=blk,
        out_shape=jax.ShapeDtypeStruct(out_shape, jnp.float32),
        compiler_params=pltpu.CompilerParams(dimension_semantics=("arbitrary",), vmem_limit_bytes=_VMEM_LIMIT),
        name="hyena_conv",
    )(g_rev, operand)
    return jnp.swapaxes(out, 0, 1).reshape(Bsz, G, nb, TB) if nb == 1 else out


def _hyena_post_body(yt_ref, u_ref, x0_ref, skip_ref, o_ref):
    nb, TB = yt_ref.shape[2], yt_ref.shape[3]
    for t in range(nb):
        rows = slice(t * TB, (t + 1) * TB)
        o_ref[0, rows, :] = (yt_ref[0, :, t, :].T + u_ref[0, rows, :] * skip_ref[...]) * x0_ref[0, rows, :]


def _hyena_post(y_t, u, x0, skip):
    Bsz, L, G = u.shape
    tok = pl.BlockSpec((1, L, G), lambda b: (b, 0, 0))
    return pl.pallas_call(
        _hyena_post_body,
        grid=(Bsz,),
        in_specs=[pl.BlockSpec((1,) + y_t.shape[1:], lambda b: (b, 0, 0, 0)), tok, tok,
                  pl.BlockSpec((1, G), lambda b: (0, 0))],
        out_specs=tok,
        out_shape=jax.ShapeDtypeStruct((Bsz, L, G), jnp.float32),
        compiler_params=pltpu.CompilerParams(dimension_semantics=("arbitrary",), vmem_limit_bytes=_VMEM_LIMIT),
        name="hyena_post",
    )(y_t, u, x0, skip.reshape(1, G))


def _hyena(py, short_w, short_b, w1, b1, w2, b2, w3, decay, skip):
    L = py.shape[1]
    g_rev = _hyena_filters_rev(L, w1, b1, w2, b2, w3, decay)
    u, u_t, x0 = _hyena_pre(py, short_w, short_b)
    return _hyena_post(_hyena_conv(g_rev, u_t), u, x0, skip)


def _split3_dot(a, b_bf16, dims=None):
    hi = a.astype(jnp.bfloat16)
    r1 = a - hi.astype(jnp.float32)
    mid = r1.astype(jnp.bfloat16)
    lo = (r1 - mid.astype(jnp.float32)).astype(jnp.bfloat16)
    out = 0.0
    for part in (hi, mid, lo):
        if dims is None:
            out = out + jnp.dot(part, b_bf16, preferred_element_type=jnp.float32)
        else:
            out = out + jnp.dot(b_bf16, part, preferred_element_type=jnp.float32)
    return out


def _softplus(x):
    return jnp.maximum(x, 0.0) + jnp.log(1.0 + jnp.exp(-jnp.abs(x)))


def _ssd_body(L, zx_ref, dtc_ref, cw_ref, cb_ref, arow_ref, acol_ref, brow_ref, bcol_ref, dsk_ref,
              ng_ref, init_ref, o_ref, fin_ref, xp_ref, xc_ref, bt_ref, y_ref, ccol_ref, crow_ref, edec_ref,
              tot_ref):
    G = GROUP_W
    Q = SSD_CHUNK
    nc = L // Q
    P = SSD_HEAD_DIM
    N = SSD_STATE
    H = SSD_HEADS
    f32 = jnp.float32
    bf16 = jnp.bfloat16
    halo = jnp.zeros((_CONV_HALO, XBC_W), f32)
    xp_ref[0:_CONV_HALO, :] = halo
    xp_ref[_CONV_HALO + L:_CONV_HALO + L + _CONV_HALO, :] = halo
    xp_ref[_CONV_HALO:_CONV_HALO + L, :] = zx_ref[0, :, G:G + XBC_W]
    pad = (SSD_CONV - 1) // 2
    tile = min(_CONV_TILE, L)
    for t in range(L // tile):
        win = xp_ref[t * tile:t * tile + tile + 2 * _CONV_HALO, :]
        acc = jnp.zeros((tile, XBC_W), f32) + cb_ref[...]
        for k, tap in enumerate(_shifted_taps(win, _CONV_HALO - pad, SSD_CONV, tile)):
            acc = acc + tap * cw_ref[k:k + 1, :]
        xc_ref[t * tile:(t + 1) * tile, :] = acc * jax.nn.sigmoid(acc)
    for c in range(nc):
        bt_ref[c] = xc_ref[c * Q:(c + 1) * Q, G:G + SSD_GROUPS * N].T
    dt_col = _softplus(dtc_ref[0] + brow_ref[...])
    a_col = dt_col * arow_ref[...]
    dt_rows = dtc_ref[0].T[0:2 * H]
    a_row = _softplus(dt_rows + bcol_ref[...]) * acol_ref[...]
    a_stack = jnp.concatenate([a_row[:, c * Q:(c + 1) * Q] for c in range(nc)], axis=0)
    ri = lax.broadcasted_iota(jnp.int32, (Q, Q), 0)
    ci = lax.broadcasted_iota(jnp.int32, (Q, Q), 1)
    one = lambda m: jnp.where(m, 1.0, 0.0).astype(bf16)
    tot_ref[...] = _split3_dot(a_stack, jnp.ones((Q, Q), bf16))

    def direction(d, y_store):
        fwd = d == 0
        m_col = one(ci <= ri) if fwd else one(ci >= ri)
        for c in range(nc):
            ccol_ref[c * Q:(c + 1) * Q, :] = _split3_dot(a_col[c * Q:(c + 1) * Q, :], m_col, dims="left")
        crow_ref[...] = _split3_dot(a_stack, one(ri <= ci) if fwd else one(ri >= ci))
        edec_ref[...] = _split3_dot(a_stack, one(ri > ci) if fwd else one(ri < ci))
        keep = (ri >= ci) if fwd else (ri <= ci)

        def chunk(step, states):
            c = step if fwd else nc - 1 - step
            r0 = pl.multiple_of(c * Q, Q)
            j0 = pl.multiple_of(c * 2 * H, 2 * H)
            xc = xc_ref[pl.ds(r0, Q), :]
            ccol = ccol_ref[pl.ds(r0, Q), :]
            crow = crow_ref[pl.ds(j0, 2 * H), :]
            edec = edec_ref[pl.ds(j0, 2 * H), :]
            tot = tot_ref[pl.ds(j0, 2 * H), :]
            dtc = _softplus(dtc_ref[0, pl.ds(r0, Q), :] + brow_ref[...])
            bt = bt_ref[c]
            new_states = []
            outs = []
            for g in range(SSD_GROUPS):
                cm = xc[:, G + SSD_GROUPS * N + g * N:G + SSD_GROUPS * N + (g + 1) * N].astype(bf16)
                bm = xc[:, G + g * N:G + (g + 1) * N].astype(bf16)
                cb = lax.dot_general(cm, bm, _NT, preferred_element_type=f32)
                for hh in range(H // SSD_GROUPS):
                    h = g * (H // SSD_GROUPS) + hh
                    j = d * H + h
                    col = jnp.broadcast_to(ccol[:, j:j + 1], (Q, Q))
                    lmat = jnp.exp(jnp.where(keep, col - crow[j:j + 1, :], _NEG))
                    xd = (xc[:, h * P:(h + 1) * P] * jnp.broadcast_to(dtc[:, j:j + 1], (Q, P))).astype(bf16)
                    st = states[h]
                    y = jnp.dot((cb * lmat).astype(bf16), xd, preferred_element_type=f32)
                    y = y + jnp.dot(cm, st.astype(bf16), preferred_element_type=f32) * jnp.exp(col[:, 0:P])
                    outs.append(y)
                    btd = (bt[g * N:(g + 1) * N, :] * jnp.exp(edec[j:j + 1, :])).astype(bf16)
                    new_states.append(jnp.exp(tot[j:j + 1, 0:P]) * st
                                      + jnp.dot(btd, xd, preferred_element_type=f32))
            y_store(r0, jnp.concatenate(outs, axis=-1))
            return tuple(new_states)

        init = tuple(init_ref[0, d, h] for h in range(H))
        final = lax.fori_loop(0, nc, chunk, init, unroll=min(8, nc))
        for h in range(H):
            fin_ref[0, d, h] = final[h]

    def store_fwd(r0, y):
        y_ref[pl.ds(r0, Q), :] = y

    def store_bwd(r0, y):
        y_ref[pl.ds(r0, Q), :] += y

    direction(0, store_fwd)
    direction(1, store_bwd)
    gw = G // SSD_GROUPS
    for t in range(L // tile):
        rows = slice(t * tile, (t + 1) * tile)
        z = zx_ref[0, rows, 0:G]
        yg = (y_ref[rows, :] + xc_ref[rows, 0:G] * dsk_ref[...]) * (z * jax.nn.sigmoid(z))
        parts = []
        for g in range(SSD_GROUPS):
            v = yg[:, g * gw:(g + 1) * gw]
            parts.append(v * lax.rsqrt(jnp.mean(v * v, -1, keepdims=True) + LN_EPS))
        o_ref[0, rows, :] = jnp.concatenate(parts, axis=-1) * ng_ref[...]


def _ssd(pzx, pdt, init, conv_w, conv_b, a_log, dt_bias, d_skip, norm_g):
    Bsz, L, _ = pzx.shape
    G, H, Q = GROUP_W, SSD_HEADS, SSD_CHUNK
    lanes = pdt.shape[-1]
    nc = L // Q
    neg_a = -jnp.exp(a_log.astype(jnp.float32)).reshape(1, 2 * H)
    a_rowv = jnp.pad(neg_a, ((0, 0), (0, lanes - 2 * H)))
    b_rowv = jnp.pad(dt_bias.astype(jnp.float32).reshape(1, 2 * H), ((0, 0), (0, lanes - 2 * H)))
    d_lane = jnp.repeat(d_skip.astype(jnp.float32), SSD_HEAD_DIM).reshape(1, G)
    const = lambda shape: pl.BlockSpec(shape, lambda b: (0,) * len(shape))
    st_spec = pl.BlockSpec((1, 2, H, SSD_STATE, SSD_HEAD_DIM), lambda b: (b, 0, 0, 0, 0))
    f32 = jnp.float32
    return pl.pallas_call(
        functools.partial(_ssd_body, L),
        grid=(Bsz,),
        in_specs=[pl.BlockSpec((1, L, G + XBC_W), lambda b: (b, 0, 0)),
                  pl.BlockSpec((1, L, lanes), lambda b: (b, 0, 0)),
                  const((SSD_CONV, XBC_W)), const((1, XBC_W)),
                  const((1, lanes)), const((2 * H, 1)), const((1, lanes)), const((2 * H, 1)),
                  const((1, G)), const((1, G)), st_spec],
        out_specs=[pl.BlockSpec((1, L, G), lambda b: (b, 0, 0)), st_spec],
        out_shape=[jax.ShapeDtypeStruct((Bsz, L, G), f32),
                   jax.ShapeDtypeStruct((Bsz, 2, H, SSD_STATE, SSD_HEAD_DIM), f32)],
        scratch_shapes=[pltpu.VMEM((L + 2 * _CONV_HALO, XBC_W), f32),
                        pltpu.VMEM((L, XBC_W), f32),
                        pltpu.VMEM((nc, SSD_GROUPS * SSD_STATE, Q), f32),
                        pltpu.VMEM((L, G), f32),
                        pltpu.VMEM((L, lanes), f32),
                        pltpu.VMEM((nc * 2 * H, Q), f32),
                        pltpu.VMEM((nc * 2 * H, Q), f32),
                        pltpu.VMEM((nc * 2 * H, Q), f32)],
        compiler_params=pltpu.CompilerParams(dimension_semantics=("arbitrary",),
                                             vmem_limit_bytes=_VMEM_LIMIT),
        name="ssd",
    )(pzx, pdt, conv_w, conv_b.reshape(1, XBC_W), a_rowv, neg_a.reshape(2 * H, 1),
      b_rowv, dt_bias.astype(f32).reshape(2 * H, 1), d_lane, norm_g.reshape(1, G), init)


def _mixer_ln(x, xc, mod, mod_c, ctx_out, alpha, w_in, w_out, ln_g, ln_b, conf, rpb, hy, ssd):
    Bsz = x.shape[0]
    sh, sc, g = mod
    shc, scc, gc = mod_c
    pa, pb, py, pzx, pdt = _in_proj(x, sc, sh, w_in)
    ca, cb, cy, czx, cdt = _in_proj(xc, scc, shc, w_in)
    zero = jnp.zeros((Bsz, 2, SSD_HEADS, SSD_STATE, SSD_HEAD_DIM), jnp.float32)
    y_dc, ctx_states = _ssd(czx, cdt, zero, *ssd)
    y_d, _ = _ssd(pzx, pdt, ctx_states, *ssd)
    ys = [_conformer(pa, *conf), _na_attention(pb, cb, rpb), _hyena(py, *hy), y_d]
    x_new = _out_proj_ln(ys, w_out, x, g, ln_g, ln_b, alpha)
    if not ctx_out:
        return x_new, None
    ycs = [_conformer(ca, *conf), _ctx_attention(cb), _hyena(cy, *hy), y_dc]
    return x_new, _out_proj_ln(ycs, w_out, xc, gc, ln_g, ln_b, alpha)


_NEG = -1e30
_ROUTE_TT = 256
_DENSE_TT = 1024
_DENSE_PARTS = 4
_DENSE_EC = 1024
_NT = (((1,), (1,)), ((), ()))


def _bf16_round(x):
    return x.astype(jnp.bfloat16).astype(jnp.float32)


def _oddeven_sort_pairs(n):
    pairs = []
    p = 1
    while p < n:
        k = p
        while k >= 1:
            for j in range(k % p, n - k, 2 * k):
                for i in range(min(k, n - j - k)):
                    if (i + j) // (2 * p) == (i + j + k) // (2 * p):
                        pairs.append((i + j, i + j + k))
            k //= 2
        p *= 2
    return pairs


_SORT16 = _oddeven_sort_pairs(PEER_TOPK)


def _order_pair(vs, i, j):
    a, b = vs[i], vs[j]
    if b is None:
        return
    if a is None:
        vs[i], vs[j] = b, None
        return
    vs[i], vs[j] = jnp.maximum(a, b), jnp.minimum(a, b)


def _top16_replicated(vs):
    vs = list(vs) + [None] * (PEER_TOPK - len(vs))
    for i, j in _SORT16:
        _order_pair(vs, i, j)
    for shift in (4, 2, 1):
        other = [None if v is None else pltpu.roll(v, shift, 0) for v in vs]
        merged = []
        for k in range(PEER_TOPK):
            a, b = vs[k], other[PEER_TOPK - 1 - k]
            merged.append(b if a is None else a if b is None else jnp.maximum(a, b))
        vs = merged
        stride = PEER_TOPK // 2
        while stride >= 1:
            for i in range(PEER_TOPK):
                if (i // stride) % 2 == 0:
                    _order_pair(vs, i, i + stride)
            stride //= 2
    return vs


def _pack_sublanes(blocks):
    sub = lax.broadcasted_iota(jnp.int32, blocks[0].shape, 0)
    out = blocks[0]
    for r in range(1, len(blocks)):
        out = jnp.where(sub == r, blocks[r], out)
    return out


def _pair_candidates(r1, lo1, hi1, r2, lo2, hi2, op, fill):
    keep = lax.broadcasted_iota(jnp.int32, lo1.shape, 0) >= 4
    pieces = [op(r1[0], lo2), op(r1[0], hi2), op(r1[1], lo2), op(r1[2], lo2), op(r1[3], lo2), op(r2[0], hi1)]
    for b in range(3):
        pieces.append(jnp.where(keep, op(r2[b], lo1), fill))
    return pieces


def _peer_route_body(x_ref, sc_ref, sh_ref, wqT_ref, keys_ref, hm_ref, e1_ref, e2_ref, pthr_ref, qT_ref):
    hm = (x_ref[0] * (1.0 + sc_ref[0]) + sh_ref[0]).astype(jnp.bfloat16)
    hm_ref[0] = hm
    qT_ref[...] = lax.dot_general(wqT_ref[...], hm, _NT, preferred_element_type=jnp.float32)
    nblk = PEER_KEYS // 8

    def head(h, carry):
        base = pl.multiple_of(h * PEER_QDIM, PEER_QDIM)
        half_q = PEER_QDIM // 2
        s_both = []
        for p in range(2):
            qb = qT_ref[pl.ds(base + p * half_q, half_q), :].astype(jnp.bfloat16)
            s_both.append(jnp.dot(keys_ref[h, p], qb, preferred_element_type=jnp.float32))
        for half in range(_ROUTE_TT // 128):
            lanes = slice(half * 128, (half + 1) * 128)
            s1 = s_both[0][:, lanes]
            s2 = s_both[1][:, lanes]
            r1 = _top16_replicated([s1[8 * k:8 * k + 8] for k in range(nblk)])
            r2 = _top16_replicated([s2[8 * k:8 * k + 8] for k in range(nblk)])
            lo1, hi1 = _pack_sublanes(r1[:8]), _pack_sublanes(r1[8:])
            lo2, hi2 = _pack_sublanes(r2[:8]), _pack_sublanes(r2[8:])
            cand = _pair_candidates(r1, lo1, hi1, r2, lo2, hi2, lambda a, b: a + b, _NEG)
            cv = _top16_replicated(cand)
            top, thr = cv[0], cv[PEER_TOPK - 1]
            z = jnp.ones_like(top)
            for k in range(1, PEER_TOPK):
                z = z + jnp.exp(cv[k] - top)
            rz = 1.0 / z
            f1 = lambda v: _bf16_round(jnp.exp(v - r1[0]) * rz)
            f2 = lambda v: _bf16_round(jnp.exp(v - r2[0]))
            prod = _pair_candidates([f1(v) for v in r1[:4]], f1(lo1), f1(hi1), [f2(v) for v in r2[:3]], f2(lo2), f2(hi2),
                                    lambda a, b: _bf16_round(a * b), 0.0)
            low = None
            for cpiece, ppiece in zip(cand, prod):
                sel = jnp.where(cpiece >= thr, ppiece, 1e30)
                low = sel if low is None else jnp.minimum(low, sel)
            e1_ref[h, :, lanes] = jnp.exp(s1 - r1[0][0:1]) * rz[0:1]
            e2_ref[h, :, lanes] = (jnp.exp(s2 - r2[0][0:1])).astype(jnp.bfloat16)
            pthr_ref[h, :, lanes] = jnp.min(low, axis=0, keepdims=True)
        return carry

    lax.fori_loop(0, PEER_HEADS, head, 0, unroll=4)


def _peer_route(x, sc, sh, wqT, keys):
    Bsz, S, D = x.shape
    assert S % _ROUTE_TT == 0, S
    nt = S // _ROUTE_TT
    T = Bsz * S
    tab = jax.ShapeDtypeStruct((PEER_HEADS, PEER_KEYS, T), jnp.float32)
    return pl.pallas_call(
        _peer_route_body,
        grid=(Bsz, nt),
        in_specs=[pl.BlockSpec((1, _ROUTE_TT, D), lambda b, i: (b, i, 0)),
                  pl.BlockSpec((1, 1, D), lambda b, i: (b, 0, 0)),
                  pl.BlockSpec((1, 1, D), lambda b, i: (b, 0, 0)),
                  pl.BlockSpec(wqT.shape, lambda b, i: (0, 0)),
                  pl.BlockSpec(keys.shape, lambda b, i: (0, 0, 0, 0))],
        out_specs=[pl.BlockSpec((1, _ROUTE_TT, D), lambda b, i: (b, i, 0)),
                   pl.BlockSpec((PEER_HEADS, PEER_KEYS, _ROUTE_TT), lambda b, i: (0, 0, b * nt + i)),
                   pl.BlockSpec((PEER_HEADS, PEER_KEYS, _ROUTE_TT), lambda b, i: (0, 0, b * nt + i)),
                   pl.BlockSpec((PEER_HEADS, 1, _ROUTE_TT), lambda b, i: (0, 0, b * nt + i))],
        out_shape=[jax.ShapeDtypeStruct((Bsz, S, D), jnp.bfloat16), tab,
                   jax.ShapeDtypeStruct(tab.shape, jnp.bfloat16),
                   jax.ShapeDtypeStruct((PEER_HEADS, 1, T), jnp.float32)],
        scratch_shapes=[pltpu.VMEM((PEER_HEADS * PEER_QDIM, _ROUTE_TT), jnp.float32)],
        compiler_params=pltpu.CompilerParams(dimension_semantics=("arbitrary", "arbitrary"),
                                             vmem_limit_bytes=_VMEM_LIMIT),
        name="peer_route",
    )(x, sc, sh, wqT, keys)


_GELU_K = math.sqrt(2.0 / math.pi)


def _gelu_tanh(x):
    half = 0.5 * x
    return half + half * jnp.tanh(x * (_GELU_K + (_GELU_K * 0.044715) * (x * x)))


_PACK = 16


def _peer_dense_body(alpha, hm_ref, e1_ref, e2_ref, pthr_ref, u_ref, vt_ref, x_ref, g_ref, lng_ref, lnb_ref,
                     o_ref, acc_ref, wt_ref, e1b_ref, pthrb_ref):
    c = pl.program_id(2)
    jrows = 32
    bf16 = jnp.bfloat16
    tp = _DENSE_TT // _DENSE_PARTS

    @pl.when(c == 0)
    def _():
        acc_ref[...] = jnp.zeros_like(acc_ref)
        for h in range(PEER_HEADS):
            pthrb_ref[h] = jnp.broadcast_to(pthr_ref[h], (_PACK, _DENSE_TT)).astype(bf16)

    parts = [slice(t * tp, (t + 1) * tp) for t in range(_DENSE_PARTS)]

    def scores(toks):
        return lax.dot_general(u_ref[...], hm_ref[0, toks, :], _NT, preferred_element_type=jnp.float32)

    def build(toks, act):
        for ii in range(_DENSE_EC // PEER_KEYS):
            for h in range(PEER_HEADS):
                e1b_ref[ii % 2, h] = jnp.broadcast_to(e1_ref[h, ii:ii + 1, toks], (_PACK, tp)).astype(bf16)
            for jb in range(PEER_KEYS // jrows):
                r0 = ii * PEER_KEYS + jb * jrows
                gate = jnp.zeros((jrows // _PACK, _PACK, tp), bf16)
                for h in range(PEER_HEADS):
                    e2 = e2_ref[h, jb * jrows:(jb + 1) * jrows, toks].reshape(jrows // _PACK, _PACK, tp)
                    val = e2 * e1b_ref[ii % 2, h][None]
                    gate = gate + jnp.where(val >= pthrb_ref[h, :, toks][None], val, jnp.zeros_like(val))
                gel = _gelu_tanh(act[r0:r0 + jrows, :]).astype(bf16)
                wt_ref[r0:r0 + jrows, toks] = gate.reshape(jrows, tp) * gel
        acc_ref[:, toks] += jnp.dot(vt_ref[...], wt_ref[:, toks], preferred_element_type=jnp.float32)

    acts = {t: scores(parts[t]) for t in range(min(2, _DENSE_PARTS))}
    for t in range(_DENSE_PARTS):
        build(parts[t], acts.pop(t))
        if t + 2 < _DENSE_PARTS:
            acts[t + 2] = scores(parts[t + 2])

    @pl.when(c == pl.num_programs(2) - 1)
    def _():
        y = alpha * x_ref[0] + g_ref[0] * acc_ref[...].T
        mu = jnp.mean(y, -1, keepdims=True)
        yc = y - mu
        var = jnp.mean(yc * yc, -1, keepdims=True)
        o_ref[0] = yc * lax.rsqrt(var + LN_EPS) * lng_ref[...] + lnb_ref[...]


def _peer_dense(hm, e1, e2, pthr, u_bf, vt_bf, layer, x, g, ln_g, ln_b, alpha):
    Bsz, S, D = x.shape
    assert S % _DENSE_TT == 0, S
    nt = S // _DENSE_TT
    nchunk = N_EXPERTS // _DENSE_EC
    rows_i = _DENSE_EC // PEER_KEYS
    return pl.pallas_call(
        functools.partial(_peer_dense_body, alpha),
        grid=(Bsz, nt, nchunk),
        in_specs=[pl.BlockSpec((1, _DENSE_TT, D), lambda b, i, c: (b, i, 0)),
                  pl.BlockSpec((PEER_HEADS, rows_i, _DENSE_TT), lambda b, i, c: (0, c, b * nt + i)),
                  pl.BlockSpec((PEER_HEADS, PEER_KEYS, _DENSE_TT), lambda b, i, c: (0, 0, b * nt + i)),
                  pl.BlockSpec((PEER_HEADS, 1, _DENSE_TT), lambda b, i, c: (0, 0, b * nt + i)),
                  pl.BlockSpec((None, _DENSE_EC, D), lambda b, i, c: (layer, c, 0)),
                  pl.BlockSpec((None, D, _DENSE_EC), lambda b, i, c: (layer, 0, c)),
                  pl.BlockSpec((1, _DENSE_TT, D), lambda b, i, c: (b, i, 0)),
                  pl.BlockSpec((1, 1, D), lambda b, i, c: (b, 0, 0)),
                  pl.BlockSpec((1, D), lambda b, i, c: (0, 0)),
                  pl.BlockSpec((1, D), lambda b, i, c: (0, 0))],
        out_specs=pl.BlockSpec((1, _DENSE_TT, D), lambda b, i, c: (b, i, 0)),
        out_shape=jax.ShapeDtypeStruct((Bsz, S, D), jnp.float32),
        scratch_shapes=[pltpu.VMEM((D, _DENSE_TT), jnp.float32),
                        pltpu.VMEM((_DENSE_EC, _DENSE_TT), jnp.bfloat16),
                        pltpu.VMEM((2, PEER_HEADS, _PACK, _DENSE_TT // _DENSE_PARTS), jnp.bfloat16),
                        pltpu.VMEM((PEER_HEADS, _PACK, _DENSE_TT), jnp.bfloat16)],
        compiler_params=pltpu.CompilerParams(dimension_semantics=("arbitrary", "arbitrary", "arbitrary"),
                                             vmem_limit_bytes=_VMEM_LIMIT),
        name="peer_dense",
    )(hm, e1, e2, pthr, u_bf, vt_bf, x, g, ln_g.reshape(1, D), ln_b.reshape(1, D))


def _peer_tables(u_tab, v_tab):
    return u_tab.astype(jnp.bfloat16), jnp.swapaxes(v_tab, 1, 2).astype(jnp.bfloat16)


def _peer_ln(x, sc, sh, g, wq, sub_keys, tables, layer, ln_g, ln_b, alpha):
    hm, e1, e2, pthr = _peer_route(x, sc, sh, wq.T.astype(jnp.bfloat16), sub_keys.astype(jnp.bfloat16))
    return _peer_dense(hm, e1, e2, pthr, tables[0], tables[1], layer, x, g, ln_g, ln_b, alpha)


def kernel(x, c, ctx, c_ctx, w_ada, b_ada, w_in, w_out, ln1_g, ln1_b, ln2_g, ln2_b,
           conf_dw_w, conf_dw_b, conf_norm_g, conf_norm_b, na_rpb, hy_short_w, hy_short_b,
           hy_w1, hy_b1, hy_w2, hy_b2, hy_w3, hy_decay, hy_bias, ssd_conv_w, ssd_conv_b,
           ssd_a_log, ssd_dt_bias, ssd_d, ssd_norm_g, peer_wq, peer_keys, peer_u, peer_v):
    alpha = (2.0 * DEPTH) ** 0.25
    xc = ctx
    Bsz, Lc, D = ctx.shape
    tables = _peer_tables(peer_u, peer_v)
    pad_rows = -(Bsz + 1) % _SUBLANES
    cond = jnp.concatenate([c, c_ctx[None, :], jnp.zeros((pad_rows, D), c.dtype)], axis=0)
    for l in range(DEPTH):
        ctx_out = l < DEPTH - 1
        mod_all = _ada_mod(cond, w_ada[l], b_ada[l])
        mod = mod_all[:Bsz, None, :]
        mod_c = jnp.broadcast_to(mod_all[Bsz][None, None, :], mod.shape)
        sh1, sc1, g1, sh2, sc2, g2 = jnp.split(mod, 6, -1)
        sh1c, sc1c, g1c, sh2c, sc2c, g2c = jnp.split(mod_c, 6, -1)
        x, xc = _mixer_ln(
            x, xc, (sh1, sc1, g1), (sh1c, sc1c, g1c), ctx_out, alpha, w_in[l], w_out[l], ln1_g[l], ln1_b[l],
            (conf_dw_w[l], conf_dw_b[l], conf_norm_g[l], conf_norm_b[l]), na_rpb[l],
            (hy_short_w[l], hy_short_b[l], hy_w1[l], hy_b1[l], hy_w2[l], hy_b2[l], hy_w3[l], hy_decay[l], hy_bias[l]),
            (ssd_conv_w[l], ssd_conv_b[l], ssd_a_log[l], ssd_dt_bias[l], ssd_d[l], ssd_norm_g[l]))
        x = _peer_ln(x, sc2, sh2, g2, peer_wq[l], peer_keys[l], tables, l, ln2_g[l], ln2_b[l], alpha)
        if ctx_out:
            xc = _peer_ln(xc.reshape(1, Bsz * Lc, D), sc2c[:1], sh2c[:1], g2c[:1], peer_wq[l], peer_keys[l], tables, l,
                          ln2_g[l], ln2_b[l], alpha).reshape(Bsz, Lc, D)
    return x
```
